```python
import math
import jax, jax.numpy as jnp
from jax import lax
import numpy as np

D_MODEL = 1024
BATCH = 8
SEQ = 2048
DEPTH = 2
DEC_BATCH = 128
DEC_SEQ = 8
PAST_LEN = 16384
PAGE_SIZE = 128

N_MIXERS = 2
N_RET_LAYERS = (DEPTH + 1) // 2
N_SSM_LAYERS = DEPTH // 2
RET_HEADS = 4
RET_DK = D_MODEL // RET_HEADS
RET_DV = 2 * RET_DK
RET_VW = RET_HEADS * RET_DV
RET_IN = 2 * RET_HEADS * RET_DK + 2 * RET_VW
SSM_DINNER = 2 * D_MODEL
SSM_HEADDIM = 64
SSM_HEADS = SSM_DINNER // SSM_HEADDIM
SSM_GROUPS = 8
SSM_DSTATE = 128
SSM_CONV = 4
SSM_CONVDIM = SSM_DINNER + 2 * SSM_GROUPS * SSM_DSTATE
SSM_IN = 2 * SSM_DINNER + 2 * SSM_GROUPS * SSM_DSTATE + SSM_HEADS
CHUNK = 128
ROPE_BASE = 10000.0
EPS = 1e-6

kernel_name = "retnet_mamba2_interleaved_decode_step"


def _rms(x, axis=-1):
    xf = x.astype(jnp.float32)
    return xf * lax.rsqrt(jnp.mean(xf * xf, axis=axis, keepdims=True) + EPS)


def rmsnorm(x, g):
    return (_rms(x) * g.astype(jnp.float32)).astype(x.dtype)


def rope(x, pos):
    half = x.shape[-1] // 2
    freqs = ROPE_BASE ** (-jnp.arange(half, dtype=jnp.float32) / half)
    ang = pos.astype(jnp.float32)[:, None] * freqs[None, :]
    cos = jnp.cos(ang)[None, :, None, :]
    sin = jnp.sin(ang)[None, :, None, :]
    xf = x.astype(jnp.float32)
    x1, x2 = xf[..., :half], xf[..., half:]
    return jnp.concatenate([x1 * cos - x2 * sin, x1 * sin + x2 * cos], axis=-1).astype(x.dtype)


def chunked_linear_recurrence(q, k, v, log_a, s0):
    b, L, h, dk = q.shape
    dv = v.shape[-1]
    c = CHUNK if L % CHUNK == 0 else L
    n = L // c

    def to_chunks(t):
        return jnp.moveaxis(t.astype(jnp.float32).reshape((b, n, c) + t.shape[2:]), 1, 0)

    qs, ks, vs, las = to_chunks(q), to_chunks(k), to_chunks(v), to_chunks(log_a)
    mask = jnp.tril(jnp.ones((c, c), dtype=bool))

    def step(s, inp):
        qc, kc, vc, lac = inp
        cum = jnp.cumsum(lac, axis=1)
        cum_t = jnp.moveaxis(cum, 1, 2)
        diff = cum_t[..., :, None] - cum_t[..., None, :]
        decay = jnp.exp(jnp.where(mask, diff, -jnp.inf))
        scores = jnp.einsum('bihd,bjhd->bhij', qc, kc) * decay
        intra = jnp.einsum('bhij,bjhe->bihe', scores, vc)
        cross = jnp.einsum('bihd,bhde->bihe', qc, s) * jnp.exp(cum)[..., None]
        tail = jnp.exp(cum[:, -1:, :] - cum)
        s_new = jnp.exp(cum[:, -1, :])[..., None, None] * s + jnp.einsum('bjhd,bjh,bjhe->bhde', kc, tail, vc)
        return s_new, intra + cross

    s_fin, out = lax.scan(step, s0.astype(jnp.float32), (qs, ks, vs, las))
    out = jnp.moveaxis(out, 0, 1).reshape(b, L, h, dv)
    return out.astype(v.dtype), s_fin.astype(s0.dtype)


def retention_layer(h, s0, pos, w_in, head_norm, w_out):
    b, L, _ = h.shape
    proj = h @ w_in
    qk = RET_HEADS * RET_DK
    q = proj[..., :qk].reshape(b, L, RET_HEADS, RET_DK)
    k = proj[..., qk:2 * qk].reshape(b, L, RET_HEADS, RET_DK)
    v = proj[..., 2 * qk:2 * qk + RET_VW].reshape(b, L, RET_HEADS, RET_DV)
    gate = proj[..., 2 * qk + RET_VW:]
    q = rope(q, pos)
    k = rope(k, pos) * (RET_DK ** -0.5)
    log_gamma = jnp.log(1.0 - 2.0 ** (-5.0 - jnp.arange(RET_HEADS, dtype=jnp.float32)))
    log_a = jnp.broadcast_to(log_gamma, (b, L, RET_HEADS))
    o, s_new = chunked_linear_recurrence(q, k, v, log_a, s0)
    o = (_rms(o).reshape(b, L, RET_VW) * head_norm.astype(jnp.float32)).astype(h.dtype)
    return (o * jax.nn.silu(gate)) @ w_out, s_new


def ssd_layer(h, s0, conv0, w_in, conv_w, conv_b, dt_bias, a_log, d_skip, gate_norm, w_out):
    b, L, _ = h.shape
    proj = h @ w_in
    z = proj[..., :SSM_DINNER]
    xbc = proj[..., SSM_DINNER:SSM_DINNER + SSM_CONVDIM]
    dt_raw = proj[..., SSM_DINNER + SSM_CONVDIM:]
    xpad = jnp.concatenate([conv0.astype(xbc.dtype), xbc], axis=1)
    conv = conv_b
    for w in range(SSM_CONV):
        conv = conv + xpad[:, w:w + L] * conv_w[w]
    xbc_c = jax.nn.silu(conv)
    new_conv = xpad[:, L:]
    gn = SSM_GROUPS * SSM_DSTATE
    xs = xbc_c[..., :SSM_DINNER].reshape(b, L, SSM_HEADS, SSM_HEADDIM)
    rep = SSM_HEADS // SSM_GROUPS
    bm = jnp.repeat(xbc_c[..., SSM_DINNER:SSM_DINNER + gn].reshape(b, L, SSM_GROUPS, SSM_DSTATE), rep, axis=2)
    cm = jnp.repeat(xbc_c[..., SSM_DINNER + gn:].reshape(b, L, SSM_GROUPS, SSM_DSTATE), rep, axis=2)
    dt = jax.nn.softplus(dt_raw.astype(jnp.float32) + dt_bias.astype(jnp.float32))
    a = -jnp.exp(a_log.astype(jnp.float32))
    log_a = dt * a
    v = xs * dt[..., None].astype(xs.dtype)
    y, s_new = chunked_linear_recurrence(cm, bm, v, log_a, s0)
    y = (y + xs * d_skip[:, None]).reshape(b, L, SSM_DINNER)
    g = y * jax.nn.silu(z)
    g = _rms(g.reshape(b, L, SSM_GROUPS, SSM_DINNER // SSM_GROUPS)).reshape(b, L, SSM_DINNER)
    g = (g * gate_norm.astype(jnp.float32)).astype(h.dtype)
    return g @ w_out, new_conv


def trunk(x, ret_states, ssm_states, conv_states, pos, ret_norm, ret_w_in, ret_head_norm, ret_w_out,
          ssm_norm, ssm_w_in, ssm_conv_w, ssm_conv_b, ssm_dt_bias, ssm_a_log, ssm_d, ssm_gate_norm,
          ssm_w_out, final_norm):
    h = x
    ret_new, ssm_new, conv_new = [], [], []
    for i in range(DEPTH):
        j = i // N_MIXERS
        if i % N_MIXERS == 0:
            out, s = retention_layer(rmsnorm(h, ret_norm[j]), ret_states[j], pos,
                                     ret_w_in[j], ret_head_norm[j], ret_w_out[j])
            ret_new.append(s)
        else:
            out, s, cv = ssd_layer_full(rmsnorm(h, ssm_norm[j]), ssm_states[j], conv_states[j],
                                        ssm_w_in[j], ssm_conv_w[j], ssm_conv_b[j], ssm_dt_bias[j],
                                        ssm_a_log[j], ssm_d[j], ssm_gate_norm[j], ssm_w_out[j])
            ssm_new.append(s)
            conv_new.append(cv)
        h = h + out
    return rmsnorm(h, final_norm), jnp.stack(ret_new), jnp.stack(ssm_new), jnp.stack(conv_new)


def ssd_layer_full(h, s0, conv0, w_in, conv_w, conv_b, dt_bias, a_log, d_skip, gate_norm, w_out):
    b, L, _ = h.shape
    proj = h @ w_in
    z = proj[..., :SSM_DINNER]
    xbc = proj[..., SSM_DINNER:SSM_DINNER + SSM_CONVDIM]
    dt_raw = proj[..., SSM_DINNER + SSM_CONVDIM:]
    xpad = jnp.concatenate([conv0.astype(xbc.dtype), xbc], axis=1)
    conv = conv_b
    for w in range(SSM_CONV):
        conv = conv + xpad[:, w:w + L] * conv_w[w]
    xbc_c = jax.nn.silu(conv)
    new_conv = xpad[:, L:]
    gn = SSM_GROUPS * SSM_DSTATE
    rep = SSM_HEADS // SSM_GROUPS
    xs = xbc_c[..., :SSM_DINNER].reshape(b, L, SSM_HEADS, SSM_HEADDIM)
    bm = jnp.repeat(xbc_c[..., SSM_DINNER:SSM_DINNER + gn].reshape(b, L, SSM_GROUPS, SSM_DSTATE), rep, axis=2)
    cm = jnp.repeat(xbc_c[..., SSM_DINNER + gn:].reshape(b, L, SSM_GROUPS, SSM_DSTATE), rep, axis=2)
    dt = jax.nn.softplus(dt_raw.astype(jnp.float32) + dt_bias.astype(jnp.float32))
    log_a = dt * (-jnp.exp(a_log.astype(jnp.float32)))
    v = xs * dt[..., None].astype(xs.dtype)
    y, s_new = chunked_linear_recurrence(cm, bm, v, log_a, s0)
    y = (y + xs * d_skip[:, None]).reshape(b, L, SSM_DINNER)
    g = y * jax.nn.silu(z)
    g = _rms(g.reshape(b, L, SSM_GROUPS, SSM_DINNER // SSM_GROUPS)).reshape(b, L, SSM_DINNER)
    g = (g * gate_norm.astype(jnp.float32)).astype(h.dtype)
    return g @ w_out, s_new, new_conv


def setup_inputs(seed: int = 0) -> dict:
    key = jax.random.key(seed)
    ks = jax.random.split(key, 24)
    nr, ns = N_RET_LAYERS, N_SSM_LAYERS
    f32 = jnp.float32
    dt0 = jnp.exp(jax.random.uniform(ks[17], (ns, SSM_HEADS), f32) * (math.log(0.1) - math.log(0.001)) + math.log(0.001))
    return {
        "x_prompt": jax.random.normal(ks[0], (BATCH, SEQ, D_MODEL), f32),
        "x_sample": jax.random.normal(ks[1], (DEC_BATCH, DEC_SEQ, D_MODEL), f32),
        "state_ret": 0.1 * jax.random.normal(ks[2], (nr, DEC_BATCH, RET_HEADS, RET_DK, RET_DV), f32),
        "state_ssm": 0.1 * jax.random.normal(ks[3], (ns, DEC_BATCH, SSM_HEADS, SSM_DSTATE, SSM_HEADDIM), f32),
        "state_conv": jax.random.normal(ks[4], (ns, DEC_BATCH, SSM_CONV - 1, SSM_CONVDIM), f32),
        "ret_norm": 1.0 + 0.02 * jax.random.normal(ks[5], (nr, D_MODEL), f32),
        "ret_w_in": jax.random.normal(ks[6], (nr, D_MODEL, RET_IN), f32) * D_MODEL ** -0.5,
        "ret_head_norm": 1.0 + 0.02 * jax.random.normal(ks[7], (nr, RET_VW), f32),
        "ret_w_out": jax.random.normal(ks[8], (nr, RET_VW, D_MODEL), f32) * RET_VW ** -0.5,
        "ssm_norm": 1.0 + 0.02 * jax.random.normal(ks[9], (ns, D_MODEL), f32),
        "ssm_w_in": jax.random.normal(ks[10], (ns, D_MODEL, SSM_IN), f32) * D_MODEL ** -0.5,
        "ssm_conv_w": jax.random.normal(ks[11], (ns, SSM_CONV, SSM_CONVDIM), f32) * SSM_CONV ** -0.5,
        "ssm_conv_b": 0.01 * jax.random.normal(ks[12], (ns, SSM_CONVDIM), f32),
        "ssm_dt_bias": dt0 + jnp.log(-jnp.expm1(-dt0)),
        "ssm_a_log": jnp.log(jax.random.uniform(ks[13], (ns, SSM_HEADS), f32, 1.0, 16.0)),
        "ssm_d": 1.0 + 0.02 * jax.random.normal(ks[14], (ns, SSM_HEADS), f32),
        "ssm_gate_norm": 1.0 + 0.02 * jax.random.normal(ks[15], (ns, SSM_DINNER), f32),
        "ssm_w_out": jax.random.normal(ks[16], (ns, SSM_DINNER, D_MODEL), f32) * SSM_DINNER ** -0.5,
        "final_norm": 1.0 + 0.02 * jax.random.normal(ks[18], (D_MODEL,), f32),
    }


def reference(x_prompt, x_sample, state_ret, state_ssm, state_conv, ret_norm, ret_w_in, ret_head_norm,
              ret_w_out, ssm_norm, ssm_w_in, ssm_conv_w, ssm_conv_b, ssm_dt_bias, ssm_a_log, ssm_d,
              ssm_gate_norm, ssm_w_out, final_norm):
    weights = (ret_norm, ret_w_in, ret_head_norm, ret_w_out, ssm_norm, ssm_w_in, ssm_conv_w, ssm_conv_b,
               ssm_dt_bias, ssm_a_log, ssm_d, ssm_gate_norm, ssm_w_out, final_norm)
    bp = x_prompt.shape[0]
    dt = x_prompt.dtype
    ret0 = jnp.zeros((N_RET_LAYERS, bp, RET_HEADS, RET_DK, RET_DV), dt)
    ssm0 = jnp.zeros((N_SSM_LAYERS, bp, SSM_HEADS, SSM_DSTATE, SSM_HEADDIM), dt)
    conv0 = jnp.zeros((N_SSM_LAYERS, bp, SSM_CONV - 1, SSM_CONVDIM), dt)
    pos_p = jnp.arange(x_prompt.shape[1], dtype=jnp.int32)
    pos_s = PAST_LEN + jnp.arange(x_sample.shape[1], dtype=jnp.int32)
    y_prompt, ret_p, ssm_p, conv_p = trunk(x_prompt, ret0, ssm0, conv0, pos_p, *weights)
    y_sample, ret_s, ssm_s, conv_s = trunk(x_sample, state_ret, state_ssm, state_conv, pos_s, *weights)
    return (y_prompt, y_sample, ret_p, ret_s, ssm_p, ssm_s, conv_p, conv_s)
```

```python
import functools
import math

import jax
import jax.numpy as jnp
from jax import lax
from jax.experimental import pallas as pl
from jax.experimental.pallas import tpu as pltpu

F32 = jnp.float32
BF16 = jnp.bfloat16

D_MODEL = 1024
RET_HEADS = 4
RET_DK = 256
RET_DV = 512
RET_VW = RET_HEADS * RET_DV
RET_QK = RET_HEADS * RET_DK
SSM_DINNER = 2048
SSM_HEADDIM = 64
SSM_HEADS = 32
SSM_GROUPS = 8
SSM_DSTATE = 128
SSM_CONV = 4
SSM_CONVDIM = 4096
SSM_GROUP_W = SSM_DINNER // SSM_GROUPS
HEADS_PER_GROUP = SSM_HEADS // SSM_GROUPS
ROPE_BASE = 10000.0
EPS = 1e-6
PAST_LEN = 16384

LANES = 128
SUBLANES = 8
VMEM_LIMIT = 56 * 1024 * 1024

RET_CHUNK = 256
SSD_CHUNK = 128
RET_SAMPLE_BB = 8
SSD_SAMPLE_BB = 2


def _cparams(sem):
    return pltpu.CompilerParams(dimension_semantics=sem, vmem_limit_bytes=VMEM_LIMIT)


def _sigmoid(x):
    return 1.0 / (1.0 + jnp.exp(-x))


def _silu(x):
    return x * _sigmoid(x)


def _softplus(x):
    return jnp.maximum(x, 0.0) + jnp.log1p(jnp.exp(-jnp.abs(x)))


def _dot(a, b):
    return jnp.dot(a, b, preferred_element_type=F32)


def _dot_nt(a, b):
    return lax.dot_general(a, b, (((1,), (1,)), ((), ())), preferred_element_type=F32)


def _dot_tn(a, b):
    return lax.dot_general(a, b, (((0,), (0,)), ((), ())), preferred_element_type=F32)


def _iota(shape, dim):
    return lax.broadcasted_iota(jnp.int32, shape, dim)


def _norm_proj_kernel(x_ref, g_ref, w_ref, *rest, has_dt):
    if has_dt:
        wdt_ref, o_ref, dt_ref, xn_ref = rest
    else:
        o_ref, xn_ref = rest

    @pl.when(pl.program_id(1) == 0)
    def _():
        x = x_ref[...]
        ms = jnp.mean(x * x, axis=-1, keepdims=True)
        xn = (x * lax.rsqrt(ms + EPS) * g_ref[...]).astype(BF16)
        xn_ref[...] = xn
        if has_dt:
            dt_ref[...] = _dot(xn, wdt_ref[...])

    o_ref[...] = _dot(xn_ref[...], w_ref[...]).astype(o_ref.dtype)


def _norm_proj(x, g, w, *, tm, tn, out_dtype, w_dt=None):
    m, d = x.shape
    n = w.shape[1]
    has_dt = w_dt is not None
    in_specs = [
        pl.BlockSpec((tm, d), lambda i, j: (i, 0)),
        pl.BlockSpec((1, d), lambda i, j: (0, 0)),
        pl.BlockSpec((d, tn), lambda i, j: (0, j)),
    ]
    args = [x, g, w]
    out_shape = [jax.ShapeDtypeStruct((m, n), out_dtype)]
    out_specs = [pl.BlockSpec((tm, tn), lambda i, j: (i, j))]
    if has_dt:
        in_specs.append(pl.BlockSpec((d, LANES), lambda i, j: (0, 0)))
        args.append(w_dt)
        out_shape.append(jax.ShapeDtypeStruct((m, LANES), F32))
        out_specs.append(pl.BlockSpec((tm, LANES), lambda i, j: (i, 0)))
    res = pl.pallas_call(
        functools.partial(_norm_proj_kernel, has_dt=has_dt),
        grid=(m // tm, n // tn),
        in_specs=in_specs,
        out_specs=out_specs,
        out_shape=out_shape,
        scratch_shapes=[pltpu.VMEM((tm, d), BF16)],
        compiler_params=_cparams(("arbitrary", "arbitrary")),
    )(*args)
    return res if has_dt else res[0]


def _out_proj_kernel(a_ref, w_ref, h_ref, *rest, has_norm):
    if has_norm:
        g_ref, o_ref = rest
    else:
        (o_ref,) = rest
    y = h_ref[...] + _dot(a_ref[...], w_ref[...])
    if has_norm:
        ms = jnp.mean(y * y, axis=-1, keepdims=True)
        y = y * lax.rsqrt(ms + EPS) * g_ref[...]
    o_ref[...] = y


def _out_proj(a, w, h, *, tm, final_g=None):
    m, k = a.shape
    d = w.shape[1]
    has_norm = final_g is not None
    in_specs = [
        pl.BlockSpec((tm, k), lambda i: (i, 0)),
        pl.BlockSpec((k, d), lambda i: (0, 0)),
        pl.BlockSpec((tm, d), lambda i: (i, 0)),
    ]
    args = [a, w, h]
    if has_norm:
        in_specs.append(pl.BlockSpec((1, d), lambda i: (0, 0)))
        args.append(final_g)
    return pl.pallas_call(
        functools.partial(_out_proj_kernel, has_norm=has_norm),
        grid=(m // tm,),
        in_specs=in_specs,
        out_specs=pl.BlockSpec((tm, d), lambda i: (i, 0)),
        out_shape=jax.ShapeDtypeStruct((m, d), F32),
        compiler_params=_cparams(("arbitrary",)),
    )(*args)


def _rope(x, cos, sin):
    half = RET_DK // 2
    x1 = x[:, :half]
    x2 = x[:, half:]
    return jnp.concatenate([x1 * cos - x2 * sin, x1 * sin + x2 * cos], axis=1)


def _head_norm_gate(o, hn, gate):
    ms = jnp.mean(o * o, axis=-1, keepdims=True)
    on = o * lax.rsqrt(ms + EPS) * hn
    return (on * _silu(gate)).astype(BF16)


def _ret_prompt_kernel(q_ref, k_ref, v_ref, gate_ref, cos_ref, sin_ref, lg_ref, hn_ref,
                       og_ref, st_ref, s_ref, dec_ref, qsc_ref, ksc_ref, gc_ref):
    c = pl.program_id(2)
    rows = q_ref.shape[0]

    @pl.when(c == 0)
    def _():
        lg = lg_ref[:, 0:1]
        i = _iota((rows, rows), 0)
        j = _iota((rows, rows), 1)
        scale = RET_DK ** -0.5
        dec_ref[...] = jnp.where(i >= j, jnp.exp((i - j).astype(F32) * lg), 0.0) * scale
        r = _iota((rows, RET_DK), 0)
        qsc_ref[...] = jnp.exp((r + 1).astype(F32) * lg)
        ksc_ref[...] = jnp.exp((rows - 1 - r).astype(F32) * lg) * scale
        gc_ref[...] = jnp.exp(jnp.full(gc_ref.shape, float(rows), F32) * lg)
        s_ref[...] = jnp.zeros_like(s_ref)

    cos = cos_ref[...]
    sin = sin_ref[...]
    qr = _rope(q_ref[...].astype(F32), cos, sin)
    kr = _rope(k_ref[...].astype(F32), cos, sin)
    v = v_ref[...]
    p = (_dot_nt(qr.astype(BF16), kr.astype(BF16)) * dec_ref[...]).astype(BF16)
    qs = (qr * qsc_ref[...]).astype(BF16)
    s = s_ref[...]
    o = _dot(p, v) + _dot(qs, s.astype(BF16))
    kt = (kr * ksc_ref[...]).T.astype(BF16)
    s_new = s * gc_ref[...] + _dot(kt, v)
    s_ref[...] = s_new
    og_ref[...] = _head_norm_gate(o, hn_ref[...], gate_ref[...].astype(F32))

    @pl.when(c == pl.num_programs(2) - 1)
    def _():
        st_ref[...] = s_new


def _ret_prompt(proj, cos, sin, lg, hn, batch, seq):
    rows = RET_CHUNK
    nc = seq // rows
    qb = RET_QK // RET_DK
    vb = 2 * RET_QK // RET_DV
    gb = vb + RET_VW // RET_DV
    return pl.pallas_call(
        _ret_prompt_kernel,
        grid=(batch, RET_HEADS, nc),
        in_specs=[
            pl.BlockSpec((rows, RET_DK), lambda b, h, c: (b * nc + c, h)),
            pl.BlockSpec((rows, RET_DK), lambda b, h, c: (b * nc + c, qb + h)),
            pl.BlockSpec((rows, RET_DV), lambda b, h, c: (b * nc + c, vb + h)),
            pl.BlockSpec((rows, RET_DV), lambda b, h, c: (b * nc + c, gb + h)),
            pl.BlockSpec((rows, RET_DK // 2), lambda b, h, c: (c, 0)),
            pl.BlockSpec((rows, RET_DK // 2), lambda b, h, c: (c, 0)),
            pl.BlockSpec((None, 1, LANES), lambda b, h, c: (h, 0, 0)),
            pl.BlockSpec((1, RET_DV), lambda b, h, c: (0, h)),
        ],
        out_specs=[
            pl.BlockSpec((rows, RET_DV), lambda b, h, c: (b * nc + c, h)),
            pl.BlockSpec((None, None, None, RET_DK, RET_DV), lambda b, h, c: (0, b, h, 0, 0)),
        ],
        out_shape=[
            jax.ShapeDtypeStruct((batch * seq, RET_VW), BF16),
            jax.ShapeDtypeStruct((1, batch, RET_HEADS, RET_DK, RET_DV), F32),
        ],
        scratch_shapes=[
            pltpu.VMEM((RET_DK, RET_DV), F32),
            pltpu.VMEM((rows, rows), F32),
            pltpu.VMEM((rows, RET_DK), F32),
            pltpu.VMEM((rows, RET_DK), F32),
            pltpu.VMEM((1, RET_DV), F32),
        ],
        compiler_params=_cparams(("arbitrary", "arbitrary", "arbitrary")),
    )(proj, proj, proj, proj, cos, sin, lg, hn)


def _ret_sample_kernel(q_ref, k_ref, v_ref, gate_ref, cos_ref, sin_ref, lg_ref, hn_ref, st_in_ref,
                       og_ref, st_out_ref, *, seq):
    bb = st_in_ref.shape[0]
    rows = bb * seq
    lg = lg_ref[:, 0:1]
    scale = RET_DK ** -0.5
    i = _iota((rows, rows), 0)
    j = _iota((rows, rows), 1)
    same = (i // seq) == (j // seq)
    dec = jnp.where(same & (i >= j), jnp.exp((i - j).astype(F32) * lg), 0.0) * scale
    pos = _iota((rows, RET_DK), 0) % seq
    qsc = jnp.exp((pos + 1).astype(F32) * lg)
    ksc = jnp.exp((seq - 1 - pos).astype(F32) * lg) * scale
    gc = jnp.exp(jnp.full((1, RET_DV), float(seq), F32) * lg)

    cos = jnp.concatenate([cos_ref[...]] * bb, axis=0)
    sin = jnp.concatenate([sin_ref[...]] * bb, axis=0)
    qr = _rope(q_ref[...].astype(F32), cos, sin)
    kr = _rope(k_ref[...].astype(F32), cos, sin)
    v = v_ref[...].astype(F32)
    p = _dot_nt(qr.astype(BF16), kr.astype(BF16)) * dec
    intra = _dot(p.astype(BF16), v.astype(BF16))
    qs = qr * qsc
    kt = kr * ksc
    vrow = _iota((rows, RET_DV), 0) // seq
    cross = []
    for n in range(bb):
        s0 = st_in_ref[n]
        cross.append(_dot(qs[n * seq:(n + 1) * seq], s0))
        vn = jnp.where(vrow == n, v, 0.0)
        st_out_ref[n] = s0 * gc + _dot_tn(kt.astype(BF16), vn.astype(BF16))
    o = intra + jnp.concatenate(cross, axis=0)
    og_ref[...] = _head_norm_gate(o, hn_ref[...], gate_ref[...].astype(F32))


def _ret_sample(proj, cos, sin, lg, hn, state, batch, seq):
    bb = RET_SAMPLE_BB
    rows = bb * seq
    qb = RET_QK // RET_DK
    vb = 2 * RET_QK // RET_DV
    gb = vb + RET_VW // RET_DV
    st_spec = pl.BlockSpec((None, bb, None, RET_DK, RET_DV), lambda n, h: (0, n, h, 0, 0))
    return pl.pallas_call(
        functools.partial(_ret_sample_kernel, seq=seq),
        grid=(batch // bb, RET_HEADS),
        in_specs=[
            pl.BlockSpec((rows, RET_DK), lambda n, h: (n, h)),
            pl.BlockSpec((rows, RET_DK), lambda n, h: (n, qb + h)),
            pl.BlockSpec((rows, RET_DV), lambda n, h: (n, vb + h)),
            pl.BlockSpec((rows, RET_DV), lambda n, h: (n, gb + h)),
            pl.BlockSpec((seq, RET_DK // 2), lambda n, h: (0, 0)),
            pl.BlockSpec((seq, RET_DK // 2), lambda n, h: (0, 0)),
            pl.BlockSpec((None, 1, LANES), lambda n, h: (h, 0, 0)),
            pl.BlockSpec((1, RET_DV), lambda n, h: (0, h)),
            st_spec,
        ],
        out_specs=[
            pl.BlockSpec((rows, RET_DV), lambda n, h: (n, h)),
            st_spec,
        ],
        out_shape=[
            jax.ShapeDtypeStruct((batch * seq, RET_VW), BF16),
            jax.ShapeDtypeStruct(state.shape, F32),
        ],
        compiler_params=_cparams(("arbitrary", "arbitrary")),
    )(proj, proj, proj, proj, cos, sin, lg, hn, state)


def _lane_head_select(vals):
    rows = vals[0].shape[0]
    low = _iota((rows, LANES), 1) < SSM_HEADDIM
    return jnp.concatenate([jnp.where(low, vals[0], vals[1]), jnp.where(low, vals[2], vals[3])], axis=1)


def _ssd_prompt_kernel(z_ref, xa_ref, xb_ref, dtr_ref, cw_ref, cb_ref, dtb_ref, alog_ref, dsk_ref, gn_ref,
                       g_ref, st_ref, xpad_ref, s_ref, xs_ref, bt_ref, cm_ref):
    c = pl.program_id(1)
    rows = z_ref.shape[0]
    pad = SUBLANES

    @pl.when(c == 0)
    def _():
        s_ref[...] = jnp.zeros_like(s_ref)
        xpad_ref[0:pad, :] = jnp.zeros((pad, SSM_CONVDIM), F32)

    xpad_ref[pad:pad + rows, 0:SSM_DINNER] = xa_ref[...].astype(F32)
    xpad_ref[pad:pad + rows, SSM_DINNER:SSM_CONVDIM] = xb_ref[...].astype(F32)

    bw = SSM_GROUP_W
    for cb in range(SSM_CONVDIM // bw):
        sl = slice(cb * bw, (cb + 1) * bw)
        acc = cb_ref[:, sl]
        for w in range(SSM_CONV):
            lo = pad - (SSM_CONV - 1) + w
            acc = acc + xpad_ref[lo:lo + rows, sl] * cw_ref[w:w + 1, sl]
        xc = _silu(acc)
        if cb < SSM_GROUPS:
            xs_ref[cb] = xc
        elif cb < SSM_GROUPS + SSM_GROUPS // 2:
            g0 = 2 * (cb - SSM_GROUPS)
            bt_ref[g0] = xc[:, :SSM_DSTATE].T.astype(BF16)
            bt_ref[g0 + 1] = xc[:, SSM_DSTATE:].T.astype(BF16)
        else:
            g0 = 2 * (cb - SSM_GROUPS - SSM_GROUPS // 2)
            cm_ref[g0] = xc[:, :SSM_DSTATE]
            cm_ref[g0 + 1] = xc[:, SSM_DSTATE:]
    xpad_ref[0:pad, :] = xpad_ref[rows:rows + pad, :]

    dt = _softplus(dtr_ref[...] + dtb_ref[...])
    la = dt * (-jnp.exp(alog_ref[...]))
    ri = _iota((rows, rows), 0)
    rj = _iota((rows, rows), 1)
    causal = ri >= rj
    cum = jnp.dot(causal.astype(F32), la, precision=lax.Precision.HIGHEST, preferred_element_type=F32)
    cum_t = cum.T
    dt_t = dt.T
    head_lane = _iota((rows, SSM_GROUP_W), 1) // SSM_HEADDIM

    for g in range(SSM_GROUPS):
        gsl = slice(g * bw, (g + 1) * bw)
        cm = cm_ref[g]
        btg = bt_ref[g]
        xs = xs_ref[g]
        s = s_ref[g]
        gmat = _dot(cm.astype(BF16), btg)
        lhs, wcols, sdec = [], [], []
        for hh in range(HEADS_PER_GROUP):
            h = g * HEADS_PER_GROUP + hh
            colb = jnp.broadcast_to(cum[:, h:h + 1], (rows, LANES))
            rowb = cum_t[h:h + 1, :]
            dec = jnp.exp(jnp.where(causal, colb - rowb, -jnp.inf))
            p = (gmat * dec * dt_t[h:h + 1, :]).astype(BF16)
            q = (cm * jnp.exp(colb)).astype(BF16)
            lhs.append(jnp.concatenate([p, q], axis=1))
            clast = cum[rows - 1:rows, h:h + 1]
            dtcol = jnp.broadcast_to(dt[:, h:h + 1], (rows, LANES))
            wcols.append(jnp.exp(clast - colb) * dtcol)
            sdec.append(jnp.broadcast_to(jnp.exp(clast), (1, LANES)))
        rhs = jnp.concatenate([xs.astype(BF16), s.astype(BF16)], axis=0)
        y4 = _dot(jnp.concatenate(lhs, axis=0), rhs)
        y = y4[0:rows]
        for hh in range(1, HEADS_PER_GROUP):
            y = jnp.where(head_lane == hh, y4[hh * rows:(hh + 1) * rows], y)
        vw = (xs * _lane_head_select(wcols)).astype(BF16)
        s_ref[g] = s * _lane_head_select(sdec) + _dot(btg, vw)

        y = y + xs * dsk_ref[:, gsl]
        gg = y * _silu(z_ref[:, gsl].astype(F32))
        ms = jnp.mean(gg * gg, axis=-1, keepdims=True)
        g_ref[:, gsl] = (gg * lax.rsqrt(ms + EPS) * gn_ref[:, gsl]).astype(BF16)

    @pl.when(c == pl.num_programs(1) - 1)
    def _():
        for g in range(SSM_GROUPS):
            sg = s_ref[g]
            for hh in range(HEADS_PER_GROUP):
                st_ref[g * HEADS_PER_GROUP + hh] = sg[:, hh * SSM_HEADDIM:(hh + 1) * SSM_HEADDIM]


def _ssd_prompt(zx, dtr, cw, cb, dtb, alog, dsk, gn, batch, seq):
    rows = SSD_CHUNK
    nc = seq // rows
    full = lambda shape: pl.BlockSpec(shape, lambda b, c: (0,) * len(shape))
    col = lambda k: pl.BlockSpec((rows, SSM_DINNER), lambda b, c: (b * nc + c, k))
    return pl.pallas_call(
        _ssd_prompt_kernel,
        grid=(batch, nc),
        in_specs=[
            col(0), col(1), col(2),
            pl.BlockSpec((rows, LANES), lambda b, c: (b * nc + c, 0)),
            full((SSM_CONV, SSM_CONVDIM)), full((1, SSM_CONVDIM)),
            full((1, LANES)), full((1, LANES)),
            full((1, SSM_DINNER)), full((1, SSM_DINNER)),
        ],
        out_specs=[
            pl.BlockSpec((rows, SSM_DINNER), lambda b, c: (b * nc + c, 0)),
            pl.BlockSpec((None, None, SSM_HEADS, SSM_DSTATE, SSM_HEADDIM), lambda b, c: (0, b, 0, 0, 0)),
        ],
        out_shape=[
            jax.ShapeDtypeStruct((batch * seq, SSM_DINNER), BF16),
            jax.ShapeDtypeStruct((1, batch, SSM_HEADS, SSM_DSTATE, SSM_HEADDIM), F32),
        ],
        scratch_shapes=[
            pltpu.VMEM((rows + SUBLANES, SSM_CONVDIM), F32),
            pltpu.VMEM((SSM_GROUPS, SSM_DSTATE, SSM_GROUP_W), F32),
            pltpu.VMEM((SSM_GROUPS, rows, SSM_GROUP_W), F32),
            pltpu.VMEM((SSM_GROUPS, SSM_DSTATE, rows), BF16),
            pltpu.VMEM((SSM_GROUPS, rows, SSM_DSTATE), F32),
        ],
        compiler_params=_cparams(("arbitrary", "arbitrary")),
    )(zx, zx, zx, dtr, cw, cb, dtb, alog, dsk, gn)


def _ssd_sample_kernel(z_ref, xa_ref, xb_ref, dtr_ref, conv_in_ref, st_in_ref,
                       cw_ref, cb_ref, dtb_ref, alog_ref, dsk_ref, gn_ref,
                       g_ref, st_out_ref, conv_out_ref, xpad_ref, *, seq):
    bb = st_in_ref.shape[0]
    pad = SUBLANES
    keep = SSM_CONV - 1
    ri = _iota((seq, seq), 0)
    rj = _iota((seq, seq), 1)
    causal = ri >= rj
    eye = ri == rj
    a_neg = -jnp.exp(alog_ref[...])
    bw = SSM_GROUP_W

    for n in range(bb):
        r0 = n * seq
        xpad_ref[pad - keep:pad, :] = conv_in_ref[n]
        xpad_ref[pad:pad + seq, 0:SSM_DINNER] = xa_ref[r0:r0 + seq, :]
        xpad_ref[pad:pad + seq, SSM_DINNER:SSM_CONVDIM] = xb_ref[r0:r0 + seq, :]
        conv_out_ref[n] = xpad_ref[pad + seq - keep:pad + seq, :]
        acc = cb_ref[...]
        for w in range(SSM_CONV):
            lo = pad - keep + w
            acc = acc + xpad_ref[lo:lo + seq, :] * cw_ref[w:w + 1, :]
        xc = _silu(acc)

        dt = _softplus(dtr_ref[r0:r0 + seq, :] + dtb_ref[...])
        la = dt * a_neg
        cum = jnp.dot(causal.astype(F32), la, precision=lax.Precision.HIGHEST, preferred_element_type=F32)
        ys = []
        for g in range(SSM_GROUPS):
            bm = xc[:, SSM_DINNER + g * SSM_DSTATE:SSM_DINNER + (g + 1) * SSM_DSTATE]
            cm = xc[:, SSM_DINNER + (SSM_GROUPS + g) * SSM_DSTATE:SSM_DINNER + (SSM_GROUPS + g + 1) * SSM_DSTATE]
            gmat = _dot_nt(cm.astype(BF16), bm.astype(BF16))
            for hh in range(HEADS_PER_GROUP):
                h = g * HEADS_PER_GROUP + hh
                xs_h = xc[:, h * SSM_HEADDIM:(h + 1) * SSM_HEADDIM]
                ccol = cum[:, h:h + 1]
                dcol = dt[:, h:h + 1]
                crow = jnp.sum(jnp.where(eye, ccol, 0.0), axis=0, keepdims=True)
                drow = jnp.sum(jnp.where(eye, dcol, 0.0), axis=0, keepdims=True)
                dec = jnp.exp(jnp.where(causal, ccol - crow, -jnp.inf))
                p = gmat * dec * drow
                s0 = st_in_ref[n, h]
                y_h = _dot(p.astype(BF16), xs_h.astype(BF16)) + _dot((cm * jnp.exp(ccol)).astype(BF16),
                                                                      s0.astype(BF16))
                clast = cum[seq - 1:seq, h:h + 1]
                bw_h = bm * (jnp.exp(clast - ccol) * dcol)
                st_out_ref[n, h] = s0 * jnp.exp(clast) + _dot_tn(bw_h.astype(BF16), xs_h.astype(BF16))
                ys.append(y_h)
        y = jnp.concatenate(ys, axis=1)
        y = y + xc[:, :SSM_DINNER] * dsk_ref[...]
        gg = y * _silu(z_ref[r0:r0 + seq, :])
        outs = []
        for g in range(SSM_GROUPS):
            gsl = slice(g * bw, (g + 1) * bw)
            blk = gg[:, gsl]
            ms = jnp.mean(blk * blk, axis=-1, keepdims=True)
            outs.append(blk * lax.rsqrt(ms + EPS) * gn_ref[:, gsl])
        g_ref[r0:r0 + seq, :] = jnp.concatenate(outs, axis=1).astype(BF16)


def _ssd_sample(zx, dtr, conv_state, ssm_state, cw, cb, dtb, alog, dsk, gn, batch, seq):
    bb = SSD_SAMPLE_BB
    rows = bb * seq
    keep = SSM_CONV - 1
    full = lambda shape: pl.BlockSpec(shape, lambda n: (0,) * len(shape))
    col = lambda k: pl.BlockSpec((rows, SSM_DINNER), lambda n: (n, k))
    st_spec = pl.BlockSpec((None, bb, SSM_HEADS, SSM_DSTATE, SSM_HEADDIM), lambda n: (0, n, 0, 0, 0))
    conv_spec = pl.BlockSpec((None, bb, keep, SSM_CONVDIM), lambda n: (0, n, 0, 0))
    return pl.pallas_call(
        functools.partial(_ssd_sample_kernel, seq=seq),
        grid=(batch // bb,),
        in_specs=[
            col(0), col(1), col(2),
            pl.BlockSpec((rows, LANES), lambda n: (n, 0)),
            conv_spec, st_spec,
            full((SSM_CONV, SSM_CONVDIM)), full((1, SSM_CONVDIM)),
            full((1, LANES)), full((1, LANES)),
            full((1, SSM_DINNER)), full((1, SSM_DINNER)),
        ],
        out_specs=[
            pl.BlockSpec((rows, SSM_DINNER), lambda n: (n, 0)),
            st_spec,
            conv_spec,
        ],
        out_shape=[
            jax.ShapeDtypeStruct((batch * seq, SSM_DINNER), BF16),
            jax.ShapeDtypeStruct(ssm_state.shape, F32),
            jax.ShapeDtypeStruct(conv_state.shape, F32),
        ],
        scratch_shapes=[pltpu.VMEM((2 * SUBLANES, SSM_CONVDIM), F32)],
        compiler_params=_cparams(("arbitrary",)),
    )(zx, zx, zx, dtr, conv_state, ssm_state, cw, cb, dtb, alog, dsk, gn)


def _rope_tables(pos):
    half = RET_DK // 2
    freqs = ROPE_BASE ** (-jnp.arange(half, dtype=F32) / half)
    ang = pos.astype(F32)[:, None] * freqs[None, :]
    return jnp.cos(ang), jnp.sin(ang)


def _row(v):
    return v.reshape(1, -1).astype(F32)


def _pad_lanes(v):
    return jnp.pad(v.astype(F32), (0, LANES - v.shape[0])).reshape(1, LANES)


def kernel(x_prompt, x_sample, state_ret, state_ssm, state_conv, ret_norm, ret_w_in, ret_head_norm, ret_w_out, ssm_norm, ssm_w_in, ssm_conv_w, ssm_conv_b, ssm_dt_bias, ssm_a_log, ssm_d, ssm_gate_norm, ssm_w_out, final_norm):
    bp, lp, d = x_prompt.shape
    bs, ls, _ = x_sample.shape
    assert ret_norm.shape[0] == 1 and ssm_norm.shape[0] == 1, "one retention and one SSD layer"

    ret_win = ret_w_in[0].astype(BF16)
    ret_wout = ret_w_out[0].astype(BF16)
    n_main = SSM_DINNER + SSM_CONVDIM
    ssm_win = ssm_w_in[0, :, :n_main].astype(BF16)
    ssm_wdt = jnp.pad(ssm_w_in[0, :, n_main:], ((0, 0), (0, LANES - SSM_HEADS))).astype(BF16)
    ssm_wxbc = ssm_w_in[0, :, SSM_DINNER:n_main].astype(BF16)
    ssm_wout = ssm_w_out[0].astype(BF16)
    lg = jnp.log(1.0 - 2.0 ** (-5.0 - jnp.arange(RET_HEADS, dtype=F32)))
    lg = jnp.broadcast_to(lg[:, None, None], (RET_HEADS, 1, LANES))
    hn = _row(ret_head_norm[0])
    dsk = _row(jnp.repeat(ssm_d[0], SSM_HEADDIM))
    gn = _row(ssm_gate_norm[0])
    cw = ssm_conv_w[0].astype(F32)
    cb = _row(ssm_conv_b[0])
    dtb = _pad_lanes(ssm_dt_bias[0])
    alog = _pad_lanes(ssm_a_log[0])
    g_ret = _row(ret_norm[0])
    g_ssm = _row(ssm_norm[0])
    g_fin = _row(final_norm)

    xp = x_prompt.reshape(bp * lp, d)
    cos_p, sin_p = _rope_tables(jnp.arange(lp, dtype=jnp.int32))
    proj = _norm_proj(xp, g_ret, ret_win, tm=1024, tn=2048, out_dtype=BF16)
    og, ret_p = _ret_prompt(proj, cos_p, sin_p, lg, hn, bp, lp)
    h1 = _out_proj(og, ret_wout, xp, tm=512)
    zx, dtr = _norm_proj(h1, g_ssm, ssm_win, tm=1024, tn=2048, out_dtype=BF16, w_dt=ssm_wdt)
    gated, ssm_p = _ssd_prompt(zx, dtr, cw, cb, dtb, alog, dsk, gn, bp, lp)
    y_p = _out_proj(gated, ssm_wout, h1, tm=512, final_g=g_fin)
    tail_rows = h1.reshape(bp, lp, d)[:, lp - SUBLANES:, :].reshape(bp * SUBLANES, d)
    tail = _norm_proj(tail_rows, g_ssm, ssm_wxbc, tm=bp * SUBLANES, tn=2048, out_dtype=F32)
    conv_p = tail.reshape(bp, SUBLANES, SSM_CONVDIM)[None, :, SUBLANES - (SSM_CONV - 1):, :]

    xs = x_sample.reshape(bs * ls, d)
    cos_s, sin_s = _rope_tables(PAST_LEN + jnp.arange(ls, dtype=jnp.int32))
    proj_s = _norm_proj(xs, g_ret, ret_win, tm=bs * ls, tn=2048, out_dtype=BF16)
    og_s, ret_s = _ret_sample(proj_s, cos_s, sin_s, lg, hn, state_ret, bs, ls)
    h1_s = _out_proj(og_s, ret_wout, xs, tm=512)
    zx_s, dtr_s = _norm_proj(h1_s, g_ssm, ssm_win, tm=bs * ls, tn=2048, out_dtype=F32, w_dt=ssm_wdt)
    gated_s, ssm_s, conv_s = _ssd_sample(zx_s, dtr_s, state_conv, state_ssm, cw, cb, dtb, alog, dsk, gn, bs, ls)
    y_s = _out_proj(gated_s, ssm_wout, h1_s, tm=512, final_g=g_fin)

    return (y_p.reshape(bp, lp, d), y_s.reshape(bs, ls, d), ret_p, ret_s, ssm_p, ssm_s, conv_p, conv_s)
```

```python
import functools
import math

import jax
import jax.numpy as jnp
from jax import lax
from jax.experimental import pallas as pl
from jax.experimental.pallas import tpu as pltpu

F32 = jnp.float32
BF16 = jnp.bfloat16

D_MODEL = 1024
RET_HEADS = 4
RET_DK = 256
RET_DV = 512
RET_VW = RET_HEADS * RET_DV
RET_QK = RET_HEADS * RET_DK
SSM_DINNER = 2048
SSM_HEADDIM = 64
SSM_HEADS = 32
SSM_GROUPS = 8
SSM_DSTATE = 128
SSM_CONV = 4
SSM_CONVDIM = 4096
SSM_GROUP_W = SSM_DINNER // SSM_GROUPS
HEADS_PER_GROUP = SSM_HEADS // SSM_GROUPS
ROPE_BASE = 10000.0
EPS = 1e-6
PAST_LEN = 16384

LANES = 128
SUBLANES = 8
VMEM_LIMIT = 56 * 1024 * 1024

RET_CHUNK = 256
SSD_CHUNK = 128
RET_SAMPLE_BB = 8
SSD_SAMPLE_BB = 4


def _cparams(sem):
    return pltpu.CompilerParams(dimension_semantics=sem, vmem_limit_bytes=VMEM_LIMIT)


def _sigmoid(x):
    return 1.0 / (1.0 + jnp.exp(-x))


def _silu(x):
    return x * _sigmoid(x)


def _softplus(x):
    return jnp.maximum(x, 0.0) + jnp.log1p(jnp.exp(-jnp.abs(x)))


def _dot(a, b):
    return jnp.dot(a, b, preferred_element_type=F32)


def _dot_nt(a, b):
    return lax.dot_general(a, b, (((1,), (1,)), ((), ())), preferred_element_type=F32)


def _dot_tn(a, b):
    return lax.dot_general(a, b, (((0,), (0,)), ((), ())), preferred_element_type=F32)


def _iota(shape, dim):
    return lax.broadcasted_iota(jnp.int32, shape, dim)


def _norm_proj_kernel(x_ref, g_ref, w_ref, *rest, has_dt):
    if has_dt:
        wdt_ref, o_ref, dt_ref, xn_ref = rest
    else:
        o_ref, xn_ref = rest

    @pl.when(pl.program_id(1) == 0)
    def _():
        x = x_ref[...]
        ms = jnp.mean(x * x, axis=-1, keepdims=True)
        xn = (x * lax.rsqrt(ms + EPS) * g_ref[...]).astype(BF16)
        xn_ref[...] = xn
        if has_dt:
            dt_ref[...] = _dot(xn, wdt_ref[...])

    o_ref[...] = _dot(xn_ref[...], w_ref[...]).astype(o_ref.dtype)


def _norm_proj(x, g, w, *, tm, tn, out_dtype, w_dt=None):
    m, d = x.shape
    n = w.shape[1]
    has_dt = w_dt is not None
    in_specs = [
        pl.BlockSpec((tm, d), lambda i, j: (i, 0)),
        pl.BlockSpec((1, d), lambda i, j: (0, 0)),
        pl.BlockSpec((d, tn), lambda i, j: (0, j)),
    ]
    args = [x, g, w]
    out_shape = [jax.ShapeDtypeStruct((m, n), out_dtype)]
    out_specs = [pl.BlockSpec((tm, tn), lambda i, j: (i, j))]
    if has_dt:
        in_specs.append(pl.BlockSpec((d, LANES), lambda i, j: (0, 0)))
        args.append(w_dt)
        out_shape.append(jax.ShapeDtypeStruct((m, LANES), F32))
        out_specs.append(pl.BlockSpec((tm, LANES), lambda i, j: (i, 0)))
    res = pl.pallas_call(
        functools.partial(_norm_proj_kernel, has_dt=has_dt),
        grid=(m // tm, n // tn),
        in_specs=in_specs,
        out_specs=out_specs,
        out_shape=out_shape,
        scratch_shapes=[pltpu.VMEM((tm, d), BF16)],
        compiler_params=_cparams(("arbitrary", "arbitrary")),
        name="in_proj",
    )(*args)
    return res if has_dt else res[0]


def _shift_rows(a, first_row):
    rolled = pltpu.roll(a, 1, axis=0)
    top = jnp.where(_iota((SUBLANES, a.shape[1]), 0) == 0, first_row, rolled[0:SUBLANES])
    return jnp.concatenate([top, rolled[SUBLANES:]], axis=0)


def _norm_proj_conv_kernel(x_ref, g_ref, w_ref, wdt_ref, cw_ref, cb_ref, o_ref, dt_ref, tail_ref,
                           xn_ref, carry_ref, *, tiles_per_seq, z_tiles, strip):
    i = pl.program_id(0)
    j = pl.program_id(1)
    tm, tn = o_ref.shape

    @pl.when(j == 0)
    def _():
        x = x_ref[...]
        ms = jnp.mean(x * x, axis=-1, keepdims=True)
        xn = (x * lax.rsqrt(ms + EPS) * g_ref[...]).astype(BF16)
        xn_ref[...] = xn
        dt_ref[...] = _dot(xn, wdt_ref[...])

    @pl.when((j == 0) & (i % tiles_per_seq == 0))
    def _():
        carry_ref[...] = jnp.zeros_like(carry_ref)

    @pl.when(j < z_tiles)
    def _():
        o_ref[...] = _dot(xn_ref[...], w_ref[...]).astype(o_ref.dtype)

    @pl.when(j >= z_tiles)
    def _():
        jc = j - z_tiles
        for s in range(tn // strip):
            sl = slice(s * strip, (s + 1) * strip)
            x = _dot(xn_ref[...], w_ref[:, sl])
            tail_ref[:, sl] = x[tm - SUBLANES:tm, :]
            carry = carry_ref[jc, :, sl]
            a3 = x * cw_ref[0:1, sl]
            a2 = x * cw_ref[1:2, sl] + _shift_rows(a3, carry[2:3])
            a1 = x * cw_ref[2:3, sl] + _shift_rows(a2, carry[1:2])
            out = x * cw_ref[3:4, sl] + _shift_rows(a1, carry[0:1]) + cb_ref[:, sl]
            o_ref[:, sl] = _silu(out).astype(o_ref.dtype)
            carry_ref[jc, 0:1, sl] = a1[tm - 1:tm]
            carry_ref[jc, 1:2, sl] = a2[tm - 1:tm]
            carry_ref[jc, 2:3, sl] = a3[tm - 1:tm]


def _norm_proj_conv(x, g, w, w_dt, cw, cb, *, tm, tn, seq):
    m, d = x.shape
    n = w.shape[1]
    z_tiles = SSM_DINNER // tn
    c_tiles = SSM_CONVDIM // tn
    tiles_per_seq = seq // tm
    cj = lambda j: jnp.maximum(j - z_tiles, 0)
    return pl.pallas_call(
        functools.partial(_norm_proj_conv_kernel, tiles_per_seq=tiles_per_seq, z_tiles=z_tiles, strip=256),
        grid=(m // tm, n // tn),
        in_specs=[
            pl.BlockSpec((tm, d), lambda i, j: (i, 0)),
            pl.BlockSpec((1, d), lambda i, j: (0, 0)),
            pl.BlockSpec((d, tn), lambda i, j: (0, j)),
            pl.BlockSpec((d, LANES), lambda i, j: (0, 0)),
            pl.BlockSpec((SSM_CONV, tn), lambda i, j: (0, cj(j))),
            pl.BlockSpec((1, tn), lambda i, j: (0, cj(j))),
        ],
        out_specs=[
            pl.BlockSpec((tm, tn), lambda i, j: (i, j)),
            pl.BlockSpec((tm, LANES), lambda i, j: (i, 0)),
            pl.BlockSpec((None, SUBLANES, tn), lambda i, j: (i, 0, cj(j))),
        ],
        out_shape=[
            jax.ShapeDtypeStruct((m, n), BF16),
            jax.ShapeDtypeStruct((m, LANES), F32),
            jax.ShapeDtypeStruct((m // tm, SUBLANES, SSM_CONVDIM), F32),
        ],
        scratch_shapes=[
            pltpu.VMEM((tm, d), BF16),
            pltpu.VMEM((c_tiles, SUBLANES, tn), F32),
        ],
        compiler_params=_cparams(("arbitrary", "arbitrary")),
        name="ssd_in_proj_conv",
    )(x, g, w, w_dt, cw, cb)


def _out_proj_kernel(a_ref, w_ref, h_ref, *rest, has_norm):
    if has_norm:
        g_ref, o_ref = rest
    else:
        (o_ref,) = rest
    y = h_ref[...] + _dot(a_ref[...], w_ref[...])
    if has_norm:
        ms = jnp.mean(y * y, axis=-1, keepdims=True)
        y = y * lax.rsqrt(ms + EPS) * g_ref[...]
    o_ref[...] = y


def _out_proj(a, w, h, *, tm, final_g=None):
    m, k = a.shape
    d = w.shape[1]
    has_norm = final_g is not None
    in_specs = [
        pl.BlockSpec((tm, k), lambda i: (i, 0)),
        pl.BlockSpec((k, d), lambda i: (0, 0)),
        pl.BlockSpec((tm, d), lambda i: (i, 0)),
    ]
    args = [a, w, h]
    if has_norm:
        in_specs.append(pl.BlockSpec((1, d), lambda i: (0, 0)))
        args.append(final_g)
    return pl.pallas_call(
        functools.partial(_out_proj_kernel, has_norm=has_norm),
        grid=(m // tm,),
        in_specs=in_specs,
        out_specs=pl.BlockSpec((tm, d), lambda i: (i, 0)),
        out_shape=jax.ShapeDtypeStruct((m, d), F32),
        compiler_params=_cparams(("arbitrary",)),
        name="out_proj",
    )(*args)


def _rope(x, cos, sin):
    half = RET_DK // 2
    x1 = x[:, :half]
    x2 = x[:, half:]
    return jnp.concatenate([x1 * cos - x2 * sin, x1 * sin + x2 * cos], axis=1)


def _head_norm_gate(o, hn, gate):
    ms = jnp.mean(o * o, axis=-1, keepdims=True)
    on = o * lax.rsqrt(ms + EPS) * hn
    return (on * _silu(gate)).astype(BF16)


def _ret_prompt_kernel(q_ref, k_ref, v_ref, gate_ref, cos_ref, sin_ref, lg_ref, hn_ref,
                       og_ref, st_ref, s_ref, dec_ref, qsc_ref, ksc_ref, gc_ref):
    c = pl.program_id(2)
    rows = q_ref.shape[0]

    @pl.when(c == 0)
    def _():
        lg = lg_ref[:, 0:1]
        i = _iota((rows, rows), 0)
        j = _iota((rows, rows), 1)
        scale = RET_DK ** -0.5
        dec_ref[...] = jnp.where(i >= j, jnp.exp((i - j).astype(F32) * lg), 0.0) * scale
        r = _iota((rows, RET_DK), 0)
        qsc_ref[...] = jnp.exp((r + 1).astype(F32) * lg)
        ksc_ref[...] = jnp.exp((rows - 1 - r).astype(F32) * lg) * scale
        gc_ref[...] = jnp.exp(jnp.full(gc_ref.shape, float(rows), F32) * lg)
        s_ref[...] = jnp.zeros_like(s_ref)

    cos = cos_ref[...]
    sin = sin_ref[...]
    qr = _rope(q_ref[...].astype(F32), cos, sin)
    kr = _rope(k_ref[...].astype(F32), cos, sin)
    v = v_ref[...]
    p = (_dot_nt(qr.astype(BF16), kr.astype(BF16)) * dec_ref[...]).astype(BF16)
    qs = (qr * qsc_ref[...]).astype(BF16)
    s = s_ref[...]
    o = _dot(p, v) + _dot(qs, s.astype(BF16))
    kt = (kr * ksc_ref[...]).T.astype(BF16)
    s_new = s * gc_ref[...] + _dot(kt, v)
    s_ref[...] = s_new
    og_ref[...] = _head_norm_gate(o, hn_ref[...], gate_ref[...].astype(F32))

    @pl.when(c == pl.num_programs(2) - 1)
    def _():
        st_ref[...] = s_new


def _ret_prompt(proj, cos, sin, lg, hn, batch, seq):
    rows = RET_CHUNK
    nc = seq // rows
    qb = RET_QK // RET_DK
    vb = 2 * RET_QK // RET_DV
    gb = vb + RET_VW // RET_DV
    return pl.pallas_call(
        _ret_prompt_kernel,
        grid=(batch, RET_HEADS, nc),
        in_specs=[
            pl.BlockSpec((rows, RET_DK), lambda b, h, c: (b * nc + c, h)),
            pl.BlockSpec((rows, RET_DK), lambda b, h, c: (b * nc + c, qb + h)),
            pl.BlockSpec((rows, RET_DV), lambda b, h, c: (b * nc + c, vb + h)),
            pl.BlockSpec((rows, RET_DV), lambda b, h, c: (b * nc + c, gb + h)),
            pl.BlockSpec((rows, RET_DK // 2), lambda b, h, c: (c, 0)),
            pl.BlockSpec((rows, RET_DK // 2), lambda b, h, c: (c, 0)),
            pl.BlockSpec((None, 1, LANES), lambda b, h, c: (h, 0, 0)),
            pl.BlockSpec((1, RET_DV), lambda b, h, c: (0, h)),
        ],
        out_specs=[
            pl.BlockSpec((rows, RET_DV), lambda b, h, c: (b * nc + c, h)),
            pl.BlockSpec((None, None, None, RET_DK, RET_DV), lambda b, h, c: (0, b, h, 0, 0)),
        ],
        out_shape=[
            jax.ShapeDtypeStruct((batch * seq, RET_VW), BF16),
            jax.ShapeDtypeStruct((1, batch, RET_HEADS, RET_DK, RET_DV), F32),
        ],
        scratch_shapes=[
            pltpu.VMEM((RET_DK, RET_DV), F32),
            pltpu.VMEM((rows, rows), F32),
            pltpu.VMEM((rows, RET_DK), F32),
            pltpu.VMEM((rows, RET_DK), F32),
            pltpu.VMEM((1, RET_DV), F32),
        ],
        compiler_params=_cparams(("arbitrary", "arbitrary", "arbitrary")),
        name="ret_prompt",
    )(proj, proj, proj, proj, cos, sin, lg, hn)


def _ret_sample_kernel(q_ref, k_ref, v_ref, gate_ref, cos_ref, sin_ref, lg_ref, hn_ref, st_in_ref,
                       og_ref, st_out_ref, *, seq):
    bb = st_in_ref.shape[0]
    rows = bb * seq
    lg = lg_ref[:, 0:1]
    scale = RET_DK ** -0.5
    i = _iota((rows, rows), 0)
    j = _iota((rows, rows), 1)
    same = (i // seq) == (j // seq)
    dec = jnp.where(same & (i >= j), jnp.exp((i - j).astype(F32) * lg), 0.0) * scale
    pos = _iota((rows, RET_DK), 0) % seq
    qsc = jnp.exp((pos + 1).astype(F32) * lg)
    ksc = jnp.exp((seq - 1 - pos).astype(F32) * lg) * scale
    gc = jnp.exp(jnp.full((1, RET_DV), float(seq), F32) * lg)

    cos = jnp.concatenate([cos_ref[...]] * bb, axis=0)
    sin = jnp.concatenate([sin_ref[...]] * bb, axis=0)
    qr = _rope(q_ref[...].astype(F32), cos, sin)
    kr = _rope(k_ref[...].astype(F32), cos, sin)
    v = v_ref[...].astype(F32)
    p = _dot_nt(qr.astype(BF16), kr.astype(BF16)) * dec
    intra = _dot(p.astype(BF16), v.astype(BF16))
    qs = qr * qsc
    kt = kr * ksc
    vrow = _iota((rows, RET_DV), 0) // seq
    cross = []
    for n in range(bb):
        s0 = st_in_ref[n]
        cross.append(_dot(qs[n * seq:(n + 1) * seq], s0))
        vn = jnp.where(vrow == n, v, 0.0)
        st_out_ref[n] = s0 * gc + _dot_tn(kt.astype(BF16), vn.astype(BF16))
    o = intra + jnp.concatenate(cross, axis=0)
    og_ref[...] = _head_norm_gate(o, hn_ref[...], gate_ref[...].astype(F32))


def _ret_sample(proj, cos, sin, lg, hn, state, batch, seq):
    bb = RET_SAMPLE_BB
    rows = bb * seq
    qb = RET_QK // RET_DK
    vb = 2 * RET_QK // RET_DV
    gb = vb + RET_VW // RET_DV
    st_spec = pl.BlockSpec((None, bb, None, RET_DK, RET_DV), lambda n, h: (0, n, h, 0, 0))
    return pl.pallas_call(
        functools.partial(_ret_sample_kernel, seq=seq),
        grid=(batch // bb, RET_HEADS),
        in_specs=[
            pl.BlockSpec((rows, RET_DK), lambda n, h: (n, h)),
            pl.BlockSpec((rows, RET_DK), lambda n, h: (n, qb + h)),
            pl.BlockSpec((rows, RET_DV), lambda n, h: (n, vb + h)),
            pl.BlockSpec((rows, RET_DV), lambda n, h: (n, gb + h)),
            pl.BlockSpec((seq, RET_DK // 2), lambda n, h: (0, 0)),
            pl.BlockSpec((seq, RET_DK // 2), lambda n, h: (0, 0)),
            pl.BlockSpec((None, 1, LANES), lambda n, h: (h, 0, 0)),
            pl.BlockSpec((1, RET_DV), lambda n, h: (0, h)),
            st_spec,
        ],
        out_specs=[
            pl.BlockSpec((rows, RET_DV), lambda n, h: (n, h)),
            st_spec,
        ],
        out_shape=[
            jax.ShapeDtypeStruct((batch * seq, RET_VW), BF16),
            jax.ShapeDtypeStruct(state.shape, F32),
        ],
        compiler_params=_cparams(("arbitrary", "arbitrary")),
        name="ret_sample",
    )(proj, proj, proj, proj, cos, sin, lg, hn, state)


def _head_expand(vals, expand):
    parts = []
    for v in vals:
        hi = v.astype(BF16)
        lo = (v - hi.astype(F32)).astype(BF16)
        parts.append(jnp.concatenate([hi, lo], axis=1))
    return _dot(jnp.concatenate(parts, axis=0), expand)


def _ssd_prompt_kernel(z_ref, xc_ref, bc_ref, dtr_ref, dtb_ref, alog_ref, dsk_ref, gn_ref, exp_ref,
                       g_ref, st_ref, s_ref, ex_ref):
    c = pl.program_id(1)
    rows = z_ref.shape[0]
    bw = SSM_GROUP_W

    @pl.when(c == 0)
    def _():
        s_ref[...] = jnp.zeros_like(s_ref)

    dt = _softplus(dtr_ref[...] + dtb_ref[...])
    la = dt * (-jnp.exp(alog_ref[...]))
    ri = _iota((rows, rows), 0)
    rj = _iota((rows, rows), 1)
    causal = ri >= rj
    cum = jnp.dot(causal.astype(F32), la, precision=lax.Precision.HIGHEST, preferred_element_type=F32)
    ecum = jnp.exp(cum)
    wdt = jnp.exp(cum[rows - 1:rows, :] - cum) * dt
    ex_ref[...] = _head_expand([ecum, wdt, dt], exp_ref[...])
    cum2 = cum * math.log2(math.e)
    cum2_t = cum2.T
    head_lane = _iota((rows, bw), 1) // SSM_HEADDIM

    for g in range(SSM_GROUPS):
        gsl = slice(g * bw, (g + 1) * bw)
        bm = bc_ref[:, g * SSM_DSTATE:(g + 1) * SSM_DSTATE]
        cm = bc_ref[:, (SSM_GROUPS + g) * SSM_DSTATE:(SSM_GROUPS + g + 1) * SSM_DSTATE]
        xs = xc_ref[:, gsl].astype(F32)
        s = s_ref[g]
        gmat = _dot_nt(cm, bm)
        ps = []
        for hh in range(HEADS_PER_GROUP):
            h = g * HEADS_PER_GROUP + hh
            colb = jnp.broadcast_to(cum2[:, h:h + 1], (rows, LANES))
            dec = jnp.exp2(jnp.where(causal, colb - cum2_t[h:h + 1, :], -jnp.inf))
            ps.append((gmat * dec).astype(BF16))
        v = (xs * ex_ref[2 * rows:3 * rows, gsl]).astype(BF16)
        y4 = _dot(jnp.concatenate(ps, axis=0), v)
        y = y4[0:rows]
        for hh in range(1, HEADS_PER_GROUP):
            y = jnp.where(head_lane == hh, y4[hh * rows:(hh + 1) * rows], y)
        y = y + _dot(cm, s.astype(BF16)) * ex_ref[0:rows, gsl]
        vw = (xs * ex_ref[rows:2 * rows, gsl]).astype(BF16)
        s_ref[g] = s * ex_ref[rows - 1:rows, gsl] + _dot_tn(bm, vw)

        y = y + xs * dsk_ref[:, gsl]
        gg = y * _silu(z_ref[:, gsl].astype(F32))
        ms = jnp.mean(gg * gg, axis=-1, keepdims=True)
        g_ref[:, gsl] = (gg * lax.rsqrt(ms + EPS) * gn_ref[:, gsl]).astype(BF16)

    @pl.when(c == pl.num_programs(1) - 1)
    def _():
        for g in range(SSM_GROUPS):
            st_ref[g * bw:(g + 1) * bw, :] = s_ref[g].T


def _ssd_prompt(zxc, dtr, dtb, alog, dsk, gn, expand, batch, seq):
    rows = SSD_CHUNK
    nc = seq // rows
    full = lambda shape: pl.BlockSpec(shape, lambda b, c: (0,) * len(shape))
    col = lambda k: pl.BlockSpec((rows, SSM_DINNER), lambda b, c: (b * nc + c, k))
    return pl.pallas_call(
        _ssd_prompt_kernel,
        grid=(batch, nc),
        in_specs=[
            col(0), col(1), col(2),
            pl.BlockSpec((rows, LANES), lambda b, c: (b * nc + c, 0)),
            full((1, LANES)), full((1, LANES)),
            full((1, SSM_DINNER)), full((1, SSM_DINNER)),
            full((2 * LANES, SSM_DINNER)),
        ],
        out_specs=[
            pl.BlockSpec((rows, SSM_DINNER), lambda b, c: (b * nc + c, 0)),
            pl.BlockSpec((None, SSM_DINNER, SSM_DSTATE), lambda b, c: (b, 0, 0)),
        ],
        out_shape=[
            jax.ShapeDtypeStruct((batch * seq, SSM_DINNER), BF16),
            jax.ShapeDtypeStruct((batch, SSM_DINNER, SSM_DSTATE), F32),
        ],
        scratch_shapes=[
            pltpu.VMEM((SSM_GROUPS, SSM_DSTATE, SSM_GROUP_W), F32),
            pltpu.VMEM((3 * rows, SSM_DINNER), F32),
        ],
        compiler_params=_cparams(("arbitrary", "arbitrary")),
        name="ssd_prompt",
    )(zxc, zxc, zxc, dtr, dtb, alog, dsk, gn, expand)


def _ssd_sample_kernel(z_ref, xa_ref, xb_ref, dtr_ref, conv_in_ref, st_in_ref,
                       cw_ref, cb_ref, dtb_ref, alog_ref, dsk_ref, gn_ref, exp_ref,
                       g_ref, st_out_ref, xpad_ref, xc_ref, ex_ref, y_ref, *, seq):
    bb = st_in_ref.shape[0]
    rows = bb * seq
    pad = SUBLANES
    keep = SSM_CONV - 1
    bw = SSM_GROUP_W

    for n in range(bb):
        r0 = n * seq
        xpad_ref[pad - keep:pad, :] = conv_in_ref[n]
        xpad_ref[pad:pad + seq, 0:SSM_DINNER] = xa_ref[r0:r0 + seq, :]
        xpad_ref[pad:pad + seq, SSM_DINNER:SSM_CONVDIM] = xb_ref[r0:r0 + seq, :]
        acc = cb_ref[...]
        for w in range(SSM_CONV):
            lo = pad - keep + w
            acc = acc + xpad_ref[lo:lo + seq, :] * cw_ref[w:w + 1, :]
        xc_ref[r0:r0 + seq, :] = _silu(acc)

    dt = _softplus(dtr_ref[...] + dtb_ref[...])
    la = dt * (-jnp.exp(alog_ref[...]))
    ri = _iota((rows, rows), 0)
    rj = _iota((rows, rows), 1)
    same = (ri // seq) == (rj // seq)
    cum = jnp.dot((same & (ri >= rj)).astype(F32), la, precision=lax.Precision.HIGHEST,
                  preferred_element_type=F32)
    clast = jnp.dot(same.astype(F32), la, precision=lax.Precision.HIGHEST, preferred_element_type=F32)
    eclast = jnp.exp(clast)
    ex_ref[...] = _head_expand([jnp.exp(cum), jnp.exp(clast - cum) * dt, dt], exp_ref[...])

    si = _iota((seq, seq), 0)
    sj = _iota((seq, seq), 1)
    causal = si >= sj
    eye = si == sj
    head_lane = _iota((seq, bw), 1) // SSM_HEADDIM

    for n in range(bb):
        r0 = n * seq
        rsl = slice(r0, r0 + seq)
        cum_n = cum[rsl]
        for g in range(SSM_GROUPS):
            gsl = slice(g * bw, (g + 1) * bw)
            bm = xc_ref[rsl, SSM_DINNER + g * SSM_DSTATE:SSM_DINNER + (g + 1) * SSM_DSTATE]
            cm = xc_ref[rsl, SSM_DINNER + (SSM_GROUPS + g) * SSM_DSTATE:
                        SSM_DINNER + (SSM_GROUPS + g + 1) * SSM_DSTATE]
            xs = xc_ref[rsl, gsl]
            gmat = _dot_nt(cm.astype(BF16), bm.astype(BF16))
            ps = []
            for hh in range(HEADS_PER_GROUP):
                h = g * HEADS_PER_GROUP + hh
                ccol = cum_n[:, h:h + 1]
                crow = jnp.sum(jnp.where(eye, ccol, 0.0), axis=0, keepdims=True)
                ps.append(gmat * jnp.exp(jnp.where(causal, ccol - crow, -jnp.inf)))
            v = xs * ex_ref[2 * rows + r0:2 * rows + r0 + seq, gsl]
            y4 = _dot(jnp.concatenate(ps, axis=0).astype(BF16), v.astype(BF16))
            y = y4[0:seq]
            for hh in range(1, HEADS_PER_GROUP):
                y = jnp.where(head_lane == hh, y4[hh * seq:(hh + 1) * seq], y)
            st = st_in_ref[n, gsl, :]
            y = y + _dot_nt(cm.astype(BF16), st.astype(BF16)) * ex_ref[rsl, gsl]
            y_ref[rsl, gsl] = y
            vw = xs * ex_ref[rows + r0:rows + r0 + seq, gsl]
            upd = _dot_tn(vw.astype(BF16), bm.astype(BF16))
            for hh in range(HEADS_PER_GROUP):
                h = g * HEADS_PER_GROUP + hh
                hsl = slice(hh * SSM_HEADDIM, (hh + 1) * SSM_HEADDIM)
                osl = slice(g * bw + hh * SSM_HEADDIM, g * bw + (hh + 1) * SSM_HEADDIM)
                st_out_ref[n, osl, :] = st[hsl] * eclast[r0:r0 + 1, h:h + 1] + upd[hsl]

    y = y_ref[...] + xc_ref[:, 0:SSM_DINNER] * dsk_ref[...]
    gg = y * _silu(z_ref[...])
    for g in range(SSM_GROUPS):
        gsl = slice(g * bw, (g + 1) * bw)
        blk = gg[:, gsl]
        ms = jnp.mean(blk * blk, axis=-1, keepdims=True)
        g_ref[:, gsl] = (blk * lax.rsqrt(ms + EPS) * gn_ref[:, gsl]).astype(BF16)


def _ssd_sample(zx, dtr, conv_state, ssm_state_t, cw, cb, dtb, alog, dsk, gn, expand, batch, seq):
    bb = SSD_SAMPLE_BB
    rows = bb * seq
    keep = SSM_CONV - 1
    full = lambda shape: pl.BlockSpec(shape, lambda n: (0,) * len(shape))
    col = lambda k: pl.BlockSpec((rows, SSM_DINNER), lambda n: (n, k))
    st_spec = pl.BlockSpec((bb, SSM_DINNER, SSM_DSTATE), lambda n: (n, 0, 0))
    return pl.pallas_call(
        functools.partial(_ssd_sample_kernel, seq=seq),
        grid=(batch // bb,),
        in_specs=[
            col(0), col(1), col(2),
            pl.BlockSpec((rows, LANES), lambda n: (n, 0)),
            pl.BlockSpec((None, bb, keep, SSM_CONVDIM), lambda n: (0, n, 0, 0)),
            st_spec,
            full((SSM_CONV, SSM_CONVDIM)), full((1, SSM_CONVDIM)),
            full((1, LANES)), full((1, LANES)),
            full((1, SSM_DINNER)), full((1, SSM_DINNER)),
            full((2 * LANES, SSM_DINNER)),
        ],
        out_specs=[
            pl.BlockSpec((rows, SSM_DINNER), lambda n: (n, 0)),
            st_spec,
        ],
        out_shape=[
            jax.ShapeDtypeStruct((batch * seq, SSM_DINNER), BF16),
            jax.ShapeDtypeStruct(ssm_state_t.shape, F32),
        ],
        scratch_shapes=[
            pltpu.VMEM((2 * SUBLANES, SSM_CONVDIM), F32),
            pltpu.VMEM((rows, SSM_CONVDIM), F32),
            pltpu.VMEM((3 * rows, SSM_DINNER), F32),
            pltpu.VMEM((rows, SSM_DINNER), F32),
        ],
        compiler_params=_cparams(("arbitrary",)),
        name="ssd_sample",
    )(zx, zx, zx, dtr, conv_state, ssm_state_t, cw, cb, dtb, alog, dsk, gn, expand)


def _rope_tables(pos):
    half = RET_DK // 2
    freqs = ROPE_BASE ** (-jnp.arange(half, dtype=F32) / half)
    ang = pos.astype(F32)[:, None] * freqs[None, :]
    return jnp.cos(ang), jnp.sin(ang)


def _row(v):
    return v.reshape(1, -1).astype(F32)


def _pad_lanes(v):
    return jnp.pad(v.astype(F32), (0, LANES - v.shape[0])).reshape(1, LANES)


def kernel(x_prompt, x_sample, state_ret, state_ssm, state_conv, ret_norm, ret_w_in, ret_head_norm, ret_w_out, ssm_norm, ssm_w_in, ssm_conv_w, ssm_conv_b, ssm_dt_bias, ssm_a_log, ssm_d, ssm_gate_norm, ssm_w_out, final_norm):
    bp, lp, d = x_prompt.shape
    bs, ls, _ = x_sample.shape
    assert ret_norm.shape[0] == 1 and ssm_norm.shape[0] == 1, "one retention and one SSD layer"

    ret_win = ret_w_in[0].astype(BF16)
    ret_wout = ret_w_out[0].astype(BF16)
    n_main = SSM_DINNER + SSM_CONVDIM
    ssm_win = ssm_w_in[0, :, :n_main].astype(BF16)
    ssm_wdt = jnp.pad(ssm_w_in[0, :, n_main:], ((0, 0), (0, LANES - SSM_HEADS))).astype(BF16)
    ssm_wout = ssm_w_out[0].astype(BF16)
    expand = (jnp.arange(2 * LANES)[:, None] % LANES == jnp.arange(SSM_DINNER)[None, :] // SSM_HEADDIM).astype(BF16)
    lg = jnp.log(1.0 - 2.0 ** (-5.0 - jnp.arange(RET_HEADS, dtype=F32)))
    lg = jnp.broadcast_to(lg[:, None, None], (RET_HEADS, 1, LANES))
    hn = _row(ret_head_norm[0])
    dsk = _row(jnp.repeat(ssm_d[0], SSM_HEADDIM))
    gn = _row(ssm_gate_norm[0])
    cw = ssm_conv_w[0].astype(F32)
    cb = _row(ssm_conv_b[0])
    dtb = _pad_lanes(ssm_dt_bias[0])
    alog = _pad_lanes(ssm_a_log[0])
    g_ret = _row(ret_norm[0])
    g_ssm = _row(ssm_norm[0])
    g_fin = _row(final_norm)

    xp = x_prompt.reshape(bp * lp, d)
    cos_p, sin_p = _rope_tables(jnp.arange(lp, dtype=jnp.int32))
    proj = _norm_proj(xp, g_ret, ret_win, tm=1024, tn=2048, out_dtype=BF16)
    og, ret_p = _ret_prompt(proj, cos_p, sin_p, lg, hn, bp, lp)
    h1 = _out_proj(og, ret_wout, xp, tm=512)
    tm_c = 1024
    zxc, dtr, tail = _norm_proj_conv(h1, g_ssm, ssm_win, ssm_wdt, cw, cb, tm=tm_c, tn=1024, seq=lp)
    tail = tail[lp // tm_c - 1::lp // tm_c]
    gated, ssm_pt = _ssd_prompt(zxc, dtr, dtb, alog, dsk, gn, expand, bp, lp)
    y_p = _out_proj(gated, ssm_wout, h1, tm=512, final_g=g_fin)
    ssm_p = jnp.swapaxes(ssm_pt.reshape(1, bp, SSM_HEADS, SSM_HEADDIM, SSM_DSTATE), -1, -2)
    conv_p = tail[None, :, SUBLANES - (SSM_CONV - 1):, :]

    xs = x_sample.reshape(bs * ls, d)
    cos_s, sin_s = _rope_tables(PAST_LEN + jnp.arange(ls, dtype=jnp.int32))
    proj_s = _norm_proj(xs, g_ret, ret_win, tm=bs * ls, tn=2048, out_dtype=BF16)
    og_s, ret_s = _ret_sample(proj_s, cos_s, sin_s, lg, hn, state_ret, bs, ls)
    h1_s = _out_proj(og_s, ret_wout, xs, tm=512)
    zx_s, dtr_s = _norm_proj(h1_s, g_ssm, ssm_win, tm=bs * ls, tn=2048, out_dtype=F32, w_dt=ssm_wdt)
    st_t = jnp.swapaxes(state_ssm[0], -1, -2).reshape(bs, SSM_DINNER, SSM_DSTATE)
    gated_s, ssm_st = _ssd_sample(zx_s, dtr_s, state_conv, st_t, cw, cb, dtb, alog, dsk, gn, expand, bs, ls)
    ssm_s = jnp.swapaxes(ssm_st.reshape(1, bs, SSM_HEADS, SSM_HEADDIM, SSM_DSTATE), -1, -2)
    keep = SSM_CONV - 1
    conv_s = zx_s.reshape(bs, ls, -1)[None, :, ls - keep:, SSM_DINNER:]
    y_s = _out_proj(gated_s, ssm_wout, h1_s, tm=512, final_g=g_fin)

    return (y_p.reshape(bp, lp, d), y_s.reshape(bs, ls, d), ret_p, ret_s, ssm_p, ssm_s, conv_p, conv_s)
```

```python
import functools
import math

import jax
import jax.numpy as jnp
from jax import lax
from jax.experimental import pallas as pl
from jax.experimental.pallas import tpu as pltpu

F32 = jnp.float32
BF16 = jnp.bfloat16

D_MODEL = 1024
RET_HEADS = 4
RET_DK = 256
RET_DV = 512
RET_VW = RET_HEADS * RET_DV
RET_QK = RET_HEADS * RET_DK
SSM_DINNER = 2048
SSM_HEADDIM = 64
SSM_HEADS = 32
SSM_GROUPS = 8
SSM_DSTATE = 128
SSM_CONV = 4
SSM_CONVDIM = 4096
SSM_GROUP_W = SSM_DINNER // SSM_GROUPS
HEADS_PER_GROUP = SSM_HEADS // SSM_GROUPS
ROPE_BASE = 10000.0
EPS = 1e-6
PAST_LEN = 16384

LANES = 128
SUBLANES = 8
VMEM_LIMIT = 56 * 1024 * 1024

RET_CHUNK = 256
SSD_CHUNK = 128
RET_SAMPLE_BB = 8
SSD_SAMPLE_BB = 4


def _cparams(sem, flags=None):
    return pltpu.CompilerParams(dimension_semantics=sem, vmem_limit_bytes=VMEM_LIMIT, flags=flags)


def _sigmoid(x):
    return 1.0 / (1.0 + jnp.exp2(x * (-math.log2(math.e))))


def _silu(x):
    return x * _sigmoid(x)


def _softplus(x):
    return jnp.maximum(x, 0.0) + jnp.log1p(jnp.exp(-jnp.abs(x)))


def _dot(a, b):
    return jnp.dot(a, b, preferred_element_type=F32)


def _dot_nt(a, b):
    return lax.dot_general(a, b, (((1,), (1,)), ((), ())), preferred_element_type=F32)


def _dot_tn(a, b):
    return lax.dot_general(a, b, (((0,), (0,)), ((), ())), preferred_element_type=F32)


def _iota(shape, dim):
    return lax.broadcasted_iota(jnp.int32, shape, dim)


def _norm_proj_kernel(x_ref, g_ref, w_ref, *rest, has_dt):
    if has_dt:
        wdt_ref, o_ref, dt_ref, xn_ref = rest
    else:
        o_ref, xn_ref = rest

    @pl.when(pl.program_id(1) == 0)
    def _():
        x = x_ref[...]
        ms = jnp.mean(x * x, axis=-1, keepdims=True)
        xn = (x * lax.rsqrt(ms + EPS) * g_ref[...]).astype(BF16)
        xn_ref[...] = xn
        if has_dt:
            dt_ref[...] = _dot(xn, wdt_ref[...])

    o_ref[...] = _dot(xn_ref[...], w_ref[...]).astype(o_ref.dtype)


def _norm_proj(x, g, w, *, tm, tn, out_dtype, w_dt=None):
    m, d = x.shape
    n = w.shape[1]
    has_dt = w_dt is not None
    in_specs = [
        pl.BlockSpec((tm, d), lambda i, j: (i, 0)),
        pl.BlockSpec((1, d), lambda i, j: (0, 0)),
        pl.BlockSpec((d, tn), lambda i, j: (0, j)),
    ]
    args = [x, g, w]
    out_shape = [jax.ShapeDtypeStruct((m, n), out_dtype)]
    out_specs = [pl.BlockSpec((tm, tn), lambda i, j: (i, j))]
    if has_dt:
        in_specs.append(pl.BlockSpec((d, LANES), lambda i, j: (0, 0)))
        args.append(w_dt)
        out_shape.append(jax.ShapeDtypeStruct((m, LANES), F32))
        out_specs.append(pl.BlockSpec((tm, LANES), lambda i, j: (i, 0)))
    res = pl.pallas_call(
        functools.partial(_norm_proj_kernel, has_dt=has_dt),
        grid=(m // tm, n // tn),
        in_specs=in_specs,
        out_specs=out_specs,
        out_shape=out_shape,
        scratch_shapes=[pltpu.VMEM((tm, d), BF16)],
        compiler_params=_cparams(("arbitrary", "arbitrary")),
        name="in_proj",
    )(*args)
    return res if has_dt else res[0]


def _shift_rows(a, first_rows):
    rows, cols = a.shape
    n = first_rows.shape[0]
    slabs = rows // SUBLANES
    rot = pltpu.roll(a.reshape(slabs, SUBLANES, cols), n, axis=1)
    first = jnp.concatenate([first_rows, jnp.zeros((SUBLANES - n, cols), a.dtype)], axis=0)
    above = jnp.concatenate([first.reshape(1, SUBLANES, cols), rot[:slabs - 1]], axis=0)
    top = _iota((slabs, SUBLANES, cols), 1) < n
    return jnp.where(top, above, rot).reshape(rows, cols)


def _norm_proj_conv_kernel(x_ref, g_ref, w_ref, wdt_ref, cw_ref, cb_ref, o_ref, dt_ref, tail_ref,
                           xn_ref, carry_ref, *, tiles_per_seq, z_tiles, strip):
    i = pl.program_id(0)
    j = pl.program_id(1)
    tm, tn = o_ref.shape

    @pl.when(j == 0)
    def _():
        x = x_ref[...]
        ms = jnp.mean(x * x, axis=-1, keepdims=True)
        xn = (x * lax.rsqrt(ms + EPS) * g_ref[...]).astype(BF16)
        xn_ref[...] = xn
        dt_ref[...] = _dot(xn, wdt_ref[...])

    @pl.when((j == 0) & (i % tiles_per_seq == 0))
    def _():
        carry_ref[...] = jnp.zeros_like(carry_ref)

    @pl.when(j < z_tiles)
    def _():
        o_ref[...] = _dot(xn_ref[...], w_ref[...]).astype(o_ref.dtype)

    @pl.when(j >= z_tiles)
    def _():
        jc = j - z_tiles
        for s in range(tn // strip):
            sl = slice(s * strip, (s + 1) * strip)
            x = _dot(xn_ref[...], w_ref[:, sl])
            tail_ref[:, sl] = x[tm - SUBLANES:tm, :]
            x2 = _shift_rows(x, carry_ref[jc, 1:3, sl])
            p = x * cw_ref[2:3, sl] + x2 * cw_ref[0:1, sl]
            out = x * cw_ref[3:4, sl] + x2 * cw_ref[1:2, sl] + _shift_rows(p, carry_ref[jc, 0:1, sl]) + cb_ref[:, sl]
            o_ref[:, sl] = _silu(out).astype(o_ref.dtype)
            carry_ref[jc, 0:1, sl] = p[tm - 1:tm]
            carry_ref[jc, 1:3, sl] = x[tm - 2:tm]


def _norm_proj_conv(x, g, w, w_dt, cw, cb, *, tm, tn, seq):
    m, d = x.shape
    n = w.shape[1]
    z_tiles = SSM_DINNER // tn
    c_tiles = SSM_CONVDIM // tn
    tiles_per_seq = seq // tm
    cj = lambda j: jnp.maximum(j - z_tiles, 0)
    return pl.pallas_call(
        functools.partial(_norm_proj_conv_kernel, tiles_per_seq=tiles_per_seq, z_tiles=z_tiles, strip=256),
        grid=(m // tm, n // tn),
        in_specs=[
            pl.BlockSpec((tm, d), lambda i, j: (i, 0)),
            pl.BlockSpec((1, d), lambda i, j: (0, 0)),
            pl.BlockSpec((d, tn), lambda i, j: (0, j)),
            pl.BlockSpec((d, LANES), lambda i, j: (0, 0)),
            pl.BlockSpec((SSM_CONV, tn), lambda i, j: (0, cj(j))),
            pl.BlockSpec((1, tn), lambda i, j: (0, cj(j))),
        ],
        out_specs=[
            pl.BlockSpec((tm, tn), lambda i, j: (i, j)),
            pl.BlockSpec((tm, LANES), lambda i, j: (i, 0)),
            pl.BlockSpec((None, SUBLANES, tn), lambda i, j: (i, 0, cj(j))),
        ],
        out_shape=[
            jax.ShapeDtypeStruct((m, n), BF16),
            jax.ShapeDtypeStruct((m, LANES), F32),
            jax.ShapeDtypeStruct((m // tm, SUBLANES, SSM_CONVDIM), F32),
        ],
        scratch_shapes=[
            pltpu.VMEM((tm, d), BF16),
            pltpu.VMEM((c_tiles, SUBLANES, tn), F32),
        ],
        compiler_params=_cparams(("arbitrary", "arbitrary")),
        name="ssd_in_proj_conv",
    )(x, g, w, w_dt, cw, cb)


def _out_proj_kernel(a_ref, w_ref, h_ref, *rest, has_norm):
    if has_norm:
        g_ref, o_ref = rest
    else:
        (o_ref,) = rest
    y = h_ref[...] + _dot(a_ref[...], w_ref[...])
    if has_norm:
        ms = jnp.mean(y * y, axis=-1, keepdims=True)
        y = y * lax.rsqrt(ms + EPS) * g_ref[...]
    o_ref[...] = y


def _out_proj(a, w, h, *, tm, final_g=None):
    m, k = a.shape
    d = w.shape[1]
    has_norm = final_g is not None
    in_specs = [
        pl.BlockSpec((tm, k), lambda i: (i, 0)),
        pl.BlockSpec((k, d), lambda i: (0, 0)),
        pl.BlockSpec((tm, d), lambda i: (i, 0)),
    ]
    args = [a, w, h]
    if has_norm:
        in_specs.append(pl.BlockSpec((1, d), lambda i: (0, 0)))
        args.append(final_g)
    return pl.pallas_call(
        functools.partial(_out_proj_kernel, has_norm=has_norm),
        grid=(m // tm,),
        in_specs=in_specs,
        out_specs=pl.BlockSpec((tm, d), lambda i: (i, 0)),
        out_shape=jax.ShapeDtypeStruct((m, d), F32),
        compiler_params=_cparams(("arbitrary",)),
        name="out_proj",
    )(*args)


def _rope(x, cos, sin):
    half = RET_DK // 2
    x1 = x[:, :half]
    x2 = x[:, half:]
    return jnp.concatenate([x1 * cos - x2 * sin, x1 * sin + x2 * cos], axis=1)


def _head_norm_gate(o, hn, gate):
    ms = jnp.mean(o * o, axis=-1, keepdims=True)
    on = o * lax.rsqrt(ms + EPS) * hn
    return (on * _silu(gate)).astype(BF16)


def _ret_prompt_kernel(q_ref, k_ref, v_ref, gate_ref, cos_ref, sin_ref, lg_ref, hn_ref,
                       og_ref, st_ref, s_ref, dec_ref, qsc_ref, ksc_ref, gc_ref):
    b = pl.program_id(0)
    c = pl.program_id(1)
    rows = q_ref.shape[0]

    @pl.when((b == 0) & (c == 0))
    def _():
        i = _iota((rows, rows), 0)
        j = _iota((rows, rows), 1)
        r = _iota((rows, RET_DK), 0)
        scale = RET_DK ** -0.5
        for h in range(RET_HEADS):
            lg = lg_ref[h, :, 0:1]
            dec_ref[h] = jnp.where(i >= j, jnp.exp((i - j).astype(F32) * lg), 0.0) * scale
            qsc_ref[h] = jnp.exp((r + 1).astype(F32) * lg)
            ksc_ref[h] = jnp.exp((rows - 1 - r).astype(F32) * lg) * scale
            gc_ref[h] = jnp.exp(jnp.full((1, RET_DV), float(rows), F32) * lg)

    @pl.when(c == 0)
    def _():
        s_ref[...] = jnp.zeros_like(s_ref)

    cos = cos_ref[...]
    sin = sin_ref[...]
    for h in range(RET_HEADS):
        ksl = slice(h * RET_DK, (h + 1) * RET_DK)
        vsl = slice(h * RET_DV, (h + 1) * RET_DV)
        qr = _rope(q_ref[:, ksl].astype(F32), cos, sin)
        kr = _rope(k_ref[:, ksl].astype(F32), cos, sin)
        v = v_ref[:, vsl]
        p = (_dot_nt(qr.astype(BF16), kr.astype(BF16)) * dec_ref[h]).astype(BF16)
        qs = (qr * qsc_ref[h]).astype(BF16)
        s = s_ref[h]
        o = _dot(p, v) + _dot(qs, s.astype(BF16))
        kt = (kr * ksc_ref[h]).T.astype(BF16)
        s_ref[h] = s * gc_ref[h] + _dot(kt, v)
        og_ref[:, vsl] = _head_norm_gate(o, hn_ref[:, vsl], gate_ref[:, vsl].astype(F32))

    @pl.when(c == pl.num_programs(1) - 1)
    def _():
        st_ref[...] = s_ref[...]


def _ret_prompt(proj, cos, sin, lg, hn, batch, seq):
    rows = RET_CHUNK
    nc = seq // rows
    return pl.pallas_call(
        _ret_prompt_kernel,
        grid=(batch, nc),
        in_specs=[
            pl.BlockSpec((rows, RET_QK), lambda b, c: (b * nc + c, 0)),
            pl.BlockSpec((rows, RET_QK), lambda b, c: (b * nc + c, 1)),
            pl.BlockSpec((rows, RET_VW), lambda b, c: (b * nc + c, 2 * RET_QK // RET_VW)),
            pl.BlockSpec((rows, RET_VW), lambda b, c: (b * nc + c, 2 * RET_QK // RET_VW + 1)),
            pl.BlockSpec((rows, RET_DK // 2), lambda b, c: (c, 0)),
            pl.BlockSpec((rows, RET_DK // 2), lambda b, c: (c, 0)),
            pl.BlockSpec((RET_HEADS, 1, LANES), lambda b, c: (0, 0, 0)),
            pl.BlockSpec((1, RET_VW), lambda b, c: (0, 0)),
        ],
        out_specs=[
            pl.BlockSpec((rows, RET_VW), lambda b, c: (b * nc + c, 0)),
            pl.BlockSpec((None, None, RET_HEADS, RET_DK, RET_DV), lambda b, c: (0, b, 0, 0, 0)),
        ],
        out_shape=[
            jax.ShapeDtypeStruct((batch * seq, RET_VW), BF16),
            jax.ShapeDtypeStruct((1, batch, RET_HEADS, RET_DK, RET_DV), F32),
        ],
        scratch_shapes=[
            pltpu.VMEM((RET_HEADS, RET_DK, RET_DV), F32),
            pltpu.VMEM((RET_HEADS, rows, rows), F32),
            pltpu.VMEM((RET_HEADS, rows, RET_DK), F32),
            pltpu.VMEM((RET_HEADS, rows, RET_DK), F32),
            pltpu.VMEM((RET_HEADS, 1, RET_DV), F32),
        ],
        compiler_params=_cparams(("arbitrary", "arbitrary")),
        name="ret_prompt",
    )(proj, proj, proj, proj, cos, sin, lg, hn)


def _ret_sample_kernel(q_ref, k_ref, v_ref, gate_ref, cos_ref, sin_ref, lg_ref, hn_ref, st_in_ref,
                       og_ref, st_out_ref, *, seq):
    bb = st_in_ref.shape[0]
    rows = bb * seq
    lg = lg_ref[:, 0:1]
    scale = RET_DK ** -0.5
    i = _iota((rows, rows), 0)
    j = _iota((rows, rows), 1)
    same = (i // seq) == (j // seq)
    dec = jnp.where(same & (i >= j), jnp.exp((i - j).astype(F32) * lg), 0.0) * scale
    pos = _iota((rows, RET_DK), 0) % seq
    qsc = jnp.exp((pos + 1).astype(F32) * lg)
    ksc = jnp.exp((seq - 1 - pos).astype(F32) * lg) * scale
    gc = jnp.exp(jnp.full((1, RET_DV), float(seq), F32) * lg)

    cos = jnp.concatenate([cos_ref[...]] * bb, axis=0)
    sin = jnp.concatenate([sin_ref[...]] * bb, axis=0)
    qr = _rope(q_ref[...].astype(F32), cos, sin)
    kr = _rope(k_ref[...].astype(F32), cos, sin)
    v = v_ref[...].astype(F32)
    p = _dot_nt(qr.astype(BF16), kr.astype(BF16)) * dec
    intra = _dot(p.astype(BF16), v.astype(BF16))
    qs = qr * qsc
    kt = kr * ksc
    vrow = _iota((rows, RET_DV), 0) // seq
    cross = []
    for n in range(bb):
        s0 = st_in_ref[n]
        cross.append(_dot(qs[n * seq:(n + 1) * seq], s0))
        vn = jnp.where(vrow == n, v, 0.0)
        st_out_ref[n] = s0 * gc + _dot_tn(kt.astype(BF16), vn.astype(BF16))
    o = intra + jnp.concatenate(cross, axis=0)
    og_ref[...] = _head_norm_gate(o, hn_ref[...], gate_ref[...].astype(F32))


def _ret_sample(proj, cos, sin, lg, hn, state, batch, seq):
    bb = RET_SAMPLE_BB
    rows = bb * seq
    qb = RET_QK // RET_DK
    vb = 2 * RET_QK // RET_DV
    gb = vb + RET_VW // RET_DV
    st_spec = pl.BlockSpec((None, bb, None, RET_DK, RET_DV), lambda n, h: (0, n, h, 0, 0))
    return pl.pallas_call(
        functools.partial(_ret_sample_kernel, seq=seq),
        grid=(batch // bb, RET_HEADS),
        in_specs=[
            pl.BlockSpec((rows, RET_DK), lambda n, h: (n, h)),
            pl.BlockSpec((rows, RET_DK), lambda n, h: (n, qb + h)),
            pl.BlockSpec((rows, RET_DV), lambda n, h: (n, vb + h)),
            pl.BlockSpec((rows, RET_DV), lambda n, h: (n, gb + h)),
            pl.BlockSpec((seq, RET_DK // 2), lambda n, h: (0, 0)),
            pl.BlockSpec((seq, RET_DK // 2), lambda n, h: (0, 0)),
            pl.BlockSpec((None, 1, LANES), lambda n, h: (h, 0, 0)),
            pl.BlockSpec((1, RET_DV), lambda n, h: (0, h)),
            st_spec,
        ],
        out_specs=[
            pl.BlockSpec((rows, RET_DV), lambda n, h: (n, h)),
            st_spec,
        ],
        out_shape=[
            jax.ShapeDtypeStruct((batch * seq, RET_VW), BF16),
            jax.ShapeDtypeStruct(state.shape, F32),
        ],
        compiler_params=_cparams(("arbitrary", "arbitrary")),
        name="ret_sample",
    )(proj, proj, proj, proj, cos, sin, lg, hn, state)


def _head_expand(vals, expand):
    parts = []
    for v in vals:
        hi = v.astype(BF16)
        lo = (v - hi.astype(F32)).astype(BF16)
        parts.append(jnp.concatenate([hi, lo], axis=1))
    return _dot(jnp.concatenate(parts, axis=0), expand)


def _ssd_prompt_kernel(z_ref, xc_ref, bc_ref, dtr_ref, dtr_next_ref, dtb_ref, alog_ref, dsk_ref, gn_ref, exp_ref,
                       g_ref, st_ref, s_ref, ex_ref, cum2_ref, cum2t_ref):
    c = pl.program_id(1)
    rows = z_ref.shape[0]
    bw = SSM_GROUP_W
    ri = _iota((rows, rows), 0)
    rj = _iota((rows, rows), 1)
    causal = ri >= rj

    def decay_tables(dtr):
        dt = _softplus(dtr + dtb_ref[...])
        la = dt * (-jnp.exp(alog_ref[...]))
        cum = jnp.dot(causal.astype(F32), la, precision=lax.Precision.HIGHEST, preferred_element_type=F32)
        ecum = jnp.exp(cum)
        wdt = jnp.exp(cum[rows - 1:rows, :] - cum) * dt
        ex_ref[...] = _head_expand([ecum, wdt, dt], exp_ref[...])
        cum2 = cum * math.log2(math.e)
        cum2_ref[...] = cum2
        cum2t_ref[...] = cum2.T

    @pl.when(c == 0)
    def _():
        s_ref[...] = jnp.zeros_like(s_ref)
        decay_tables(dtr_ref[...])

    cum2 = cum2_ref[...]
    cum2_t = cum2t_ref[...]
    head_lane = _iota((rows, bw), 1) // SSM_HEADDIM

    for g in range(SSM_GROUPS):
        gsl = slice(g * bw, (g + 1) * bw)
        bm = bc_ref[:, g * SSM_DSTATE:(g + 1) * SSM_DSTATE]
        cm = bc_ref[:, (SSM_GROUPS + g) * SSM_DSTATE:(SSM_GROUPS + g + 1) * SSM_DSTATE]
        xs = xc_ref[:, gsl].astype(F32)
        s = s_ref[g]
        gmat = _dot_nt(cm, bm)
        ps = []
        for hh in range(HEADS_PER_GROUP):
            h = g * HEADS_PER_GROUP + hh
            colb = jnp.broadcast_to(cum2[:, h:h + 1], (rows, LANES))
            dec = jnp.exp2(jnp.where(causal, colb - cum2_t[h:h + 1, :], -jnp.inf))
            ps.append((gmat * dec).astype(BF16))
        v = (xs * ex_ref[2 * rows:3 * rows, gsl]).astype(BF16)
        y4 = _dot(jnp.concatenate(ps, axis=0), v)
        y = y4[0:rows]
        for hh in range(1, HEADS_PER_GROUP):
            y = jnp.where(head_lane == hh, y4[hh * rows:(hh + 1) * rows], y)
        y = y + _dot(cm, s.astype(BF16)) * ex_ref[0:rows, gsl]
        vw = (xs * ex_ref[rows:2 * rows, gsl]).astype(BF16)
        s_ref[g] = s * ex_ref[rows - 1:rows, gsl] + _dot_tn(bm, vw)

        y = y + xs * dsk_ref[:, gsl]
        gg = y * _silu(z_ref[:, gsl].astype(F32))
        ms = jnp.mean(gg * gg, axis=-1, keepdims=True)
        g_ref[:, gsl] = (gg * lax.rsqrt(ms + EPS) * gn_ref[:, gsl]).astype(BF16)

    decay_tables(dtr_next_ref[...])

    @pl.when(c == pl.num_programs(1) - 1)
    def _():
        for g in range(SSM_GROUPS):
            st_ref[g * bw:(g + 1) * bw, :] = s_ref[g].T


def _ssd_prompt(zxc, dtr, dtb, alog, dsk, gn, expand, batch, seq):
    rows = SSD_CHUNK
    nc = seq // rows
    full = lambda shape: pl.BlockSpec(shape, lambda b, c: (0,) * len(shape))
    col = lambda k: pl.BlockSpec((rows, SSM_DINNER), lambda b, c: (b * nc + c, k))
    return pl.pallas_call(
        _ssd_prompt_kernel,
        grid=(batch, nc),
        in_specs=[
            col(0), col(1), col(2),
            pl.BlockSpec((rows, LANES), lambda b, c: (b * nc + c, 0)),
            pl.BlockSpec((rows, LANES), lambda b, c: (b * nc + jnp.minimum(c + 1, nc - 1), 0)),
            full((1, LANES)), full((1, LANES)),
            full((1, SSM_DINNER)), full((1, SSM_DINNER)),
            full((2 * LANES, SSM_DINNER)),
        ],
        out_specs=[
            pl.BlockSpec((rows, SSM_DINNER), lambda b, c: (b * nc + c, 0)),
            pl.BlockSpec((None, SSM_DINNER, SSM_DSTATE), lambda b, c: (b, 0, 0)),
        ],
        out_shape=[
            jax.ShapeDtypeStruct((batch * seq, SSM_DINNER), BF16),
            jax.ShapeDtypeStruct((batch, SSM_DINNER, SSM_DSTATE), F32),
        ],
        scratch_shapes=[
            pltpu.VMEM((SSM_GROUPS, SSM_DSTATE, SSM_GROUP_W), F32),
            pltpu.VMEM((3 * rows, SSM_DINNER), F32),
            pltpu.VMEM((rows, LANES), F32),
            pltpu.VMEM((LANES, rows), F32),
        ],
        compiler_params=_cparams(("arbitrary", "arbitrary")),
        name="ssd_prompt",
    )(zxc, zxc, zxc, dtr, dtr, dtb, alog, dsk, gn, expand)


def _ssd_sample_kernel(z_ref, xa_ref, xb_ref, dtr_ref, conv_in_ref, st_in_ref,
                       cw_ref, cb_ref, dtb_ref, alog_ref, dsk_ref, gn_ref, exp_ref,
                       g_ref, st_out_ref, xpad_ref, xc_ref, ex_ref, y_ref, *, seq):
    bb = st_in_ref.shape[0]
    rows = bb * seq
    pad = SUBLANES
    keep = SSM_CONV - 1
    bw = SSM_GROUP_W

    for n in range(bb):
        r0 = n * seq
        xpad_ref[pad - keep:pad, :] = conv_in_ref[n]
        xpad_ref[pad:pad + seq, 0:SSM_DINNER] = xa_ref[r0:r0 + seq, :]
        xpad_ref[pad:pad + seq, SSM_DINNER:SSM_CONVDIM] = xb_ref[r0:r0 + seq, :]
        acc = cb_ref[...]
        for w in range(SSM_CONV):
            lo = pad - keep + w
            acc = acc + xpad_ref[lo:lo + seq, :] * cw_ref[w:w + 1, :]
        xc_ref[r0:r0 + seq, :] = _silu(acc)

    dt = _softplus(dtr_ref[...] + dtb_ref[...])
    la = dt * (-jnp.exp(alog_ref[...]))
    ri = _iota((rows, rows), 0)
    rj = _iota((rows, rows), 1)
    same = (ri // seq) == (rj // seq)
    cum = jnp.dot((same & (ri >= rj)).astype(F32), la, precision=lax.Precision.HIGHEST,
                  preferred_element_type=F32)
    clast = jnp.dot(same.astype(F32), la, precision=lax.Precision.HIGHEST, preferred_element_type=F32)
    eclast = jnp.exp(clast)
    ex_ref[...] = _head_expand([jnp.exp(cum), jnp.exp(clast - cum) * dt, dt], exp_ref[...])

    si = _iota((seq, seq), 0)
    sj = _iota((seq, seq), 1)
    causal = si >= sj
    eye = si == sj
    head_lane = _iota((seq, bw), 1) // SSM_HEADDIM

    for n in range(bb):
        r0 = n * seq
        rsl = slice(r0, r0 + seq)
        cum_n = cum[rsl]
        for g in range(SSM_GROUPS):
            gsl = slice(g * bw, (g + 1) * bw)
            bm = xc_ref[rsl, SSM_DINNER + g * SSM_DSTATE:SSM_DINNER + (g + 1) * SSM_DSTATE]
            cm = xc_ref[rsl, SSM_DINNER + (SSM_GROUPS + g) * SSM_DSTATE:
                        SSM_DINNER + (SSM_GROUPS + g + 1) * SSM_DSTATE]
            xs = xc_ref[rsl, gsl]
            gmat = _dot_nt(cm.astype(BF16), bm.astype(BF16))
            ps = []
            for hh in range(HEADS_PER_GROUP):
                h = g * HEADS_PER_GROUP + hh
                ccol = cum_n[:, h:h + 1]
                crow = jnp.sum(jnp.where(eye, ccol, 0.0), axis=0, keepdims=True)
                ps.append(gmat * jnp.exp(jnp.where(causal, ccol - crow, -jnp.inf)))
            v = xs * ex_ref[2 * rows + r0:2 * rows + r0 + seq, gsl]
            y4 = _dot(jnp.concatenate(ps, axis=0).astype(BF16), v.astype(BF16))
            y = y4[0:seq]
            for hh in range(1, HEADS_PER_GROUP):
                y = jnp.where(head_lane == hh, y4[hh * seq:(hh + 1) * seq], y)
            st = st_in_ref[n, gsl, :]
            y = y + _dot_nt(cm.astype(BF16), st.astype(BF16)) * ex_ref[rsl, gsl]
            y_ref[rsl, gsl] = y
            vw = xs * ex_ref[rows + r0:rows + r0 + seq, gsl]
            upd = _dot_tn(vw.astype(BF16), bm.astype(BF16))
            for hh in range(HEADS_PER_GROUP):
                h = g * HEADS_PER_GROUP + hh
                hsl = slice(hh * SSM_HEADDIM, (hh + 1) * SSM_HEADDIM)
                osl = slice(g * bw + hh * SSM_HEADDIM, g * bw + (hh + 1) * SSM_HEADDIM)
                st_out_ref[n, osl, :] = st[hsl] * eclast[r0:r0 + 1, h:h + 1] + upd[hsl]

    y = y_ref[...] + xc_ref[:, 0:SSM_DINNER] * dsk_ref[...]
    gg = y * _silu(z_ref[...])
    for g in range(SSM_GROUPS):
        gsl = slice(g * bw, (g + 1) * bw)
        blk = gg[:, gsl]
        ms = jnp.mean(blk * blk, axis=-1, keepdims=True)
        g_ref[:, gsl] = (blk * lax.rsqrt(ms + EPS) * gn_ref[:, gsl]).astype(BF16)


def _ssd_sample(zx, dtr, conv_state, ssm_state_t, cw, cb, dtb, alog, dsk, gn, expand, batch, seq):
    bb = SSD_SAMPLE_BB
    rows = bb * seq
    keep = SSM_CONV - 1
    full = lambda shape: pl.BlockSpec(shape, lambda n: (0,) * len(shape))
    col = lambda k: pl.BlockSpec((rows, SSM_DINNER), lambda n: (n, k))
    st_spec = pl.BlockSpec((bb, SSM_DINNER, SSM_DSTATE), lambda n: (n, 0, 0))
    return pl.pallas_call(
        functools.partial(_ssd_sample_kernel, seq=seq),
        grid=(batch // bb,),
        in_specs=[
            col(0), col(1), col(2),
            pl.BlockSpec((rows, LANES), lambda n: (n, 0)),
            pl.BlockSpec((None, bb, keep, SSM_CONVDIM), lambda n: (0, n, 0, 0)),
            st_spec,
            full((SSM_CONV, SSM_CONVDIM)), full((1, SSM_CONVDIM)),
            full((1, LANES)), full((1, LANES)),
            full((1, SSM_DINNER)), full((1, SSM_DINNER)),
            full((2 * LANES, SSM_DINNER)),
        ],
        out_specs=[
            pl.BlockSpec((rows, SSM_DINNER), lambda n: (n, 0)),
            st_spec,
        ],
        out_shape=[
            jax.ShapeDtypeStruct((batch * seq, SSM_DINNER), BF16),
            jax.ShapeDtypeStruct(ssm_state_t.shape, F32),
        ],
        scratch_shapes=[
            pltpu.VMEM((2 * SUBLANES, SSM_CONVDIM), F32),
            pltpu.VMEM((rows, SSM_CONVDIM), F32),
            pltpu.VMEM((3 * rows, SSM_DINNER), F32),
            pltpu.VMEM((rows, SSM_DINNER), F32),
        ],
        compiler_params=_cparams(("arbitrary",)),
        name="ssd_sample",
    )(zx, zx, zx, dtr, conv_state, ssm_state_t, cw, cb, dtb, alog, dsk, gn, expand)


def _rope_tables(pos):
    half = RET_DK // 2
    freqs = ROPE_BASE ** (-jnp.arange(half, dtype=F32) / half)
    ang = pos.astype(F32)[:, None] * freqs[None, :]
    return jnp.cos(ang), jnp.sin(ang)


def _row(v):
    return v.reshape(1, -1).astype(F32)


def _pad_lanes(v):
    return jnp.pad(v.astype(F32), (0, LANES - v.shape[0])).reshape(1, LANES)


def kernel(x_prompt, x_sample, state_ret, state_ssm, state_conv, ret_norm, ret_w_in, ret_head_norm, ret_w_out, ssm_norm, ssm_w_in, ssm_conv_w, ssm_conv_b, ssm_dt_bias, ssm_a_log, ssm_d, ssm_gate_norm, ssm_w_out, final_norm):
    bp, lp, d = x_prompt.shape
    bs, ls, _ = x_sample.shape
    assert ret_norm.shape[0] == 1 and ssm_norm.shape[0] == 1, "one retention and one SSD layer"

    ret_win = ret_w_in[0].astype(BF16)
    ret_wout = ret_w_out[0].astype(BF16)
    n_main = SSM_DINNER + SSM_CONVDIM
    ssm_win = ssm_w_in[0, :, :n_main].astype(BF16)
    ssm_wdt = jnp.pad(ssm_w_in[0, :, n_main:], ((0, 0), (0, LANES - SSM_HEADS))).astype(BF16)
    ssm_wout = ssm_w_out[0].astype(BF16)
    expand = (jnp.arange(2 * LANES)[:, None] % LANES == jnp.arange(SSM_DINNER)[None, :] // SSM_HEADDIM).astype(BF16)
    lg = jnp.log(1.0 - 2.0 ** (-5.0 - jnp.arange(RET_HEADS, dtype=F32)))
    lg = jnp.broadcast_to(lg[:, None, None], (RET_HEADS, 1, LANES))
    hn = _row(ret_head_norm[0])
    dsk = _row(jnp.repeat(ssm_d[0], SSM_HEADDIM))
    gn = _row(ssm_gate_norm[0])
    cw = ssm_conv_w[0].astype(F32)
    cb = _row(ssm_conv_b[0])
    dtb = _pad_lanes(ssm_dt_bias[0])
    alog = _pad_lanes(ssm_a_log[0])
    g_ret = _row(ret_norm[0])
    g_ssm = _row(ssm_norm[0])
    g_fin = _row(final_norm)

    xp = x_prompt.reshape(bp * lp, d)
    cos_p, sin_p = _rope_tables(jnp.arange(lp, dtype=jnp.int32))
    proj = _norm_proj(xp, g_ret, ret_win, tm=1024, tn=2048, out_dtype=BF16)
    og, ret_p = _ret_prompt(proj, cos_p, sin_p, lg, hn, bp, lp)
    h1 = _out_proj(og, ret_wout, xp, tm=512)
    tm_c = 1024
    zxc, dtr, tail = _norm_proj_conv(h1, g_ssm, ssm_win, ssm_wdt, cw, cb, tm=tm_c, tn=1024, seq=lp)
    tail = tail[lp // tm_c - 1::lp // tm_c]
    gated, ssm_pt = _ssd_prompt(zxc, dtr, dtb, alog, dsk, gn, expand, bp, lp)
    y_p = _out_proj(gated, ssm_wout, h1, tm=512, final_g=g_fin)
    ssm_p = jnp.swapaxes(ssm_pt.reshape(1, bp, SSM_HEADS, SSM_HEADDIM, SSM_DSTATE), -1, -2)
    conv_p = tail[None, :, SUBLANES - (SSM_CONV - 1):, :]

    xs = x_sample.reshape(bs * ls, d)
    cos_s, sin_s = _rope_tables(PAST_LEN + jnp.arange(ls, dtype=jnp.int32))
    proj_s = _norm_proj(xs, g_ret, ret_win, tm=bs * ls, tn=2048, out_dtype=BF16)
    og_s, ret_s = _ret_sample(proj_s, cos_s, sin_s, lg, hn, state_ret, bs, ls)
    h1_s = _out_proj(og_s, ret_wout, xs, tm=512)
    zx_s, dtr_s = _norm_proj(h1_s, g_ssm, ssm_win, tm=bs * ls, tn=2048, out_dtype=F32, w_dt=ssm_wdt)
    st_t = jnp.swapaxes(state_ssm[0], -1, -2).reshape(bs, SSM_DINNER, SSM_DSTATE)
    gated_s, ssm_st = _ssd_sample(zx_s, dtr_s, state_conv, st_t, cw, cb, dtb, alog, dsk, gn, expand, bs, ls)
    ssm_s = jnp.swapaxes(ssm_st.reshape(1, bs, SSM_HEADS, SSM_HEADDIM, SSM_DSTATE), -1, -2)
    keep = SSM_CONV - 1
    conv_s = zx_s.reshape(bs, ls, -1)[None, :, ls - keep:, SSM_DINNER:]
    y_s = _out_proj(gated_s, ssm_wout, h1_s, tm=512, final_g=g_fin)

    return (y_p.reshape(bp, lp, d), y_s.reshape(bs, ls, d), ret_p, ret_s, ssm_p, ssm_s, conv_p, conv_s)
```

```python
import functools
import math

import jax
import jax.numpy as jnp
from jax import lax
from jax.experimental import pallas as pl
from jax.experimental.pallas import tpu as pltpu

F32 = jnp.float32
BF16 = jnp.bfloat16

D_MODEL = 1024
RET_HEADS = 4
RET_DK = 256
RET_DV = 512
RET_VW = RET_HEADS * RET_DV
RET_QK = RET_HEADS * RET_DK
SSM_DINNER = 2048
SSM_HEADDIM = 64
SSM_HEADS = 32
SSM_GROUPS = 8
SSM_DSTATE = 128
SSM_CONV = 4
SSM_CONVDIM = 4096
SSM_GROUP_W = SSM_DINNER // SSM_GROUPS
HEADS_PER_GROUP = SSM_HEADS // SSM_GROUPS
ROPE_BASE = 10000.0
EPS = 1e-6
PAST_LEN = 16384

LANES = 128
SUBLANES = 8
VMEM_LIMIT = 56 * 1024 * 1024

RET_CHUNK = 256
SSD_CHUNK = 128
SSD_PROMPT_SEQS = 2
RET_SAMPLE_BB = 8
SSD_SAMPLE_BB = 4


def _cparams(sem, flags=None):
    return pltpu.CompilerParams(dimension_semantics=sem, vmem_limit_bytes=VMEM_LIMIT, flags=flags)


def _sigmoid(x):
    return 1.0 / (1.0 + jnp.exp2(x * (-math.log2(math.e))))


def _silu(x):
    return x * _sigmoid(x)


def _softplus(x):
    return jnp.maximum(x, 0.0) + jnp.log1p(jnp.exp(-jnp.abs(x)))


def _dot(a, b):
    return jnp.dot(a, b, preferred_element_type=F32)


def _dot_nt(a, b):
    return lax.dot_general(a, b, (((1,), (1,)), ((), ())), preferred_element_type=F32)


def _dot_tn(a, b):
    return lax.dot_general(a, b, (((0,), (0,)), ((), ())), preferred_element_type=F32)


def _iota(shape, dim):
    return lax.broadcasted_iota(jnp.int32, shape, dim)


def _norm_proj_kernel(x_ref, g_ref, w_ref, *rest, has_dt):
    if has_dt:
        wdt_ref, o_ref, dt_ref, xn_ref = rest
    else:
        o_ref, xn_ref = rest

    @pl.when(pl.program_id(1) == 0)
    def _():
        x = x_ref[...]
        ms = jnp.mean(x * x, axis=-1, keepdims=True)
        xn = (x * lax.rsqrt(ms + EPS) * g_ref[...]).astype(BF16)
        xn_ref[...] = xn
        if has_dt:
            dt_ref[...] = _dot(xn, wdt_ref[...])

    o_ref[...] = _dot(xn_ref[...], w_ref[...]).astype(o_ref.dtype)


def _norm_proj(x, g, w, *, tm, tn, out_dtype, w_dt=None):
    m, d = x.shape
    n = w.shape[1]
    has_dt = w_dt is not None
    in_specs = [
        pl.BlockSpec((tm, d), lambda i, j: (i, 0)),
        pl.BlockSpec((1, d), lambda i, j: (0, 0)),
        pl.BlockSpec((d, tn), lambda i, j: (0, j)),
    ]
    args = [x, g, w]
    out_shape = [jax.ShapeDtypeStruct((m, n), out_dtype)]
    out_specs = [pl.BlockSpec((tm, tn), lambda i, j: (i, j))]
    if has_dt:
        in_specs.append(pl.BlockSpec((d, LANES), lambda i, j: (0, 0)))
        args.append(w_dt)
        out_shape.append(jax.ShapeDtypeStruct((m, LANES), F32))
        out_specs.append(pl.BlockSpec((tm, LANES), lambda i, j: (i, 0)))
    res = pl.pallas_call(
        functools.partial(_norm_proj_kernel, has_dt=has_dt),
        grid=(m // tm, n // tn),
        in_specs=in_specs,
        out_specs=out_specs,
        out_shape=out_shape,
        scratch_shapes=[pltpu.VMEM((tm, d), BF16)],
        compiler_params=_cparams(("arbitrary", "arbitrary")),
        name="in_proj",
    )(*args)
    return res if has_dt else res[0]


def _shift_rows(a, first_rows):
    rows, cols = a.shape
    n = first_rows.shape[0]
    slabs = rows // SUBLANES
    rot = pltpu.roll(a.reshape(slabs, SUBLANES, cols), n, axis=1)
    first = jnp.concatenate([first_rows, jnp.zeros((SUBLANES - n, cols), a.dtype)], axis=0)
    above = jnp.concatenate([first.reshape(1, SUBLANES, cols), rot[:slabs - 1]], axis=0)
    top = _iota((slabs, SUBLANES, cols), 1) < n
    return jnp.where(top, above, rot).reshape(rows, cols)


def _norm_proj_conv_kernel(x_ref, g_ref, w_ref, wdt_ref, cw_ref, cb_ref, o_ref, dt_ref, tail_ref,
                           xn_ref, carry_ref, *, tiles_per_seq, z_tiles, strip):
    i = pl.program_id(0)
    j = pl.program_id(1)
    tm, tn = o_ref.shape

    @pl.when(j == 0)
    def _():
        x = x_ref[...]
        ms = jnp.mean(x * x, axis=-1, keepdims=True)
        xn = (x * lax.rsqrt(ms + EPS) * g_ref[...]).astype(BF16)
        xn_ref[...] = xn
        dt_ref[...] = _dot(xn, wdt_ref[...])

    @pl.when((j == 0) & (i % tiles_per_seq == 0))
    def _():
        carry_ref[...] = jnp.zeros_like(carry_ref)

    @pl.when(j < z_tiles)
    def _():
        o_ref[...] = _dot(xn_ref[...], w_ref[...]).astype(o_ref.dtype)

    @pl.when(j >= z_tiles)
    def _():
        jc = j - z_tiles
        for s in range(tn // strip):
            sl = slice(s * strip, (s + 1) * strip)
            x = _dot(xn_ref[...], w_ref[:, sl])
            tail_ref[:, sl] = x[tm - SUBLANES:tm, :]
            x2 = _shift_rows(x, carry_ref[jc, 1:3, sl])
            p = x * cw_ref[2:3, sl] + x2 * cw_ref[0:1, sl]
            out = x * cw_ref[3:4, sl] + x2 * cw_ref[1:2, sl] + _shift_rows(p, carry_ref[jc, 0:1, sl]) + cb_ref[:, sl]
            o_ref[:, sl] = _silu(out).astype(o_ref.dtype)
            carry_ref[jc, 0:1, sl] = p[tm - 1:tm]
            carry_ref[jc, 1:3, sl] = x[tm - 2:tm]


def _norm_proj_conv(x, g, w, w_dt, cw, cb, *, tm, tn, seq):
    m, d = x.shape
    n = w.shape[1]
    z_tiles = SSM_DINNER // tn
    c_tiles = SSM_CONVDIM // tn
    tiles_per_seq = seq // tm
    cj = lambda j: jnp.maximum(j - z_tiles, 0)
    return pl.pallas_call(
        functools.partial(_norm_proj_conv_kernel, tiles_per_seq=tiles_per_seq, z_tiles=z_tiles, strip=256),
        grid=(m // tm, n // tn),
        in_specs=[
            pl.BlockSpec((tm, d), lambda i, j: (i, 0)),
            pl.BlockSpec((1, d), lambda i, j: (0, 0)),
            pl.BlockSpec((d, tn), lambda i, j: (0, j)),
            pl.BlockSpec((d, LANES), lambda i, j: (0, 0)),
            pl.BlockSpec((SSM_CONV, tn), lambda i, j: (0, cj(j))),
            pl.BlockSpec((1, tn), lambda i, j: (0, cj(j))),
        ],
        out_specs=[
            pl.BlockSpec((tm, tn), lambda i, j: (i, j)),
            pl.BlockSpec((tm, LANES), lambda i, j: (i, 0)),
            pl.BlockSpec((None, SUBLANES, tn), lambda i, j: (i, 0, cj(j))),
        ],
        out_shape=[
            jax.ShapeDtypeStruct((m, n), BF16),
            jax.ShapeDtypeStruct((m, LANES), F32),
            jax.ShapeDtypeStruct((m // tm, SUBLANES, SSM_CONVDIM), F32),
        ],
        scratch_shapes=[
            pltpu.VMEM((tm, d), BF16),
            pltpu.VMEM((c_tiles, SUBLANES, tn), F32),
        ],
        compiler_params=_cparams(("arbitrary", "arbitrary")),
        name="ssd_in_proj_conv",
    )(x, g, w, w_dt, cw, cb)


def _out_proj_kernel(a_ref, w_ref, h_ref, *rest, has_norm):
    if has_norm:
        g_ref, o_ref = rest
    else:
        (o_ref,) = rest
    y = h_ref[...] + _dot(a_ref[...], w_ref[...])
    if has_norm:
        ms = jnp.mean(y * y, axis=-1, keepdims=True)
        y = y * lax.rsqrt(ms + EPS) * g_ref[...]
    o_ref[...] = y


def _out_proj(a, w, h, *, tm, final_g=None):
    m, k = a.shape
    d = w.shape[1]
    has_norm = final_g is not None
    in_specs = [
        pl.BlockSpec((tm, k), lambda i: (i, 0)),
        pl.BlockSpec((k, d), lambda i: (0, 0)),
        pl.BlockSpec((tm, d), lambda i: (i, 0)),
    ]
    args = [a, w, h]
    if has_norm:
        in_specs.append(pl.BlockSpec((1, d), lambda i: (0, 0)))
        args.append(final_g)
    return pl.pallas_call(
        functools.partial(_out_proj_kernel, has_norm=has_norm),
        grid=(m // tm,),
        in_specs=in_specs,
        out_specs=pl.BlockSpec((tm, d), lambda i: (i, 0)),
        out_shape=jax.ShapeDtypeStruct((m, d), F32),
        compiler_params=_cparams(("arbitrary",)),
        name="out_proj",
    )(*args)


def _rope(x, cos, sin):
    half = RET_DK // 2
    x1 = x[:, :half]
    x2 = x[:, half:]
    return jnp.concatenate([x1 * cos - x2 * sin, x1 * sin + x2 * cos], axis=1)


def _head_norm_gate(o, hn, gate):
    ms = jnp.mean(o * o, axis=-1, keepdims=True)
    on = o * lax.rsqrt(ms + EPS) * hn
    return (on * _silu(gate)).astype(BF16)


def _ret_prompt_kernel(q_ref, k_ref, v_ref, gate_ref, cos_ref, sin_ref, lg_ref, hn_ref,
                       og_ref, st_ref, s_ref, dec_ref, qsc_ref, ksc_ref, gc_ref):
    b = pl.program_id(0)
    c = pl.program_id(1)
    rows = q_ref.shape[0]

    @pl.when((b == 0) & (c == 0))
    def _():
        i = _iota((rows, rows), 0)
        j = _iota((rows, rows), 1)
        r = _iota((rows, RET_DK), 0)
        scale = RET_DK ** -0.5
        for h in range(RET_HEADS):
            lg = lg_ref[h, :, 0:1]
            dec_ref[h] = jnp.where(i >= j, jnp.exp((i - j).astype(F32) * lg), 0.0) * scale
            qsc_ref[h] = jnp.exp((r + 1).astype(F32) * lg)
            ksc_ref[h] = jnp.exp((rows - 1 - r).astype(F32) * lg) * scale
            gc_ref[h] = jnp.exp(jnp.full((1, RET_DV), float(rows), F32) * lg)

    @pl.when(c == 0)
    def _():
        s_ref[...] = jnp.zeros_like(s_ref)

    cos = cos_ref[...]
    sin = sin_ref[...]
    for h in range(RET_HEADS):
        ksl = slice(h * RET_DK, (h + 1) * RET_DK)
        vsl = slice(h * RET_DV, (h + 1) * RET_DV)
        qr = _rope(q_ref[:, ksl].astype(F32), cos, sin)
        kr = _rope(k_ref[:, ksl].astype(F32), cos, sin)
        v = v_ref[:, vsl]
        p = (_dot_nt(qr.astype(BF16), kr.astype(BF16)) * dec_ref[h]).astype(BF16)
        qs = (qr * qsc_ref[h]).astype(BF16)
        s = s_ref[h]
        o = _dot(p, v) + _dot(qs, s.astype(BF16))
        kt = (kr * ksc_ref[h]).T.astype(BF16)
        s_ref[h] = s * gc_ref[h] + _dot(kt, v)
        og_ref[:, vsl] = _head_norm_gate(o, hn_ref[:, vsl], gate_ref[:, vsl].astype(F32))

    @pl.when(c == pl.num_programs(1) - 1)
    def _():
        st_ref[...] = s_ref[...]


def _ret_prompt(proj, cos, sin, lg, hn, batch, seq):
    rows = RET_CHUNK
    nc = seq // rows
    return pl.pallas_call(
        _ret_prompt_kernel,
        grid=(batch, nc),
        in_specs=[
            pl.BlockSpec((rows, RET_QK), lambda b, c: (b * nc + c, 0)),
            pl.BlockSpec((rows, RET_QK), lambda b, c: (b * nc + c, 1)),
            pl.BlockSpec((rows, RET_VW), lambda b, c: (b * nc + c, 2 * RET_QK // RET_VW)),
            pl.BlockSpec((rows, RET_VW), lambda b, c: (b * nc + c, 2 * RET_QK // RET_VW + 1)),
            pl.BlockSpec((rows, RET_DK // 2), lambda b, c: (c, 0)),
            pl.BlockSpec((rows, RET_DK // 2), lambda b, c: (c, 0)),
            pl.BlockSpec((RET_HEADS, 1, LANES), lambda b, c: (0, 0, 0)),
            pl.BlockSpec((1, RET_VW), lambda b, c: (0, 0)),
        ],
        out_specs=[
            pl.BlockSpec((rows, RET_VW), lambda b, c: (b * nc + c, 0)),
            pl.BlockSpec((None, None, RET_HEADS, RET_DK, RET_DV), lambda b, c: (0, b, 0, 0, 0)),
        ],
        out_shape=[
            jax.ShapeDtypeStruct((batch * seq, RET_VW), BF16),
            jax.ShapeDtypeStruct((1, batch, RET_HEADS, RET_DK, RET_DV), F32),
        ],
        scratch_shapes=[
            pltpu.VMEM((RET_HEADS, RET_DK, RET_DV), F32),
            pltpu.VMEM((RET_HEADS, rows, rows), F32),
            pltpu.VMEM((RET_HEADS, rows, RET_DK), F32),
            pltpu.VMEM((RET_HEADS, rows, RET_DK), F32),
            pltpu.VMEM((RET_HEADS, 1, RET_DV), F32),
        ],
        compiler_params=_cparams(("arbitrary", "arbitrary")),
        name="ret_prompt",
    )(proj, proj, proj, proj, cos, sin, lg, hn)


def _ret_sample_kernel(q_ref, k_ref, v_ref, gate_ref, cos_ref, sin_ref, lg_ref, hn_ref, st_in_ref,
                       og_ref, st_out_ref, *, seq):
    bb = st_in_ref.shape[0]
    rows = bb * seq
    lg = lg_ref[:, 0:1]
    scale = RET_DK ** -0.5
    i = _iota((rows, rows), 0)
    j = _iota((rows, rows), 1)
    same = (i // seq) == (j // seq)
    dec = jnp.where(same & (i >= j), jnp.exp((i - j).astype(F32) * lg), 0.0) * scale
    pos = _iota((rows, RET_DK), 0) % seq
    qsc = jnp.exp((pos + 1).astype(F32) * lg)
    ksc = jnp.exp((seq - 1 - pos).astype(F32) * lg) * scale
    gc = jnp.exp(jnp.full((1, RET_DV), float(seq), F32) * lg)

    cos = jnp.concatenate([cos_ref[...]] * bb, axis=0)
    sin = jnp.concatenate([sin_ref[...]] * bb, axis=0)
    qr = _rope(q_ref[...].astype(F32), cos, sin)
    kr = _rope(k_ref[...].astype(F32), cos, sin)
    v = v_ref[...].astype(F32)
    p = _dot_nt(qr.astype(BF16), kr.astype(BF16)) * dec
    intra = _dot(p.astype(BF16), v.astype(BF16))
    qs = qr * qsc
    kt = kr * ksc
    vrow = _iota((rows, RET_DV), 0) // seq
    cross = []
    for n in range(bb):
        s0 = st_in_ref[n]
        cross.append(_dot(qs[n * seq:(n + 1) * seq], s0))
        vn = jnp.where(vrow == n, v, 0.0)
        st_out_ref[n] = s0 * gc + _dot_tn(kt.astype(BF16), vn.astype(BF16))
    o = intra + jnp.concatenate(cross, axis=0)
    og_ref[...] = _head_norm_gate(o, hn_ref[...], gate_ref[...].astype(F32))


def _ret_sample(proj, cos, sin, lg, hn, state, batch, seq):
    bb = RET_SAMPLE_BB
    rows = bb * seq
    qb = RET_QK // RET_DK
    vb = 2 * RET_QK // RET_DV
    gb = vb + RET_VW // RET_DV
    st_spec = pl.BlockSpec((None, bb, None, RET_DK, RET_DV), lambda n, h: (0, n, h, 0, 0))
    return pl.pallas_call(
        functools.partial(_ret_sample_kernel, seq=seq),
        grid=(batch // bb, RET_HEADS),
        in_specs=[
            pl.BlockSpec((rows, RET_DK), lambda n, h: (n, h)),
            pl.BlockSpec((rows, RET_DK), lambda n, h: (n, qb + h)),
            pl.BlockSpec((rows, RET_DV), lambda n, h: (n, vb + h)),
            pl.BlockSpec((rows, RET_DV), lambda n, h: (n, gb + h)),
            pl.BlockSpec((seq, RET_DK // 2), lambda n, h: (0, 0)),
            pl.BlockSpec((seq, RET_DK // 2), lambda n, h: (0, 0)),
            pl.BlockSpec((None, 1, LANES), lambda n, h: (h, 0, 0)),
            pl.BlockSpec((1, RET_DV), lambda n, h: (0, h)),
            st_spec,
        ],
        out_specs=[
            pl.BlockSpec((rows, RET_DV), lambda n, h: (n, h)),
            st_spec,
        ],
        out_shape=[
            jax.ShapeDtypeStruct((batch * seq, RET_VW), BF16),
            jax.ShapeDtypeStruct(state.shape, F32),
        ],
        compiler_params=_cparams(("arbitrary", "arbitrary")),
        name="ret_sample",
    )(proj, proj, proj, proj, cos, sin, lg, hn, state)


def _head_expand(vals, expand):
    parts = []
    for v in vals:
        hi = v.astype(BF16)
        lo = (v - hi.astype(F32)).astype(BF16)
        parts.append(jnp.concatenate([hi, lo], axis=1))
    return _dot(jnp.concatenate(parts, axis=0), expand)


def _ssd_prompt_kernel(z_ref, xc_ref, bc_ref, dtr_ref, dtr_next_ref, dtb_ref, alog_ref, dsk_ref, gn_ref, exp_ref,
                       g_ref, st_ref, s_ref, ex_ref, cum2_ref, cum2t_ref):
    c = pl.program_id(1)
    n_seq, rows = z_ref.shape[0], z_ref.shape[1]
    bw = SSM_GROUP_W
    ri = _iota((rows, rows), 0)
    rj = _iota((rows, rows), 1)
    causal = ri >= rj

    def decay_tables(n, dtr):
        dt = _softplus(dtr + dtb_ref[...])
        la = dt * (-jnp.exp(alog_ref[...]))
        cum = jnp.dot(causal.astype(F32), la, precision=lax.Precision.HIGHEST, preferred_element_type=F32)
        ecum = jnp.exp(cum)
        wdt = jnp.exp(cum[rows - 1:rows, :] - cum) * dt
        ex_ref[n] = _head_expand([ecum, wdt, dt], exp_ref[...])
        cum2 = cum * math.log2(math.e)
        cum2_ref[n] = cum2
        cum2t_ref[n] = cum2.T

    @pl.when(c == 0)
    def _():
        s_ref[...] = jnp.zeros_like(s_ref)
        for n in range(n_seq):
            decay_tables(n, dtr_ref[n])

    head_lane = _iota((rows, bw), 1) // SSM_HEADDIM

    for n in range(n_seq):
        cum2 = cum2_ref[n]
        cum2_t = cum2t_ref[n]
        for g in range(SSM_GROUPS):
            gsl = slice(g * bw, (g + 1) * bw)
            bm = bc_ref[n, :, g * SSM_DSTATE:(g + 1) * SSM_DSTATE]
            cm = bc_ref[n, :, (SSM_GROUPS + g) * SSM_DSTATE:(SSM_GROUPS + g + 1) * SSM_DSTATE]
            xs = xc_ref[n, :, gsl].astype(F32)
            s = s_ref[n, g]
            gmat = _dot_nt(cm, bm)
            ps = []
            for hh in range(HEADS_PER_GROUP):
                h = g * HEADS_PER_GROUP + hh
                colb = jnp.broadcast_to(cum2[:, h:h + 1], (rows, LANES))
                dec = jnp.exp2(jnp.where(causal, colb - cum2_t[h:h + 1, :], -jnp.inf))
                ps.append((gmat * dec).astype(BF16))
            v = (xs * ex_ref[n, 2 * rows:3 * rows, gsl]).astype(BF16)
            y4 = _dot(jnp.concatenate(ps, axis=0), v)
            y = y4[0:rows]
            for hh in range(1, HEADS_PER_GROUP):
                y = jnp.where(head_lane == hh, y4[hh * rows:(hh + 1) * rows], y)
            y = y + _dot(cm, s.astype(BF16)) * ex_ref[n, 0:rows, gsl]
            vw = (xs * ex_ref[n, rows:2 * rows, gsl]).astype(BF16)
            s_ref[n, g] = s * ex_ref[n, rows - 1:rows, gsl] + _dot_tn(bm, vw)

            y = y + xs * dsk_ref[:, gsl]
            gg = y * _silu(z_ref[n, :, gsl].astype(F32))
            ms = jnp.mean(gg * gg, axis=-1, keepdims=True)
            g_ref[n, :, gsl] = (gg * lax.rsqrt(ms + EPS) * gn_ref[:, gsl]).astype(BF16)

    for n in range(n_seq):
        decay_tables(n, dtr_next_ref[n])

    @pl.when(c == pl.num_programs(1) - 1)
    def _():
        for n in range(n_seq):
            for g in range(SSM_GROUPS):
                st_ref[n, g * bw:(g + 1) * bw, :] = s_ref[n, g].T


def _ssd_prompt(zxc, dtr, dtb, alog, dsk, gn, expand, batch, seq):
    rows = SSD_CHUNK
    nc = seq // rows
    ns = SSD_PROMPT_SEQS
    full = lambda shape: pl.BlockSpec(shape, lambda b, c: (0,) * len(shape))
    col = lambda k: pl.BlockSpec((ns, rows, SSM_DINNER), lambda b, c: (b, c, k))
    return pl.pallas_call(
        _ssd_prompt_kernel,
        grid=(batch // ns, nc),
        in_specs=[
            col(0), col(1), col(2),
            pl.BlockSpec((ns, rows, LANES), lambda b, c: (b, c, 0)),
            pl.BlockSpec((ns, rows, LANES), lambda b, c: (b, jnp.minimum(c + 1, nc - 1), 0)),
            full((1, LANES)), full((1, LANES)),
            full((1, SSM_DINNER)), full((1, SSM_DINNER)),
            full((2 * LANES, SSM_DINNER)),
        ],
        out_specs=[
            pl.BlockSpec((ns, rows, SSM_DINNER), lambda b, c: (b, c, 0)),
            pl.BlockSpec((ns, SSM_DINNER, SSM_DSTATE), lambda b, c: (b, 0, 0)),
        ],
        out_shape=[
            jax.ShapeDtypeStruct((batch, seq, SSM_DINNER), BF16),
            jax.ShapeDtypeStruct((batch, SSM_DINNER, SSM_DSTATE), F32),
        ],
        scratch_shapes=[
            pltpu.VMEM((ns, SSM_GROUPS, SSM_DSTATE, SSM_GROUP_W), F32),
            pltpu.VMEM((ns, 3 * rows, SSM_DINNER), F32),
            pltpu.VMEM((ns, rows, LANES), F32),
            pltpu.VMEM((ns, LANES, rows), F32),
        ],
        compiler_params=_cparams(("arbitrary", "arbitrary")),
        name="ssd_prompt",
    )(zxc, zxc, zxc, dtr, dtr, dtb, alog, dsk, gn, expand)


def _ssd_sample_kernel(z_ref, xa_ref, xb_ref, dtr_ref, conv_in_ref, st_in_ref,
                       cw_ref, cb_ref, dtb_ref, alog_ref, dsk_ref, gn_ref, exp_ref,
                       g_ref, st_out_ref, xpad_ref, xc_ref, ex_ref, y_ref, *, seq):
    bb = st_in_ref.shape[0]
    rows = bb * seq
    pad = SUBLANES
    keep = SSM_CONV - 1
    bw = SSM_GROUP_W

    for n in range(bb):
        r0 = n * seq
        xpad_ref[pad - keep:pad, :] = conv_in_ref[n]
        xpad_ref[pad:pad + seq, 0:SSM_DINNER] = xa_ref[r0:r0 + seq, :]
        xpad_ref[pad:pad + seq, SSM_DINNER:SSM_CONVDIM] = xb_ref[r0:r0 + seq, :]
        acc = cb_ref[...]
        for w in range(SSM_CONV):
            lo = pad - keep + w
            acc = acc + xpad_ref[lo:lo + seq, :] * cw_ref[w:w + 1, :]
        xc_ref[r0:r0 + seq, :] = _silu(acc)

    dt = _softplus(dtr_ref[...] + dtb_ref[...])
    la = dt * (-jnp.exp(alog_ref[...]))
    ri = _iota((rows, rows), 0)
    rj = _iota((rows, rows), 1)
    same = (ri // seq) == (rj // seq)
    cum = jnp.dot((same & (ri >= rj)).astype(F32), la, precision=lax.Precision.HIGHEST,
                  preferred_element_type=F32)
    clast = jnp.dot(same.astype(F32), la, precision=lax.Precision.HIGHEST, preferred_element_type=F32)
    eclast = jnp.exp(clast)
    ex_ref[...] = _head_expand([jnp.exp(cum), jnp.exp(clast - cum) * dt, dt], exp_ref[...])

    si = _iota((seq, seq), 0)
    sj = _iota((seq, seq), 1)
    causal = si >= sj
    eye = si == sj
    head_lane = _iota((seq, bw), 1) // SSM_HEADDIM

    for n in range(bb):
        r0 = n * seq
        rsl = slice(r0, r0 + seq)
        cum_n = cum[rsl]
        for g in range(SSM_GROUPS):
            gsl = slice(g * bw, (g + 1) * bw)
            bm = xc_ref[rsl, SSM_DINNER + g * SSM_DSTATE:SSM_DINNER + (g + 1) * SSM_DSTATE]
            cm = xc_ref[rsl, SSM_DINNER + (SSM_GROUPS + g) * SSM_DSTATE:
                        SSM_DINNER + (SSM_GROUPS + g + 1) * SSM_DSTATE]
            xs = xc_ref[rsl, gsl]
            gmat = _dot_nt(cm.astype(BF16), bm.astype(BF16))
            ps = []
            for hh in range(HEADS_PER_GROUP):
                h = g * HEADS_PER_GROUP + hh
                ccol = cum_n[:, h:h + 1]
                crow = jnp.sum(jnp.where(eye, ccol, 0.0), axis=0, keepdims=True)
                ps.append(gmat * jnp.exp(jnp.where(causal, ccol - crow, -jnp.inf)))
            v = xs * ex_ref[2 * rows + r0:2 * rows + r0 + seq, gsl]
            y4 = _dot(jnp.concatenate(ps, axis=0).astype(BF16), v.astype(BF16))
            y = y4[0:seq]
            for hh in range(1, HEADS_PER_GROUP):
                y = jnp.where(head_lane == hh, y4[hh * seq:(hh + 1) * seq], y)
            st = st_in_ref[n, gsl, :]
            y = y + _dot_nt(cm.astype(BF16), st.astype(BF16)) * ex_ref[rsl, gsl]
            y_ref[rsl, gsl] = y
            vw = xs * ex_ref[rows + r0:rows + r0 + seq, gsl]
            upd = _dot_tn(vw.astype(BF16), bm.astype(BF16))
            for hh in range(HEADS_PER_GROUP):
                h = g * HEADS_PER_GROUP + hh
                hsl = slice(hh * SSM_HEADDIM, (hh + 1) * SSM_HEADDIM)
                osl = slice(g * bw + hh * SSM_HEADDIM, g * bw + (hh + 1) * SSM_HEADDIM)
                st_out_ref[n, osl, :] = st[hsl] * eclast[r0:r0 + 1, h:h + 1] + upd[hsl]

    y = y_ref[...] + xc_ref[:, 0:SSM_DINNER] * dsk_ref[...]
    gg = y * _silu(z_ref[...])
    for g in range(SSM_GROUPS):
        gsl = slice(g * bw, (g + 1) * bw)
        blk = gg[:, gsl]
        ms = jnp.mean(blk * blk, axis=-1, keepdims=True)
        g_ref[:, gsl] = (blk * lax.rsqrt(ms + EPS) * gn_ref[:, gsl]).astype(BF16)


def _ssd_sample(zx, dtr, conv_state, ssm_state_t, cw, cb, dtb, alog, dsk, gn, expand, batch, seq):
    bb = SSD_SAMPLE_BB
    rows = bb * seq
    keep = SSM_CONV - 1
    full = lambda shape: pl.BlockSpec(shape, lambda n: (0,) * len(shape))
    col = lambda k: pl.BlockSpec((rows, SSM_DINNER), lambda n: (n, k))
    st_spec = pl.BlockSpec((bb, SSM_DINNER, SSM_DSTATE), lambda n: (n, 0, 0))
    return pl.pallas_call(
        functools.partial(_ssd_sample_kernel, seq=seq),
        grid=(batch // bb,),
        in_specs=[
            col(0), col(1), col(2),
            pl.BlockSpec((rows, LANES), lambda n: (n, 0)),
            pl.BlockSpec((None, bb, keep, SSM_CONVDIM), lambda n: (0, n, 0, 0)),
            st_spec,
            full((SSM_CONV, SSM_CONVDIM)), full((1, SSM_CONVDIM)),
            full((1, LANES)), full((1, LANES)),
            full((1, SSM_DINNER)), full((1, SSM_DINNER)),
            full((2 * LANES, SSM_DINNER)),
        ],
        out_specs=[
            pl.BlockSpec((rows, SSM_DINNER), lambda n: (n, 0)),
            st_spec,
        ],
        out_shape=[
            jax.ShapeDtypeStruct((batch * seq, SSM_DINNER), BF16),
            jax.ShapeDtypeStruct(ssm_state_t.shape, F32),
        ],
        scratch_shapes=[
            pltpu.VMEM((2 * SUBLANES, SSM_CONVDIM), F32),
            pltpu.VMEM((rows, SSM_CONVDIM), F32),
            pltpu.VMEM((3 * rows, SSM_DINNER), F32),
            pltpu.VMEM((rows, SSM_DINNER), F32),
        ],
        compiler_params=_cparams(("arbitrary",)),
        name="ssd_sample",
    )(zx, zx, zx, dtr, conv_state, ssm_state_t, cw, cb, dtb, alog, dsk, gn, expand)


def _rope_tables(pos):
    half = RET_DK // 2
    freqs = ROPE_BASE ** (-jnp.arange(half, dtype=F32) / half)
    ang = pos.astype(F32)[:, None] * freqs[None, :]
    return jnp.cos(ang), jnp.sin(ang)


def _row(v):
    return v.reshape(1, -1).astype(F32)


def _pad_lanes(v):
    return jnp.pad(v.astype(F32), (0, LANES - v.shape[0])).reshape(1, LANES)


def kernel(x_prompt, x_sample, state_ret, state_ssm, state_conv, ret_norm, ret_w_in, ret_head_norm, ret_w_out, ssm_norm, ssm_w_in, ssm_conv_w, ssm_conv_b, ssm_dt_bias, ssm_a_log, ssm_d, ssm_gate_norm, ssm_w_out, final_norm):
    bp, lp, d = x_prompt.shape
    bs, ls, _ = x_sample.shape
    assert ret_norm.shape[0] == 1 and ssm_norm.shape[0] == 1, "one retention and one SSD layer"

    ret_win = ret_w_in[0].astype(BF16)
    ret_wout = ret_w_out[0].astype(BF16)
    n_main = SSM_DINNER + SSM_CONVDIM
    ssm_win = ssm_w_in[0, :, :n_main].astype(BF16)
    ssm_wdt = jnp.pad(ssm_w_in[0, :, n_main:], ((0, 0), (0, LANES - SSM_HEADS))).astype(BF16)
    ssm_wout = ssm_w_out[0].astype(BF16)
    expand = (jnp.arange(2 * LANES)[:, None] % LANES == jnp.arange(SSM_DINNER)[None, :] // SSM_HEADDIM).astype(BF16)
    lg = jnp.log(1.0 - 2.0 ** (-5.0 - jnp.arange(RET_HEADS, dtype=F32)))
    lg = jnp.broadcast_to(lg[:, None, None], (RET_HEADS, 1, LANES))
    hn = _row(ret_head_norm[0])
    dsk = _row(jnp.repeat(ssm_d[0], SSM_HEADDIM))
    gn = _row(ssm_gate_norm[0])
    cw = ssm_conv_w[0].astype(F32)
    cb = _row(ssm_conv_b[0])
    dtb = _pad_lanes(ssm_dt_bias[0])
    alog = _pad_lanes(ssm_a_log[0])
    g_ret = _row(ret_norm[0])
    g_ssm = _row(ssm_norm[0])
    g_fin = _row(final_norm)

    xp = x_prompt.reshape(bp * lp, d)
    cos_p, sin_p = _rope_tables(jnp.arange(lp, dtype=jnp.int32))
    proj = _norm_proj(xp, g_ret, ret_win, tm=1024, tn=2048, out_dtype=BF16)
    og, ret_p = _ret_prompt(proj, cos_p, sin_p, lg, hn, bp, lp)
    h1 = _out_proj(og, ret_wout, xp, tm=512)
    tm_c = 1024
    zxc, dtr, tail = _norm_proj_conv(h1, g_ssm, ssm_win, ssm_wdt, cw, cb, tm=tm_c, tn=2048, seq=lp)
    tail = tail[lp // tm_c - 1::lp // tm_c]
    gated, ssm_pt = _ssd_prompt(zxc.reshape(bp, lp, -1), dtr.reshape(bp, lp, LANES), dtb, alog, dsk, gn, expand, bp, lp)
    y_p = _out_proj(gated.reshape(bp * lp, SSM_DINNER), ssm_wout, h1, tm=512, final_g=g_fin)
    ssm_p = jnp.swapaxes(ssm_pt.reshape(1, bp, SSM_HEADS, SSM_HEADDIM, SSM_DSTATE), -1, -2)
    conv_p = tail[None, :, SUBLANES - (SSM_CONV - 1):, :]

    xs = x_sample.reshape(bs * ls, d)
    cos_s, sin_s = _rope_tables(PAST_LEN + jnp.arange(ls, dtype=jnp.int32))
    proj_s = _norm_proj(xs, g_ret, ret_win, tm=bs * ls, tn=2048, out_dtype=BF16)
    og_s, ret_s = _ret_sample(proj_s, cos_s, sin_s, lg, hn, state_ret, bs, ls)
    h1_s = _out_proj(og_s, ret_wout, xs, tm=512)
    zx_s, dtr_s = _norm_proj(h1_s, g_ssm, ssm_win, tm=bs * ls, tn=2048, out_dtype=F32, w_dt=ssm_wdt)
    st_t = jnp.swapaxes(state_ssm[0], -1, -2).reshape(bs, SSM_DINNER, SSM_DSTATE)
    gated_s, ssm_st = _ssd_sample(zx_s, dtr_s, state_conv, st_t, cw, cb, dtb, alog, dsk, gn, expand, bs, ls)
    ssm_s = jnp.swapaxes(ssm_st.reshape(1, bs, SSM_HEADS, SSM_HEADDIM, SSM_DSTATE), -1, -2)
    keep = SSM_CONV - 1
    conv_s = zx_s.reshape(bs, ls, -1)[None, :, ls - keep:, SSM_DINNER:]
    y_s = _out_proj(gated_s, ssm_wout, h1_s, tm=512, final_g=g_fin)

    return (y_p.reshape(bp, lp, d), y_s.reshape(bs, ls, d), ret_p, ret_s, ssm_p, ssm_s, conv_p, conv_s)
```

```python
import functools
import math

import jax
import jax.numpy as jnp
from jax import lax
from jax.experimental import pallas as pl
from jax.experimental.pallas import tpu as pltpu

F32 = jnp.float32
BF16 = jnp.bfloat16

D_MODEL = 1024
RET_HEADS = 4
RET_DK = 256
RET_DV = 512
RET_VW = RET_HEADS * RET_DV
RET_QK = RET_HEADS * RET_DK
SSM_DINNER = 2048
SSM_HEADDIM = 64
SSM_HEADS = 32
SSM_GROUPS = 8
SSM_DSTATE = 128
SSM_CONV = 4
SSM_CONVDIM = 4096
SSM_GROUP_W = SSM_DINNER // SSM_GROUPS
HEADS_PER_GROUP = SSM_HEADS // SSM_GROUPS
ROPE_BASE = 10000.0
EPS = 1e-6
PAST_LEN = 16384

LANES = 128
SUBLANES = 8
VMEM_LIMIT = 56 * 1024 * 1024

RET_CHUNK = 256
SSD_CHUNK = 128
SSD_PROMPT_SEQS = 2
RET_SAMPLE_BB = 16
SSD_SAMPLE_BB = 4


def _cparams(sem, flags=None):
    return pltpu.CompilerParams(dimension_semantics=sem, vmem_limit_bytes=VMEM_LIMIT, flags=flags)


def _sigmoid(x):
    return 1.0 / (1.0 + jnp.exp2(x * (-math.log2(math.e))))


def _silu(x):
    return x * _sigmoid(x)


def _softplus(x):
    return jnp.maximum(x, 0.0) + jnp.log1p(jnp.exp(-jnp.abs(x)))


def _dot(a, b):
    return jnp.dot(a, b, preferred_element_type=F32)


def _dot_nt(a, b):
    return lax.dot_general(a, b, (((1,), (1,)), ((), ())), preferred_element_type=F32)


def _dot_tn(a, b):
    return lax.dot_general(a, b, (((0,), (0,)), ((), ())), preferred_element_type=F32)


def _iota(shape, dim):
    return lax.broadcasted_iota(jnp.int32, shape, dim)


def _norm_proj_kernel(x_ref, g_ref, w_ref, *rest, has_dt):
    if has_dt:
        wdt_ref, o_ref, dt_ref, xn_ref = rest
    else:
        o_ref, xn_ref = rest

    @pl.when(pl.program_id(1) == 0)
    def _():
        x = x_ref[...]
        ms = jnp.mean(x * x, axis=-1, keepdims=True)
        xn = (x * lax.rsqrt(ms + EPS) * g_ref[...]).astype(BF16)
        xn_ref[...] = xn
        if has_dt:
            dt_ref[...] = _dot(xn, wdt_ref[...])

    o_ref[...] = _dot(xn_ref[...], w_ref[...].astype(BF16)).astype(o_ref.dtype)


def _norm_proj(x, g, w, *, n, tm, tn, out_dtype, w_dt=None):
    m, d = x.shape
    has_dt = w_dt is not None
    in_specs = [
        pl.BlockSpec((tm, d), lambda i, j: (i, 0)),
        pl.BlockSpec((1, d), lambda i, j: (0, 0)),
        pl.BlockSpec((d, tn), lambda i, j: (0, j)),
    ]
    args = [x, g, w]
    out_shape = [jax.ShapeDtypeStruct((m, n), out_dtype)]
    out_specs = [pl.BlockSpec((tm, tn), lambda i, j: (i, j))]
    if has_dt:
        in_specs.append(pl.BlockSpec((d, LANES), lambda i, j: (0, 0)))
        args.append(w_dt)
        out_shape.append(jax.ShapeDtypeStruct((m, LANES), F32))
        out_specs.append(pl.BlockSpec((tm, LANES), lambda i, j: (i, 0)))
    res = pl.pallas_call(
        functools.partial(_norm_proj_kernel, has_dt=has_dt),
        grid=(m // tm, n // tn),
        in_specs=in_specs,
        out_specs=out_specs,
        out_shape=out_shape,
        scratch_shapes=[pltpu.VMEM((tm, d), BF16)],
        compiler_params=_cparams(("arbitrary", "arbitrary")),
        name="in_proj",
    )(*args)
    return res if has_dt else res[0]


def _shift_rows(a, first_rows):
    rows, cols = a.shape
    n = first_rows.shape[0]
    slabs = rows // SUBLANES
    rot = pltpu.roll(a.reshape(slabs, SUBLANES, cols), n, axis=1)
    first = jnp.concatenate([first_rows, jnp.zeros((SUBLANES - n, cols), a.dtype)], axis=0)
    above = jnp.concatenate([first.reshape(1, SUBLANES, cols), rot[:slabs - 1]], axis=0)
    top = _iota((slabs, SUBLANES, cols), 1) < n
    return jnp.where(top, above, rot).reshape(rows, cols)


def _norm_proj_conv_kernel(x_ref, g_ref, w_ref, wdt_ref, cw_ref, cb_ref, o_ref, dt_ref, tail_ref,
                           xn_ref, carry_ref, *, tiles_per_seq, z_tiles, strip):
    i = pl.program_id(0)
    j = pl.program_id(1)
    tm, tn = o_ref.shape

    @pl.when(j == 0)
    def _():
        x = x_ref[...]
        ms = jnp.mean(x * x, axis=-1, keepdims=True)
        xn = (x * lax.rsqrt(ms + EPS) * g_ref[...]).astype(BF16)
        xn_ref[...] = xn
        dt_ref[...] = _dot(xn, wdt_ref[...])

    @pl.when((j == 0) & (i % tiles_per_seq == 0))
    def _():
        carry_ref[...] = jnp.zeros_like(carry_ref)

    @pl.when(j < z_tiles)
    def _():
        o_ref[...] = _dot(xn_ref[...], w_ref[...].astype(BF16)).astype(o_ref.dtype)

    @pl.when(j >= z_tiles)
    def _():
        jc = j - z_tiles
        for s in range(tn // strip):
            sl = slice(s * strip, (s + 1) * strip)
            x = _dot(xn_ref[...], w_ref[:, sl].astype(BF16))
            tail_ref[:, sl] = x[tm - SUBLANES:tm, :]
            x2 = _shift_rows(x, carry_ref[jc, 1:3, sl])
            p = x * cw_ref[2:3, sl] + x2 * cw_ref[0:1, sl]
            out = x * cw_ref[3:4, sl] + x2 * cw_ref[1:2, sl] + _shift_rows(p, carry_ref[jc, 0:1, sl]) + cb_ref[:, sl]
            o_ref[:, sl] = _silu(out).astype(o_ref.dtype)
            carry_ref[jc, 0:1, sl] = p[tm - 1:tm]
            carry_ref[jc, 1:3, sl] = x[tm - 2:tm]


def _norm_proj_conv(x, g, w, w_dt, cw, cb, *, tm, tn, seq):
    m, d = x.shape
    n = SSM_DINNER + SSM_CONVDIM
    z_tiles = SSM_DINNER // tn
    c_tiles = SSM_CONVDIM // tn
    tiles_per_seq = seq // tm
    cj = lambda j: jnp.maximum(j - z_tiles, 0)
    return pl.pallas_call(
        functools.partial(_norm_proj_conv_kernel, tiles_per_seq=tiles_per_seq, z_tiles=z_tiles, strip=256),
        grid=(m // tm, n // tn),
        in_specs=[
            pl.BlockSpec((tm, d), lambda i, j: (i, 0)),
            pl.BlockSpec((1, d), lambda i, j: (0, 0)),
            pl.BlockSpec((d, tn), lambda i, j: (0, j)),
            pl.BlockSpec((d, LANES), lambda i, j: (0, 0)),
            pl.BlockSpec((SSM_CONV, tn), lambda i, j: (0, cj(j))),
            pl.BlockSpec((1, tn), lambda i, j: (0, cj(j))),
        ],
        out_specs=[
            pl.BlockSpec((tm, tn), lambda i, j: (i, j)),
            pl.BlockSpec((tm, LANES), lambda i, j: (i, 0)),
            pl.BlockSpec((None, SUBLANES, tn), lambda i, j: (i, 0, cj(j))),
        ],
        out_shape=[
            jax.ShapeDtypeStruct((m, n), BF16),
            jax.ShapeDtypeStruct((m, LANES), F32),
            jax.ShapeDtypeStruct((m // tm, SUBLANES, SSM_CONVDIM), F32),
        ],
        scratch_shapes=[
            pltpu.VMEM((tm, d), BF16),
            pltpu.VMEM((c_tiles, SUBLANES, tn), F32),
        ],
        compiler_params=_cparams(("arbitrary", "arbitrary")),
        name="ssd_in_proj_conv",
    )(x, g, w, w_dt, cw, cb)


def _out_proj_kernel(a_ref, w_ref, h_ref, *rest, has_norm):
    if has_norm:
        g_ref, o_ref = rest
    else:
        (o_ref,) = rest
    y = h_ref[...] + _dot(a_ref[...], w_ref[...])
    if has_norm:
        ms = jnp.mean(y * y, axis=-1, keepdims=True)
        y = y * lax.rsqrt(ms + EPS) * g_ref[...]
    o_ref[...] = y


def _out_proj(a, w, h, *, tm, final_g=None):
    m, k = a.shape
    d = w.shape[1]
    has_norm = final_g is not None
    in_specs = [
        pl.BlockSpec((tm, k), lambda i: (i, 0)),
        pl.BlockSpec((k, d), lambda i: (0, 0)),
        pl.BlockSpec((tm, d), lambda i: (i, 0)),
    ]
    args = [a, w, h]
    if has_norm:
        in_specs.append(pl.BlockSpec((1, d), lambda i: (0, 0)))
        args.append(final_g)
    return pl.pallas_call(
        functools.partial(_out_proj_kernel, has_norm=has_norm),
        grid=(m // tm,),
        in_specs=in_specs,
        out_specs=pl.BlockSpec((tm, d), lambda i: (i, 0)),
        out_shape=jax.ShapeDtypeStruct((m, d), F32),
        compiler_params=_cparams(("arbitrary",)),
        name="out_proj",
    )(*args)


def _rope(x, cos, sin):
    half = RET_DK // 2
    x1 = x[:, :half]
    x2 = x[:, half:]
    return jnp.concatenate([x1 * cos - x2 * sin, x1 * sin + x2 * cos], axis=1)


def _head_norm_gate(o, hn, gate):
    ms = jnp.mean(o * o, axis=-1, keepdims=True)
    on = o * lax.rsqrt(ms + EPS) * hn
    return (on * _silu(gate)).astype(BF16)


def _ret_prompt_kernel(q_ref, k_ref, v_ref, gate_ref, cos_ref, sin_ref, lg_ref, hn_ref,
                       og_ref, st_ref, s_ref, dec_ref, qsc_ref, ksc_ref, gc_ref):
    b = pl.program_id(0)
    c = pl.program_id(1)
    rows = q_ref.shape[0]

    @pl.when((b == 0) & (c == 0))
    def _():
        i = _iota((rows, rows), 0)
        j = _iota((rows, rows), 1)
        r = _iota((rows, RET_DK), 0)
        scale = RET_DK ** -0.5
        for h in range(RET_HEADS):
            lg = lg_ref[h, :, 0:1]
            dec_ref[h] = jnp.where(i >= j, jnp.exp((i - j).astype(F32) * lg), 0.0) * scale
            qsc_ref[h] = jnp.exp((r + 1).astype(F32) * lg)
            ksc_ref[h] = jnp.exp((rows - 1 - r).astype(F32) * lg) * scale
            gc_ref[h] = jnp.exp(jnp.full((1, RET_DV), float(rows), F32) * lg)

    @pl.when(c == 0)
    def _():
        s_ref[...] = jnp.zeros_like(s_ref)

    cos = cos_ref[...]
    sin = sin_ref[...]
    for h in range(RET_HEADS):
        ksl = slice(h * RET_DK, (h + 1) * RET_DK)
        vsl = slice(h * RET_DV, (h + 1) * RET_DV)
        qr = _rope(q_ref[:, ksl].astype(F32), cos, sin)
        kr = _rope(k_ref[:, ksl].astype(F32), cos, sin)
        v = v_ref[:, vsl]
        p = (_dot_nt(qr.astype(BF16), kr.astype(BF16)) * dec_ref[h]).astype(BF16)
        qs = (qr * qsc_ref[h]).astype(BF16)
        s = s_ref[h]
        o = _dot(p, v) + _dot(qs, s.astype(BF16))
        kt = (kr * ksc_ref[h]).T.astype(BF16)
        s_ref[h] = s * gc_ref[h] + _dot(kt, v)
        og_ref[:, vsl] = _head_norm_gate(o, hn_ref[:, vsl], gate_ref[:, vsl].astype(F32))

    @pl.when(c == pl.num_programs(1) - 1)
    def _():
        st_ref[...] = s_ref[...]


def _ret_prompt(proj, cos, sin, lg, hn, batch, seq):
    rows = RET_CHUNK
    nc = seq // rows
    return pl.pallas_call(
        _ret_prompt_kernel,
        grid=(batch, nc),
        in_specs=[
            pl.BlockSpec((rows, RET_QK), lambda b, c: (b * nc + c, 0)),
            pl.BlockSpec((rows, RET_QK), lambda b, c: (b * nc + c, 1)),
            pl.BlockSpec((rows, RET_VW), lambda b, c: (b * nc + c, 2 * RET_QK // RET_VW)),
            pl.BlockSpec((rows, RET_VW), lambda b, c: (b * nc + c, 2 * RET_QK // RET_VW + 1)),
            pl.BlockSpec((rows, RET_DK // 2), lambda b, c: (c, 0)),
            pl.BlockSpec((rows, RET_DK // 2), lambda b, c: (c, 0)),
            pl.BlockSpec((RET_HEADS, 1, LANES), lambda b, c: (0, 0, 0)),
            pl.BlockSpec((1, RET_VW), lambda b, c: (0, 0)),
        ],
        out_specs=[
            pl.BlockSpec((rows, RET_VW), lambda b, c: (b * nc + c, 0)),
            pl.BlockSpec((None, None, RET_HEADS, RET_DK, RET_DV), lambda b, c: (0, b, 0, 0, 0)),
        ],
        out_shape=[
            jax.ShapeDtypeStruct((batch * seq, RET_VW), BF16),
            jax.ShapeDtypeStruct((1, batch, RET_HEADS, RET_DK, RET_DV), F32),
        ],
        scratch_shapes=[
            pltpu.VMEM((RET_HEADS, RET_DK, RET_DV), F32),
            pltpu.VMEM((RET_HEADS, rows, rows), F32),
            pltpu.VMEM((RET_HEADS, rows, RET_DK), F32),
            pltpu.VMEM((RET_HEADS, rows, RET_DK), F32),
            pltpu.VMEM((RET_HEADS, 1, RET_DV), F32),
        ],
        compiler_params=_cparams(("arbitrary", "arbitrary")),
        name="ret_prompt",
    )(proj, proj, proj, proj, cos, sin, lg, hn)


def _ret_sample_kernel(q_ref, k_ref, v_ref, gate_ref, cos_ref, sin_ref, lg_ref, hn_ref, st_in_ref,
                       og_ref, st_out_ref, *, seq):
    bb = st_in_ref.shape[0]
    rows = bb * seq
    lg = lg_ref[:, 0:1]
    scale = RET_DK ** -0.5
    i = _iota((rows, rows), 0)
    j = _iota((rows, rows), 1)
    same = (i // seq) == (j // seq)
    dec = jnp.where(same & (i >= j), jnp.exp((i - j).astype(F32) * lg), 0.0) * scale
    pos = _iota((rows, RET_DK), 0) % seq
    qsc = jnp.exp((pos + 1).astype(F32) * lg)
    ksc = jnp.exp((seq - 1 - pos).astype(F32) * lg) * scale
    gc = jnp.exp(jnp.full((1, RET_DV), float(seq), F32) * lg)

    cos = jnp.concatenate([cos_ref[...]] * bb, axis=0)
    sin = jnp.concatenate([sin_ref[...]] * bb, axis=0)
    qr = _rope(q_ref[...].astype(F32), cos, sin)
    kr = _rope(k_ref[...].astype(F32), cos, sin)
    v = v_ref[...].astype(F32)
    p = _dot_nt(qr.astype(BF16), kr.astype(BF16)) * dec
    intra = _dot(p.astype(BF16), v.astype(BF16))
    qs = qr * qsc
    kt = kr * ksc
    vrow = _iota((rows, RET_DV), 0) // seq
    cross = []
    for n in range(bb):
        s0 = st_in_ref[n]
        cross.append(_dot(qs[n * seq:(n + 1) * seq], s0))
        vn = jnp.where(vrow == n, v, 0.0)
        st_out_ref[n] = s0 * gc + _dot_tn(kt.astype(BF16), vn.astype(BF16))
    o = intra + jnp.concatenate(cross, axis=0)
    og_ref[...] = _head_norm_gate(o, hn_ref[...], gate_ref[...].astype(F32))


def _ret_sample(proj, cos, sin, lg, hn, state, batch, seq):
    bb = RET_SAMPLE_BB
    rows = bb * seq
    qb = RET_QK // RET_DK
    vb = 2 * RET_QK // RET_DV
    gb = vb + RET_VW // RET_DV
    st_spec = pl.BlockSpec((None, bb, None, RET_DK, RET_DV), lambda n, h: (0, n, h, 0, 0))
    return pl.pallas_call(
        functools.partial(_ret_sample_kernel, seq=seq),
        grid=(batch // bb, RET_HEADS),
        in_specs=[
            pl.BlockSpec((rows, RET_DK), lambda n, h: (n, h)),
            pl.BlockSpec((rows, RET_DK), lambda n, h: (n, qb + h)),
            pl.BlockSpec((rows, RET_DV), lambda n, h: (n, vb + h)),
            pl.BlockSpec((rows, RET_DV), lambda n, h: (n, gb + h)),
            pl.BlockSpec((seq, RET_DK // 2), lambda n, h: (0, 0)),
            pl.BlockSpec((seq, RET_DK // 2), lambda n, h: (0, 0)),
            pl.BlockSpec((None, 1, LANES), lambda n, h: (h, 0, 0)),
            pl.BlockSpec((1, RET_DV), lambda n, h: (0, h)),
            st_spec,
        ],
        out_specs=[
            pl.BlockSpec((rows, RET_DV), lambda n, h: (n, h)),
            st_spec,
        ],
        out_shape=[
            jax.ShapeDtypeStruct((batch * seq, RET_VW), BF16),
            jax.ShapeDtypeStruct(state.shape, F32),
        ],
        compiler_params=_cparams(("arbitrary", "arbitrary")),
        name="ret_sample",
    )(proj, proj, proj, proj, cos, sin, lg, hn, state)


def _head_expand(vals, expand):
    parts = []
    for v in vals:
        hi = v.astype(BF16)
        lo = (v - hi.astype(F32)).astype(BF16)
        parts.append(jnp.concatenate([hi, lo], axis=1))
    return _dot(jnp.concatenate(parts, axis=0), expand)


def _ssd_prompt_kernel(z_ref, xc_ref, bc_ref, dtr_ref, dtr_next_ref, dtb_ref, alog_ref, dsk_ref, gn_ref, exp_ref,
                       g_ref, st_ref, s_ref, ex_ref, cum2_ref, cum2t_ref):
    c = pl.program_id(1)
    n_seq, rows = z_ref.shape[0], z_ref.shape[1]
    bw = SSM_GROUP_W
    ri = _iota((rows, rows), 0)
    rj = _iota((rows, rows), 1)
    causal = ri >= rj

    def decay_tables(n, dtr):
        dt = _softplus(dtr + dtb_ref[...])
        la = dt * (-jnp.exp(alog_ref[...]))
        cum = jnp.dot(causal.astype(F32), la, precision=lax.Precision.HIGHEST, preferred_element_type=F32)
        ecum = jnp.exp(cum)
        wdt = jnp.exp(cum[rows - 1:rows, :] - cum) * dt
        ex_ref[n] = _head_expand([ecum, wdt, dt], exp_ref[...])
        cum2 = cum * math.log2(math.e)
        cum2_ref[n] = cum2
        cum2t_ref[n] = cum2.T

    @pl.when(c == 0)
    def _():
        s_ref[...] = jnp.zeros_like(s_ref)
        for n in range(n_seq):
            decay_tables(n, dtr_ref[n])

    head_lane = _iota((rows, bw), 1) // SSM_HEADDIM

    for n in range(n_seq):
        cum2 = cum2_ref[n]
        cum2_t = cum2t_ref[n]
        for g in range(SSM_GROUPS):
            gsl = slice(g * bw, (g + 1) * bw)
            bm = bc_ref[n, :, g * SSM_DSTATE:(g + 1) * SSM_DSTATE]
            cm = bc_ref[n, :, (SSM_GROUPS + g) * SSM_DSTATE:(SSM_GROUPS + g + 1) * SSM_DSTATE]
            xs = xc_ref[n, :, gsl].astype(F32)
            s = s_ref[n, g]
            gmat = _dot_nt(cm, bm)
            ps = []
            for hh in range(HEADS_PER_GROUP):
                h = g * HEADS_PER_GROUP + hh
                colb = jnp.broadcast_to(cum2[:, h:h + 1], (rows, LANES))
                dec = jnp.exp2(jnp.where(causal, colb - cum2_t[h:h + 1, :], -jnp.inf))
                ps.append((gmat * dec).astype(BF16))
            v = (xs * ex_ref[n, 2 * rows:3 * rows, gsl]).astype(BF16)
            y4 = _dot(jnp.concatenate(ps, axis=0), v)
            y = y4[0:rows]
            for hh in range(1, HEADS_PER_GROUP):
                y = jnp.where(head_lane == hh, y4[hh * rows:(hh + 1) * rows], y)
            y = y + _dot(cm, s.astype(BF16)) * ex_ref[n, 0:rows, gsl]
            vw = (xs * ex_ref[n, rows:2 * rows, gsl]).astype(BF16)
            s_ref[n, g] = s * ex_ref[n, rows - 1:rows, gsl] + _dot_tn(bm, vw)

            y = y + xs * dsk_ref[:, gsl]
            gg = y * _silu(z_ref[n, :, gsl].astype(F32))
            ms = jnp.mean(gg * gg, axis=-1, keepdims=True)
            g_ref[n, :, gsl] = (gg * lax.rsqrt(ms + EPS) * gn_ref[:, gsl]).astype(BF16)

    for n in range(n_seq):
        decay_tables(n, dtr_next_ref[n])

    @pl.when(c == pl.num_programs(1) - 1)
    def _():
        for n in range(n_seq):
            for g in range(SSM_GROUPS):
                st_ref[n, g * bw:(g + 1) * bw, :] = s_ref[n, g].T


def _ssd_prompt(zxc, dtr, dtb, alog, dsk, gn, expand, batch, seq):
    rows = SSD_CHUNK
    nc = seq // rows
    ns = SSD_PROMPT_SEQS
    full = lambda shape: pl.BlockSpec(shape, lambda b, c: (0,) * len(shape))
    col = lambda k: pl.BlockSpec((ns, rows, SSM_DINNER), lambda b, c: (b, c, k))
    return pl.pallas_call(
        _ssd_prompt_kernel,
        grid=(batch // ns, nc),
        in_specs=[
            col(0), col(1), col(2),
            pl.BlockSpec((ns, rows, LANES), lambda b, c: (b, c, 0)),
            pl.BlockSpec((ns, rows, LANES), lambda b, c: (b, jnp.minimum(c + 1, nc - 1), 0)),
            full((1, LANES)), full((1, LANES)),
            full((1, SSM_DINNER)), full((1, SSM_DINNER)),
            full((2 * LANES, SSM_DINNER)),
        ],
        out_specs=[
            pl.BlockSpec((ns, rows, SSM_DINNER), lambda b, c: (b, c, 0)),
            pl.BlockSpec((ns, SSM_DINNER, SSM_DSTATE), lambda b, c: (b, 0, 0)),
        ],
        out_shape=[
            jax.ShapeDtypeStruct((batch, seq, SSM_DINNER), BF16),
            jax.ShapeDtypeStruct((batch, SSM_DINNER, SSM_DSTATE), F32),
        ],
        scratch_shapes=[
            pltpu.VMEM((ns, SSM_GROUPS, SSM_DSTATE, SSM_GROUP_W), F32),
            pltpu.VMEM((ns, 3 * rows, SSM_DINNER), F32),
            pltpu.VMEM((ns, rows, LANES), F32),
            pltpu.VMEM((ns, LANES, rows), F32),
        ],
        compiler_params=_cparams(("arbitrary", "arbitrary")),
        name="ssd_prompt",
    )(zxc, zxc, zxc, dtr, dtr, dtb, alog, dsk, gn, expand)


def _ssd_sample_kernel(z_ref, xa_ref, xb_ref, dtr_ref, conv_in_ref, st_in_ref,
                       cw_ref, cb_ref, dtb_ref, alog_ref, dsk_ref, gn_ref, exp_ref,
                       g_ref, st_out_ref, xpad_ref, xc_ref, ex_ref, y_ref, *, seq):
    bb = st_in_ref.shape[0]
    rows = bb * seq
    pad = SUBLANES
    keep = SSM_CONV - 1
    bw = SSM_GROUP_W

    for n in range(bb):
        r0 = n * seq
        xpad_ref[pad - keep:pad, :] = conv_in_ref[n]
        xpad_ref[pad:pad + seq, 0:SSM_DINNER] = xa_ref[r0:r0 + seq, :]
        xpad_ref[pad:pad + seq, SSM_DINNER:SSM_CONVDIM] = xb_ref[r0:r0 + seq, :]
        acc = cb_ref[...]
        for w in range(SSM_CONV):
            lo = pad - keep + w
            acc = acc + xpad_ref[lo:lo + seq, :] * cw_ref[w:w + 1, :]
        xc_ref[r0:r0 + seq, :] = _silu(acc)

    dt = _softplus(dtr_ref[...] + dtb_ref[...])
    la = dt * (-jnp.exp(alog_ref[...]))
    ri = _iota((rows, rows), 0)
    rj = _iota((rows, rows), 1)
    same = (ri // seq) == (rj // seq)
    cum = jnp.dot((same & (ri >= rj)).astype(F32), la, precision=lax.Precision.HIGHEST,
                  preferred_element_type=F32)
    clast = jnp.dot(same.astype(F32), la, precision=lax.Precision.HIGHEST, preferred_element_type=F32)
    eclast = jnp.exp(clast)
    ex_ref[...] = _head_expand([jnp.exp(cum), jnp.exp(clast - cum) * dt, dt], exp_ref[...])

    si = _iota((seq, seq), 0)
    sj = _iota((seq, seq), 1)
    causal = si >= sj
    eye = si == sj
    head_lane = _iota((seq, bw), 1) // SSM_HEADDIM

    for n in range(bb):
        r0 = n * seq
        rsl = slice(r0, r0 + seq)
        cum_n = cum[rsl]
        for g in range(SSM_GROUPS):
            gsl = slice(g * bw, (g + 1) * bw)
            bm = xc_ref[rsl, SSM_DINNER + g * SSM_DSTATE:SSM_DINNER + (g + 1) * SSM_DSTATE]
            cm = xc_ref[rsl, SSM_DINNER + (SSM_GROUPS + g) * SSM_DSTATE:
                        SSM_DINNER + (SSM_GROUPS + g + 1) * SSM_DSTATE]
            xs = xc_ref[rsl, gsl]
            gmat = _dot_nt(cm.astype(BF16), bm.astype(BF16))
            ps = []
            for hh in range(HEADS_PER_GROUP):
                h = g * HEADS_PER_GROUP + hh
                ccol = cum_n[:, h:h + 1]
                crow = jnp.sum(jnp.where(eye, ccol, 0.0), axis=0, keepdims=True)
                ps.append(gmat * jnp.exp(jnp.where(causal, ccol - crow, -jnp.inf)))
            v = xs * ex_ref[2 * rows + r0:2 * rows + r0 + seq, gsl]
            y4 = _dot(jnp.concatenate(ps, axis=0).astype(BF16), v.astype(BF16))
            y = y4[0:seq]
            for hh in range(1, HEADS_PER_GROUP):
                y = jnp.where(head_lane == hh, y4[hh * seq:(hh + 1) * seq], y)
            st = st_in_ref[n, gsl, :]
            y = y + _dot_nt(cm.astype(BF16), st.astype(BF16)) * ex_ref[rsl, gsl]
            y_ref[rsl, gsl] = y
            vw = xs * ex_ref[rows + r0:rows + r0 + seq, gsl]
            upd = _dot_tn(vw.astype(BF16), bm.astype(BF16))
            for hh in range(HEADS_PER_GROUP):
                h = g * HEADS_PER_GROUP + hh
                hsl = slice(hh * SSM_HEADDIM, (hh + 1) * SSM_HEADDIM)
                osl = slice(g * bw + hh * SSM_HEADDIM, g * bw + (hh + 1) * SSM_HEADDIM)
                st_out_ref[n, osl, :] = st[hsl] * eclast[r0:r0 + 1, h:h + 1] + upd[hsl]

    y = y_ref[...] + xc_ref[:, 0:SSM_DINNER] * dsk_ref[...]
    gg = y * _silu(z_ref[...])
    for g in range(SSM_GROUPS):
        gsl = slice(g * bw, (g + 1) * bw)
        blk = gg[:, gsl]
        ms = jnp.mean(blk * blk, axis=-1, keepdims=True)
        g_ref[:, gsl] = (blk * lax.rsqrt(ms + EPS) * gn_ref[:, gsl]).astype(BF16)


def _ssd_sample(zx, dtr, conv_state, ssm_state_t, cw, cb, dtb, alog, dsk, gn, expand, batch, seq):
    bb = SSD_SAMPLE_BB
    rows = bb * seq
    keep = SSM_CONV - 1
    full = lambda shape: pl.BlockSpec(shape, lambda n: (0,) * len(shape))
    col = lambda k: pl.BlockSpec((rows, SSM_DINNER), lambda n: (n, k))
    st_spec = pl.BlockSpec((bb, SSM_DINNER, SSM_DSTATE), lambda n: (n, 0, 0))
    return pl.pallas_call(
        functools.partial(_ssd_sample_kernel, seq=seq),
        grid=(batch // bb,),
        in_specs=[
            col(0), col(1), col(2),
            pl.BlockSpec((rows, LANES), lambda n: (n, 0)),
            pl.BlockSpec((None, bb, keep, SSM_CONVDIM), lambda n: (0, n, 0, 0)),
            st_spec,
            full((SSM_CONV, SSM_CONVDIM)), full((1, SSM_CONVDIM)),
            full((1, LANES)), full((1, LANES)),
            full((1, SSM_DINNER)), full((1, SSM_DINNER)),
            full((2 * LANES, SSM_DINNER)),
        ],
        out_specs=[
            pl.BlockSpec((rows, SSM_DINNER), lambda n: (n, 0)),
            st_spec,
        ],
        out_shape=[
            jax.ShapeDtypeStruct((batch * seq, SSM_DINNER), BF16),
            jax.ShapeDtypeStruct(ssm_state_t.shape, F32),
        ],
        scratch_shapes=[
            pltpu.VMEM((2 * SUBLANES, SSM_CONVDIM), F32),
            pltpu.VMEM((rows, SSM_CONVDIM), F32),
            pltpu.VMEM((3 * rows, SSM_DINNER), F32),
            pltpu.VMEM((rows, SSM_DINNER), F32),
        ],
        compiler_params=_cparams(("arbitrary",)),
        name="ssd_sample",
    )(zx, zx, zx, dtr, conv_state, ssm_state_t, cw, cb, dtb, alog, dsk, gn, expand)


def _rope_tables(pos):
    half = RET_DK // 2
    freqs = ROPE_BASE ** (-jnp.arange(half, dtype=F32) / half)
    ang = pos.astype(F32)[:, None] * freqs[None, :]
    return jnp.cos(ang), jnp.sin(ang)


def _row(v):
    return v.reshape(1, -1).astype(F32)


def _pad_lanes(v):
    return jnp.pad(v.astype(F32), (0, LANES - v.shape[0])).reshape(1, LANES)


def kernel(x_prompt, x_sample, state_ret, state_ssm, state_conv, ret_norm, ret_w_in, ret_head_norm, ret_w_out, ssm_norm, ssm_w_in, ssm_conv_w, ssm_conv_b, ssm_dt_bias, ssm_a_log, ssm_d, ssm_gate_norm, ssm_w_out, final_norm):
    bp, lp, d = x_prompt.shape
    bs, ls, _ = x_sample.shape
    assert ret_norm.shape[0] == 1 and ssm_norm.shape[0] == 1, "one retention and one SSD layer"

    ret_win = ret_w_in[0]
    ret_wout = ret_w_out[0].astype(BF16)
    n_main = SSM_DINNER + SSM_CONVDIM
    ssm_win = ssm_w_in[0]
    ssm_wdt = jnp.pad(ssm_w_in[0, :, n_main:], ((0, 0), (0, LANES - SSM_HEADS))).astype(BF16)
    ssm_wout = ssm_w_out[0].astype(BF16)
    expand = (jnp.arange(2 * LANES)[:, None] % LANES == jnp.arange(SSM_DINNER)[None, :] // SSM_HEADDIM).astype(BF16)
    lg = jnp.log(1.0 - 2.0 ** (-5.0 - jnp.arange(RET_HEADS, dtype=F32)))
    lg = jnp.broadcast_to(lg[:, None, None], (RET_HEADS, 1, LANES))
    hn = _row(ret_head_norm[0])
    dsk = _row(jnp.repeat(ssm_d[0], SSM_HEADDIM))
    gn = _row(ssm_gate_norm[0])
    cw = ssm_conv_w[0].astype(F32)
    cb = _row(ssm_conv_b[0])
    dtb = _pad_lanes(ssm_dt_bias[0])
    alog = _pad_lanes(ssm_a_log[0])
    g_ret = _row(ret_norm[0])
    g_ssm = _row(ssm_norm[0])
    g_fin = _row(final_norm)

    xp = x_prompt.reshape(bp * lp, d)
    cos_p, sin_p = _rope_tables(jnp.arange(lp, dtype=jnp.int32))
    n_ret = 2 * RET_QK + 2 * RET_VW
    proj = _norm_proj(xp, g_ret, ret_win, n=n_ret, tm=1024, tn=2048, out_dtype=BF16)
    og, ret_p = _ret_prompt(proj, cos_p, sin_p, lg, hn, bp, lp)
    h1 = _out_proj(og, ret_wout, xp, tm=1024)
    tm_c = 1024
    zxc, dtr, tail = _norm_proj_conv(h1, g_ssm, ssm_win, ssm_wdt, cw, cb, tm=tm_c, tn=2048, seq=lp)
    tail = tail[lp // tm_c - 1::lp // tm_c]
    gated, ssm_pt = _ssd_prompt(zxc.reshape(bp, lp, -1), dtr.reshape(bp, lp, LANES), dtb, alog, dsk, gn, expand, bp, lp)
    y_p = _out_proj(gated.reshape(bp * lp, SSM_DINNER), ssm_wout, h1, tm=1024, final_g=g_fin)
    ssm_p = jnp.swapaxes(ssm_pt.reshape(1, bp, SSM_HEADS, SSM_HEADDIM, SSM_DSTATE), -1, -2)
    conv_p = tail[None, :, SUBLANES - (SSM_CONV - 1):, :]

    xs = x_sample.reshape(bs * ls, d)
    cos_s, sin_s = _rope_tables(PAST_LEN + jnp.arange(ls, dtype=jnp.int32))
    proj_s = _norm_proj(xs, g_ret, ret_win, n=n_ret, tm=bs * ls, tn=2048, out_dtype=BF16)
    og_s, ret_s = _ret_sample(proj_s, cos_s, sin_s, lg, hn, state_ret, bs, ls)
    h1_s = _out_proj(og_s, ret_wout, xs, tm=512)
    zx_s, dtr_s = _norm_proj(h1_s, g_ssm, ssm_win, n=n_main, tm=bs * ls, tn=2048, out_dtype=F32, w_dt=ssm_wdt)
    st_t = jnp.swapaxes(state_ssm[0], -1, -2).reshape(bs, SSM_DINNER, SSM_DSTATE)
    gated_s, ssm_st = _ssd_sample(zx_s, dtr_s, state_conv, st_t, cw, cb, dtb, alog, dsk, gn, expand, bs, ls)
    ssm_s = jnp.swapaxes(ssm_st.reshape(1, bs, SSM_HEADS, SSM_HEADDIM, SSM_DSTATE), -1, -2)
    keep = SSM_CONV - 1
    conv_s = zx_s.reshape(bs, ls, -1)[None, :, ls - keep:, SSM_DINNER:]
    y_s = _out_proj(gated_s, ssm_wout, h1_s, tm=512, final_g=g_fin)

    return (y_p.reshape(bp, lp, d), y_s.reshape(bs, ls, d), ret_p, ret_s, ssm_p, ssm_s, conv_p, conv_s)
```

```python
import functools
import math

import jax
import jax.numpy as jnp
from jax import lax
from jax.experimental import pallas as pl
from jax.experimental.pallas import tpu as pltpu

F32 = jnp.float32
BF16 = jnp.bfloat16

D_MODEL = 1024
RET_HEADS = 4
RET_DK = 256
RET_DV = 512
RET_VW = RET_HEADS * RET_DV
RET_QK = RET_HEADS * RET_DK
SSM_DINNER = 2048
SSM_HEADDIM = 64
SSM_HEADS = 32
SSM_GROUPS = 8
SSM_DSTATE = 128
SSM_CONV = 4
SSM_CONVDIM = 4096
SSM_GROUP_W = SSM_DINNER // SSM_GROUPS
HEADS_PER_GROUP = SSM_HEADS // SSM_GROUPS
ROPE_BASE = 10000.0
EPS = 1e-6
PAST_LEN = 16384

LANES = 128
SUBLANES = 8
VMEM_LIMIT = 56 * 1024 * 1024

RET_CHUNK = 256
SSD_CHUNK = 128
SSD_PROMPT_SEQS = 2
RET_SAMPLE_BB = 16
SSD_SAMPLE_BB = 4


def _cparams(sem, flags=None):
    return pltpu.CompilerParams(dimension_semantics=sem, vmem_limit_bytes=VMEM_LIMIT, flags=flags)


def _sigmoid(x):
    return 1.0 / (1.0 + jnp.exp2(x * (-math.log2(math.e))))


def _silu(x):
    return x * _sigmoid(x)


def _softplus(x):
    return jnp.maximum(x, 0.0) + jnp.log1p(jnp.exp(-jnp.abs(x)))


def _dot(a, b):
    return jnp.dot(a, b, preferred_element_type=F32)


def _dot_nt(a, b):
    return lax.dot_general(a, b, (((1,), (1,)), ((), ())), preferred_element_type=F32)


def _dot_tn(a, b):
    return lax.dot_general(a, b, (((0,), (0,)), ((), ())), preferred_element_type=F32)


def _iota(shape, dim):
    return lax.broadcasted_iota(jnp.int32, shape, dim)


def _norm_proj_kernel(x_ref, g_ref, w_ref, *rest, has_dt, w_is_t):
    if has_dt:
        wdt_ref, o_ref, dt_ref, xn_ref = rest
    else:
        o_ref, xn_ref = rest

    @pl.when(pl.program_id(1) == 0)
    def _():
        x = x_ref[...]
        ms = jnp.mean(x * x, axis=-1, keepdims=True)
        xn = (x * lax.rsqrt(ms + EPS) * g_ref[...]).astype(BF16)
        xn_ref[...] = xn
        if has_dt:
            dt_ref[...] = _dot(xn, wdt_ref[...])

    w = w_ref[...].astype(BF16)
    o_ref[...] = (_dot_nt(xn_ref[...], w) if w_is_t else _dot(xn_ref[...], w)).astype(o_ref.dtype)


def _norm_proj(x, g, w, *, n, tm, tn, out_dtype, w_dt=None, w_is_t=False):
    m, d = x.shape
    has_dt = w_dt is not None
    in_specs = [
        pl.BlockSpec((tm, d), lambda i, j: (i, 0)),
        pl.BlockSpec((1, d), lambda i, j: (0, 0)),
        pl.BlockSpec((tn, d), lambda i, j: (j, 0)) if w_is_t else pl.BlockSpec((d, tn), lambda i, j: (0, j)),
    ]
    args = [x, g, w]
    out_shape = [jax.ShapeDtypeStruct((m, n), out_dtype)]
    out_specs = [pl.BlockSpec((tm, tn), lambda i, j: (i, j))]
    if has_dt:
        in_specs.append(pl.BlockSpec((d, LANES), lambda i, j: (0, 0)))
        args.append(w_dt)
        out_shape.append(jax.ShapeDtypeStruct((m, LANES), F32))
        out_specs.append(pl.BlockSpec((tm, LANES), lambda i, j: (i, 0)))
    res = pl.pallas_call(
        functools.partial(_norm_proj_kernel, has_dt=has_dt, w_is_t=w_is_t),
        grid=(m // tm, n // tn),
        in_specs=in_specs,
        out_specs=out_specs,
        out_shape=out_shape,
        scratch_shapes=[pltpu.VMEM((tm, d), BF16)],
        compiler_params=_cparams(("arbitrary", "arbitrary")),
        name="in_proj",
    )(*args)
    return res if has_dt else res[0]


def _shift_rows(a, first_rows):
    rows, cols = a.shape
    n = first_rows.shape[0]
    slabs = rows // SUBLANES
    rot = pltpu.roll(a.reshape(slabs, SUBLANES, cols), n, axis=1)
    first = jnp.concatenate([first_rows, jnp.zeros((SUBLANES - n, cols), a.dtype)], axis=0)
    above = jnp.concatenate([first.reshape(1, SUBLANES, cols), rot[:slabs - 1]], axis=0)
    top = _iota((slabs, SUBLANES, cols), 1) < n
    return jnp.where(top, above, rot).reshape(rows, cols)


def _norm_proj_conv_kernel(x_ref, g_ref, w_ref, wdt_ref, cw_ref, cb_ref, o_ref, dt_ref, tail_ref,
                           xn_ref, carry_ref, *, tiles_per_seq, z_tiles, strip):
    i = pl.program_id(0)
    j = pl.program_id(1)
    tm, tn = o_ref.shape

    @pl.when(j == 0)
    def _():
        x = x_ref[...]
        ms = jnp.mean(x * x, axis=-1, keepdims=True)
        xn = (x * lax.rsqrt(ms + EPS) * g_ref[...]).astype(BF16)
        xn_ref[...] = xn
        dt_ref[...] = _dot(xn, wdt_ref[...])

    @pl.when((j == 0) & (i % tiles_per_seq == 0))
    def _():
        carry_ref[...] = jnp.zeros_like(carry_ref)

    @pl.when(j < z_tiles)
    def _():
        o_ref[...] = _dot_nt(xn_ref[...], w_ref[...].astype(BF16)).astype(o_ref.dtype)

    @pl.when(j >= z_tiles)
    def _():
        jc = j - z_tiles
        for s in range(tn // strip):
            sl = slice(s * strip, (s + 1) * strip)
            x = _dot_nt(xn_ref[...], w_ref[sl, :].astype(BF16))
            tail_ref[:, sl] = x[tm - SUBLANES:tm, :]
            x2 = _shift_rows(x, carry_ref[jc, 1:3, sl])
            p = x * cw_ref[2:3, sl] + x2 * cw_ref[0:1, sl]
            out = x * cw_ref[3:4, sl] + x2 * cw_ref[1:2, sl] + _shift_rows(p, carry_ref[jc, 0:1, sl]) + cb_ref[:, sl]
            o_ref[:, sl] = _silu(out).astype(o_ref.dtype)
            carry_ref[jc, 0:1, sl] = p[tm - 1:tm]
            carry_ref[jc, 1:3, sl] = x[tm - 2:tm]


def _norm_proj_conv(x, g, wt, w_dt, cw, cb, *, tm, tn, seq):
    m, d = x.shape
    n = SSM_DINNER + SSM_CONVDIM
    z_tiles = SSM_DINNER // tn
    c_tiles = SSM_CONVDIM // tn
    tiles_per_seq = seq // tm
    cj = lambda j: jnp.maximum(j - z_tiles, 0)
    return pl.pallas_call(
        functools.partial(_norm_proj_conv_kernel, tiles_per_seq=tiles_per_seq, z_tiles=z_tiles, strip=256),
        grid=(m // tm, n // tn),
        in_specs=[
            pl.BlockSpec((tm, d), lambda i, j: (i, 0)),
            pl.BlockSpec((1, d), lambda i, j: (0, 0)),
            pl.BlockSpec((tn, d), lambda i, j: (j, 0)),
            pl.BlockSpec((d, LANES), lambda i, j: (0, 0)),
            pl.BlockSpec((SSM_CONV, tn), lambda i, j: (0, cj(j))),
            pl.BlockSpec((1, tn), lambda i, j: (0, cj(j))),
        ],
        out_specs=[
            pl.BlockSpec((tm, tn), lambda i, j: (i, j)),
            pl.BlockSpec((tm, LANES), lambda i, j: (i, 0)),
            pl.BlockSpec((None, SUBLANES, tn), lambda i, j: (i, 0, cj(j))),
        ],
        out_shape=[
            jax.ShapeDtypeStruct((m, n), BF16),
            jax.ShapeDtypeStruct((m, LANES), F32),
            jax.ShapeDtypeStruct((m // tm, SUBLANES, SSM_CONVDIM), F32),
        ],
        scratch_shapes=[
            pltpu.VMEM((tm, d), BF16),
            pltpu.VMEM((c_tiles, SUBLANES, tn), F32),
        ],
        compiler_params=_cparams(("arbitrary", "arbitrary")),
        name="ssd_in_proj_conv",
    )(x, g, wt, w_dt, cw, cb)


def _out_proj_kernel(a_ref, w_ref, h_ref, *rest, has_norm):
    if has_norm:
        g_ref, o_ref = rest
    else:
        (o_ref,) = rest
    y = h_ref[...] + _dot(a_ref[...], w_ref[...])
    if has_norm:
        ms = jnp.mean(y * y, axis=-1, keepdims=True)
        y = y * lax.rsqrt(ms + EPS) * g_ref[...]
    o_ref[...] = y


def _out_proj(a, w, h, *, tm, final_g=None):
    m, k = a.shape
    d = w.shape[1]
    has_norm = final_g is not None
    in_specs = [
        pl.BlockSpec((tm, k), lambda i: (i, 0)),
        pl.BlockSpec((k, d), lambda i: (0, 0)),
        pl.BlockSpec((tm, d), lambda i: (i, 0)),
    ]
    args = [a, w, h]
    if has_norm:
        in_specs.append(pl.BlockSpec((1, d), lambda i: (0, 0)))
        args.append(final_g)
    return pl.pallas_call(
        functools.partial(_out_proj_kernel, has_norm=has_norm),
        grid=(m // tm,),
        in_specs=in_specs,
        out_specs=pl.BlockSpec((tm, d), lambda i: (i, 0)),
        out_shape=jax.ShapeDtypeStruct((m, d), F32),
        compiler_params=_cparams(("arbitrary",)),
        name="out_proj",
    )(*args)


def _rope(x, cos, sin):
    half = RET_DK // 2
    x1 = x[:, :half]
    x2 = x[:, half:]
    return jnp.concatenate([x1 * cos - x2 * sin, x1 * sin + x2 * cos], axis=1)


def _head_norm_gate(o, hn, gate):
    ms = jnp.mean(o * o, axis=-1, keepdims=True)
    on = o * lax.rsqrt(ms + EPS) * hn
    return (on * _silu(gate)).astype(BF16)


def _ret_prompt_kernel(q_ref, k_ref, v_ref, gate_ref, cos_ref, sin_ref, lg_ref, hn_ref,
                       og_ref, st_ref, s_ref, dec_ref, qsc_ref, ksc_ref, gc_ref):
    b = pl.program_id(0)
    c = pl.program_id(1)
    rows = q_ref.shape[0]

    @pl.when((b == 0) & (c == 0))
    def _():
        i = _iota((rows, rows), 0)
        j = _iota((rows, rows), 1)
        r = _iota((rows, RET_DK), 0)
        scale = RET_DK ** -0.5
        for h in range(RET_HEADS):
            lg = lg_ref[h, :, 0:1]
            dec_ref[h] = jnp.where(i >= j, jnp.exp((i - j).astype(F32) * lg), 0.0) * scale
            qsc_ref[h] = jnp.exp((r + 1).astype(F32) * lg)
            ksc_ref[h] = jnp.exp((rows - 1 - r).astype(F32) * lg) * scale
            gc_ref[h] = jnp.exp(jnp.full((1, RET_DV), float(rows), F32) * lg)

    @pl.when(c == 0)
    def _():
        s_ref[...] = jnp.zeros_like(s_ref)

    cos = cos_ref[...]
    sin = sin_ref[...]
    for h in range(RET_HEADS):
        ksl = slice(h * RET_DK, (h + 1) * RET_DK)
        vsl = slice(h * RET_DV, (h + 1) * RET_DV)
        qr = _rope(q_ref[:, ksl].astype(F32), cos, sin)
        kr = _rope(k_ref[:, ksl].astype(F32), cos, sin)
        v = v_ref[:, vsl]
        p = (_dot_nt(qr.astype(BF16), kr.astype(BF16)) * dec_ref[h]).astype(BF16)
        qs = (qr * qsc_ref[h]).astype(BF16)
        s = s_ref[h]
        o = _dot(p, v) + _dot(qs, s.astype(BF16))
        kt = (kr * ksc_ref[h]).T.astype(BF16)
        s_ref[h] = s * gc_ref[h] + _dot(kt, v)
        og_ref[:, vsl] = _head_norm_gate(o, hn_ref[:, vsl], gate_ref[:, vsl].astype(F32))

    @pl.when(c == pl.num_programs(1) - 1)
    def _():
        st_ref[...] = s_ref[...]


def _ret_prompt(proj, cos, sin, lg, hn, batch, seq):
    rows = RET_CHUNK
    nc = seq // rows
    return pl.pallas_call(
        _ret_prompt_kernel,
        grid=(batch, nc),
        in_specs=[
            pl.BlockSpec((rows, RET_QK), lambda b, c: (b * nc + c, 0)),
            pl.BlockSpec((rows, RET_QK), lambda b, c: (b * nc + c, 1)),
            pl.BlockSpec((rows, RET_VW), lambda b, c: (b * nc + c, 2 * RET_QK // RET_VW)),
            pl.BlockSpec((rows, RET_VW), lambda b, c: (b * nc + c, 2 * RET_QK // RET_VW + 1)),
            pl.BlockSpec((rows, RET_DK // 2), lambda b, c: (c, 0)),
            pl.BlockSpec((rows, RET_DK // 2), lambda b, c: (c, 0)),
            pl.BlockSpec((RET_HEADS, 1, LANES), lambda b, c: (0, 0, 0)),
            pl.BlockSpec((1, RET_VW), lambda b, c: (0, 0)),
        ],
        out_specs=[
            pl.BlockSpec((rows, RET_VW), lambda b, c: (b * nc + c, 0)),
            pl.BlockSpec((None, None, RET_HEADS, RET_DK, RET_DV), lambda b, c: (0, b, 0, 0, 0)),
        ],
        out_shape=[
            jax.ShapeDtypeStruct((batch * seq, RET_VW), BF16),
            jax.ShapeDtypeStruct((1, batch, RET_HEADS, RET_DK, RET_DV), F32),
        ],
        scratch_shapes=[
            pltpu.VMEM((RET_HEADS, RET_DK, RET_DV), F32),
            pltpu.VMEM((RET_HEADS, rows, rows), F32),
            pltpu.VMEM((RET_HEADS, rows, RET_DK), F32),
            pltpu.VMEM((RET_HEADS, rows, RET_DK), F32),
            pltpu.VMEM((RET_HEADS, 1, RET_DV), F32),
        ],
        compiler_params=_cparams(("arbitrary", "arbitrary")),
        name="ret_prompt",
    )(proj, proj, proj, proj, cos, sin, lg, hn)


def _ret_sample_kernel(q_ref, k_ref, v_ref, gate_ref, cos_ref, sin_ref, lg_ref, hn_ref, st_in_ref,
                       og_ref, st_out_ref, *, seq):
    bb = st_in_ref.shape[0]
    rows = bb * seq
    lg = lg_ref[:, 0:1]
    scale = RET_DK ** -0.5
    i = _iota((rows, rows), 0)
    j = _iota((rows, rows), 1)
    same = (i // seq) == (j // seq)
    dec = jnp.where(same & (i >= j), jnp.exp((i - j).astype(F32) * lg), 0.0) * scale
    pos = _iota((rows, RET_DK), 0) % seq
    qsc = jnp.exp((pos + 1).astype(F32) * lg)
    ksc = jnp.exp((seq - 1 - pos).astype(F32) * lg) * scale
    gc = jnp.exp(jnp.full((1, RET_DV), float(seq), F32) * lg)

    cos = jnp.concatenate([cos_ref[...]] * bb, axis=0)
    sin = jnp.concatenate([sin_ref[...]] * bb, axis=0)
    qr = _rope(q_ref[...].astype(F32), cos, sin)
    kr = _rope(k_ref[...].astype(F32), cos, sin)
    v = v_ref[...].astype(F32)
    p = _dot_nt(qr.astype(BF16), kr.astype(BF16)) * dec
    intra = _dot(p.astype(BF16), v.astype(BF16))
    qs = qr * qsc
    kt = kr * ksc
    vrow = _iota((rows, RET_DV), 0) // seq
    cross = []
    for n in range(bb):
        s0 = st_in_ref[n]
        cross.append(_dot(qs[n * seq:(n + 1) * seq], s0))
        vn = jnp.where(vrow == n, v, 0.0)
        st_out_ref[n] = s0 * gc + _dot_tn(kt.astype(BF16), vn.astype(BF16))
    o = intra + jnp.concatenate(cross, axis=0)
    og_ref[...] = _head_norm_gate(o, hn_ref[...], gate_ref[...].astype(F32))


def _ret_sample(proj, cos, sin, lg, hn, state, batch, seq):
    bb = RET_SAMPLE_BB
    rows = bb * seq
    qb = RET_QK // RET_DK
    vb = 2 * RET_QK // RET_DV
    gb = vb + RET_VW // RET_DV
    st_spec = pl.BlockSpec((None, bb, None, RET_DK, RET_DV), lambda n, h: (0, n, h, 0, 0))
    return pl.pallas_call(
        functools.partial(_ret_sample_kernel, seq=seq),
        grid=(batch // bb, RET_HEADS),
        in_specs=[
            pl.BlockSpec((rows, RET_DK), lambda n, h: (n, h)),
            pl.BlockSpec((rows, RET_DK), lambda n, h: (n, qb + h)),
            pl.BlockSpec((rows, RET_DV), lambda n, h: (n, vb + h)),
            pl.BlockSpec((rows, RET_DV), lambda n, h: (n, gb + h)),
            pl.BlockSpec((seq, RET_DK // 2), lambda n, h: (0, 0)),
            pl.BlockSpec((seq, RET_DK // 2), lambda n, h: (0, 0)),
            pl.BlockSpec((None, 1, LANES), lambda n, h: (h, 0, 0)),
            pl.BlockSpec((1, RET_DV), lambda n, h: (0, h)),
            st_spec,
        ],
        out_specs=[
            pl.BlockSpec((rows, RET_DV), lambda n, h: (n, h)),
            st_spec,
        ],
        out_shape=[
            jax.ShapeDtypeStruct((batch * seq, RET_VW), BF16),
            jax.ShapeDtypeStruct(state.shape, F32),
        ],
        compiler_params=_cparams(("arbitrary", "arbitrary")),
        name="ret_sample",
    )(proj, proj, proj, proj, cos, sin, lg, hn, state)


def _head_expand(vals, expand):
    parts = []
    for v in vals:
        hi = v.astype(BF16)
        lo = (v - hi.astype(F32)).astype(BF16)
        parts.append(jnp.concatenate([hi, lo], axis=1))
    return _dot(jnp.concatenate(parts, axis=0), expand)


def _ssd_prompt_kernel(z_ref, xc_ref, bc_ref, dtr_ref, dtr_next_ref, dtb_ref, alog_ref, dsk_ref, gn_ref, exp_ref,
                       g_ref, st_ref, s_ref, ex_ref, cum2_ref, cum2t_ref):
    c = pl.program_id(1)
    n_seq, rows = z_ref.shape[0], z_ref.shape[1]
    bw = SSM_GROUP_W
    ri = _iota((rows, rows), 0)
    rj = _iota((rows, rows), 1)
    causal = ri >= rj

    def decay_tables(n, dtr):
        dt = _softplus(dtr + dtb_ref[...])
        la = dt * (-jnp.exp(alog_ref[...]))
        cum = jnp.dot(causal.astype(F32), la, precision=lax.Precision.HIGHEST, preferred_element_type=F32)
        ecum = jnp.exp(cum)
        wdt = jnp.exp(cum[rows - 1:rows, :] - cum) * dt
        ex_ref[n] = _head_expand([ecum, wdt, dt], exp_ref[...])
        cum2 = cum * math.log2(math.e)
        cum2_ref[n] = cum2
        cum2t_ref[n] = cum2.T

    @pl.when(c == 0)
    def _():
        s_ref[...] = jnp.zeros_like(s_ref)
        for n in range(n_seq):
            decay_tables(n, dtr_ref[n])

    head_lane = _iota((rows, bw), 1) // SSM_HEADDIM

    for n in range(n_seq):
        cum2 = cum2_ref[n]
        cum2_t = cum2t_ref[n]
        for g in range(SSM_GROUPS):
            gsl = slice(g * bw, (g + 1) * bw)
            bm = bc_ref[n, :, g * SSM_DSTATE:(g + 1) * SSM_DSTATE]
            cm = bc_ref[n, :, (SSM_GROUPS + g) * SSM_DSTATE:(SSM_GROUPS + g + 1) * SSM_DSTATE]
            xs = xc_ref[n, :, gsl].astype(F32)
            s = s_ref[n, g]
            gmat = _dot_nt(cm, bm)
            ps = []
            for hh in range(HEADS_PER_GROUP):
                h = g * HEADS_PER_GROUP + hh
                colb = jnp.broadcast_to(cum2[:, h:h + 1], (rows, LANES))
                dec = jnp.exp2(jnp.where(causal, colb - cum2_t[h:h + 1, :], -jnp.inf))
                ps.append((gmat * dec).astype(BF16))
            v = (xs * ex_ref[n, 2 * rows:3 * rows, gsl]).astype(BF16)
            y4 = _dot(jnp.concatenate(ps, axis=0), v)
            y = y4[0:rows]
            for hh in range(1, HEADS_PER_GROUP):
                y = jnp.where(head_lane == hh, y4[hh * rows:(hh + 1) * rows], y)
            y = y + _dot(cm, s.astype(BF16)) * ex_ref[n, 0:rows, gsl]
            vw = (xs * ex_ref[n, rows:2 * rows, gsl]).astype(BF16)
            s_ref[n, g] = s * ex_ref[n, rows - 1:rows, gsl] + _dot_tn(bm, vw)

            y = y + xs * dsk_ref[:, gsl]
            gg = y * _silu(z_ref[n, :, gsl].astype(F32))
            ms = jnp.mean(gg * gg, axis=-1, keepdims=True)
            g_ref[n, :, gsl] = (gg * lax.rsqrt(ms + EPS) * gn_ref[:, gsl]).astype(BF16)

    for n in range(n_seq):
        decay_tables(n, dtr_next_ref[n])

    @pl.when(c == pl.num_programs(1) - 1)
    def _():
        for n in range(n_seq):
            for g in range(SSM_GROUPS):
                st_ref[n, g * bw:(g + 1) * bw, :] = s_ref[n, g].T


def _ssd_prompt(zxc, dtr, dtb, alog, dsk, gn, expand, batch, seq):
    rows = SSD_CHUNK
    nc = seq // rows
    ns = SSD_PROMPT_SEQS
    full = lambda shape: pl.BlockSpec(shape, lambda b, c: (0,) * len(shape))
    col = lambda k: pl.BlockSpec((ns, rows, SSM_DINNER), lambda b, c: (b, c, k))
    return pl.pallas_call(
        _ssd_prompt_kernel,
        grid=(batch // ns, nc),
        in_specs=[
            col(0), col(1), col(2),
            pl.BlockSpec((ns, rows, LANES), lambda b, c: (b, c, 0)),
            pl.BlockSpec((ns, rows, LANES), lambda b, c: (b, jnp.minimum(c + 1, nc - 1), 0)),
            full((1, LANES)), full((1, LANES)),
            full((1, SSM_DINNER)), full((1, SSM_DINNER)),
            full((2 * LANES, SSM_DINNER)),
        ],
        out_specs=[
            pl.BlockSpec((ns, rows, SSM_DINNER), lambda b, c: (b, c, 0)),
            pl.BlockSpec((ns, SSM_DINNER, SSM_DSTATE), lambda b, c: (b, 0, 0)),
        ],
        out_shape=[
            jax.ShapeDtypeStruct((batch, seq, SSM_DINNER), BF16),
            jax.ShapeDtypeStruct((batch, SSM_DINNER, SSM_DSTATE), F32),
        ],
        scratch_shapes=[
            pltpu.VMEM((ns, SSM_GROUPS, SSM_DSTATE, SSM_GROUP_W), F32),
            pltpu.VMEM((ns, 3 * rows, SSM_DINNER), F32),
            pltpu.VMEM((ns, rows, LANES), F32),
            pltpu.VMEM((ns, LANES, rows), F32),
        ],
        compiler_params=_cparams(("arbitrary", "arbitrary")),
        name="ssd_prompt",
    )(zxc, zxc, zxc, dtr, dtr, dtb, alog, dsk, gn, expand)


def _ssd_sample_kernel(z_ref, xa_ref, xb_ref, dtr_ref, conv_in_ref, st_in_ref,
                       cw_ref, cb_ref, dtb_ref, alog_ref, dsk_ref, gn_ref, exp_ref,
                       g_ref, st_out_ref, xpad_ref, xc_ref, ex_ref, y_ref, *, seq):
    bb = st_in_ref.shape[0]
    rows = bb * seq
    pad = SUBLANES
    keep = SSM_CONV - 1
    bw = SSM_GROUP_W

    for n in range(bb):
        r0 = n * seq
        xpad_ref[pad - keep:pad, :] = conv_in_ref[n]
        xpad_ref[pad:pad + seq, 0:SSM_DINNER] = xa_ref[r0:r0 + seq, :]
        xpad_ref[pad:pad + seq, SSM_DINNER:SSM_CONVDIM] = xb_ref[r0:r0 + seq, :]
        acc = cb_ref[...]
        for w in range(SSM_CONV):
            lo = pad - keep + w
            acc = acc + xpad_ref[lo:lo + seq, :] * cw_ref[w:w + 1, :]
        xc_ref[r0:r0 + seq, :] = _silu(acc)

    dt = _softplus(dtr_ref[...] + dtb_ref[...])
    la = dt * (-jnp.exp(alog_ref[...]))
    ri = _iota((rows, rows), 0)
    rj = _iota((rows, rows), 1)
    same = (ri // seq) == (rj // seq)
    cum = jnp.dot((same & (ri >= rj)).astype(F32), la, precision=lax.Precision.HIGHEST,
                  preferred_element_type=F32)
    clast = jnp.dot(same.astype(F32), la, precision=lax.Precision.HIGHEST, preferred_element_type=F32)
    eclast = jnp.exp(clast)
    ex_ref[...] = _head_expand([jnp.exp(cum), jnp.exp(clast - cum) * dt, dt], exp_ref[...])

    si = _iota((seq, seq), 0)
    sj = _iota((seq, seq), 1)
    causal = si >= sj
    eye = si == sj
    head_lane = _iota((seq, bw), 1) // SSM_HEADDIM

    for n in range(bb):
        r0 = n * seq
        rsl = slice(r0, r0 + seq)
        cum_n = cum[rsl]
        for g in range(SSM_GROUPS):
            gsl = slice(g * bw, (g + 1) * bw)
            bm = xc_ref[rsl, SSM_DINNER + g * SSM_DSTATE:SSM_DINNER + (g + 1) * SSM_DSTATE]
            cm = xc_ref[rsl, SSM_DINNER + (SSM_GROUPS + g) * SSM_DSTATE:
                        SSM_DINNER + (SSM_GROUPS + g + 1) * SSM_DSTATE]
            xs = xc_ref[rsl, gsl]
            gmat = _dot_nt(cm.astype(BF16), bm.astype(BF16))
            ps = []
            for hh in range(HEADS_PER_GROUP):
                h = g * HEADS_PER_GROUP + hh
                ccol = cum_n[:, h:h + 1]
                crow = jnp.sum(jnp.where(eye, ccol, 0.0), axis=0, keepdims=True)
                ps.append(gmat * jnp.exp(jnp.where(causal, ccol - crow, -jnp.inf)))
            v = xs * ex_ref[2 * rows + r0:2 * rows + r0 + seq, gsl]
            y4 = _dot(jnp.concatenate(ps, axis=0).astype(BF16), v.astype(BF16))
            y = y4[0:seq]
            for hh in range(1, HEADS_PER_GROUP):
                y = jnp.where(head_lane == hh, y4[hh * seq:(hh + 1) * seq], y)
            st = st_in_ref[n, gsl, :]
            y = y + _dot_nt(cm.astype(BF16), st.astype(BF16)) * ex_ref[rsl, gsl]
            y_ref[rsl, gsl] = y
            vw = xs * ex_ref[rows + r0:rows + r0 + seq, gsl]
            upd = _dot_tn(vw.astype(BF16), bm.astype(BF16))
            for hh in range(HEADS_PER_GROUP):
                h = g * HEADS_PER_GROUP + hh
                hsl = slice(hh * SSM_HEADDIM, (hh + 1) * SSM_HEADDIM)
                osl = slice(g * bw + hh * SSM_HEADDIM, g * bw + (hh + 1) * SSM_HEADDIM)
                st_out_ref[n, osl, :] = st[hsl] * eclast[r0:r0 + 1, h:h + 1] + upd[hsl]

    y = y_ref[...] + xc_ref[:, 0:SSM_DINNER] * dsk_ref[...]
    gg = y * _silu(z_ref[...])
    for g in range(SSM_GROUPS):
        gsl = slice(g * bw, (g + 1) * bw)
        blk = gg[:, gsl]
        ms = jnp.mean(blk * blk, axis=-1, keepdims=True)
        g_ref[:, gsl] = (blk * lax.rsqrt(ms + EPS) * gn_ref[:, gsl]).astype(BF16)


def _ssd_sample(zx, dtr, conv_state, ssm_state_t, cw, cb, dtb, alog, dsk, gn, expand, batch, seq):
    bb = SSD_SAMPLE_BB
    rows = bb * seq
    keep = SSM_CONV - 1
    full = lambda shape: pl.BlockSpec(shape, lambda n: (0,) * len(shape))
    col = lambda k: pl.BlockSpec((rows, SSM_DINNER), lambda n: (n, k))
    st_spec = pl.BlockSpec((bb, SSM_DINNER, SSM_DSTATE), lambda n: (n, 0, 0))
    return pl.pallas_call(
        functools.partial(_ssd_sample_kernel, seq=seq),
        grid=(batch // bb,),
        in_specs=[
            col(0), col(1), col(2),
            pl.BlockSpec((rows, LANES), lambda n: (n, 0)),
            pl.BlockSpec((None, bb, keep, SSM_CONVDIM), lambda n: (0, n, 0, 0)),
            st_spec,
            full((SSM_CONV, SSM_CONVDIM)), full((1, SSM_CONVDIM)),
            full((1, LANES)), full((1, LANES)),
            full((1, SSM_DINNER)), full((1, SSM_DINNER)),
            full((2 * LANES, SSM_DINNER)),
        ],
        out_specs=[
            pl.BlockSpec((rows, SSM_DINNER), lambda n: (n, 0)),
            st_spec,
        ],
        out_shape=[
            jax.ShapeDtypeStruct((batch * seq, SSM_DINNER), BF16),
            jax.ShapeDtypeStruct(ssm_state_t.shape, F32),
        ],
        scratch_shapes=[
            pltpu.VMEM((2 * SUBLANES, SSM_CONVDIM), F32),
            pltpu.VMEM((rows, SSM_CONVDIM), F32),
            pltpu.VMEM((3 * rows, SSM_DINNER), F32),
            pltpu.VMEM((rows, SSM_DINNER), F32),
        ],
        compiler_params=_cparams(("arbitrary",)),
        name="ssd_sample",
    )(zx, zx, zx, dtr, conv_state, ssm_state_t, cw, cb, dtb, alog, dsk, gn, expand)


def _rope_tables(pos):
    half = RET_DK // 2
    freqs = ROPE_BASE ** (-jnp.arange(half, dtype=F32) / half)
    ang = pos.astype(F32)[:, None] * freqs[None, :]
    return jnp.cos(ang), jnp.sin(ang)


def _row(v):
    return v.reshape(1, -1).astype(F32)


def _pad_lanes(v):
    return jnp.pad(v.astype(F32), (0, LANES - v.shape[0])).reshape(1, LANES)


def kernel(x_prompt, x_sample, state_ret, state_ssm, state_conv, ret_norm, ret_w_in, ret_head_norm, ret_w_out, ssm_norm, ssm_w_in, ssm_conv_w, ssm_conv_b, ssm_dt_bias, ssm_a_log, ssm_d, ssm_gate_norm, ssm_w_out, final_norm):
    bp, lp, d = x_prompt.shape
    bs, ls, _ = x_sample.shape
    assert ret_norm.shape[0] == 1 and ssm_norm.shape[0] == 1, "one retention and one SSD layer"

    ret_win = ret_w_in[0]
    ret_wout = ret_w_out[0].astype(BF16)
    n_main = SSM_DINNER + SSM_CONVDIM
    ssm_win_t = jnp.swapaxes(ssm_w_in[0], 0, 1)
    ssm_wdt = jnp.pad(ssm_w_in[0, :, n_main:], ((0, 0), (0, LANES - SSM_HEADS))).astype(BF16)
    ssm_wout = ssm_w_out[0].astype(BF16)
    expand = (jnp.arange(2 * LANES)[:, None] % LANES == jnp.arange(SSM_DINNER)[None, :] // SSM_HEADDIM).astype(BF16)
    lg = jnp.log(1.0 - 2.0 ** (-5.0 - jnp.arange(RET_HEADS, dtype=F32)))
    lg = jnp.broadcast_to(lg[:, None, None], (RET_HEADS, 1, LANES))
    hn = _row(ret_head_norm[0])
    dsk = _row(jnp.repeat(ssm_d[0], SSM_HEADDIM))
    gn = _row(ssm_gate_norm[0])
    cw = ssm_conv_w[0].astype(F32)
    cb = _row(ssm_conv_b[0])
    dtb = _pad_lanes(ssm_dt_bias[0])
    alog = _pad_lanes(ssm_a_log[0])
    g_ret = _row(ret_norm[0])
    g_ssm = _row(ssm_norm[0])
    g_fin = _row(final_norm)

    xp = x_prompt.reshape(bp * lp, d)
    cos_p, sin_p = _rope_tables(jnp.arange(lp, dtype=jnp.int32))
    n_ret = 2 * RET_QK + 2 * RET_VW
    proj = _norm_proj(xp, g_ret, ret_win, n=n_ret, tm=2048, tn=1024, out_dtype=BF16)
    og, ret_p = _ret_prompt(proj, cos_p, sin_p, lg, hn, bp, lp)
    h1 = _out_proj(og, ret_wout, xp, tm=1024)
    tm_c = 1024
    zxc, dtr, tail = _norm_proj_conv(h1, g_ssm, ssm_win_t, ssm_wdt, cw, cb, tm=tm_c, tn=2048, seq=lp)
    tail = tail[lp // tm_c - 1::lp // tm_c]
    gated, ssm_pt = _ssd_prompt(zxc.reshape(bp, lp, -1), dtr.reshape(bp, lp, LANES), dtb, alog, dsk, gn, expand, bp, lp)
    y_p = _out_proj(gated.reshape(bp * lp, SSM_DINNER), ssm_wout, h1, tm=1024, final_g=g_fin)
    ssm_p = jnp.swapaxes(ssm_pt.reshape(1, bp, SSM_HEADS, SSM_HEADDIM, SSM_DSTATE), -1, -2)
    conv_p = tail[None, :, SUBLANES - (SSM_CONV - 1):, :]

    xs = x_sample.reshape(bs * ls, d)
    cos_s, sin_s = _rope_tables(PAST_LEN + jnp.arange(ls, dtype=jnp.int32))
    proj_s = _norm_proj(xs, g_ret, ret_win, n=n_ret, tm=bs * ls, tn=2048, out_dtype=BF16)
    og_s, ret_s = _ret_sample(proj_s, cos_s, sin_s, lg, hn, state_ret, bs, ls)
    h1_s = _out_proj(og_s, ret_wout, xs, tm=512)
    zx_s, dtr_s = _norm_proj(h1_s, g_ssm, ssm_win_t, n=n_main, tm=bs * ls, tn=2048, out_dtype=F32, w_dt=ssm_wdt,
                             w_is_t=True)
    st_t = jnp.swapaxes(state_ssm[0], -1, -2).reshape(bs, SSM_DINNER, SSM_DSTATE)
    gated_s, ssm_st = _ssd_sample(zx_s, dtr_s, state_conv, st_t, cw, cb, dtb, alog, dsk, gn, expand, bs, ls)
    ssm_s = jnp.swapaxes(ssm_st.reshape(1, bs, SSM_HEADS, SSM_HEADDIM, SSM_DSTATE), -1, -2)
    keep = SSM_CONV - 1
    conv_s = zx_s.reshape(bs, ls, -1)[None, :, ls - keep:, SSM_DINNER:]
    y_s = _out_proj(gated_s, ssm_wout, h1_s, tm=512, final_g=g_fin)

    return (y_p.reshape(bp, lp, d), y_s.reshape(bs, ls, d), ret_p, ret_s, ssm_p, ssm_s, conv_p, conv_s)
```

```python
import functools
import math

import jax
import jax.numpy as jnp
from jax import lax
from jax.experimental import pallas as pl
from jax.experimental.pallas import tpu as pltpu

F32 = jnp.float32
BF16 = jnp.bfloat16

D_MODEL = 1024
RET_HEADS = 4
RET_DK = 256
RET_DV = 512
RET_VW = RET_HEADS * RET_DV
RET_QK = RET_HEADS * RET_DK
SSM_DINNER = 2048
SSM_HEADDIM = 64
SSM_HEADS = 32
SSM_GROUPS = 8
SSM_DSTATE = 128
SSM_CONV = 4
SSM_CONVDIM = 4096
SSM_GROUP_W = SSM_DINNER // SSM_GROUPS
HEADS_PER_GROUP = SSM_HEADS // SSM_GROUPS
ROPE_BASE = 10000.0
EPS = 1e-6
PAST_LEN = 16384

LANES = 128
SUBLANES = 8
VMEM_LIMIT = 56 * 1024 * 1024

RET_CHUNK = 256
SSD_CHUNK = 128
SSD_PROMPT_SEQS = 2
RET_SAMPLE_BB = 16
SSD_SAMPLE_BB = 4


def _cparams(sem, flags=None):
    return pltpu.CompilerParams(dimension_semantics=sem, vmem_limit_bytes=VMEM_LIMIT, flags=flags)


def _sigmoid(x):
    return 1.0 / (1.0 + jnp.exp2(x * (-math.log2(math.e))))


def _silu(x):
    return x * _sigmoid(x)


def _softplus(x):
    return jnp.maximum(x, 0.0) + jnp.log1p(jnp.exp(-jnp.abs(x)))


def _dot(a, b):
    return jnp.dot(a, b, preferred_element_type=F32)


def _dot_nt(a, b):
    return lax.dot_general(a, b, (((1,), (1,)), ((), ())), preferred_element_type=F32)


def _dot_tn(a, b):
    return lax.dot_general(a, b, (((0,), (0,)), ((), ())), preferred_element_type=F32)


def _iota(shape, dim):
    return lax.broadcasted_iota(jnp.int32, shape, dim)


def _norm_proj_kernel(x_ref, g_ref, w_ref, *rest, has_dt, w_is_t):
    if has_dt:
        wdt_ref, o_ref, dt_ref, xn_ref = rest
    else:
        o_ref, xn_ref = rest

    @pl.when(pl.program_id(1) == 0)
    def _():
        x = x_ref[...]
        ms = jnp.mean(x * x, axis=-1, keepdims=True)
        xn = (x * lax.rsqrt(ms + EPS) * g_ref[...]).astype(BF16)
        xn_ref[...] = xn
        if has_dt:
            dt_ref[...] = _dot(xn, wdt_ref[...])

    w = w_ref[...].astype(BF16)
    o_ref[...] = (_dot_nt(xn_ref[...], w) if w_is_t else _dot(xn_ref[...], w)).astype(o_ref.dtype)


def _norm_proj(x, g, w, *, n, tm, tn, out_dtype, w_dt=None, w_is_t=False):
    m, d = x.shape
    has_dt = w_dt is not None
    in_specs = [
        pl.BlockSpec((tm, d), lambda i, j: (i, 0)),
        pl.BlockSpec((1, d), lambda i, j: (0, 0)),
        pl.BlockSpec((tn, d), lambda i, j: (j, 0)) if w_is_t else pl.BlockSpec((d, tn), lambda i, j: (0, j)),
    ]
    args = [x, g, w]
    out_shape = [jax.ShapeDtypeStruct((m, n), out_dtype)]
    out_specs = [pl.BlockSpec((tm, tn), lambda i, j: (i, j))]
    if has_dt:
        in_specs.append(pl.BlockSpec((d, LANES), lambda i, j: (0, 0)))
        args.append(w_dt)
        out_shape.append(jax.ShapeDtypeStruct((m, LANES), F32))
        out_specs.append(pl.BlockSpec((tm, LANES), lambda i, j: (i, 0)))
    res = pl.pallas_call(
        functools.partial(_norm_proj_kernel, has_dt=has_dt, w_is_t=w_is_t),
        grid=(m // tm, n // tn),
        in_specs=in_specs,
        out_specs=out_specs,
        out_shape=out_shape,
        scratch_shapes=[pltpu.VMEM((tm, d), BF16)],
        compiler_params=_cparams(("arbitrary", "arbitrary")),
        name="in_proj",
    )(*args)
    return res if has_dt else res[0]


def _shift_rows(a, first_rows):
    rows, cols = a.shape
    n = first_rows.shape[0]
    slabs = rows // SUBLANES
    rot = pltpu.roll(a.reshape(slabs, SUBLANES, cols), n, axis=1)
    first = jnp.concatenate([first_rows, jnp.zeros((SUBLANES - n, cols), a.dtype)], axis=0)
    above = jnp.concatenate([first.reshape(1, SUBLANES, cols), rot[:slabs - 1]], axis=0)
    top = _iota((slabs, SUBLANES, cols), 1) < n
    return jnp.where(top, above, rot).reshape(rows, cols)


def _norm_proj_conv_kernel(x_ref, g_ref, w_ref, wdt_ref, cw_ref, cb_ref, o_ref, dt_ref, tail_ref,
                           xn_ref, carry_ref, *, tiles_per_seq, z_tiles, strip):
    i = pl.program_id(0)
    j = pl.program_id(1)
    tm, tn = o_ref.shape

    @pl.when(j == 0)
    def _():
        x = x_ref[...]
        ms = jnp.mean(x * x, axis=-1, keepdims=True)
        xn = (x * lax.rsqrt(ms + EPS) * g_ref[...]).astype(BF16)
        xn_ref[...] = xn
        dt_ref[...] = _dot(xn, wdt_ref[...])

    @pl.when((j == 0) & (i % tiles_per_seq == 0))
    def _():
        carry_ref[...] = jnp.zeros_like(carry_ref)

    @pl.when(j < z_tiles)
    def _():
        o_ref[...] = _dot_nt(xn_ref[...], w_ref[...].astype(BF16)).astype(o_ref.dtype)

    @pl.when(j >= z_tiles)
    def _():
        jc = j - z_tiles
        for s in range(tn // strip):
            sl = slice(s * strip, (s + 1) * strip)
            x = _dot_nt(xn_ref[...], w_ref[sl, :].astype(BF16))
            tail_ref[:, sl] = x[tm - SUBLANES:tm, :]
            x2 = _shift_rows(x, carry_ref[jc, 1:3, sl])
            p = x * cw_ref[2:3, sl] + x2 * cw_ref[0:1, sl]
            out = x * cw_ref[3:4, sl] + x2 * cw_ref[1:2, sl] + _shift_rows(p, carry_ref[jc, 0:1, sl]) + cb_ref[:, sl]
            o_ref[:, sl] = _silu(out.astype(BF16))
            carry_ref[jc, 0:1, sl] = p[tm - 1:tm]
            carry_ref[jc, 1:3, sl] = x[tm - 2:tm]


def _norm_proj_conv(x, g, wt, w_dt, cw, cb, *, tm, tn, seq):
    m, d = x.shape
    n = SSM_DINNER + SSM_CONVDIM
    z_tiles = SSM_DINNER // tn
    c_tiles = SSM_CONVDIM // tn
    tiles_per_seq = seq // tm
    cj = lambda j: jnp.maximum(j - z_tiles, 0)
    return pl.pallas_call(
        functools.partial(_norm_proj_conv_kernel, tiles_per_seq=tiles_per_seq, z_tiles=z_tiles, strip=256),
        grid=(m // tm, n // tn),
        in_specs=[
            pl.BlockSpec((tm, d), lambda i, j: (i, 0)),
            pl.BlockSpec((1, d), lambda i, j: (0, 0)),
            pl.BlockSpec((tn, d), lambda i, j: (j, 0)),
            pl.BlockSpec((d, LANES), lambda i, j: (0, 0)),
            pl.BlockSpec((SSM_CONV, tn), lambda i, j: (0, cj(j))),
            pl.BlockSpec((1, tn), lambda i, j: (0, cj(j))),
        ],
        out_specs=[
            pl.BlockSpec((tm, tn), lambda i, j: (i, j)),
            pl.BlockSpec((tm, LANES), lambda i, j: (i, 0)),
            pl.BlockSpec((None, SUBLANES, tn), lambda i, j: (i, 0, cj(j))),
        ],
        out_shape=[
            jax.ShapeDtypeStruct((m, n), BF16),
            jax.ShapeDtypeStruct((m, LANES), F32),
            jax.ShapeDtypeStruct((m // tm, SUBLANES, SSM_CONVDIM), F32),
        ],
        scratch_shapes=[
            pltpu.VMEM((tm, d), BF16),
            pltpu.VMEM((c_tiles, SUBLANES, tn), F32),
        ],
        compiler_params=_cparams(("arbitrary", "arbitrary")),
        name="ssd_in_proj_conv",
    )(x, g, wt, w_dt, cw, cb)


def _out_proj_kernel(a_ref, w_ref, h_ref, *rest, has_norm):
    if has_norm:
        g_ref, o_ref = rest
    else:
        (o_ref,) = rest
    y = h_ref[...] + _dot(a_ref[...], w_ref[...])
    if has_norm:
        ms = jnp.mean(y * y, axis=-1, keepdims=True)
        y = y * lax.rsqrt(ms + EPS) * g_ref[...]
    o_ref[...] = y


def _out_proj(a, w, h, *, tm, final_g=None):
    m, k = a.shape
    d = w.shape[1]
    has_norm = final_g is not None
    in_specs = [
        pl.BlockSpec((tm, k), lambda i: (i, 0)),
        pl.BlockSpec((k, d), lambda i: (0, 0)),
        pl.BlockSpec((tm, d), lambda i: (i, 0)),
    ]
    args = [a, w, h]
    if has_norm:
        in_specs.append(pl.BlockSpec((1, d), lambda i: (0, 0)))
        args.append(final_g)
    return pl.pallas_call(
        functools.partial(_out_proj_kernel, has_norm=has_norm),
        grid=(m // tm,),
        in_specs=in_specs,
        out_specs=pl.BlockSpec((tm, d), lambda i: (i, 0)),
        out_shape=jax.ShapeDtypeStruct((m, d), F32),
        compiler_params=_cparams(("arbitrary",)),
        name="out_proj",
    )(*args)


def _rope(x, cos, sin):
    half = RET_DK // 2
    x1 = x[:, :half]
    x2 = x[:, half:]
    return jnp.concatenate([x1 * cos - x2 * sin, x1 * sin + x2 * cos], axis=1)


def _head_norm_gate(o, hn, gate):
    ms = jnp.mean(o * o, axis=-1, keepdims=True)
    on = o * lax.rsqrt(ms + EPS) * hn
    return (on * _silu(gate).astype(F32)).astype(BF16)


def _ret_prompt_kernel(q_ref, k_ref, v_ref, gate_ref, cos_ref, sin_ref, lg_ref, hn_ref,
                       og_ref, st_ref, s_ref, dec_ref, qsc_ref, ksc_ref, gc_ref):
    b = pl.program_id(0)
    c = pl.program_id(1)
    rows = q_ref.shape[0]

    @pl.when((b == 0) & (c == 0))
    def _():
        i = _iota((rows, rows), 0)
        j = _iota((rows, rows), 1)
        r = _iota((rows, RET_DK), 0)
        scale = RET_DK ** -0.5
        for h in range(RET_HEADS):
            lg = lg_ref[h, :, 0:1]
            dec_ref[h] = jnp.where(i >= j, jnp.exp((i - j).astype(F32) * lg), 0.0) * scale
            qsc_ref[h] = jnp.exp((r + 1).astype(F32) * lg)
            ksc_ref[h] = jnp.exp((rows - 1 - r).astype(F32) * lg) * scale
            gc_ref[h] = jnp.exp(jnp.full((1, RET_DV), float(rows), F32) * lg)

    @pl.when(c == 0)
    def _():
        s_ref[...] = jnp.zeros_like(s_ref)

    cos = cos_ref[...]
    sin = sin_ref[...]
    for h in range(RET_HEADS):
        ksl = slice(h * RET_DK, (h + 1) * RET_DK)
        vsl = slice(h * RET_DV, (h + 1) * RET_DV)
        qr = _rope(q_ref[:, ksl].astype(F32), cos, sin)
        kr = _rope(k_ref[:, ksl].astype(F32), cos, sin)
        v = v_ref[:, vsl]
        p = (_dot_nt(qr.astype(BF16), kr.astype(BF16)) * dec_ref[h]).astype(BF16)
        qs = (qr * qsc_ref[h]).astype(BF16)
        s = s_ref[h]
        o = _dot(p, v) + _dot(qs, s.astype(BF16))
        kt = (kr * ksc_ref[h]).T.astype(BF16)
        s_ref[h] = s * gc_ref[h] + _dot(kt, v)
        og_ref[:, vsl] = _head_norm_gate(o, hn_ref[:, vsl], gate_ref[:, vsl])

    @pl.when(c == pl.num_programs(1) - 1)
    def _():
        st_ref[...] = s_ref[...]


def _ret_prompt(proj, cos, sin, lg, hn, batch, seq):
    rows = RET_CHUNK
    nc = seq // rows
    return pl.pallas_call(
        _ret_prompt_kernel,
        grid=(batch, nc),
        in_specs=[
            pl.BlockSpec((rows, RET_QK), lambda b, c: (b * nc + c, 0)),
            pl.BlockSpec((rows, RET_QK), lambda b, c: (b * nc + c, 1)),
            pl.BlockSpec((rows, RET_VW), lambda b, c: (b * nc + c, 2 * RET_QK // RET_VW)),
            pl.BlockSpec((rows, RET_VW), lambda b, c: (b * nc + c, 2 * RET_QK // RET_VW + 1)),
            pl.BlockSpec((rows, RET_DK // 2), lambda b, c: (c, 0)),
            pl.BlockSpec((rows, RET_DK // 2), lambda b, c: (c, 0)),
            pl.BlockSpec((RET_HEADS, 1, LANES), lambda b, c: (0, 0, 0)),
            pl.BlockSpec((1, RET_VW), lambda b, c: (0, 0)),
        ],
        out_specs=[
            pl.BlockSpec((rows, RET_VW), lambda b, c: (b * nc + c, 0)),
            pl.BlockSpec((None, None, RET_HEADS, RET_DK, RET_DV), lambda b, c: (0, b, 0, 0, 0)),
        ],
        out_shape=[
            jax.ShapeDtypeStruct((batch * seq, RET_VW), BF16),
            jax.ShapeDtypeStruct((1, batch, RET_HEADS, RET_DK, RET_DV), F32),
        ],
        scratch_shapes=[
            pltpu.VMEM((RET_HEADS, RET_DK, RET_DV), F32),
            pltpu.VMEM((RET_HEADS, rows, rows), F32),
            pltpu.VMEM((RET_HEADS, rows, RET_DK), F32),
            pltpu.VMEM((RET_HEADS, rows, RET_DK), F32),
            pltpu.VMEM((RET_HEADS, 1, RET_DV), F32),
        ],
        compiler_params=_cparams(("arbitrary", "arbitrary")),
        name="ret_prompt",
    )(proj, proj, proj, proj, cos, sin, lg, hn)


def _ret_sample_kernel(q_ref, k_ref, v_ref, gate_ref, cos_ref, sin_ref, lg_ref, hn_ref, st_in_ref,
                       og_ref, st_out_ref, *, seq):
    bb = st_in_ref.shape[0]
    rows = bb * seq
    lg = lg_ref[:, 0:1]
    scale = RET_DK ** -0.5
    i = _iota((rows, rows), 0)
    j = _iota((rows, rows), 1)
    same = (i // seq) == (j // seq)
    dec = jnp.where(same & (i >= j), jnp.exp((i - j).astype(F32) * lg), 0.0) * scale
    pos = _iota((rows, RET_DK), 0) % seq
    qsc = jnp.exp((pos + 1).astype(F32) * lg)
    ksc = jnp.exp((seq - 1 - pos).astype(F32) * lg) * scale
    gc = jnp.exp(jnp.full((1, RET_DV), float(seq), F32) * lg)

    cos = jnp.concatenate([cos_ref[...]] * bb, axis=0)
    sin = jnp.concatenate([sin_ref[...]] * bb, axis=0)
    qr = _rope(q_ref[...].astype(F32), cos, sin)
    kr = _rope(k_ref[...].astype(F32), cos, sin)
    v = v_ref[...].astype(F32)
    p = _dot_nt(qr.astype(BF16), kr.astype(BF16)) * dec
    intra = _dot(p.astype(BF16), v.astype(BF16))
    qs = qr * qsc
    kt = kr * ksc
    vrow = _iota((rows, RET_DV), 0) // seq
    cross = []
    for n in range(bb):
        s0 = st_in_ref[n]
        cross.append(_dot(qs[n * seq:(n + 1) * seq], s0))
        vn = jnp.where(vrow == n, v, 0.0)
        st_out_ref[n] = s0 * gc + _dot_tn(kt.astype(BF16), vn.astype(BF16))
    o = intra + jnp.concatenate(cross, axis=0)
    og_ref[...] = _head_norm_gate(o, hn_ref[...], gate_ref[...])


def _ret_sample(proj, cos, sin, lg, hn, state, batch, seq):
    bb = RET_SAMPLE_BB
    rows = bb * seq
    qb = RET_QK // RET_DK
    vb = 2 * RET_QK // RET_DV
    gb = vb + RET_VW // RET_DV
    st_spec = pl.BlockSpec((None, bb, None, RET_DK, RET_DV), lambda n, h: (0, n, h, 0, 0))
    return pl.pallas_call(
        functools.partial(_ret_sample_kernel, seq=seq),
        grid=(batch // bb, RET_HEADS),
        in_specs=[
            pl.BlockSpec((rows, RET_DK), lambda n, h: (n, h)),
            pl.BlockSpec((rows, RET_DK), lambda n, h: (n, qb + h)),
            pl.BlockSpec((rows, RET_DV), lambda n, h: (n, vb + h)),
            pl.BlockSpec((rows, RET_DV), lambda n, h: (n, gb + h)),
            pl.BlockSpec((seq, RET_DK // 2), lambda n, h: (0, 0)),
            pl.BlockSpec((seq, RET_DK // 2), lambda n, h: (0, 0)),
            pl.BlockSpec((None, 1, LANES), lambda n, h: (h, 0, 0)),
            pl.BlockSpec((1, RET_DV), lambda n, h: (0, h)),
            st_spec,
        ],
        out_specs=[
            pl.BlockSpec((rows, RET_DV), lambda n, h: (n, h)),
            st_spec,
        ],
        out_shape=[
            jax.ShapeDtypeStruct((batch * seq, RET_VW), BF16),
            jax.ShapeDtypeStruct(state.shape, F32),
        ],
        compiler_params=_cparams(("arbitrary", "arbitrary")),
        name="ret_sample",
    )(proj, proj, proj, proj, cos, sin, lg, hn, state)


def _head_expand(vals, expand):
    parts = []
    for v in vals:
        hi = v.astype(BF16)
        lo = (v - hi.astype(F32)).astype(BF16)
        parts.append(jnp.concatenate([hi, lo], axis=1))
    return _dot(jnp.concatenate(parts, axis=0), expand)


def _ssd_prompt_kernel(z_ref, xc_ref, bc_ref, dtr_ref, dtr_next_ref, dtb_ref, alog_ref, dsk_ref, gn_ref, exp_ref,
                       g_ref, st_ref, s_ref, ex_ref, cum2_ref, cum2t_ref):
    c = pl.program_id(1)
    n_seq, rows = z_ref.shape[0], z_ref.shape[1]
    bw = SSM_GROUP_W
    ri = _iota((rows, rows), 0)
    rj = _iota((rows, rows), 1)
    causal = ri >= rj

    def decay_tables(n, dtr):
        dt = _softplus(dtr + dtb_ref[...])
        la = dt * (-jnp.exp(alog_ref[...]))
        cum = jnp.dot(causal.astype(F32), la, precision=lax.Precision.HIGHEST, preferred_element_type=F32)
        ecum = jnp.exp(cum)
        wdt = jnp.exp(cum[rows - 1:rows, :] - cum) * dt
        ex_ref[n] = _head_expand([ecum, wdt, dt], exp_ref[...])
        cum2 = cum * math.log2(math.e)
        cum2_ref[n] = cum2
        cum2t_ref[n] = cum2.T

    @pl.when(c == 0)
    def _():
        s_ref[...] = jnp.zeros_like(s_ref)
        for n in range(n_seq):
            decay_tables(n, dtr_ref[n])

    head_lane = _iota((rows, bw), 1) // SSM_HEADDIM

    for n in range(n_seq):
        cum2 = cum2_ref[n]
        cum2_t = cum2t_ref[n]
        for g in range(SSM_GROUPS):
            gsl = slice(g * bw, (g + 1) * bw)
            bm = bc_ref[n, :, g * SSM_DSTATE:(g + 1) * SSM_DSTATE]
            cm = bc_ref[n, :, (SSM_GROUPS + g) * SSM_DSTATE:(SSM_GROUPS + g + 1) * SSM_DSTATE]
            xs = xc_ref[n, :, gsl].astype(F32)
            s = s_ref[n, g]
            gmat = _dot_nt(cm, bm)
            ps = []
            for hh in range(HEADS_PER_GROUP):
                h = g * HEADS_PER_GROUP + hh
                colb = jnp.broadcast_to(cum2[:, h:h + 1], (rows, LANES))
                dec = jnp.exp2(jnp.where(causal, colb - cum2_t[h:h + 1, :], -jnp.inf))
                ps.append((gmat * dec).astype(BF16))
            v = (xs * ex_ref[n, 2 * rows:3 * rows, gsl]).astype(BF16)
            y4 = _dot(jnp.concatenate(ps, axis=0), v)
            y = y4[0:rows]
            for hh in range(1, HEADS_PER_GROUP):
                y = jnp.where(head_lane == hh, y4[hh * rows:(hh + 1) * rows], y)
            y = y + _dot(cm, s.astype(BF16)) * ex_ref[n, 0:rows, gsl]
            vw = (xs * ex_ref[n, rows:2 * rows, gsl]).astype(BF16)
            s_ref[n, g] = s * ex_ref[n, rows - 1:rows, gsl] + _dot_tn(bm, vw)

            y = y + xs * dsk_ref[:, gsl]
            gg = y * _silu(z_ref[n, :, gsl]).astype(F32)
            ms = jnp.mean(gg * gg, axis=-1, keepdims=True)
            g_ref[n, :, gsl] = (gg * lax.rsqrt(ms + EPS) * gn_ref[:, gsl]).astype(BF16)

    for n in range(n_seq):
        decay_tables(n, dtr_next_ref[n])

    @pl.when(c == pl.num_programs(1) - 1)
    def _():
        for n in range(n_seq):
            for g in range(SSM_GROUPS):
                st_ref[n, g * bw:(g + 1) * bw, :] = s_ref[n, g].T


def _ssd_prompt(zxc, dtr, dtb, alog, dsk, gn, expand, batch, seq):
    rows = SSD_CHUNK
    nc = seq // rows
    ns = SSD_PROMPT_SEQS
    full = lambda shape: pl.BlockSpec(shape, lambda b, c: (0,) * len(shape))
    col = lambda k: pl.BlockSpec((ns, rows, SSM_DINNER), lambda b, c: (b, c, k))
    return pl.pallas_call(
        _ssd_prompt_kernel,
        grid=(batch // ns, nc),
        in_specs=[
            col(0), col(1), col(2),
            pl.BlockSpec((ns, rows, LANES), lambda b, c: (b, c, 0)),
            pl.BlockSpec((ns, rows, LANES), lambda b, c: (b, jnp.minimum(c + 1, nc - 1), 0)),
            full((1, LANES)), full((1, LANES)),
            full((1, SSM_DINNER)), full((1, SSM_DINNER)),
            full((2 * LANES, SSM_DINNER)),
        ],
        out_specs=[
            pl.BlockSpec((ns, rows, SSM_DINNER), lambda b, c: (b, c, 0)),
            pl.BlockSpec((ns, SSM_DINNER, SSM_DSTATE), lambda b, c: (b, 0, 0)),
        ],
        out_shape=[
            jax.ShapeDtypeStruct((batch, seq, SSM_DINNER), BF16),
            jax.ShapeDtypeStruct((batch, SSM_DINNER, SSM_DSTATE), F32),
        ],
        scratch_shapes=[
            pltpu.VMEM((ns, SSM_GROUPS, SSM_DSTATE, SSM_GROUP_W), F32),
            pltpu.VMEM((ns, 3 * rows, SSM_DINNER), F32),
            pltpu.VMEM((ns, rows, LANES), F32),
            pltpu.VMEM((ns, LANES, rows), F32),
        ],
        compiler_params=_cparams(("arbitrary", "arbitrary")),
        name="ssd_prompt",
    )(zxc, zxc, zxc, dtr, dtr, dtb, alog, dsk, gn, expand)


def _ssd_sample_kernel(z_ref, xa_ref, xb_ref, dtr_ref, conv_in_ref, st_in_ref,
                       cw_ref, cb_ref, dtb_ref, alog_ref, dsk_ref, gn_ref, exp_ref,
                       g_ref, st_out_ref, xc_ref, ex_ref, y_ref, *, seq):
    bb = st_in_ref.shape[0]
    rows = bb * seq
    keep = SSM_CONV - 1
    bw = SSM_GROUP_W

    assert seq == SUBLANES
    sub = _iota((bb, SUBLANES, SSM_DINNER), 1)
    hist_pad = jnp.zeros((bb, SUBLANES - keep, SSM_DINNER), F32)
    for half, x_ref in enumerate((xa_ref, xb_ref)):
        csl = slice(half * SSM_DINNER, (half + 1) * SSM_DINNER)
        w0, w1, w2, w3 = (cw_ref[k:k + 1, csl] for k in range(SSM_CONV))
        x = x_ref[...].reshape(bb, SUBLANES, SSM_DINNER)
        hist = jnp.concatenate([hist_pad, conv_in_ref[:, :, csl]], axis=1)
        hist2 = pltpu.roll(hist, 2, axis=1)
        x2 = jnp.where(sub < 2, hist2, pltpu.roll(x, 2, axis=1))
        p = x * w2 + x2 * w0
        p_hist = hist * w2 + hist2 * w0
        p1 = jnp.where(sub < 1, pltpu.roll(p_hist, 1, axis=1), pltpu.roll(p, 1, axis=1))
        out = x * w3 + x2 * w1 + p1 + cb_ref[:, csl]
        xc_ref[:, csl] = _silu(out).reshape(rows, SSM_DINNER)

    dt = _softplus(dtr_ref[...] + dtb_ref[...])
    la = dt * (-jnp.exp(alog_ref[...]))
    ri = _iota((rows, rows), 0)
    rj = _iota((rows, rows), 1)
    same = (ri // seq) == (rj // seq)
    cum = jnp.dot((same & (ri >= rj)).astype(F32), la, precision=lax.Precision.HIGHEST,
                  preferred_element_type=F32)
    clast = jnp.dot(same.astype(F32), la, precision=lax.Precision.HIGHEST, preferred_element_type=F32)
    eclast = jnp.exp(clast)
    ex_ref[...] = _head_expand([jnp.exp(cum), jnp.exp(clast - cum) * dt, dt], exp_ref[...])

    si = _iota((seq, seq), 0)
    sj = _iota((seq, seq), 1)
    causal = si >= sj
    eye = si == sj
    head_lane = _iota((seq, bw), 1) // SSM_HEADDIM

    for n in range(bb):
        r0 = n * seq
        rsl = slice(r0, r0 + seq)
        cum_n = cum[rsl]
        for g in range(SSM_GROUPS):
            gsl = slice(g * bw, (g + 1) * bw)
            bm = xc_ref[rsl, SSM_DINNER + g * SSM_DSTATE:SSM_DINNER + (g + 1) * SSM_DSTATE]
            cm = xc_ref[rsl, SSM_DINNER + (SSM_GROUPS + g) * SSM_DSTATE:
                        SSM_DINNER + (SSM_GROUPS + g + 1) * SSM_DSTATE]
            xs = xc_ref[rsl, gsl]
            gmat = _dot_nt(cm.astype(BF16), bm.astype(BF16))
            ps = []
            for hh in range(HEADS_PER_GROUP):
                h = g * HEADS_PER_GROUP + hh
                ccol = cum_n[:, h:h + 1]
                crow = jnp.sum(jnp.where(eye, ccol, 0.0), axis=0, keepdims=True)
                ps.append(gmat * jnp.exp(jnp.where(causal, ccol - crow, -jnp.inf)))
            v = xs * ex_ref[2 * rows + r0:2 * rows + r0 + seq, gsl]
            y4 = _dot(jnp.concatenate(ps, axis=0).astype(BF16), v.astype(BF16))
            y = y4[0:seq]
            for hh in range(1, HEADS_PER_GROUP):
                y = jnp.where(head_lane == hh, y4[hh * seq:(hh + 1) * seq], y)
            st = st_in_ref[n, gsl, :]
            y = y + _dot_nt(cm, st) * ex_ref[rsl, gsl]
            y_ref[rsl, gsl] = y
            vw = xs * ex_ref[rows + r0:rows + r0 + seq, gsl]
            upd = _dot_tn(vw.astype(BF16), bm.astype(BF16))
            for hh in range(HEADS_PER_GROUP):
                h = g * HEADS_PER_GROUP + hh
                hsl = slice(hh * SSM_HEADDIM, (hh + 1) * SSM_HEADDIM)
                osl = slice(g * bw + hh * SSM_HEADDIM, g * bw + (hh + 1) * SSM_HEADDIM)
                st_out_ref[n, osl, :] = st[hsl] * eclast[r0:r0 + 1, h:h + 1] + upd[hsl]

    y = y_ref[...] + xc_ref[:, 0:SSM_DINNER] * dsk_ref[...]
    gg = y * _silu(z_ref[...])
    for g in range(SSM_GROUPS):
        gsl = slice(g * bw, (g + 1) * bw)
        blk = gg[:, gsl]
        ms = jnp.mean(blk * blk, axis=-1, keepdims=True)
        g_ref[:, gsl] = (blk * lax.rsqrt(ms + EPS) * gn_ref[:, gsl]).astype(BF16)


def _ssd_sample(zx, dtr, conv_state, ssm_state_t, cw, cb, dtb, alog, dsk, gn, expand, batch, seq):
    bb = SSD_SAMPLE_BB
    rows = bb * seq
    keep = SSM_CONV - 1
    full = lambda shape: pl.BlockSpec(shape, lambda n: (0,) * len(shape))
    col = lambda k: pl.BlockSpec((rows, SSM_DINNER), lambda n: (n, k))
    st_spec = pl.BlockSpec((bb, SSM_DINNER, SSM_DSTATE), lambda n: (n, 0, 0))
    return pl.pallas_call(
        functools.partial(_ssd_sample_kernel, seq=seq),
        grid=(batch // bb,),
        in_specs=[
            col(0), col(1), col(2),
            pl.BlockSpec((rows, LANES), lambda n: (n, 0)),
            pl.BlockSpec((None, bb, keep, SSM_CONVDIM), lambda n: (0, n, 0, 0)),
            st_spec,
            full((SSM_CONV, SSM_CONVDIM)), full((1, SSM_CONVDIM)),
            full((1, LANES)), full((1, LANES)),
            full((1, SSM_DINNER)), full((1, SSM_DINNER)),
            full((2 * LANES, SSM_DINNER)),
        ],
        out_specs=[
            pl.BlockSpec((rows, SSM_DINNER), lambda n: (n, 0)),
            st_spec,
        ],
        out_shape=[
            jax.ShapeDtypeStruct((batch * seq, SSM_DINNER), BF16),
            jax.ShapeDtypeStruct(ssm_state_t.shape, F32),
        ],
        scratch_shapes=[
            pltpu.VMEM((rows, SSM_CONVDIM), F32),
            pltpu.VMEM((3 * rows, SSM_DINNER), F32),
            pltpu.VMEM((rows, SSM_DINNER), F32),
        ],
        compiler_params=_cparams(("arbitrary",)),
        name="ssd_sample",
    )(zx, zx, zx, dtr, conv_state, ssm_state_t, cw, cb, dtb, alog, dsk, gn, expand)


def _rope_tables(pos):
    half = RET_DK // 2
    freqs = ROPE_BASE ** (-jnp.arange(half, dtype=F32) / half)
    ang = pos.astype(F32)[:, None] * freqs[None, :]
    return jnp.cos(ang), jnp.sin(ang)


def _row(v):
    return v.reshape(1, -1).astype(F32)


def _pad_lanes(v):
    return jnp.pad(v.astype(F32), (0, LANES - v.shape[0])).reshape(1, LANES)


def kernel(x_prompt, x_sample, state_ret, state_ssm, state_conv, ret_norm, ret_w_in, ret_head_norm, ret_w_out, ssm_norm, ssm_w_in, ssm_conv_w, ssm_conv_b, ssm_dt_bias, ssm_a_log, ssm_d, ssm_gate_norm, ssm_w_out, final_norm):
    bp, lp, d = x_prompt.shape
    bs, ls, _ = x_sample.shape
    assert ret_norm.shape[0] == 1 and ssm_norm.shape[0] == 1, "one retention and one SSD layer"

    ret_win = ret_w_in[0]
    ret_wout = ret_w_out[0].astype(BF16)
    n_main = SSM_DINNER + SSM_CONVDIM
    ssm_win_t = jnp.swapaxes(ssm_w_in[0], 0, 1)
    ssm_wdt = jnp.pad(ssm_w_in[0, :, n_main:], ((0, 0), (0, LANES - SSM_HEADS))).astype(BF16)
    ssm_wout = ssm_w_out[0].astype(BF16)
    expand = (jnp.arange(2 * LANES)[:, None] % LANES == jnp.arange(SSM_DINNER)[None, :] // SSM_HEADDIM).astype(BF16)
    lg = jnp.log(1.0 - 2.0 ** (-5.0 - jnp.arange(RET_HEADS, dtype=F32)))
    lg = jnp.broadcast_to(lg[:, None, None], (RET_HEADS, 1, LANES))
    hn = _row(ret_head_norm[0])
    dsk = _row(jnp.repeat(ssm_d[0], SSM_HEADDIM))
    gn = _row(ssm_gate_norm[0])
    cw = ssm_conv_w[0].astype(F32)
    cb = _row(ssm_conv_b[0])
    dtb = _pad_lanes(ssm_dt_bias[0])
    alog = _pad_lanes(ssm_a_log[0])
    g_ret = _row(ret_norm[0])
    g_ssm = _row(ssm_norm[0])
    g_fin = _row(final_norm)

    xp = x_prompt.reshape(bp * lp, d)
    cos_p, sin_p = _rope_tables(jnp.arange(lp, dtype=jnp.int32))
    n_ret = 2 * RET_QK + 2 * RET_VW
    proj = _norm_proj(xp, g_ret, ret_win, n=n_ret, tm=2048, tn=1024, out_dtype=BF16)
    og, ret_p = _ret_prompt(proj, cos_p, sin_p, lg, hn, bp, lp)
    h1 = _out_proj(og, ret_wout, xp, tm=1024)
    tm_c = 1024
    zxc, dtr, tail = _norm_proj_conv(h1, g_ssm, ssm_win_t, ssm_wdt, cw, cb, tm=tm_c, tn=2048, seq=lp)
    tail = tail[lp // tm_c - 1::lp // tm_c]
    gated, ssm_pt = _ssd_prompt(zxc.reshape(bp, lp, -1), dtr.reshape(bp, lp, LANES), dtb, alog, dsk, gn, expand, bp, lp)
    y_p = _out_proj(gated.reshape(bp * lp, SSM_DINNER), ssm_wout, h1, tm=1024, final_g=g_fin)
    ssm_p = jnp.swapaxes(ssm_pt.reshape(1, bp, SSM_HEADS, SSM_HEADDIM, SSM_DSTATE), -1, -2)
    conv_p = tail[None, :, SUBLANES - (SSM_CONV - 1):, :]

    xs = x_sample.reshape(bs * ls, d)
    cos_s, sin_s = _rope_tables(PAST_LEN + jnp.arange(ls, dtype=jnp.int32))
    proj_s = _norm_proj(xs, g_ret, ret_win, n=n_ret, tm=bs * ls, tn=2048, out_dtype=BF16)
    og_s, ret_s = _ret_sample(proj_s, cos_s, sin_s, lg, hn, state_ret, bs, ls)
    h1_s = _out_proj(og_s, ret_wout, xs, tm=512)
    zx_s, dtr_s = _norm_proj(h1_s, g_ssm, ssm_win_t, n=n_main, tm=bs * ls, tn=2048, out_dtype=F32, w_dt=ssm_wdt,
                             w_is_t=True)
    st_t = jnp.swapaxes(state_ssm[0], -1, -2).reshape(bs, SSM_DINNER, SSM_DSTATE)
    gated_s, ssm_st = _ssd_sample(zx_s, dtr_s, state_conv, st_t, cw, cb, dtb, alog, dsk, gn, expand, bs, ls)
    ssm_s = jnp.swapaxes(ssm_st.reshape(1, bs, SSM_HEADS, SSM_HEADDIM, SSM_DSTATE), -1, -2)
    keep = SSM_CONV - 1
    conv_s = zx_s.reshape(bs, ls, -1)[None, :, ls - keep:, SSM_DINNER:]
    y_s = _out_proj(gated_s, ssm_wout, h1_s, tm=512, final_g=g_fin)

    return (y_p.reshape(bp, lp, d), y_s.reshape(bs, ls, d), ret_p, ret_s, ssm_p, ssm_s, conv_p, conv_s)
```

```python
import functools
import math

import jax
import jax.numpy as jnp
from jax import lax
from jax.experimental import pallas as pl
from jax.experimental.pallas import tpu as pltpu

F32 = jnp.float32
BF16 = jnp.bfloat16

D_MODEL = 1024
RET_HEADS = 4
RET_DK = 256
RET_DV = 512
RET_VW = RET_HEADS * RET_DV
RET_QK = RET_HEADS * RET_DK
SSM_DINNER = 2048
SSM_HEADDIM = 64
SSM_HEADS = 32
SSM_GROUPS = 8
SSM_DSTATE = 128
SSM_CONV = 4
SSM_CONVDIM = 4096
SSM_GROUP_W = SSM_DINNER // SSM_GROUPS
HEADS_PER_GROUP = SSM_HEADS // SSM_GROUPS
ROPE_BASE = 10000.0
EPS = 1e-6
PAST_LEN = 16384

LANES = 128
SUBLANES = 8
VMEM_LIMIT = 56 * 1024 * 1024

RET_CHUNK = 256
SSD_CHUNK = 128
SSD_PROMPT_SEQS = 4
RET_SAMPLE_BB = 16
SSD_SAMPLE_BB = 4


def _cparams(sem, flags=None):
    return pltpu.CompilerParams(dimension_semantics=sem, vmem_limit_bytes=VMEM_LIMIT, flags=flags)


def _sigmoid(x):
    return 1.0 / (1.0 + jnp.exp2(x * (-math.log2(math.e))))


def _silu(x):
    return x * _sigmoid(x)


def _softplus(x):
    return jnp.maximum(x, 0.0) + jnp.log1p(jnp.exp(-jnp.abs(x)))


def _dot(a, b):
    return jnp.dot(a, b, preferred_element_type=F32)


def _dot_nt(a, b):
    return lax.dot_general(a, b, (((1,), (1,)), ((), ())), preferred_element_type=F32)


def _dot_tn(a, b):
    return lax.dot_general(a, b, (((0,), (0,)), ((), ())), preferred_element_type=F32)


def _iota(shape, dim):
    return lax.broadcasted_iota(jnp.int32, shape, dim)


def _norm_proj_kernel(x_ref, g_ref, w_ref, *rest, has_dt, w_is_t):
    if has_dt:
        wdt_ref, o_ref, dt_ref, xn_ref = rest
    else:
        o_ref, xn_ref = rest

    @pl.when(pl.program_id(1) == 0)
    def _():
        x = x_ref[...]
        ms = jnp.mean(x * x, axis=-1, keepdims=True)
        xn = (x * lax.rsqrt(ms + EPS) * g_ref[...]).astype(BF16)
        xn_ref[...] = xn
        if has_dt:
            dt_ref[...] = _dot(xn, wdt_ref[...])

    w = w_ref[...].astype(BF16)
    o_ref[...] = (_dot_nt(xn_ref[...], w) if w_is_t else _dot(xn_ref[...], w)).astype(o_ref.dtype)


def _norm_proj(x, g, w, *, n, tm, tn, out_dtype, w_dt=None, w_is_t=False):
    m, d = x.shape
    has_dt = w_dt is not None
    in_specs = [
        pl.BlockSpec((tm, d), lambda i, j: (i, 0)),
        pl.BlockSpec((1, d), lambda i, j: (0, 0)),
        pl.BlockSpec((tn, d), lambda i, j: (j, 0)) if w_is_t else pl.BlockSpec((d, tn), lambda i, j: (0, j)),
    ]
    args = [x, g, w]
    out_shape = [jax.ShapeDtypeStruct((m, n), out_dtype)]
    out_specs = [pl.BlockSpec((tm, tn), lambda i, j: (i, j))]
    if has_dt:
        in_specs.append(pl.BlockSpec((d, LANES), lambda i, j: (0, 0)))
        args.append(w_dt)
        out_shape.append(jax.ShapeDtypeStruct((m, LANES), F32))
        out_specs.append(pl.BlockSpec((tm, LANES), lambda i, j: (i, 0)))
    res = pl.pallas_call(
        functools.partial(_norm_proj_kernel, has_dt=has_dt, w_is_t=w_is_t),
        grid=(m // tm, n // tn),
        in_specs=in_specs,
        out_specs=out_specs,
        out_shape=out_shape,
        scratch_shapes=[pltpu.VMEM((tm, d), BF16)],
        compiler_params=_cparams(("arbitrary", "arbitrary")),
        name="in_proj",
    )(*args)
    return res if has_dt else res[0]


def _shift_rows(a, first_rows):
    rows, cols = a.shape
    n = first_rows.shape[0]
    slabs = rows // SUBLANES
    rot = pltpu.roll(a.reshape(slabs, SUBLANES, cols), n, axis=1)
    first = jnp.concatenate([first_rows, jnp.zeros((SUBLANES - n, cols), a.dtype)], axis=0)
    above = jnp.concatenate([first.reshape(1, SUBLANES, cols), rot[:slabs - 1]], axis=0)
    top = _iota((slabs, SUBLANES, cols), 1) < n
    return jnp.where(top, above, rot).reshape(rows, cols)


def _norm_proj_conv_kernel(x_ref, g_ref, w_ref, wdt_ref, cw_ref, cb_ref, o_ref, dt_ref, tail_ref,
                           xn_ref, carry_ref, *, tiles_per_seq, z_tiles, strip):
    i = pl.program_id(0)
    j = pl.program_id(1)
    tm, tn = o_ref.shape

    @pl.when(j == 0)
    def _():
        x = x_ref[...]
        ms = jnp.mean(x * x, axis=-1, keepdims=True)
        xn = (x * lax.rsqrt(ms + EPS) * g_ref[...]).astype(BF16)
        xn_ref[...] = xn
        dt_ref[...] = _dot(xn, wdt_ref[...])

    @pl.when((j == 0) & (i % tiles_per_seq == 0))
    def _():
        carry_ref[...] = jnp.zeros_like(carry_ref)

    @pl.when(j < z_tiles)
    def _():
        o_ref[...] = _dot_nt(xn_ref[...], w_ref[...].astype(BF16)).astype(o_ref.dtype)

    @pl.when(j >= z_tiles)
    def _():
        jc = j - z_tiles
        for s in range(tn // strip):
            sl = slice(s * strip, (s + 1) * strip)
            x = _dot_nt(xn_ref[...], w_ref[sl, :].astype(BF16))
            tail_ref[:, sl] = x[tm - SUBLANES:tm, :]
            x2 = _shift_rows(x, carry_ref[jc, 1:3, sl])
            p = x * cw_ref[2:3, sl] + x2 * cw_ref[0:1, sl]
            out = x * cw_ref[3:4, sl] + x2 * cw_ref[1:2, sl] + _shift_rows(p, carry_ref[jc, 0:1, sl]) + cb_ref[:, sl]
            o_ref[:, sl] = _silu(out).astype(o_ref.dtype)
            carry_ref[jc, 0:1, sl] = p[tm - 1:tm]
            carry_ref[jc, 1:3, sl] = x[tm - 2:tm]


def _norm_proj_conv(x, g, wt, w_dt, cw, cb, *, tm, tn, seq):
    m, d = x.shape
    n = SSM_DINNER + SSM_CONVDIM
    z_tiles = SSM_DINNER // tn
    c_tiles = SSM_CONVDIM // tn
    tiles_per_seq = seq // tm
    cj = lambda j: jnp.maximum(j - z_tiles, 0)
    return pl.pallas_call(
        functools.partial(_norm_proj_conv_kernel, tiles_per_seq=tiles_per_seq, z_tiles=z_tiles, strip=256),
        grid=(m // tm, n // tn),
        in_specs=[
            pl.BlockSpec((tm, d), lambda i, j: (i, 0)),
            pl.BlockSpec((1, d), lambda i, j: (0, 0)),
            pl.BlockSpec((tn, d), lambda i, j: (j, 0)),
            pl.BlockSpec((d, LANES), lambda i, j: (0, 0)),
            pl.BlockSpec((SSM_CONV, tn), lambda i, j: (0, cj(j))),
            pl.BlockSpec((1, tn), lambda i, j: (0, cj(j))),
        ],
        out_specs=[
            pl.BlockSpec((tm, tn), lambda i, j: (i, j)),
            pl.BlockSpec((tm, LANES), lambda i, j: (i, 0)),
            pl.BlockSpec((None, SUBLANES, tn), lambda i, j: (i, 0, cj(j))),
        ],
        out_shape=[
            jax.ShapeDtypeStruct((m, n), BF16),
            jax.ShapeDtypeStruct((m, LANES), F32),
            jax.ShapeDtypeStruct((m // tm, SUBLANES, SSM_CONVDIM), F32),
        ],
        scratch_shapes=[
            pltpu.VMEM((tm, d), BF16),
            pltpu.VMEM((c_tiles, SUBLANES, tn), F32),
        ],
        compiler_params=_cparams(("arbitrary", "arbitrary")),
        name="ssd_in_proj_conv",
    )(x, g, wt, w_dt, cw, cb)


def _out_proj_kernel(a_ref, w_ref, h_ref, *rest, has_norm):
    if has_norm:
        g_ref, o_ref = rest
    else:
        (o_ref,) = rest
    y = h_ref[...] + _dot(a_ref[...], w_ref[...])
    if has_norm:
        ms = jnp.mean(y * y, axis=-1, keepdims=True)
        y = y * lax.rsqrt(ms + EPS) * g_ref[...]
    o_ref[...] = y


def _out_proj(a, w, h, *, tm, final_g=None):
    m, k = a.shape
    d = w.shape[1]
    has_norm = final_g is not None
    in_specs = [
        pl.BlockSpec((tm, k), lambda i: (i, 0)),
        pl.BlockSpec((k, d), lambda i: (0, 0)),
        pl.BlockSpec((tm, d), lambda i: (i, 0)),
    ]
    args = [a, w, h]
    if has_norm:
        in_specs.append(pl.BlockSpec((1, d), lambda i: (0, 0)))
        args.append(final_g)
    return pl.pallas_call(
        functools.partial(_out_proj_kernel, has_norm=has_norm),
        grid=(m // tm,),
        in_specs=in_specs,
        out_specs=pl.BlockSpec((tm, d), lambda i: (i, 0)),
        out_shape=jax.ShapeDtypeStruct((m, d), F32),
        compiler_params=_cparams(("arbitrary",)),
        name="out_proj",
    )(*args)


def _rope(x, cos, sin):
    half = RET_DK // 2
    x1 = x[:, :half]
    x2 = x[:, half:]
    return jnp.concatenate([x1 * cos - x2 * sin, x1 * sin + x2 * cos], axis=1)


def _head_norm_gate(o, hn, gate):
    ms = jnp.mean(o * o, axis=-1, keepdims=True)
    on = o * lax.rsqrt(ms + EPS) * hn
    return (on * _silu(gate.astype(F32))).astype(BF16)


def _ret_prompt_kernel(q_ref, k_ref, v_ref, gate_ref, cos_ref, sin_ref, lg_ref, hn_ref,
                       og_ref, st_ref, s_ref, dec_ref, qsc_ref, ksc_ref, gc_ref):
    b = pl.program_id(0)
    c = pl.program_id(1)
    rows = q_ref.shape[0]

    @pl.when((b == 0) & (c == 0))
    def _():
        i = _iota((rows, rows), 0)
        j = _iota((rows, rows), 1)
        r = _iota((rows, RET_DK), 0)
        scale = RET_DK ** -0.5
        for h in range(RET_HEADS):
            lg = lg_ref[h, :, 0:1]
            dec_ref[h] = jnp.where(i >= j, jnp.exp((i - j).astype(F32) * lg), 0.0) * scale
            qsc_ref[h] = jnp.exp((r + 1).astype(F32) * lg)
            ksc_ref[h] = jnp.exp((rows - 1 - r).astype(F32) * lg) * scale
            gc_ref[h] = jnp.exp(jnp.full((1, RET_DV), float(rows), F32) * lg)

    @pl.when(c == 0)
    def _():
        s_ref[...] = jnp.zeros_like(s_ref)

    cos = cos_ref[...]
    sin = sin_ref[...]
    for h in range(RET_HEADS):
        ksl = slice(h * RET_DK, (h + 1) * RET_DK)
        vsl = slice(h * RET_DV, (h + 1) * RET_DV)
        qr = _rope(q_ref[:, ksl].astype(F32), cos, sin)
        kr = _rope(k_ref[:, ksl].astype(F32), cos, sin)
        v = v_ref[:, vsl]
        p = (_dot_nt(qr.astype(BF16), kr.astype(BF16)) * dec_ref[h]).astype(BF16)
        qs = (qr * qsc_ref[h]).astype(BF16)
        s = s_ref[h]
        o = _dot(p, v) + _dot(qs, s.astype(BF16))
        kt = (kr * ksc_ref[h]).T.astype(BF16)
        s_ref[h] = s * gc_ref[h] + _dot(kt, v)
        og_ref[:, vsl] = _head_norm_gate(o, hn_ref[:, vsl], gate_ref[:, vsl])

    @pl.when(c == pl.num_programs(1) - 1)
    def _():
        st_ref[...] = s_ref[...]


def _ret_prompt(proj, cos, sin, lg, hn, batch, seq):
    rows = RET_CHUNK
    nc = seq // rows
    return pl.pallas_call(
        _ret_prompt_kernel,
        grid=(batch, nc),
        in_specs=[
            pl.BlockSpec((rows, RET_QK), lambda b, c: (b * nc + c, 0)),
            pl.BlockSpec((rows, RET_QK), lambda b, c: (b * nc + c, 1)),
            pl.BlockSpec((rows, RET_VW), lambda b, c: (b * nc + c, 2 * RET_QK // RET_VW)),
            pl.BlockSpec((rows, RET_VW), lambda b, c: (b * nc + c, 2 * RET_QK // RET_VW + 1)),
            pl.BlockSpec((rows, RET_DK // 2), lambda b, c: (c, 0)),
            pl.BlockSpec((rows, RET_DK // 2), lambda b, c: (c, 0)),
            pl.BlockSpec((RET_HEADS, 1, LANES), lambda b, c: (0, 0, 0)),
            pl.BlockSpec((1, RET_VW), lambda b, c: (0, 0)),
        ],
        out_specs=[
            pl.BlockSpec((rows, RET_VW), lambda b, c: (b * nc + c, 0)),
            pl.BlockSpec((None, None, RET_HEADS, RET_DK, RET_DV), lambda b, c: (0, b, 0, 0, 0)),
        ],
        out_shape=[
            jax.ShapeDtypeStruct((batch * seq, RET_VW), BF16),
            jax.ShapeDtypeStruct((1, batch, RET_HEADS, RET_DK, RET_DV), F32),
        ],
        scratch_shapes=[
            pltpu.VMEM((RET_HEADS, RET_DK, RET_DV), F32),
            pltpu.VMEM((RET_HEADS, rows, rows), F32),
            pltpu.VMEM((RET_HEADS, rows, RET_DK), F32),
            pltpu.VMEM((RET_HEADS, rows, RET_DK), F32),
            pltpu.VMEM((RET_HEADS, 1, RET_DV), F32),
        ],
        compiler_params=_cparams(("arbitrary", "arbitrary")),
        name="ret_prompt",
    )(proj, proj, proj, proj, cos, sin, lg, hn)


def _ret_sample_kernel(q_ref, k_ref, v_ref, gate_ref, cos_ref, sin_ref, lg_ref, hn_ref, st_in_ref,
                       og_ref, st_out_ref, *, seq):
    bb = st_in_ref.shape[0]
    rows = bb * seq
    lg = lg_ref[:, 0:1]
    scale = RET_DK ** -0.5
    i = _iota((rows, rows), 0)
    j = _iota((rows, rows), 1)
    same = (i // seq) == (j // seq)
    dec = jnp.where(same & (i >= j), jnp.exp((i - j).astype(F32) * lg), 0.0) * scale
    pos = _iota((rows, RET_DK), 0) % seq
    qsc = jnp.exp((pos + 1).astype(F32) * lg)
    ksc = jnp.exp((seq - 1 - pos).astype(F32) * lg) * scale
    gc = jnp.exp(jnp.full((1, RET_DV), float(seq), F32) * lg)

    cos = jnp.concatenate([cos_ref[...]] * bb, axis=0)
    sin = jnp.concatenate([sin_ref[...]] * bb, axis=0)
    qr = _rope(q_ref[...].astype(F32), cos, sin)
    kr = _rope(k_ref[...].astype(F32), cos, sin)
    v = v_ref[...].astype(F32)
    p = _dot_nt(qr.astype(BF16), kr.astype(BF16)) * dec
    intra = _dot(p.astype(BF16), v.astype(BF16))
    qs = qr * qsc
    kt = kr * ksc
    vrow = _iota((rows, RET_DV), 0) // seq
    cross = []
    for n in range(bb):
        s0 = st_in_ref[n]
        cross.append(_dot(qs[n * seq:(n + 1) * seq], s0))
        vn = jnp.where(vrow == n, v, 0.0)
        st_out_ref[n] = s0 * gc + _dot_tn(kt.astype(BF16), vn.astype(BF16))
    o = intra + jnp.concatenate(cross, axis=0)
    og_ref[...] = _head_norm_gate(o, hn_ref[...], gate_ref[...])


def _ret_sample(proj, cos, sin, lg, hn, state, batch, seq):
    bb = RET_SAMPLE_BB
    rows = bb * seq
    qb = RET_QK // RET_DK
    vb = 2 * RET_QK // RET_DV
    gb = vb + RET_VW // RET_DV
    st_spec = pl.BlockSpec((None, bb, None, RET_DK, RET_DV), lambda n, h: (0, n, h, 0, 0))
    return pl.pallas_call(
        functools.partial(_ret_sample_kernel, seq=seq),
        grid=(batch // bb, RET_HEADS),
        in_specs=[
            pl.BlockSpec((rows, RET_DK), lambda n, h: (n, h)),
            pl.BlockSpec((rows, RET_DK), lambda n, h: (n, qb + h)),
            pl.BlockSpec((rows, RET_DV), lambda n, h: (n, vb + h)),
            pl.BlockSpec((rows, RET_DV), lambda n, h: (n, gb + h)),
            pl.BlockSpec((seq, RET_DK // 2), lambda n, h: (0, 0)),
            pl.BlockSpec((seq, RET_DK // 2), lambda n, h: (0, 0)),
            pl.BlockSpec((None, 1, LANES), lambda n, h: (h, 0, 0)),
            pl.BlockSpec((1, RET_DV), lambda n, h: (0, h)),
            st_spec,
        ],
        out_specs=[
            pl.BlockSpec((rows, RET_DV), lambda n, h: (n, h)),
            st_spec,
        ],
        out_shape=[
            jax.ShapeDtypeStruct((batch * seq, RET_VW), BF16),
            jax.ShapeDtypeStruct(state.shape, F32),
        ],
        compiler_params=_cparams(("arbitrary", "arbitrary")),
        name="ret_sample",
    )(proj, proj, proj, proj, cos, sin, lg, hn, state)


def _head_expand(vals, expand):
    parts = []
    for v in vals:
        hi = v.astype(BF16)
        lo = (v - hi.astype(F32)).astype(BF16)
        parts.append(jnp.concatenate([hi, lo], axis=1))
    return _dot(jnp.concatenate(parts, axis=0), expand)


def _ssd_prompt_kernel(z_ref, xc_ref, bc_ref, dtr_ref, dtr_next_ref, dtb_ref, alog_ref, dsk_ref, gn_ref, exp_ref,
                       g_ref, st_ref, s_ref, ex_ref, cum2_ref, cum2t_ref):
    c = pl.program_id(1)
    n_seq, rows = z_ref.shape[0], z_ref.shape[1]
    bw = SSM_GROUP_W
    ri = _iota((rows, rows), 0)
    rj = _iota((rows, rows), 1)
    causal = ri >= rj

    def decay_tables(n, dtr):
        dt = _softplus(dtr + dtb_ref[...])
        la = dt * (-jnp.exp(alog_ref[...]))
        cum = jnp.dot(causal.astype(F32), la, precision=lax.Precision.HIGHEST, preferred_element_type=F32)
        ecum = jnp.exp(cum)
        wdt = jnp.exp(cum[rows - 1:rows, :] - cum) * dt
        ex_ref[n] = _head_expand([ecum, wdt, dt], exp_ref[...])
        cum2 = cum * math.log2(math.e)
        cum2_ref[n] = cum2
        cum2t_ref[n] = cum2.T

    @pl.when(c == 0)
    def _():
        s_ref[...] = jnp.zeros_like(s_ref)
        for n in range(n_seq):
            decay_tables(n, dtr_ref[n])

    head_lane = _iota((rows, bw), 1) // SSM_HEADDIM

    for n in range(n_seq):
        cum2 = cum2_ref[n]
        cum2_t = cum2t_ref[n]
        for g in range(SSM_GROUPS):
            gsl = slice(g * bw, (g + 1) * bw)
            bm = bc_ref[n, :, g * SSM_DSTATE:(g + 1) * SSM_DSTATE]
            cm = bc_ref[n, :, (SSM_GROUPS + g) * SSM_DSTATE:(SSM_GROUPS + g + 1) * SSM_DSTATE]
            xs = xc_ref[n, :, gsl].astype(F32)
            s = s_ref[n, g]
            gmat = _dot_nt(cm, bm)
            ps = []
            for hh in range(HEADS_PER_GROUP):
                h = g * HEADS_PER_GROUP + hh
                colb = jnp.broadcast_to(cum2[:, h:h + 1], (rows, LANES))
                dec = jnp.exp2(jnp.where(causal, colb - cum2_t[h:h + 1, :], -jnp.inf))
                ps.append((gmat * dec).astype(BF16))
            v = (xs * ex_ref[n, 2 * rows:3 * rows, gsl]).astype(BF16)
            y4 = _dot(jnp.concatenate(ps, axis=0), v)
            y = y4[0:rows]
            for hh in range(1, HEADS_PER_GROUP):
                y = jnp.where(head_lane == hh, y4[hh * rows:(hh + 1) * rows], y)
            y = y + _dot(cm, s.astype(BF16)) * ex_ref[n, 0:rows, gsl]
            vw = (xs * ex_ref[n, rows:2 * rows, gsl]).astype(BF16)
            s_ref[n, g] = s * ex_ref[n, rows - 1:rows, gsl] + _dot_tn(bm, vw)

            y = y + xs * dsk_ref[:, gsl]
            gg = y * _silu(z_ref[n, :, gsl].astype(F32))
            ms = jnp.mean(gg * gg, axis=-1, keepdims=True)
            g_ref[n, :, gsl] = (gg * lax.rsqrt(ms + EPS) * gn_ref[:, gsl]).astype(BF16)

        decay_tables(n, dtr_next_ref[n])

    @pl.when(c == pl.num_programs(1) - 1)
    def _():
        for n in range(n_seq):
            for g in range(SSM_GROUPS):
                st_ref[n, g * bw:(g + 1) * bw, :] = s_ref[n, g].T


def _ssd_prompt(zxc, dtr, dtb, alog, dsk, gn, expand, batch, seq):
    rows = SSD_CHUNK
    nc = seq // rows
    ns = SSD_PROMPT_SEQS
    full = lambda shape: pl.BlockSpec(shape, lambda b, c: (0,) * len(shape))
    col = lambda k: pl.BlockSpec((ns, rows, SSM_DINNER), lambda b, c: (b, c, k))
    return pl.pallas_call(
        _ssd_prompt_kernel,
        grid=(batch // ns, nc),
        in_specs=[
            col(0), col(1), col(2),
            pl.BlockSpec((ns, rows, LANES), lambda b, c: (b, c, 0)),
            pl.BlockSpec((ns, rows, LANES), lambda b, c: (b, jnp.minimum(c + 1, nc - 1), 0)),
            full((1, LANES)), full((1, LANES)),
            full((1, SSM_DINNER)), full((1, SSM_DINNER)),
            full((2 * LANES, SSM_DINNER)),
        ],
        out_specs=[
            pl.BlockSpec((ns, rows, SSM_DINNER), lambda b, c: (b, c, 0)),
            pl.BlockSpec((ns, SSM_DINNER, SSM_DSTATE), lambda b, c: (b, 0, 0)),
        ],
        out_shape=[
            jax.ShapeDtypeStruct((batch, seq, SSM_DINNER), BF16),
            jax.ShapeDtypeStruct((batch, SSM_DINNER, SSM_DSTATE), F32),
        ],
        scratch_shapes=[
            pltpu.VMEM((ns, SSM_GROUPS, SSM_DSTATE, SSM_GROUP_W), F32),
            pltpu.VMEM((ns, 3 * rows, SSM_DINNER), F32),
            pltpu.VMEM((ns, rows, LANES), F32),
            pltpu.VMEM((ns, LANES, rows), F32),
        ],
        compiler_params=_cparams(("arbitrary", "arbitrary")),
        name="ssd_prompt",
    )(zxc, zxc, zxc, dtr, dtr, dtb, alog, dsk, gn, expand)


def _ssd_sample_kernel(z_ref, xa_ref, xb_ref, dtr_ref, conv_in_ref, st_in_ref,
                       cw_ref, cb_ref, dtb_ref, alog_ref, dsk_ref, gn_ref, exp_ref,
                       g_ref, st_out_ref, xc_ref, ex_ref, y_ref, *, seq):
    bb = st_in_ref.shape[0]
    rows = bb * seq
    keep = SSM_CONV - 1
    bw = SSM_GROUP_W

    assert seq == SUBLANES
    sub = _iota((bb, SUBLANES, SSM_DINNER), 1)
    hist_pad = jnp.zeros((bb, SUBLANES - keep, SSM_DINNER), F32)
    for half, x_ref in enumerate((xa_ref, xb_ref)):
        csl = slice(half * SSM_DINNER, (half + 1) * SSM_DINNER)
        w0, w1, w2, w3 = (cw_ref[k:k + 1, csl] for k in range(SSM_CONV))
        x = x_ref[...].reshape(bb, SUBLANES, SSM_DINNER)
        hist = jnp.concatenate([hist_pad, conv_in_ref[:, :, csl]], axis=1)
        hist2 = pltpu.roll(hist, 2, axis=1)
        x2 = jnp.where(sub < 2, hist2, pltpu.roll(x, 2, axis=1))
        p = x * w2 + x2 * w0
        p_hist = hist * w2 + hist2 * w0
        p1 = jnp.where(sub < 1, pltpu.roll(p_hist, 1, axis=1), pltpu.roll(p, 1, axis=1))
        out = x * w3 + x2 * w1 + p1 + cb_ref[:, csl]
        xc_ref[:, csl] = _silu(out).reshape(rows, SSM_DINNER)

    dt = _softplus(dtr_ref[...] + dtb_ref[...])
    la = dt * (-jnp.exp(alog_ref[...]))
    ri = _iota((rows, rows), 0)
    rj = _iota((rows, rows), 1)
    same = (ri // seq) == (rj // seq)
    cum = jnp.dot((same & (ri >= rj)).astype(F32), la, precision=lax.Precision.HIGHEST,
                  preferred_element_type=F32)
    clast = jnp.dot(same.astype(F32), la, precision=lax.Precision.HIGHEST, preferred_element_type=F32)
    eclast = jnp.exp(clast)
    ex_ref[...] = _head_expand([jnp.exp(cum), jnp.exp(clast - cum) * dt, dt], exp_ref[...])

    si = _iota((seq, seq), 0)
    sj = _iota((seq, seq), 1)
    causal = si >= sj
    eye = si == sj
    head_lane = _iota((seq, bw), 1) // SSM_HEADDIM

    for n in range(bb):
        r0 = n * seq
        rsl = slice(r0, r0 + seq)
        cum_n = cum[rsl]
        for g in range(SSM_GROUPS):
            gsl = slice(g * bw, (g + 1) * bw)
            bm = xc_ref[rsl, SSM_DINNER + g * SSM_DSTATE:SSM_DINNER + (g + 1) * SSM_DSTATE]
            cm = xc_ref[rsl, SSM_DINNER + (SSM_GROUPS + g) * SSM_DSTATE:
                        SSM_DINNER + (SSM_GROUPS + g + 1) * SSM_DSTATE]
            xs = xc_ref[rsl, gsl]
            gmat = _dot_nt(cm.astype(BF16), bm.astype(BF16))
            ps = []
            for hh in range(HEADS_PER_GROUP):
                h = g * HEADS_PER_GROUP + hh
                ccol = cum_n[:, h:h + 1]
                crow = jnp.sum(jnp.where(eye, ccol, 0.0), axis=0, keepdims=True)
                ps.append(gmat * jnp.exp(jnp.where(causal, ccol - crow, -jnp.inf)))
            v = xs * ex_ref[2 * rows + r0:2 * rows + r0 + seq, gsl]
            y4 = _dot(jnp.concatenate(ps, axis=0).astype(BF16), v.astype(BF16))
            y = y4[0:seq]
            for hh in range(1, HEADS_PER_GROUP):
                y = jnp.where(head_lane == hh, y4[hh * seq:(hh + 1) * seq], y)
            st = st_in_ref[n, gsl, :]
            y = y + _dot_nt(cm, st) * ex_ref[rsl, gsl]
            y_ref[rsl, gsl] = y
            vw = xs * ex_ref[rows + r0:rows + r0 + seq, gsl]
            upd = _dot_tn(vw.astype(BF16), bm.astype(BF16))
            for hh in range(HEADS_PER_GROUP):
                h = g * HEADS_PER_GROUP + hh
                hsl = slice(hh * SSM_HEADDIM, (hh + 1) * SSM_HEADDIM)
                osl = slice(g * bw + hh * SSM_HEADDIM, g * bw + (hh + 1) * SSM_HEADDIM)
                st_out_ref[n, osl, :] = st[hsl] * eclast[r0:r0 + 1, h:h + 1] + upd[hsl]

    y = y_ref[...] + xc_ref[:, 0:SSM_DINNER] * dsk_ref[...]
    gg = y * _silu(z_ref[...])
    for g in range(SSM_GROUPS):
        gsl = slice(g * bw, (g + 1) * bw)
        blk = gg[:, gsl]
        ms = jnp.mean(blk * blk, axis=-1, keepdims=True)
        g_ref[:, gsl] = (blk * lax.rsqrt(ms + EPS) * gn_ref[:, gsl]).astype(BF16)


def _ssd_sample(zx, dtr, conv_state, ssm_state_t, cw, cb, dtb, alog, dsk, gn, expand, batch, seq):
    bb = SSD_SAMPLE_BB
    rows = bb * seq
    keep = SSM_CONV - 1
    full = lambda shape: pl.BlockSpec(shape, lambda n: (0,) * len(shape))
    col = lambda k: pl.BlockSpec((rows, SSM_DINNER), lambda n: (n, k))
    st_spec = pl.BlockSpec((bb, SSM_DINNER, SSM_DSTATE), lambda n: (n, 0, 0))
    return pl.pallas_call(
        functools.partial(_ssd_sample_kernel, seq=seq),
        grid=(batch // bb,),
        in_specs=[
            col(0), col(1), col(2),
            pl.BlockSpec((rows, LANES), lambda n: (n, 0)),
            pl.BlockSpec((None, bb, keep, SSM_CONVDIM), lambda n: (0, n, 0, 0)),
            st_spec,
            full((SSM_CONV, SSM_CONVDIM)), full((1, SSM_CONVDIM)),
            full((1, LANES)), full((1, LANES)),
            full((1, SSM_DINNER)), full((1, SSM_DINNER)),
            full((2 * LANES, SSM_DINNER)),
        ],
        out_specs=[
            pl.BlockSpec((rows, SSM_DINNER), lambda n: (n, 0)),
            st_spec,
        ],
        out_shape=[
            jax.ShapeDtypeStruct((batch * seq, SSM_DINNER), BF16),
            jax.ShapeDtypeStruct(ssm_state_t.shape, F32),
        ],
        scratch_shapes=[
            pltpu.VMEM((rows, SSM_CONVDIM), F32),
            pltpu.VMEM((3 * rows, SSM_DINNER), F32),
            pltpu.VMEM((rows, SSM_DINNER), F32),
        ],
        compiler_params=_cparams(("arbitrary",)),
        name="ssd_sample",
    )(zx, zx, zx, dtr, conv_state, ssm_state_t, cw, cb, dtb, alog, dsk, gn, expand)


def _rope_tables(pos):
    half = RET_DK // 2
    freqs = ROPE_BASE ** (-jnp.arange(half, dtype=F32) / half)
    ang = pos.astype(F32)[:, None] * freqs[None, :]
    return jnp.cos(ang), jnp.sin(ang)


def _row(v):
    return v.reshape(1, -1).astype(F32)


def _pad_lanes(v):
    return jnp.pad(v.astype(F32), (0, LANES - v.shape[0])).reshape(1, LANES)


def kernel(x_prompt, x_sample, state_ret, state_ssm, state_conv, ret_norm, ret_w_in, ret_head_norm, ret_w_out, ssm_norm, ssm_w_in, ssm_conv_w, ssm_conv_b, ssm_dt_bias, ssm_a_log, ssm_d, ssm_gate_norm, ssm_w_out, final_norm):
    bp, lp, d = x_prompt.shape
    bs, ls, _ = x_sample.shape
    assert ret_norm.shape[0] == 1 and ssm_norm.shape[0] == 1, "one retention and one SSD layer"

    ret_win = ret_w_in[0]
    ret_wout = ret_w_out[0].astype(BF16)
    n_main = SSM_DINNER + SSM_CONVDIM
    ssm_win_t = jnp.swapaxes(ssm_w_in[0], 0, 1)
    ssm_wdt = jnp.pad(ssm_w_in[0, :, n_main:], ((0, 0), (0, LANES - SSM_HEADS))).astype(BF16)
    ssm_wout = ssm_w_out[0].astype(BF16)
    expand = (jnp.arange(2 * LANES)[:, None] % LANES == jnp.arange(SSM_DINNER)[None, :] // SSM_HEADDIM).astype(BF16)
    lg = jnp.log(1.0 - 2.0 ** (-5.0 - jnp.arange(RET_HEADS, dtype=F32)))
    lg = jnp.broadcast_to(lg[:, None, None], (RET_HEADS, 1, LANES))
    hn = _row(ret_head_norm[0])
    dsk = _row(jnp.repeat(ssm_d[0], SSM_HEADDIM))
    gn = _row(ssm_gate_norm[0])
    cw = ssm_conv_w[0].astype(F32)
    cb = _row(ssm_conv_b[0])
    dtb = _pad_lanes(ssm_dt_bias[0])
    alog = _pad_lanes(ssm_a_log[0])
    g_ret = _row(ret_norm[0])
    g_ssm = _row(ssm_norm[0])
    g_fin = _row(final_norm)

    xp = x_prompt.reshape(bp * lp, d)
    cos_p, sin_p = _rope_tables(jnp.arange(lp, dtype=jnp.int32))
    n_ret = 2 * RET_QK + 2 * RET_VW
    proj = _norm_proj(xp, g_ret, ret_win, n=n_ret, tm=2048, tn=1024, out_dtype=BF16)
    og, ret_p = _ret_prompt(proj, cos_p, sin_p, lg, hn, bp, lp)
    h1 = _out_proj(og, ret_wout, xp, tm=1024)
    tm_c = 1024
    zxc, dtr, tail = _norm_proj_conv(h1, g_ssm, ssm_win_t, ssm_wdt, cw, cb, tm=tm_c, tn=2048, seq=lp)
    tail = tail[lp // tm_c - 1::lp // tm_c]
    gated, ssm_pt = _ssd_prompt(zxc.reshape(bp, lp, -1), dtr.reshape(bp, lp, LANES), dtb, alog, dsk, gn, expand, bp, lp)
    y_p = _out_proj(gated.reshape(bp * lp, SSM_DINNER), ssm_wout, h1, tm=1024, final_g=g_fin)
    ssm_p = jnp.swapaxes(ssm_pt.reshape(1, bp, SSM_HEADS, SSM_HEADDIM, SSM_DSTATE), -1, -2)
    conv_p = tail[None, :, SUBLANES - (SSM_CONV - 1):, :]

    xs = x_sample.reshape(bs * ls, d)
    cos_s, sin_s = _rope_tables(PAST_LEN + jnp.arange(ls, dtype=jnp.int32))
    proj_s = _norm_proj(xs, g_ret, ret_win, n=n_ret, tm=bs * ls, tn=2048, out_dtype=BF16)
    og_s, ret_s = _ret_sample(proj_s, cos_s, sin_s, lg, hn, state_ret, bs, ls)
    h1_s = _out_proj(og_s, ret_wout, xs, tm=512)
    zx_s, dtr_s = _norm_proj(h1_s, g_ssm, ssm_win_t, n=n_main, tm=bs * ls, tn=2048, out_dtype=F32, w_dt=ssm_wdt,
                             w_is_t=True)
    st_t = jnp.swapaxes(state_ssm[0], -1, -2).reshape(bs, SSM_DINNER, SSM_DSTATE)
    gated_s, ssm_st = _ssd_sample(zx_s, dtr_s, state_conv, st_t, cw, cb, dtb, alog, dsk, gn, expand, bs, ls)
    ssm_s = jnp.swapaxes(ssm_st.reshape(1, bs, SSM_HEADS, SSM_HEADDIM, SSM_DSTATE), -1, -2)
    keep = SSM_CONV - 1
    conv_s = zx_s.reshape(bs, ls, -1)[None, :, ls - keep:, SSM_DINNER:]
    y_s = _out_proj(gated_s, ssm_wout, h1_s, tm=512, final_g=g_fin)

    return (y_p.reshape(bp, lp, d), y_s.reshape(bs, ls, d), ret_p, ret_s, ssm_p, ssm_s, conv_p, conv_s)
```

```python
import functools
import math

import jax
import jax.numpy as jnp
from jax import lax
from jax.experimental import pallas as pl
from jax.experimental.pallas import tpu as pltpu

F32 = jnp.float32
BF16 = jnp.bfloat16

D_MODEL = 1024
RET_HEADS = 4
RET_DK = 256
RET_DV = 512
RET_VW = RET_HEADS * RET_DV
RET_QK = RET_HEADS * RET_DK
SSM_DINNER = 2048
SSM_HEADDIM = 64
SSM_HEADS = 32
SSM_GROUPS = 8
SSM_DSTATE = 128
SSM_CONV = 4
SSM_CONVDIM = 4096
SSM_GROUP_W = SSM_DINNER // SSM_GROUPS
HEADS_PER_GROUP = SSM_HEADS // SSM_GROUPS
ROPE_BASE = 10000.0
EPS = 1e-6
PAST_LEN = 16384

LANES = 128
SUBLANES = 8
VMEM_LIMIT = 56 * 1024 * 1024

RET_CHUNK = 256
RET_PROMPT_SEQS = 2
SSD_CHUNK = 128
SSD_PROMPT_SEQS = 4
RET_SAMPLE_BB = 16
SSD_SAMPLE_BB = 8


def _cparams(sem, flags=None):
    return pltpu.CompilerParams(dimension_semantics=sem, vmem_limit_bytes=VMEM_LIMIT, flags=flags)


def _sigmoid(x):
    return 1.0 / (1.0 + jnp.exp2(x * (-math.log2(math.e))))


def _silu(x):
    return x * _sigmoid(x)


def _softplus(x):
    return jnp.maximum(x, 0.0) + jnp.log1p(jnp.exp(-jnp.abs(x)))


def _dot(a, b):
    return jnp.dot(a, b, preferred_element_type=F32)


def _dot_nt(a, b):
    return lax.dot_general(a, b, (((1,), (1,)), ((), ())), preferred_element_type=F32)


def _dot_tn(a, b):
    return lax.dot_general(a, b, (((0,), (0,)), ((), ())), preferred_element_type=F32)


def _iota(shape, dim):
    return lax.broadcasted_iota(jnp.int32, shape, dim)


def _norm_proj_kernel(x_ref, g_ref, w_ref, *rest, has_dt, w_is_t):
    if has_dt:
        wdt_ref, o_ref, dt_ref, xn_ref = rest
    else:
        o_ref, xn_ref = rest

    @pl.when(pl.program_id(1) == 0)
    def _():
        x = x_ref[...]
        ms = jnp.mean(x * x, axis=-1, keepdims=True)
        xn = (x * lax.rsqrt(ms + EPS) * g_ref[...]).astype(BF16)
        xn_ref[...] = xn
        if has_dt:
            dt_ref[...] = _dot(xn, wdt_ref[...])

    w = w_ref[...].astype(BF16)
    o_ref[...] = (_dot_nt(xn_ref[...], w) if w_is_t else _dot(xn_ref[...], w)).astype(o_ref.dtype)


def _norm_proj(x, g, w, *, n, tm, tn, out_dtype, w_dt=None, w_is_t=False):
    m, d = x.shape
    has_dt = w_dt is not None
    in_specs = [
        pl.BlockSpec((tm, d), lambda i, j: (i, 0)),
        pl.BlockSpec((1, d), lambda i, j: (0, 0)),
        pl.BlockSpec((tn, d), lambda i, j: (j, 0)) if w_is_t else pl.BlockSpec((d, tn), lambda i, j: (0, j)),
    ]
    args = [x, g, w]
    out_shape = [jax.ShapeDtypeStruct((m, n), out_dtype)]
    out_specs = [pl.BlockSpec((tm, tn), lambda i, j: (i, j))]
    if has_dt:
        in_specs.append(pl.BlockSpec((d, LANES), lambda i, j: (0, 0)))
        args.append(w_dt)
        out_shape.append(jax.ShapeDtypeStruct((m, LANES), F32))
        out_specs.append(pl.BlockSpec((tm, LANES), lambda i, j: (i, 0)))
    res = pl.pallas_call(
        functools.partial(_norm_proj_kernel, has_dt=has_dt, w_is_t=w_is_t),
        grid=(m // tm, n // tn),
        in_specs=in_specs,
        out_specs=out_specs,
        out_shape=out_shape,
        scratch_shapes=[pltpu.VMEM((tm, d), BF16)],
        compiler_params=_cparams(("arbitrary", "arbitrary")),
        name="in_proj",
    )(*args)
    return res if has_dt else res[0]


def _rope(x, cos, sin):
    half = RET_DK // 2
    x1 = x[:, :half]
    x2 = x[:, half:]
    return jnp.concatenate([x1 * cos - x2 * sin, x1 * sin + x2 * cos], axis=1)


def _ret_proj_kernel(x_ref, g_ref, w_ref, cos_ref, sin_ref, o_ref, xn_ref, *, rope_tiles):
    j = pl.program_id(1)
    tn = o_ref.shape[1]

    @pl.when(j == 0)
    def _():
        x = x_ref[...]
        ms = jnp.mean(x * x, axis=-1, keepdims=True)
        xn_ref[...] = (x * lax.rsqrt(ms + EPS) * g_ref[...]).astype(BF16)

    @pl.when(j < rope_tiles)
    def _():
        cos = cos_ref[...]
        sin = sin_ref[...]
        for h in range(tn // RET_DK):
            sl = slice(h * RET_DK, (h + 1) * RET_DK)
            y = _dot(xn_ref[...], w_ref[:, sl].astype(BF16))
            o_ref[:, sl] = _rope(y, cos, sin).astype(o_ref.dtype)

    @pl.when(j >= rope_tiles)
    def _():
        o_ref[...] = _dot(xn_ref[...], w_ref[...].astype(BF16)).astype(o_ref.dtype)


def _ret_proj(x, g, w, cos, sin, *, tm, tn):
    m, d = x.shape
    n = w.shape[1]
    period_tiles = max(cos.shape[0] // tm, 1)
    return pl.pallas_call(
        functools.partial(_ret_proj_kernel, rope_tiles=2 * RET_QK // tn),
        grid=(m // tm, n // tn),
        in_specs=[
            pl.BlockSpec((tm, d), lambda i, j: (i, 0)),
            pl.BlockSpec((1, d), lambda i, j: (0, 0)),
            pl.BlockSpec((d, tn), lambda i, j: (0, j)),
            pl.BlockSpec((tm, RET_DK // 2), lambda i, j: (i % period_tiles, 0)),
            pl.BlockSpec((tm, RET_DK // 2), lambda i, j: (i % period_tiles, 0)),
        ],
        out_specs=pl.BlockSpec((tm, tn), lambda i, j: (i, j)),
        out_shape=jax.ShapeDtypeStruct((m, n), BF16),
        scratch_shapes=[pltpu.VMEM((tm, d), BF16)],
        compiler_params=_cparams(("arbitrary", "arbitrary")),
        name="ret_in_proj",
    )(x, g, w, cos, sin)


def _shift_rows(a, first_rows):
    rows, cols = a.shape
    n = first_rows.shape[0]
    slabs = rows // SUBLANES
    rot = pltpu.roll(a.reshape(slabs, SUBLANES, cols), n, axis=1)
    first = jnp.concatenate([first_rows, jnp.zeros((SUBLANES - n, cols), a.dtype)], axis=0)
    above = jnp.concatenate([first.reshape(1, SUBLANES, cols), rot[:slabs - 1]], axis=0)
    top = _iota((slabs, SUBLANES, cols), 1) < n
    return jnp.where(top, above, rot).reshape(rows, cols)


def _norm_proj_conv_kernel(x_ref, g_ref, w_ref, wdt_ref, cw_ref, cb_ref, o_ref, dt_ref, tail_ref,
                           xn_ref, carry_ref, *, tiles_per_seq, z_tiles, strip):
    i = pl.program_id(0)
    j = pl.program_id(1)
    tm, tn = o_ref.shape

    @pl.when(j == 0)
    def _():
        x = x_ref[...]
        ms = jnp.mean(x * x, axis=-1, keepdims=True)
        xn = (x * lax.rsqrt(ms + EPS) * g_ref[...]).astype(BF16)
        xn_ref[...] = xn
        dt_ref[...] = _dot(xn, wdt_ref[...])

    @pl.when((j == 0) & (i % tiles_per_seq == 0))
    def _():
        carry_ref[...] = jnp.zeros_like(carry_ref)

    @pl.when(j < z_tiles)
    def _():
        o_ref[...] = _dot_nt(xn_ref[...], w_ref[...].astype(BF16)).astype(o_ref.dtype)

    @pl.when(j >= z_tiles)
    def _():
        jc = j - z_tiles
        for s in range(tn // strip):
            sl = slice(s * strip, (s + 1) * strip)
            x = _dot_nt(xn_ref[...], w_ref[sl, :].astype(BF16))
            tail_ref[:, sl] = x[tm - SUBLANES:tm, :]
            x2 = _shift_rows(x, carry_ref[jc, 1:3, sl])
            p = x * cw_ref[2:3, sl] + x2 * cw_ref[0:1, sl]
            out = x * cw_ref[3:4, sl] + x2 * cw_ref[1:2, sl] + _shift_rows(p, carry_ref[jc, 0:1, sl]) + cb_ref[:, sl]
            o_ref[:, sl] = _silu(out).astype(o_ref.dtype)
            carry_ref[jc, 0:1, sl] = p[tm - 1:tm]
            carry_ref[jc, 1:3, sl] = x[tm - 2:tm]


def _norm_proj_conv(x, g, wt, w_dt, cw, cb, *, tm, tn, seq):
    m, d = x.shape
    n = SSM_DINNER + SSM_CONVDIM
    z_tiles = SSM_DINNER // tn
    c_tiles = SSM_CONVDIM // tn
    tiles_per_seq = seq // tm
    cj = lambda j: jnp.maximum(j - z_tiles, 0)
    return pl.pallas_call(
        functools.partial(_norm_proj_conv_kernel, tiles_per_seq=tiles_per_seq, z_tiles=z_tiles, strip=256),
        grid=(m // tm, n // tn),
        in_specs=[
            pl.BlockSpec((tm, d), lambda i, j: (i, 0)),
            pl.BlockSpec((1, d), lambda i, j: (0, 0)),
            pl.BlockSpec((tn, d), lambda i, j: (j, 0)),
            pl.BlockSpec((d, LANES), lambda i, j: (0, 0)),
            pl.BlockSpec((SSM_CONV, tn), lambda i, j: (0, cj(j))),
            pl.BlockSpec((1, tn), lambda i, j: (0, cj(j))),
        ],
        out_specs=[
            pl.BlockSpec((tm, tn), lambda i, j: (i, j)),
            pl.BlockSpec((tm, LANES), lambda i, j: (i, 0)),
            pl.BlockSpec((None, SUBLANES, tn), lambda i, j: (i, 0, cj(j))),
        ],
        out_shape=[
            jax.ShapeDtypeStruct((m, n), BF16),
            jax.ShapeDtypeStruct((m, LANES), F32),
            jax.ShapeDtypeStruct((m // tm, SUBLANES, SSM_CONVDIM), F32),
        ],
        scratch_shapes=[
            pltpu.VMEM((tm, d), BF16),
            pltpu.VMEM((c_tiles, SUBLANES, tn), F32),
        ],
        compiler_params=_cparams(("arbitrary", "arbitrary")),
        name="ssd_in_proj_conv",
    )(x, g, wt, w_dt, cw, cb)


def _out_proj_kernel(a_ref, w_ref, h_ref, *rest, has_norm):
    if has_norm:
        g_ref, o_ref = rest
    else:
        (o_ref,) = rest
    y = h_ref[...] + _dot(a_ref[...], w_ref[...])
    if has_norm:
        ms = jnp.mean(y * y, axis=-1, keepdims=True)
        y = y * lax.rsqrt(ms + EPS) * g_ref[...]
    o_ref[...] = y


def _out_proj(a, w, h, *, tm, final_g=None):
    m, k = a.shape
    d = w.shape[1]
    has_norm = final_g is not None
    in_specs = [
        pl.BlockSpec((tm, k), lambda i: (i, 0)),
        pl.BlockSpec((k, d), lambda i: (0, 0)),
        pl.BlockSpec((tm, d), lambda i: (i, 0)),
    ]
    args = [a, w, h]
    if has_norm:
        in_specs.append(pl.BlockSpec((1, d), lambda i: (0, 0)))
        args.append(final_g)
    return pl.pallas_call(
        functools.partial(_out_proj_kernel, has_norm=has_norm),
        grid=(m // tm,),
        in_specs=in_specs,
        out_specs=pl.BlockSpec((tm, d), lambda i: (i, 0)),
        out_shape=jax.ShapeDtypeStruct((m, d), F32),
        compiler_params=_cparams(("arbitrary",)),
        name="out_proj",
    )(*args)


def _head_norm_gate(o, hn, gate):
    ms = jnp.mean(o * o, axis=-1, keepdims=True)
    on = o * lax.rsqrt(ms + EPS) * hn
    return (on * _silu(gate.astype(F32))).astype(BF16)


def _ret_prompt_kernel(q_ref, k_ref, v_ref, gate_ref, lg_ref, hn_ref,
                       og_ref, st_ref, s_ref, dec_ref, qsc_ref, ksc_ref, gc_ref):
    b = pl.program_id(0)
    c = pl.program_id(1)
    n_seq, rows = q_ref.shape[0], q_ref.shape[1]

    @pl.when((b == 0) & (c == 0))
    def _():
        i = _iota((rows, rows), 0)
        j = _iota((rows, rows), 1)
        r = _iota((rows, RET_DK), 0)
        scale = RET_DK ** -0.5
        for h in range(RET_HEADS):
            lg = lg_ref[h, :, 0:1]
            dec_ref[h] = jnp.where(i >= j, jnp.exp((i - j).astype(F32) * lg), 0.0) * scale
            qsc_ref[h] = jnp.exp((r + 1).astype(F32) * lg)
            ksc_ref[h] = jnp.exp((rows - 1 - r).astype(F32) * lg) * scale
            gc_ref[h] = jnp.exp(jnp.full((1, RET_DV), float(rows), F32) * lg)

    @pl.when(c == 0)
    def _():
        s_ref[...] = jnp.zeros_like(s_ref)

    for n in range(n_seq):
        for h in range(RET_HEADS):
            ksl = slice(h * RET_DK, (h + 1) * RET_DK)
            vsl = slice(h * RET_DV, (h + 1) * RET_DV)
            q = q_ref[n, :, ksl]
            k = k_ref[n, :, ksl]
            v = v_ref[n, :, vsl]
            p = (_dot_nt(q, k) * dec_ref[h]).astype(BF16)
            qs = (q.astype(F32) * qsc_ref[h]).astype(BF16)
            kr = k.astype(F32)
            s = s_ref[n, h]
            o = _dot(p, v) + _dot(qs, s.astype(BF16))
            kt = (kr * ksc_ref[h]).T.astype(BF16)
            s_ref[n, h] = s * gc_ref[h] + _dot(kt, v)
            og_ref[n, :, vsl] = _head_norm_gate(o, hn_ref[:, vsl], gate_ref[n, :, vsl])

    @pl.when(c == pl.num_programs(1) - 1)
    def _():
        st_ref[...] = s_ref[...]


def _ret_prompt(proj, lg, hn, batch, seq):
    rows = RET_CHUNK
    nc = seq // rows
    ns = RET_PROMPT_SEQS
    return pl.pallas_call(
        _ret_prompt_kernel,
        grid=(batch // ns, nc),
        in_specs=[
            pl.BlockSpec((ns, rows, RET_QK), lambda b, c: (b, c, 0)),
            pl.BlockSpec((ns, rows, RET_QK), lambda b, c: (b, c, 1)),
            pl.BlockSpec((ns, rows, RET_VW), lambda b, c: (b, c, 2 * RET_QK // RET_VW)),
            pl.BlockSpec((ns, rows, RET_VW), lambda b, c: (b, c, 2 * RET_QK // RET_VW + 1)),
            pl.BlockSpec((RET_HEADS, 1, LANES), lambda b, c: (0, 0, 0)),
            pl.BlockSpec((1, RET_VW), lambda b, c: (0, 0)),
        ],
        out_specs=[
            pl.BlockSpec((ns, rows, RET_VW), lambda b, c: (b, c, 0)),
            pl.BlockSpec((None, ns, RET_HEADS, RET_DK, RET_DV), lambda b, c: (0, b, 0, 0, 0)),
        ],
        out_shape=[
            jax.ShapeDtypeStruct((batch, seq, RET_VW), BF16),
            jax.ShapeDtypeStruct((1, batch, RET_HEADS, RET_DK, RET_DV), F32),
        ],
        scratch_shapes=[
            pltpu.VMEM((ns, RET_HEADS, RET_DK, RET_DV), F32),
            pltpu.VMEM((RET_HEADS, rows, rows), F32),
            pltpu.VMEM((RET_HEADS, rows, RET_DK), F32),
            pltpu.VMEM((RET_HEADS, rows, RET_DK), F32),
            pltpu.VMEM((RET_HEADS, 1, RET_DV), F32),
        ],
        compiler_params=_cparams(("arbitrary", "arbitrary")),
        name="ret_prompt",
    )(proj, proj, proj, proj, lg, hn)


def _ret_sample_kernel(q_ref, k_ref, v_ref, gate_ref, lg_ref, hn_ref, st_in_ref,
                       og_ref, st_out_ref, *, seq):
    bb = st_in_ref.shape[0]
    rows = bb * seq
    lg = lg_ref[:, 0:1]
    scale = RET_DK ** -0.5
    i = _iota((rows, rows), 0)
    j = _iota((rows, rows), 1)
    same = (i // seq) == (j // seq)
    dec = jnp.where(same & (i >= j), jnp.exp((i - j).astype(F32) * lg), 0.0) * scale
    pos = _iota((rows, RET_DK), 0) % seq
    qsc = jnp.exp((pos + 1).astype(F32) * lg)
    ksc = jnp.exp((seq - 1 - pos).astype(F32) * lg) * scale
    gc = jnp.exp(jnp.full((1, RET_DV), float(seq), F32) * lg)

    qr = q_ref[...].astype(F32)
    kr = k_ref[...].astype(F32)
    v = v_ref[...].astype(F32)
    p = _dot_nt(q_ref[...], k_ref[...]) * dec
    intra = _dot(p.astype(BF16), v.astype(BF16))
    qs = qr * qsc
    kt = kr * ksc
    vrow = _iota((rows, RET_DV), 0) // seq
    cross = []
    for n in range(bb):
        s0 = st_in_ref[n]
        cross.append(_dot(qs[n * seq:(n + 1) * seq], s0))
        vn = jnp.where(vrow == n, v, 0.0)
        st_out_ref[n] = s0 * gc + _dot_tn(kt.astype(BF16), vn.astype(BF16))
    o = intra + jnp.concatenate(cross, axis=0)
    og_ref[...] = _head_norm_gate(o, hn_ref[...], gate_ref[...])


def _ret_sample(proj, lg, hn, state, batch, seq):
    bb = RET_SAMPLE_BB
    rows = bb * seq
    qb = RET_QK // RET_DK
    vb = 2 * RET_QK // RET_DV
    gb = vb + RET_VW // RET_DV
    st_spec = pl.BlockSpec((None, bb, None, RET_DK, RET_DV), lambda n, h: (0, n, h, 0, 0))
    return pl.pallas_call(
        functools.partial(_ret_sample_kernel, seq=seq),
        grid=(batch // bb, RET_HEADS),
        in_specs=[
            pl.BlockSpec((rows, RET_DK), lambda n, h: (n, h)),
            pl.BlockSpec((rows, RET_DK), lambda n, h: (n, qb + h)),
            pl.BlockSpec((rows, RET_DV), lambda n, h: (n, vb + h)),
            pl.BlockSpec((rows, RET_DV), lambda n, h: (n, gb + h)),
            pl.BlockSpec((None, 1, LANES), lambda n, h: (h, 0, 0)),
            pl.BlockSpec((1, RET_DV), lambda n, h: (0, h)),
            st_spec,
        ],
        out_specs=[
            pl.BlockSpec((rows, RET_DV), lambda n, h: (n, h)),
            st_spec,
        ],
        out_shape=[
            jax.ShapeDtypeStruct((batch * seq, RET_VW), BF16),
            jax.ShapeDtypeStruct(state.shape, F32),
        ],
        compiler_params=_cparams(("arbitrary", "arbitrary")),
        name="ret_sample",
    )(proj, proj, proj, proj, lg, hn, state)


def _head_expand(vals, expand):
    parts = []
    for v in vals:
        hi = v.astype(BF16)
        lo = (v - hi.astype(F32)).astype(BF16)
        parts.append(jnp.concatenate([hi, lo], axis=1))
    return _dot(jnp.concatenate(parts, axis=0), expand)


def _ssd_prompt_kernel(z_ref, xc_ref, bc_ref, dtr_ref, dtr_next_ref, dtb_ref, alog_ref, dsk_ref, gn_ref, exp_ref,
                       g_ref, st_ref, s_ref, ex_ref, cum2_ref, cum2t_ref):
    c = pl.program_id(1)
    n_seq, rows = z_ref.shape[0], z_ref.shape[1]
    bw = SSM_GROUP_W
    ri = _iota((rows, rows), 0)
    rj = _iota((rows, rows), 1)
    causal = ri >= rj

    def decay_tables(n, dtr):
        dt = _softplus(dtr + dtb_ref[...])
        la = dt * (-jnp.exp(alog_ref[...]))
        cum = jnp.dot(causal.astype(F32), la, precision=lax.Precision.HIGHEST, preferred_element_type=F32)
        ecum = jnp.exp(cum)
        wdt = jnp.exp(cum[rows - 1:rows, :] - cum) * dt
        ex_ref[n] = _head_expand([ecum, wdt, dt], exp_ref[...])
        cum2 = cum * math.log2(math.e)
        cum2_ref[n] = cum2
        cum2t_ref[n] = cum2.T

    @pl.when(c == 0)
    def _():
        s_ref[...] = jnp.zeros_like(s_ref)
        for n in range(n_seq):
            decay_tables(n, dtr_ref[n])

    head_lane = _iota((rows, bw), 1) // SSM_HEADDIM

    for n in range(n_seq):
        cum2 = cum2_ref[n]
        cum2_t = cum2t_ref[n]
        for g in range(SSM_GROUPS):
            gsl = slice(g * bw, (g + 1) * bw)
            bm = bc_ref[n, :, g * SSM_DSTATE:(g + 1) * SSM_DSTATE]
            cm = bc_ref[n, :, (SSM_GROUPS + g) * SSM_DSTATE:(SSM_GROUPS + g + 1) * SSM_DSTATE]
            xs = xc_ref[n, :, gsl].astype(F32)
            s = s_ref[n, g]
            gmat = _dot_nt(cm, bm)
            ps = []
            for hh in range(HEADS_PER_GROUP):
                h = g * HEADS_PER_GROUP + hh
                colb = jnp.broadcast_to(cum2[:, h:h + 1], (rows, LANES))
                dec = jnp.exp2(jnp.where(causal, colb - cum2_t[h:h + 1, :], -jnp.inf))
                ps.append((gmat * dec).astype(BF16))
            v = (xs * ex_ref[n, 2 * rows:3 * rows, gsl]).astype(BF16)
            y4 = _dot(jnp.concatenate(ps, axis=0), v)
            y = y4[0:rows]
            for hh in range(1, HEADS_PER_GROUP):
                y = jnp.where(head_lane == hh, y4[hh * rows:(hh + 1) * rows], y)
            y = y + _dot(cm, s.astype(BF16)) * ex_ref[n, 0:rows, gsl]
            vw = (xs * ex_ref[n, rows:2 * rows, gsl]).astype(BF16)
            s_ref[n, g] = s * ex_ref[n, rows - 1:rows, gsl] + _dot_tn(bm, vw)

            y = y + xs * dsk_ref[:, gsl]
            gg = y * _silu(z_ref[n, :, gsl].astype(F32))
            ms = jnp.mean(gg * gg, axis=-1, keepdims=True)
            g_ref[n, :, gsl] = (gg * lax.rsqrt(ms + EPS) * gn_ref[:, gsl]).astype(BF16)

        decay_tables(n, dtr_next_ref[n])

    @pl.when(c == pl.num_programs(1) - 1)
    def _():
        for n in range(n_seq):
            for g in range(SSM_GROUPS):
                st_ref[n, g * bw:(g + 1) * bw, :] = s_ref[n, g].T


def _ssd_prompt(zxc, dtr, dtb, alog, dsk, gn, expand, batch, seq):
    rows = SSD_CHUNK
    nc = seq // rows
    ns = SSD_PROMPT_SEQS
    full = lambda shape: pl.BlockSpec(shape, lambda b, c: (0,) * len(shape))
    col = lambda k: pl.BlockSpec((ns, rows, SSM_DINNER), lambda b, c: (b, c, k))
    return pl.pallas_call(
        _ssd_prompt_kernel,
        grid=(batch // ns, nc),
        in_specs=[
            col(0), col(1), col(2),
            pl.BlockSpec((ns, rows, LANES), lambda b, c: (b, c, 0)),
            pl.BlockSpec((ns, rows, LANES), lambda b, c: (b, jnp.minimum(c + 1, nc - 1), 0)),
            full((1, LANES)), full((1, LANES)),
            full((1, SSM_DINNER)), full((1, SSM_DINNER)),
            full((2 * LANES, SSM_DINNER)),
        ],
        out_specs=[
            pl.BlockSpec((ns, rows, SSM_DINNER), lambda b, c: (b, c, 0)),
            pl.BlockSpec((ns, SSM_DINNER, SSM_DSTATE), lambda b, c: (b, 0, 0)),
        ],
        out_shape=[
            jax.ShapeDtypeStruct((batch, seq, SSM_DINNER), BF16),
            jax.ShapeDtypeStruct((batch, SSM_DINNER, SSM_DSTATE), F32),
        ],
        scratch_shapes=[
            pltpu.VMEM((ns, SSM_GROUPS, SSM_DSTATE, SSM_GROUP_W), F32),
            pltpu.VMEM((ns, 3 * rows, SSM_DINNER), F32),
            pltpu.VMEM((ns, rows, LANES), F32),
            pltpu.VMEM((ns, LANES, rows), F32),
        ],
        compiler_params=_cparams(("arbitrary", "arbitrary")),
        name="ssd_prompt",
    )(zxc, zxc, zxc, dtr, dtr, dtb, alog, dsk, gn, expand)


def _ssd_sample_kernel(z_ref, xa_ref, xb_ref, dtr_ref, conv_in_ref, st_in_ref,
                       cw_ref, cb_ref, dtb_ref, alog_ref, dsk_ref, gn_ref, exp_ref,
                       g_ref, st_out_ref, xc_ref, ex_ref, y_ref, *, seq):
    bb = st_in_ref.shape[0]
    rows = bb * seq
    keep = SSM_CONV - 1
    bw = SSM_GROUP_W

    assert seq == SUBLANES
    sub = _iota((bb, SUBLANES, SSM_DINNER), 1)
    hist_pad = jnp.zeros((bb, SUBLANES - keep, SSM_DINNER), F32)
    for half, x_ref in enumerate((xa_ref, xb_ref)):
        csl = slice(half * SSM_DINNER, (half + 1) * SSM_DINNER)
        w0, w1, w2, w3 = (cw_ref[k:k + 1, csl] for k in range(SSM_CONV))
        x = x_ref[...].reshape(bb, SUBLANES, SSM_DINNER)
        hist = jnp.concatenate([hist_pad, conv_in_ref[:, :, csl]], axis=1)
        hist2 = pltpu.roll(hist, 2, axis=1)
        x2 = jnp.where(sub < 2, hist2, pltpu.roll(x, 2, axis=1))
        p = x * w2 + x2 * w0
        p_hist = hist * w2 + hist2 * w0
        p1 = jnp.where(sub < 1, pltpu.roll(p_hist, 1, axis=1), pltpu.roll(p, 1, axis=1))
        out = x * w3 + x2 * w1 + p1 + cb_ref[:, csl]
        xc_ref[:, csl] = _silu(out).reshape(rows, SSM_DINNER)

    dt = _softplus(dtr_ref[...] + dtb_ref[...])
    la = dt * (-jnp.exp(alog_ref[...]))
    ri = _iota((rows, rows), 0)
    rj = _iota((rows, rows), 1)
    same = (ri // seq) == (rj // seq)
    cum = jnp.dot((same & (ri >= rj)).astype(F32), la, precision=lax.Precision.HIGHEST,
                  preferred_element_type=F32)
    clast = jnp.dot(same.astype(F32), la, precision=lax.Precision.HIGHEST, preferred_element_type=F32)
    eclast = jnp.exp(clast)
    ex_ref[...] = _head_expand([jnp.exp(cum), jnp.exp(clast - cum) * dt, dt], exp_ref[...])

    si = _iota((seq, seq), 0)
    sj = _iota((seq, seq), 1)
    causal = si >= sj
    eye = si == sj
    head_lane = _iota((seq, bw), 1) // SSM_HEADDIM

    for n in range(bb):
        r0 = n * seq
        rsl = slice(r0, r0 + seq)
        cum_n = cum[rsl]
        for g in range(SSM_GROUPS):
            gsl = slice(g * bw, (g + 1) * bw)
            bm = xc_ref[rsl, SSM_DINNER + g * SSM_DSTATE:SSM_DINNER + (g + 1) * SSM_DSTATE]
            cm = xc_ref[rsl, SSM_DINNER + (SSM_GROUPS + g) * SSM_DSTATE:
                        SSM_DINNER + (SSM_GROUPS + g + 1) * SSM_DSTATE]
            xs = xc_ref[rsl, gsl]
            gmat = _dot_nt(cm.astype(BF16), bm.astype(BF16))
            ps = []
            for hh in range(HEADS_PER_GROUP):
                h = g * HEADS_PER_GROUP + hh
                ccol = cum_n[:, h:h + 1]
                crow = jnp.sum(jnp.where(eye, ccol, 0.0), axis=0, keepdims=True)
                ps.append(gmat * jnp.exp(jnp.where(causal, ccol - crow, -jnp.inf)))
            v = xs * ex_ref[2 * rows + r0:2 * rows + r0 + seq, gsl]
            y4 = _dot(jnp.concatenate(ps, axis=0).astype(BF16), v.astype(BF16))
            y = y4[0:seq]
            for hh in range(1, HEADS_PER_GROUP):
                y = jnp.where(head_lane == hh, y4[hh * seq:(hh + 1) * seq], y)
            st = st_in_ref[n, gsl, :]
            y = y + _dot_nt(cm, st) * ex_ref[rsl, gsl]
            y_ref[rsl, gsl] = y
            vw = xs * ex_ref[rows + r0:rows + r0 + seq, gsl]
            upd = _dot_tn(vw.astype(BF16), bm.astype(BF16))
            for hh in range(HEADS_PER_GROUP):
                h = g * HEADS_PER_GROUP + hh
                hsl = slice(hh * SSM_HEADDIM, (hh + 1) * SSM_HEADDIM)
                osl = slice(g * bw + hh * SSM_HEADDIM, g * bw + (hh + 1) * SSM_HEADDIM)
                st_out_ref[n, osl, :] = st[hsl] * eclast[r0:r0 + 1, h:h + 1] + upd[hsl]

    y = y_ref[...] + xc_ref[:, 0:SSM_DINNER] * dsk_ref[...]
    gg = y * _silu(z_ref[...])
    for g in range(SSM_GROUPS):
        gsl = slice(g * bw, (g + 1) * bw)
        blk = gg[:, gsl]
        ms = jnp.mean(blk * blk, axis=-1, keepdims=True)
        g_ref[:, gsl] = (blk * lax.rsqrt(ms + EPS) * gn_ref[:, gsl]).astype(BF16)


def _ssd_sample(zx, dtr, conv_state, ssm_state_t, cw, cb, dtb, alog, dsk, gn, expand, batch, seq):
    bb = SSD_SAMPLE_BB
    rows = bb * seq
    keep = SSM_CONV - 1
    full = lambda shape: pl.BlockSpec(shape, lambda n: (0,) * len(shape))
    col = lambda k: pl.BlockSpec((rows, SSM_DINNER), lambda n: (n, k))
    st_spec = pl.BlockSpec((bb, SSM_DINNER, SSM_DSTATE), lambda n: (n, 0, 0))
    return pl.pallas_call(
        functools.partial(_ssd_sample_kernel, seq=seq),
        grid=(batch // bb,),
        in_specs=[
            col(0), col(1), col(2),
            pl.BlockSpec((rows, LANES), lambda n: (n, 0)),
            pl.BlockSpec((None, bb, keep, SSM_CONVDIM), lambda n: (0, n, 0, 0)),
            st_spec,
            full((SSM_CONV, SSM_CONVDIM)), full((1, SSM_CONVDIM)),
            full((1, LANES)), full((1, LANES)),
            full((1, SSM_DINNER)), full((1, SSM_DINNER)),
            full((2 * LANES, SSM_DINNER)),
        ],
        out_specs=[
            pl.BlockSpec((rows, SSM_DINNER), lambda n: (n, 0)),
            st_spec,
        ],
        out_shape=[
            jax.ShapeDtypeStruct((batch * seq, SSM_DINNER), BF16),
            jax.ShapeDtypeStruct(ssm_state_t.shape, F32),
        ],
        scratch_shapes=[
            pltpu.VMEM((rows, SSM_CONVDIM), F32),
            pltpu.VMEM((3 * rows, SSM_DINNER), F32),
            pltpu.VMEM((rows, SSM_DINNER), F32),
        ],
        compiler_params=_cparams(("arbitrary",)),
        name="ssd_sample",
    )(zx, zx, zx, dtr, conv_state, ssm_state_t, cw, cb, dtb, alog, dsk, gn, expand)


def _rope_tables(pos):
    half = RET_DK // 2
    freqs = ROPE_BASE ** (-jnp.arange(half, dtype=F32) / half)
    ang = pos.astype(F32)[:, None] * freqs[None, :]
    return jnp.cos(ang), jnp.sin(ang)


def _row(v):
    return v.reshape(1, -1).astype(F32)


def _pad_lanes(v):
    return jnp.pad(v.astype(F32), (0, LANES - v.shape[0])).reshape(1, LANES)


def kernel(x_prompt, x_sample, state_ret, state_ssm, state_conv, ret_norm, ret_w_in, ret_head_norm, ret_w_out, ssm_norm, ssm_w_in, ssm_conv_w, ssm_conv_b, ssm_dt_bias, ssm_a_log, ssm_d, ssm_gate_norm, ssm_w_out, final_norm):
    bp, lp, d = x_prompt.shape
    bs, ls, _ = x_sample.shape
    assert ret_norm.shape[0] == 1 and ssm_norm.shape[0] == 1, "one retention and one SSD layer"

    ret_win = ret_w_in[0]
    ret_wout = ret_w_out[0].astype(BF16)
    n_main = SSM_DINNER + SSM_CONVDIM
    ssm_win_t = jnp.swapaxes(ssm_w_in[0], 0, 1)
    ssm_wdt = jnp.pad(ssm_w_in[0, :, n_main:], ((0, 0), (0, LANES - SSM_HEADS))).astype(BF16)
    ssm_wout = ssm_w_out[0].astype(BF16)
    expand = (jnp.arange(2 * LANES)[:, None] % LANES == jnp.arange(SSM_DINNER)[None, :] // SSM_HEADDIM).astype(BF16)
    lg = jnp.log(1.0 - 2.0 ** (-5.0 - jnp.arange(RET_HEADS, dtype=F32)))
    lg = jnp.broadcast_to(lg[:, None, None], (RET_HEADS, 1, LANES))
    hn = _row(ret_head_norm[0])
    dsk = _row(jnp.repeat(ssm_d[0], SSM_HEADDIM))
    gn = _row(ssm_gate_norm[0])
    cw = ssm_conv_w[0].astype(F32)
    cb = _row(ssm_conv_b[0])
    dtb = _pad_lanes(ssm_dt_bias[0])
    alog = _pad_lanes(ssm_a_log[0])
    g_ret = _row(ret_norm[0])
    g_ssm = _row(ssm_norm[0])
    g_fin = _row(final_norm)

    xp = x_prompt.reshape(bp * lp, d)
    cos_p, sin_p = _rope_tables(jnp.arange(lp, dtype=jnp.int32))
    n_ret = 2 * RET_QK + 2 * RET_VW
    proj = _ret_proj(xp, g_ret, ret_win, cos_p, sin_p, tm=lp, tn=1024)
    og, ret_p = _ret_prompt(proj.reshape(bp, lp, n_ret), lg, hn, bp, lp)
    h1 = _out_proj(og.reshape(bp * lp, RET_VW), ret_wout, xp, tm=1024)
    tm_c = 1024
    zxc, dtr, tail = _norm_proj_conv(h1, g_ssm, ssm_win_t, ssm_wdt, cw, cb, tm=tm_c, tn=2048, seq=lp)
    tail = tail[lp // tm_c - 1::lp // tm_c]
    gated, ssm_pt = _ssd_prompt(zxc.reshape(bp, lp, -1), dtr.reshape(bp, lp, LANES), dtb, alog, dsk, gn, expand, bp, lp)
    y_p = _out_proj(gated.reshape(bp * lp, SSM_DINNER), ssm_wout, h1, tm=1024, final_g=g_fin)
    ssm_p = jnp.swapaxes(ssm_pt.reshape(1, bp, SSM_HEADS, SSM_HEADDIM, SSM_DSTATE), -1, -2)
    conv_p = tail[None, :, SUBLANES - (SSM_CONV - 1):, :]

    xs = x_sample.reshape(bs * ls, d)
    cos_s, sin_s = _rope_tables(PAST_LEN + jnp.arange(ls, dtype=jnp.int32))
    proj_s = _ret_proj(xs, g_ret, ret_win, jnp.tile(cos_s, (bs, 1)), jnp.tile(sin_s, (bs, 1)), tm=bs * ls, tn=1024)
    og_s, ret_s = _ret_sample(proj_s, lg, hn, state_ret, bs, ls)
    h1_s = _out_proj(og_s, ret_wout, xs, tm=512)
    zx_s, dtr_s = _norm_proj(h1_s, g_ssm, ssm_win_t, n=n_main, tm=bs * ls, tn=2048, out_dtype=F32, w_dt=ssm_wdt,
                             w_is_t=True)
    st_t = jnp.swapaxes(state_ssm[0], -1, -2).reshape(bs, SSM_DINNER, SSM_DSTATE)
    gated_s, ssm_st = _ssd_sample(zx_s, dtr_s, state_conv, st_t, cw, cb, dtb, alog, dsk, gn, expand, bs, ls)
    ssm_s = jnp.swapaxes(ssm_st.reshape(1, bs, SSM_HEADS, SSM_HEADDIM, SSM_DSTATE), -1, -2)
    keep = SSM_CONV - 1
    conv_s = zx_s.reshape(bs, ls, -1)[None, :, ls - keep:, SSM_DINNER:]
    y_s = _out_proj(gated_s, ssm_wout, h1_s, tm=512, final_g=g_fin)

    return (y_p.reshape(bp, lp, d), y_s.reshape(bs, ls, d), ret_p, ret_s, ssm_p, ssm_s, conv_p, conv_s)
```

```python
import functools
import math

import jax
import jax.numpy as jnp
from jax import lax
from jax.experimental import pallas as pl
from jax.experimental.pallas import tpu as pltpu

F32 = jnp.float32
BF16 = jnp.bfloat16

D_MODEL = 1024
RET_HEADS = 4
RET_DK = 256
RET_DV = 512
RET_VW = RET_HEADS * RET_DV
RET_QK = RET_HEADS * RET_DK
SSM_DINNER = 2048
SSM_HEADDIM = 64
SSM_HEADS = 32
SSM_GROUPS = 8
SSM_DSTATE = 128
SSM_CONV = 4
SSM_CONVDIM = 4096
SSM_GROUP_W = SSM_DINNER // SSM_GROUPS
HEADS_PER_GROUP = SSM_HEADS // SSM_GROUPS
ROPE_BASE = 10000.0
EPS = 1e-6
PAST_LEN = 16384

LANES = 128
SUBLANES = 8
VMEM_LIMIT = 56 * 1024 * 1024

RET_CHUNK = 256
RET_PROMPT_SEQS = 2
SSD_CHUNK = 128
SSD_PROMPT_SEQS = 2
RET_SAMPLE_BB = 16
SSD_SAMPLE_BB = 8


def _cparams(sem, flags=None):
    return pltpu.CompilerParams(dimension_semantics=sem, vmem_limit_bytes=VMEM_LIMIT, flags=flags)


def _sigmoid(x):
    return 1.0 / (1.0 + jnp.exp2(x * (-math.log2(math.e))))


def _silu(x):
    return x * _sigmoid(x)


def _softplus(x):
    return jnp.maximum(x, 0.0) + jnp.log1p(jnp.exp(-jnp.abs(x)))


def _dot(a, b):
    return jnp.dot(a, b, preferred_element_type=F32)


def _dot_nt(a, b):
    return lax.dot_general(a, b, (((1,), (1,)), ((), ())), preferred_element_type=F32)


def _dot_tn(a, b):
    return lax.dot_general(a, b, (((0,), (0,)), ((), ())), preferred_element_type=F32)


def _iota(shape, dim):
    return lax.broadcasted_iota(jnp.int32, shape, dim)


def _norm_proj_kernel(x_ref, g_ref, w_ref, *rest, has_dt, w_is_t):
    if has_dt:
        wdt_ref, o_ref, dt_ref, xn_ref = rest
    else:
        o_ref, xn_ref = rest

    @pl.when(pl.program_id(1) == 0)
    def _():
        x = x_ref[...]
        ms = jnp.mean(x * x, axis=-1, keepdims=True)
        xn = (x * lax.rsqrt(ms + EPS) * g_ref[...]).astype(BF16)
        xn_ref[...] = xn
        if has_dt:
            dt_ref[...] = _dot(xn, wdt_ref[...])

    w = w_ref[...].astype(BF16)
    o_ref[...] = (_dot_nt(xn_ref[...], w) if w_is_t else _dot(xn_ref[...], w)).astype(o_ref.dtype)


def _norm_proj(x, g, w, *, n, tm, tn, out_dtype, w_dt=None, w_is_t=False):
    m, d = x.shape
    has_dt = w_dt is not None
    in_specs = [
        pl.BlockSpec((tm, d), lambda i, j: (i, 0)),
        pl.BlockSpec((1, d), lambda i, j: (0, 0)),
        pl.BlockSpec((tn, d), lambda i, j: (j, 0)) if w_is_t else pl.BlockSpec((d, tn), lambda i, j: (0, j)),
    ]
    args = [x, g, w]
    out_shape = [jax.ShapeDtypeStruct((m, n), out_dtype)]
    out_specs = [pl.BlockSpec((tm, tn), lambda i, j: (i, j))]
    if has_dt:
        in_specs.append(pl.BlockSpec((d, LANES), lambda i, j: (0, 0)))
        args.append(w_dt)
        out_shape.append(jax.ShapeDtypeStruct((m, LANES), F32))
        out_specs.append(pl.BlockSpec((tm, LANES), lambda i, j: (i, 0)))
    res = pl.pallas_call(
        functools.partial(_norm_proj_kernel, has_dt=has_dt, w_is_t=w_is_t),
        grid=(m // tm, n // tn),
        in_specs=in_specs,
        out_specs=out_specs,
        out_shape=out_shape,
        scratch_shapes=[pltpu.VMEM((tm, d), BF16)],
        compiler_params=_cparams(("arbitrary", "arbitrary")),
        name="in_proj",
    )(*args)
    return res if has_dt else res[0]


def _rope(x, cos, sin):
    half = RET_DK // 2
    x1 = x[:, :half]
    x2 = x[:, half:]
    return jnp.concatenate([x1 * cos - x2 * sin, x1 * sin + x2 * cos], axis=1)


def _ret_proj_kernel(x_ref, g_ref, w_ref, cos_ref, sin_ref, o_ref, xn_ref, *, rope_tiles):
    j = pl.program_id(1)
    tn = o_ref.shape[1]

    @pl.when(j == 0)
    def _():
        x = x_ref[...]
        ms = jnp.mean(x * x, axis=-1, keepdims=True)
        xn_ref[...] = (x * lax.rsqrt(ms + EPS) * g_ref[...]).astype(BF16)

    @pl.when(j < rope_tiles)
    def _():
        cos = cos_ref[...]
        sin = sin_ref[...]
        for h in range(tn // RET_DK):
            sl = slice(h * RET_DK, (h + 1) * RET_DK)
            y = _dot(xn_ref[...], w_ref[:, sl].astype(BF16))
            o_ref[:, sl] = _rope(y, cos, sin).astype(o_ref.dtype)

    @pl.when(j >= rope_tiles)
    def _():
        o_ref[...] = _dot(xn_ref[...], w_ref[...].astype(BF16)).astype(o_ref.dtype)


def _ret_proj(x, g, w, cos, sin, *, tm, tn):
    m, d = x.shape
    n = w.shape[1]
    period_tiles = max(cos.shape[0] // tm, 1)
    return pl.pallas_call(
        functools.partial(_ret_proj_kernel, rope_tiles=2 * RET_QK // tn),
        grid=(m // tm, n // tn),
        in_specs=[
            pl.BlockSpec((tm, d), lambda i, j: (i, 0)),
            pl.BlockSpec((1, d), lambda i, j: (0, 0)),
            pl.BlockSpec((d, tn), lambda i, j: (0, j)),
            pl.BlockSpec((tm, RET_DK // 2), lambda i, j: (i % period_tiles, 0)),
            pl.BlockSpec((tm, RET_DK // 2), lambda i, j: (i % period_tiles, 0)),
        ],
        out_specs=pl.BlockSpec((tm, tn), lambda i, j: (i, j)),
        out_shape=jax.ShapeDtypeStruct((m, n), BF16),
        scratch_shapes=[pltpu.VMEM((tm, d), BF16)],
        compiler_params=_cparams(("arbitrary", "arbitrary")),
        name="ret_in_proj",
    )(x, g, w, cos, sin)


def _shift_rows(a, first_rows):
    rows, cols = a.shape
    n = first_rows.shape[0]
    slabs = rows // SUBLANES
    rot = pltpu.roll(a.reshape(slabs, SUBLANES, cols), n, axis=1)
    first = jnp.concatenate([first_rows, jnp.zeros((SUBLANES - n, cols), a.dtype)], axis=0)
    above = jnp.concatenate([first.reshape(1, SUBLANES, cols), rot[:slabs - 1]], axis=0)
    top = _iota((slabs, SUBLANES, cols), 1) < n
    return jnp.where(top, above, rot).reshape(rows, cols)


def _norm_proj_conv_kernel(x_ref, g_ref, w_ref, wdt_ref, cw_ref, cb_ref, o_ref, dt_ref, tail_ref,
                           xn_ref, carry_ref, *, tiles_per_seq, z_tiles, strip):
    i = pl.program_id(0)
    j = pl.program_id(1)
    tm, tn = o_ref.shape

    @pl.when(j == 0)
    def _():
        x = x_ref[...]
        ms = jnp.mean(x * x, axis=-1, keepdims=True)
        xn = (x * lax.rsqrt(ms + EPS) * g_ref[...]).astype(BF16)
        xn_ref[...] = xn
        dt_ref[...] = _dot(xn, wdt_ref[...])

    @pl.when((j == 0) & (i % tiles_per_seq == 0))
    def _():
        carry_ref[...] = jnp.zeros_like(carry_ref)

    @pl.when(j < z_tiles)
    def _():
        o_ref[...] = _dot_nt(xn_ref[...], w_ref[...].astype(BF16)).astype(o_ref.dtype)

    @pl.when(j >= z_tiles)
    def _():
        jc = j - z_tiles
        for s in range(tn // strip):
            sl = slice(s * strip, (s + 1) * strip)
            x = _dot_nt(xn_ref[...], w_ref[sl, :].astype(BF16))
            tail_ref[:, sl] = x[tm - SUBLANES:tm, :]
            x2 = _shift_rows(x, carry_ref[jc, 1:3, sl])
            p = x * cw_ref[2:3, sl] + x2 * cw_ref[0:1, sl]
            out = x * cw_ref[3:4, sl] + x2 * cw_ref[1:2, sl] + _shift_rows(p, carry_ref[jc, 0:1, sl]) + cb_ref[:, sl]
            o_ref[:, sl] = _silu(out).astype(o_ref.dtype)
            carry_ref[jc, 0:1, sl] = p[tm - 1:tm]
            carry_ref[jc, 1:3, sl] = x[tm - 2:tm]


def _norm_proj_conv(x, g, wt, w_dt, cw, cb, *, tm, tn, seq):
    m, d = x.shape
    n = SSM_DINNER + SSM_CONVDIM
    z_tiles = SSM_DINNER // tn
    c_tiles = SSM_CONVDIM // tn
    tiles_per_seq = seq // tm
    cj = lambda j: jnp.maximum(j - z_tiles, 0)
    return pl.pallas_call(
        functools.partial(_norm_proj_conv_kernel, tiles_per_seq=tiles_per_seq, z_tiles=z_tiles, strip=256),
        grid=(m // tm, n // tn),
        in_specs=[
            pl.BlockSpec((tm, d), lambda i, j: (i, 0)),
            pl.BlockSpec((1, d), lambda i, j: (0, 0)),
            pl.BlockSpec((tn, d), lambda i, j: (j, 0)),
            pl.BlockSpec((d, LANES), lambda i, j: (0, 0)),
            pl.BlockSpec((SSM_CONV, tn), lambda i, j: (0, cj(j))),
            pl.BlockSpec((1, tn), lambda i, j: (0, cj(j))),
        ],
        out_specs=[
            pl.BlockSpec((tm, tn), lambda i, j: (i, j)),
            pl.BlockSpec((tm, LANES), lambda i, j: (i, 0)),
            pl.BlockSpec((None, SUBLANES, tn), lambda i, j: (i, 0, cj(j))),
        ],
        out_shape=[
            jax.ShapeDtypeStruct((m, n), BF16),
            jax.ShapeDtypeStruct((m, LANES), F32),
            jax.ShapeDtypeStruct((m // tm, SUBLANES, SSM_CONVDIM), F32),
        ],
        scratch_shapes=[
            pltpu.VMEM((tm, d), BF16),
            pltpu.VMEM((c_tiles, SUBLANES, tn), F32),
        ],
        compiler_params=_cparams(("arbitrary", "arbitrary")),
        name="ssd_in_proj_conv",
    )(x, g, wt, w_dt, cw, cb)


def _out_proj_kernel(a_ref, w_ref, h_ref, *rest, has_norm):
    if has_norm:
        g_ref, o_ref = rest
    else:
        (o_ref,) = rest
    y = h_ref[...] + _dot(a_ref[...], w_ref[...])
    if has_norm:
        ms = jnp.mean(y * y, axis=-1, keepdims=True)
        y = y * lax.rsqrt(ms + EPS) * g_ref[...]
    o_ref[...] = y


def _out_proj(a, w, h, *, tm, final_g=None):
    m, k = a.shape
    d = w.shape[1]
    has_norm = final_g is not None
    in_specs = [
        pl.BlockSpec((tm, k), lambda i: (i, 0)),
        pl.BlockSpec((k, d), lambda i: (0, 0)),
        pl.BlockSpec((tm, d), lambda i: (i, 0)),
    ]
    args = [a, w, h]
    if has_norm:
        in_specs.append(pl.BlockSpec((1, d), lambda i: (0, 0)))
        args.append(final_g)
    return pl.pallas_call(
        functools.partial(_out_proj_kernel, has_norm=has_norm),
        grid=(m // tm,),
        in_specs=in_specs,
        out_specs=pl.BlockSpec((tm, d), lambda i: (i, 0)),
        out_shape=jax.ShapeDtypeStruct((m, d), F32),
        compiler_params=_cparams(("arbitrary",)),
        name="out_proj",
    )(*args)


def _head_norm_gate(o, hn, gate):
    ms = jnp.mean(o * o, axis=-1, keepdims=True)
    on = o * lax.rsqrt(ms + EPS) * hn
    return (on * _silu(gate.astype(F32))).astype(BF16)


def _ret_prompt_kernel(q_ref, k_ref, v_ref, gate_ref, lg_ref, hn_ref,
                       og_ref, st_ref, s_ref, dec_ref, qsc_ref, ksc_ref, gc_ref):
    b = pl.program_id(0)
    c = pl.program_id(1)
    n_seq, rows = q_ref.shape[0], q_ref.shape[1]

    @pl.when((b == 0) & (c == 0))
    def _():
        i = _iota((rows, rows), 0)
        j = _iota((rows, rows), 1)
        r = _iota((rows, RET_DK), 0)
        scale = RET_DK ** -0.5
        for h in range(RET_HEADS):
            lg = lg_ref[h, :, 0:1]
            dec_ref[h] = jnp.where(i >= j, jnp.exp((i - j).astype(F32) * lg), 0.0) * scale
            qsc_ref[h] = jnp.exp((r + 1).astype(F32) * lg)
            ksc_ref[h] = jnp.exp((rows - 1 - r).astype(F32) * lg) * scale
            gc_ref[h] = jnp.exp(jnp.full((1, RET_DV), float(rows), F32) * lg)

    @pl.when(c == 0)
    def _():
        s_ref[...] = jnp.zeros_like(s_ref)

    for n in range(n_seq):
        for h in range(RET_HEADS):
            ksl = slice(h * RET_DK, (h + 1) * RET_DK)
            vsl = slice(h * RET_DV, (h + 1) * RET_DV)
            q = q_ref[n, :, ksl]
            k = k_ref[n, :, ksl]
            v = v_ref[n, :, vsl]
            p = (_dot_nt(q, k) * dec_ref[h]).astype(BF16)
            qs = (q.astype(F32) * qsc_ref[h]).astype(BF16)
            kr = k.astype(F32)
            s = s_ref[n, h]
            o = _dot(p, v) + _dot(qs, s.astype(BF16))
            kt = (kr * ksc_ref[h]).T.astype(BF16)
            s_ref[n, h] = s * gc_ref[h] + _dot(kt, v)
            og_ref[n, :, vsl] = _head_norm_gate(o, hn_ref[:, vsl], gate_ref[n, :, vsl])

    @pl.when(c == pl.num_programs(1) - 1)
    def _():
        st_ref[...] = s_ref[...]


def _ret_prompt(proj, lg, hn, batch, seq):
    rows = RET_CHUNK
    nc = seq // rows
    ns = RET_PROMPT_SEQS
    return pl.pallas_call(
        _ret_prompt_kernel,
        grid=(batch // ns, nc),
        in_specs=[
            pl.BlockSpec((ns, rows, RET_QK), lambda b, c: (b, c, 0)),
            pl.BlockSpec((ns, rows, RET_QK), lambda b, c: (b, c, 1)),
            pl.BlockSpec((ns, rows, RET_VW), lambda b, c: (b, c, 2 * RET_QK // RET_VW)),
            pl.BlockSpec((ns, rows, RET_VW), lambda b, c: (b, c, 2 * RET_QK // RET_VW + 1)),
            pl.BlockSpec((RET_HEADS, 1, LANES), lambda b, c: (0, 0, 0)),
            pl.BlockSpec((1, RET_VW), lambda b, c: (0, 0)),
        ],
        out_specs=[
            pl.BlockSpec((ns, rows, RET_VW), lambda b, c: (b, c, 0)),
            pl.BlockSpec((None, ns, RET_HEADS, RET_DK, RET_DV), lambda b, c: (0, b, 0, 0, 0)),
        ],
        out_shape=[
            jax.ShapeDtypeStruct((batch, seq, RET_VW), BF16),
            jax.ShapeDtypeStruct((1, batch, RET_HEADS, RET_DK, RET_DV), F32),
        ],
        scratch_shapes=[
            pltpu.VMEM((ns, RET_HEADS, RET_DK, RET_DV), F32),
            pltpu.VMEM((RET_HEADS, rows, rows), F32),
            pltpu.VMEM((RET_HEADS, rows, RET_DK), F32),
            pltpu.VMEM((RET_HEADS, rows, RET_DK), F32),
            pltpu.VMEM((RET_HEADS, 1, RET_DV), F32),
        ],
        compiler_params=_cparams(("arbitrary", "arbitrary")),
        name="ret_prompt",
    )(proj, proj, proj, proj, lg, hn)


def _ret_sample_kernel(q_ref, k_ref, v_ref, gate_ref, lg_ref, hn_ref, st_in_ref,
                       og_ref, st_out_ref, *, seq):
    bb = st_in_ref.shape[0]
    rows = bb * seq
    lg = lg_ref[:, 0:1]
    scale = RET_DK ** -0.5
    i = _iota((rows, rows), 0)
    j = _iota((rows, rows), 1)
    same = (i // seq) == (j // seq)
    dec = jnp.where(same & (i >= j), jnp.exp((i - j).astype(F32) * lg), 0.0) * scale
    pos = _iota((rows, RET_DK), 0) % seq
    qsc = jnp.exp((pos + 1).astype(F32) * lg)
    ksc = jnp.exp((seq - 1 - pos).astype(F32) * lg) * scale
    gc = jnp.exp(jnp.full((1, RET_DV), float(seq), F32) * lg)

    qr = q_ref[...].astype(F32)
    kr = k_ref[...].astype(F32)
    v = v_ref[...].astype(F32)
    p = _dot_nt(q_ref[...], k_ref[...]) * dec
    intra = _dot(p.astype(BF16), v.astype(BF16))
    qs = qr * qsc
    kt = kr * ksc
    vrow = _iota((rows, RET_DV), 0) // seq
    cross = []
    for n in range(bb):
        s0 = st_in_ref[n]
        cross.append(_dot(qs[n * seq:(n + 1) * seq], s0))
        vn = jnp.where(vrow == n, v, 0.0)
        st_out_ref[n] = s0 * gc + _dot_tn(kt.astype(BF16), vn.astype(BF16))
    o = intra + jnp.concatenate(cross, axis=0)
    og_ref[...] = _head_norm_gate(o, hn_ref[...], gate_ref[...])


def _ret_sample(proj, lg, hn, state, batch, seq):
    bb = RET_SAMPLE_BB
    rows = bb * seq
    qb = RET_QK // RET_DK
    vb = 2 * RET_QK // RET_DV
    gb = vb + RET_VW // RET_DV
    st_spec = pl.BlockSpec((None, bb, None, RET_DK, RET_DV), lambda n, h: (0, n, h, 0, 0))
    return pl.pallas_call(
        functools.partial(_ret_sample_kernel, seq=seq),
        grid=(batch // bb, RET_HEADS),
        in_specs=[
            pl.BlockSpec((rows, RET_DK), lambda n, h: (n, h)),
            pl.BlockSpec((rows, RET_DK), lambda n, h: (n, qb + h)),
            pl.BlockSpec((rows, RET_DV), lambda n, h: (n, vb + h)),
            pl.BlockSpec((rows, RET_DV), lambda n, h: (n, gb + h)),
            pl.BlockSpec((None, 1, LANES), lambda n, h: (h, 0, 0)),
            pl.BlockSpec((1, RET_DV), lambda n, h: (0, h)),
            st_spec,
        ],
        out_specs=[
            pl.BlockSpec((rows, RET_DV), lambda n, h: (n, h)),
            st_spec,
        ],
        out_shape=[
            jax.ShapeDtypeStruct((batch * seq, RET_VW), BF16),
            jax.ShapeDtypeStruct(state.shape, F32),
        ],
        compiler_params=_cparams(("arbitrary", "arbitrary")),
        name="ret_sample",
    )(proj, proj, proj, proj, lg, hn, state)


def _head_expand(vals, expand):
    parts = []
    for v in vals:
        hi = v.astype(BF16)
        lo = (v - hi.astype(F32)).astype(BF16)
        parts.append(jnp.concatenate([hi, lo], axis=1))
    return _dot(jnp.concatenate(parts, axis=0), expand)


def _ssd_prompt_kernel(z_ref, xc_ref, bc_ref, dtr_ref, dtr_next_ref, dtb_ref, alog_ref, dsk_ref, gn_ref, exp_ref,
                       g_ref, st_ref, s_ref, ex_ref, cum2_ref, cum2t_ref):
    c = pl.program_id(1)
    n_seq, rows = z_ref.shape[0], z_ref.shape[1]
    bw = SSM_GROUP_W
    ri = _iota((rows, rows), 0)
    rj = _iota((rows, rows), 1)
    causal = ri >= rj

    def decay_tables(n, dtr):
        dt = _softplus(dtr + dtb_ref[...])
        la = dt * (-jnp.exp(alog_ref[...]))
        cum = jnp.dot(causal.astype(F32), la, precision=lax.Precision.HIGHEST, preferred_element_type=F32)
        ecum = jnp.exp(cum)
        wdt = jnp.exp(cum[rows - 1:rows, :] - cum) * dt
        ex_ref[n] = _head_expand([ecum, wdt, dt], exp_ref[...])
        cum2 = cum * math.log2(math.e)
        cum2_ref[n] = cum2
        cum2t_ref[n] = cum2.T

    @pl.when(c == 0)
    def _():
        s_ref[...] = jnp.zeros_like(s_ref)
        for n in range(n_seq):
            decay_tables(n, dtr_ref[n])

    head_lane = _iota((rows, bw), 1) // SSM_HEADDIM

    for n in range(n_seq):
        cum2 = cum2_ref[n]
        cum2_t = cum2t_ref[n]
        for g in range(SSM_GROUPS):
            gsl = slice(g * bw, (g + 1) * bw)
            bm = bc_ref[n, :, g * SSM_DSTATE:(g + 1) * SSM_DSTATE]
            cm = bc_ref[n, :, (SSM_GROUPS + g) * SSM_DSTATE:(SSM_GROUPS + g + 1) * SSM_DSTATE]
            xs = xc_ref[n, :, gsl].astype(F32)
            s = s_ref[n, g]
            gmat = _dot_nt(cm, bm)
            ps = []
            for hh in range(HEADS_PER_GROUP):
                h = g * HEADS_PER_GROUP + hh
                colb = jnp.broadcast_to(cum2[:, h:h + 1], (rows, LANES))
                dec = jnp.exp2(jnp.where(causal, colb - cum2_t[h:h + 1, :], -jnp.inf))
                ps.append((gmat * dec).astype(BF16))
            v = (xs * ex_ref[n, 2 * rows:3 * rows, gsl]).astype(BF16)
            y4 = _dot(jnp.concatenate(ps, axis=0), v)
            y = y4[0:rows]
            for hh in range(1, HEADS_PER_GROUP):
                y = jnp.where(head_lane == hh, y4[hh * rows:(hh + 1) * rows], y)
            y = y + _dot(cm, s.astype(BF16)) * ex_ref[n, 0:rows, gsl]
            vw = (xs * ex_ref[n, rows:2 * rows, gsl]).astype(BF16)
            s_ref[n, g] = s * ex_ref[n, rows - 1:rows, gsl] + _dot_tn(bm, vw)

            y = y + xs * dsk_ref[:, gsl]
            gg = y * _silu(z_ref[n, :, gsl].astype(F32))
            ms = jnp.mean(gg * gg, axis=-1, keepdims=True)
            g_ref[n, :, gsl] = (gg * lax.rsqrt(ms + EPS) * gn_ref[:, gsl]).astype(BF16)

        decay_tables(n, dtr_next_ref[n])

    @pl.when(c == pl.num_programs(1) - 1)
    def _():
        for n in range(n_seq):
            for g in range(SSM_GROUPS):
                st_ref[n, g * bw:(g + 1) * bw, :] = s_ref[n, g].T


def _ssd_prompt(zxc, dtr, dtb, alog, dsk, gn, expand, batch, seq):
    rows = SSD_CHUNK
    nc = seq // rows
    ns = SSD_PROMPT_SEQS
    full = lambda shape: pl.BlockSpec(shape, lambda b, c: (0,) * len(shape))
    col = lambda k: pl.BlockSpec((ns, rows, SSM_DINNER), lambda b, c: (b, c, k))
    return pl.pallas_call(
        _ssd_prompt_kernel,
        grid=(batch // ns, nc),
        in_specs=[
            col(0), col(1), col(2),
            pl.BlockSpec((ns, rows, LANES), lambda b, c: (b, c, 0)),
            pl.BlockSpec((ns, rows, LANES), lambda b, c: (b, jnp.minimum(c + 1, nc - 1), 0)),
            full((1, LANES)), full((1, LANES)),
            full((1, SSM_DINNER)), full((1, SSM_DINNER)),
            full((2 * LANES, SSM_DINNER)),
        ],
        out_specs=[
            pl.BlockSpec((ns, rows, SSM_DINNER), lambda b, c: (b, c, 0)),
            pl.BlockSpec((ns, SSM_DINNER, SSM_DSTATE), lambda b, c: (b, 0, 0)),
        ],
        out_shape=[
            jax.ShapeDtypeStruct((batch, seq, SSM_DINNER), BF16),
            jax.ShapeDtypeStruct((batch, SSM_DINNER, SSM_DSTATE), F32),
        ],
        scratch_shapes=[
            pltpu.VMEM((ns, SSM_GROUPS, SSM_DSTATE, SSM_GROUP_W), F32),
            pltpu.VMEM((ns, 3 * rows, SSM_DINNER), F32),
            pltpu.VMEM((ns, rows, LANES), F32),
            pltpu.VMEM((ns, LANES, rows), F32),
        ],
        compiler_params=_cparams(("arbitrary", "arbitrary")),
        name="ssd_prompt",
    )(zxc, zxc, zxc, dtr, dtr, dtb, alog, dsk, gn, expand)


def _ssd_prompt_ret_sample_kernel(*refs, seq_s):
    ssd_in, ret_in = refs[0:10], refs[10:17]
    ssd_out, ret_out = refs[17:19], refs[19:21]
    ssd_scratch = refs[21:25]
    _ssd_prompt_kernel(*ssd_in, *ssd_out, *ssd_scratch)
    _ret_sample_kernel(*ret_in, *ret_out, seq=seq_s)


def _ssd_prompt_ret_sample(zxc, dtr, dtb, alog, dsk, gn, expand, batch, seq,
                           proj_s, lg, hn, state_ret, batch_s, seq_s):
    rows = SSD_CHUNK
    nc = seq // rows
    ns = SSD_PROMPT_SEQS
    steps = (batch // ns) * nc
    assert steps % RET_HEADS == 0 and batch_s % (steps // RET_HEADS) == 0
    bb = batch_s // (steps // RET_HEADS)
    rows_s = bb * seq_s
    full = lambda shape: pl.BlockSpec(shape, lambda b, c: (0,) * len(shape))
    col = lambda k: pl.BlockSpec((ns, rows, SSM_DINNER), lambda b, c: (b, c, k))
    sn = lambda b, c: (b * nc + c) // RET_HEADS
    sh = lambda b, c: (b * nc + c) % RET_HEADS
    qb = RET_QK // RET_DK
    vb = 2 * RET_QK // RET_DV
    gb = vb + RET_VW // RET_DV
    st_spec = pl.BlockSpec((None, bb, None, RET_DK, RET_DV), lambda b, c: (0, sn(b, c), sh(b, c), 0, 0))
    return pl.pallas_call(
        functools.partial(_ssd_prompt_ret_sample_kernel, seq_s=seq_s),
        grid=(batch // ns, nc),
        in_specs=[
            col(0), col(1), col(2),
            pl.BlockSpec((ns, rows, LANES), lambda b, c: (b, c, 0)),
            pl.BlockSpec((ns, rows, LANES), lambda b, c: (b, jnp.minimum(c + 1, nc - 1), 0)),
            full((1, LANES)), full((1, LANES)),
            full((1, SSM_DINNER)), full((1, SSM_DINNER)),
            full((2 * LANES, SSM_DINNER)),
            pl.BlockSpec((rows_s, RET_DK), lambda b, c: (sn(b, c), sh(b, c))),
            pl.BlockSpec((rows_s, RET_DK), lambda b, c: (sn(b, c), qb + sh(b, c))),
            pl.BlockSpec((rows_s, RET_DV), lambda b, c: (sn(b, c), vb + sh(b, c))),
            pl.BlockSpec((rows_s, RET_DV), lambda b, c: (sn(b, c), gb + sh(b, c))),
            pl.BlockSpec((None, 1, LANES), lambda b, c: (sh(b, c), 0, 0)),
            pl.BlockSpec((1, RET_DV), lambda b, c: (0, sh(b, c))),
            st_spec,
        ],
        out_specs=[
            pl.BlockSpec((ns, rows, SSM_DINNER), lambda b, c: (b, c, 0)),
            pl.BlockSpec((ns, SSM_DINNER, SSM_DSTATE), lambda b, c: (b, 0, 0)),
            pl.BlockSpec((rows_s, RET_DV), lambda b, c: (sn(b, c), sh(b, c))),
            st_spec,
        ],
        out_shape=[
            jax.ShapeDtypeStruct((batch, seq, SSM_DINNER), BF16),
            jax.ShapeDtypeStruct((batch, SSM_DINNER, SSM_DSTATE), F32),
            jax.ShapeDtypeStruct((batch_s * seq_s, RET_VW), BF16),
            jax.ShapeDtypeStruct(state_ret.shape, F32),
        ],
        scratch_shapes=[
            pltpu.VMEM((ns, SSM_GROUPS, SSM_DSTATE, SSM_GROUP_W), F32),
            pltpu.VMEM((ns, 3 * rows, SSM_DINNER), F32),
            pltpu.VMEM((ns, rows, LANES), F32),
            pltpu.VMEM((ns, LANES, rows), F32),
        ],
        compiler_params=_cparams(("arbitrary", "arbitrary")),
        name="ssd_prompt_ret_sample",
    )(zxc, zxc, zxc, dtr, dtr, dtb, alog, dsk, gn, expand, proj_s, proj_s, proj_s, proj_s, lg, hn, state_ret)


def _ssd_sample_kernel(z_ref, xa_ref, xb_ref, dtr_ref, conv_in_ref, st_in_ref,
                       cw_ref, cb_ref, dtb_ref, alog_ref, dsk_ref, gn_ref, exp_ref,
                       g_ref, st_out_ref, xc_ref, ex_ref, y_ref, *, seq):
    bb = st_in_ref.shape[0]
    rows = bb * seq
    keep = SSM_CONV - 1
    bw = SSM_GROUP_W

    assert seq == SUBLANES
    sub = _iota((bb, SUBLANES, SSM_DINNER), 1)
    hist_pad = jnp.zeros((bb, SUBLANES - keep, SSM_DINNER), F32)
    for half, x_ref in enumerate((xa_ref, xb_ref)):
        csl = slice(half * SSM_DINNER, (half + 1) * SSM_DINNER)
        w0, w1, w2, w3 = (cw_ref[k:k + 1, csl] for k in range(SSM_CONV))
        x = x_ref[...].reshape(bb, SUBLANES, SSM_DINNER)
        hist = jnp.concatenate([hist_pad, conv_in_ref[:, :, csl]], axis=1)
        hist2 = pltpu.roll(hist, 2, axis=1)
        x2 = jnp.where(sub < 2, hist2, pltpu.roll(x, 2, axis=1))
        p = x * w2 + x2 * w0
        p_hist = hist * w2 + hist2 * w0
        p1 = jnp.where(sub < 1, pltpu.roll(p_hist, 1, axis=1), pltpu.roll(p, 1, axis=1))
        out = x * w3 + x2 * w1 + p1 + cb_ref[:, csl]
        xc_ref[:, csl] = _silu(out).reshape(rows, SSM_DINNER)

    dt = _softplus(dtr_ref[...] + dtb_ref[...])
    la = dt * (-jnp.exp(alog_ref[...]))
    ri = _iota((rows, rows), 0)
    rj = _iota((rows, rows), 1)
    same = (ri // seq) == (rj // seq)
    cum = jnp.dot((same & (ri >= rj)).astype(F32), la, precision=lax.Precision.HIGHEST,
                  preferred_element_type=F32)
    clast = jnp.dot(same.astype(F32), la, precision=lax.Precision.HIGHEST, preferred_element_type=F32)
    eclast = jnp.exp(clast)
    ex_ref[...] = _head_expand([jnp.exp(cum), jnp.exp(clast - cum) * dt, dt], exp_ref[...])

    si = _iota((seq, seq), 0)
    sj = _iota((seq, seq), 1)
    causal = si >= sj
    eye = si == sj
    head_lane = _iota((seq, bw), 1) // SSM_HEADDIM

    for n in range(bb):
        r0 = n * seq
        rsl = slice(r0, r0 + seq)
        cum_n = cum[rsl]
        for g in range(SSM_GROUPS):
            gsl = slice(g * bw, (g + 1) * bw)
            bm = xc_ref[rsl, SSM_DINNER + g * SSM_DSTATE:SSM_DINNER + (g + 1) * SSM_DSTATE]
            cm = xc_ref[rsl, SSM_DINNER + (SSM_GROUPS + g) * SSM_DSTATE:
                        SSM_DINNER + (SSM_GROUPS + g + 1) * SSM_DSTATE]
            xs = xc_ref[rsl, gsl]
            gmat = _dot_nt(cm.astype(BF16), bm.astype(BF16))
            ps = []
            for hh in range(HEADS_PER_GROUP):
                h = g * HEADS_PER_GROUP + hh
                ccol = cum_n[:, h:h + 1]
                crow = jnp.sum(jnp.where(eye, ccol, 0.0), axis=0, keepdims=True)
                ps.append(gmat * jnp.exp(jnp.where(causal, ccol - crow, -jnp.inf)))
            v = xs * ex_ref[2 * rows + r0:2 * rows + r0 + seq, gsl]
            y4 = _dot(jnp.concatenate(ps, axis=0).astype(BF16), v.astype(BF16))
            y = y4[0:seq]
            for hh in range(1, HEADS_PER_GROUP):
                y = jnp.where(head_lane == hh, y4[hh * seq:(hh + 1) * seq], y)
            st = st_in_ref[n, gsl, :]
            y = y + _dot_nt(cm, st) * ex_ref[rsl, gsl]
            y_ref[rsl, gsl] = y
            vw = xs * ex_ref[rows + r0:rows + r0 + seq, gsl]
            upd = _dot_tn(vw.astype(BF16), bm.astype(BF16))
            for hh in range(HEADS_PER_GROUP):
                h = g * HEADS_PER_GROUP + hh
                hsl = slice(hh * SSM_HEADDIM, (hh + 1) * SSM_HEADDIM)
                osl = slice(g * bw + hh * SSM_HEADDIM, g * bw + (hh + 1) * SSM_HEADDIM)
                st_out_ref[n, osl, :] = st[hsl] * eclast[r0:r0 + 1, h:h + 1] + upd[hsl]

    y = y_ref[...] + xc_ref[:, 0:SSM_DINNER] * dsk_ref[...]
    gg = y * _silu(z_ref[...])
    for g in range(SSM_GROUPS):
        gsl = slice(g * bw, (g + 1) * bw)
        blk = gg[:, gsl]
        ms = jnp.mean(blk * blk, axis=-1, keepdims=True)
        g_ref[:, gsl] = (blk * lax.rsqrt(ms + EPS) * gn_ref[:, gsl]).astype(BF16)


def _ssd_sample(zx, dtr, conv_state, ssm_state_t, cw, cb, dtb, alog, dsk, gn, expand, batch, seq):
    bb = SSD_SAMPLE_BB
    rows = bb * seq
    keep = SSM_CONV - 1
    full = lambda shape: pl.BlockSpec(shape, lambda n: (0,) * len(shape))
    col = lambda k: pl.BlockSpec((rows, SSM_DINNER), lambda n: (n, k))
    st_spec = pl.BlockSpec((bb, SSM_DINNER, SSM_DSTATE), lambda n: (n, 0, 0))
    return pl.pallas_call(
        functools.partial(_ssd_sample_kernel, seq=seq),
        grid=(batch // bb,),
        in_specs=[
            col(0), col(1), col(2),
            pl.BlockSpec((rows, LANES), lambda n: (n, 0)),
            pl.BlockSpec((None, bb, keep, SSM_CONVDIM), lambda n: (0, n, 0, 0)),
            st_spec,
            full((SSM_CONV, SSM_CONVDIM)), full((1, SSM_CONVDIM)),
            full((1, LANES)), full((1, LANES)),
            full((1, SSM_DINNER)), full((1, SSM_DINNER)),
            full((2 * LANES, SSM_DINNER)),
        ],
        out_specs=[
            pl.BlockSpec((rows, SSM_DINNER), lambda n: (n, 0)),
            st_spec,
        ],
        out_shape=[
            jax.ShapeDtypeStruct((batch * seq, SSM_DINNER), BF16),
            jax.ShapeDtypeStruct(ssm_state_t.shape, F32),
        ],
        scratch_shapes=[
            pltpu.VMEM((rows, SSM_CONVDIM), F32),
            pltpu.VMEM((3 * rows, SSM_DINNER), F32),
            pltpu.VMEM((rows, SSM_DINNER), F32),
        ],
        compiler_params=_cparams(("arbitrary",)),
        name="ssd_sample",
    )(zx, zx, zx, dtr, conv_state, ssm_state_t, cw, cb, dtb, alog, dsk, gn, expand)


def _rope_tables(pos):
    half = RET_DK // 2
    freqs = ROPE_BASE ** (-jnp.arange(half, dtype=F32) / half)
    ang = pos.astype(F32)[:, None] * freqs[None, :]
    return jnp.cos(ang), jnp.sin(ang)


def _row(v):
    return v.reshape(1, -1).astype(F32)


def _pad_lanes(v):
    return jnp.pad(v.astype(F32), (0, LANES - v.shape[0])).reshape(1, LANES)


def kernel(x_prompt, x_sample, state_ret, state_ssm, state_conv, ret_norm, ret_w_in, ret_head_norm, ret_w_out, ssm_norm, ssm_w_in, ssm_conv_w, ssm_conv_b, ssm_dt_bias, ssm_a_log, ssm_d, ssm_gate_norm, ssm_w_out, final_norm):
    bp, lp, d = x_prompt.shape
    bs, ls, _ = x_sample.shape
    assert ret_norm.shape[0] == 1 and ssm_norm.shape[0] == 1, "one retention and one SSD layer"

    ret_win = ret_w_in[0]
    ret_wout = ret_w_out[0].astype(BF16)
    n_main = SSM_DINNER + SSM_CONVDIM
    ssm_win_t = jnp.swapaxes(ssm_w_in[0], 0, 1)
    ssm_wdt = jnp.pad(ssm_w_in[0, :, n_main:], ((0, 0), (0, LANES - SSM_HEADS))).astype(BF16)
    ssm_wout = ssm_w_out[0].astype(BF16)
    expand = (jnp.arange(2 * LANES)[:, None] % LANES == jnp.arange(SSM_DINNER)[None, :] // SSM_HEADDIM).astype(BF16)
    lg = jnp.log(1.0 - 2.0 ** (-5.0 - jnp.arange(RET_HEADS, dtype=F32)))
    lg = jnp.broadcast_to(lg[:, None, None], (RET_HEADS, 1, LANES))
    hn = _row(ret_head_norm[0])
    dsk = _row(jnp.repeat(ssm_d[0], SSM_HEADDIM))
    gn = _row(ssm_gate_norm[0])
    cw = ssm_conv_w[0].astype(F32)
    cb = _row(ssm_conv_b[0])
    dtb = _pad_lanes(ssm_dt_bias[0])
    alog = _pad_lanes(ssm_a_log[0])
    g_ret = _row(ret_norm[0])
    g_ssm = _row(ssm_norm[0])
    g_fin = _row(final_norm)

    xp = x_prompt.reshape(bp * lp, d)
    cos_p, sin_p = _rope_tables(jnp.arange(lp, dtype=jnp.int32))
    n_ret = 2 * RET_QK + 2 * RET_VW
    proj = _ret_proj(xp, g_ret, ret_win, cos_p, sin_p, tm=lp, tn=1024)
    og, ret_p = _ret_prompt(proj.reshape(bp, lp, n_ret), lg, hn, bp, lp)
    h1 = _out_proj(og.reshape(bp * lp, RET_VW), ret_wout, xp, tm=1024)
    tm_c = 1024
    zxc, dtr, tail = _norm_proj_conv(h1, g_ssm, ssm_win_t, ssm_wdt, cw, cb, tm=tm_c, tn=2048, seq=lp)
    tail = tail[lp // tm_c - 1::lp // tm_c]
    xs = x_sample.reshape(bs * ls, d)
    cos_s, sin_s = _rope_tables(PAST_LEN + jnp.arange(ls, dtype=jnp.int32))
    proj_s = _ret_proj(xs, g_ret, ret_win, jnp.tile(cos_s, (bs, 1)), jnp.tile(sin_s, (bs, 1)), tm=bs * ls, tn=1024)

    gated, ssm_pt, og_s, ret_s = _ssd_prompt_ret_sample(
        zxc.reshape(bp, lp, -1), dtr.reshape(bp, lp, LANES), dtb, alog, dsk, gn, expand, bp, lp,
        proj_s, lg, hn, state_ret, bs, ls)
    y_p = _out_proj(gated.reshape(bp * lp, SSM_DINNER), ssm_wout, h1, tm=1024, final_g=g_fin)
    ssm_p = jnp.swapaxes(ssm_pt.reshape(1, bp, SSM_HEADS, SSM_HEADDIM, SSM_DSTATE), -1, -2)
    conv_p = tail[None, :, SUBLANES - (SSM_CONV - 1):, :]

    h1_s = _out_proj(og_s, ret_wout, xs, tm=512)
    zx_s, dtr_s = _norm_proj(h1_s, g_ssm, ssm_win_t, n=n_main, tm=bs * ls, tn=2048, out_dtype=F32, w_dt=ssm_wdt,
                             w_is_t=True)
    st_t = jnp.swapaxes(state_ssm[0], -1, -2).reshape(bs, SSM_DINNER, SSM_DSTATE)
    gated_s, ssm_st = _ssd_sample(zx_s, dtr_s, state_conv, st_t, cw, cb, dtb, alog, dsk, gn, expand, bs, ls)
    ssm_s = jnp.swapaxes(ssm_st.reshape(1, bs, SSM_HEADS, SSM_HEADDIM, SSM_DSTATE), -1, -2)
    keep = SSM_CONV - 1
    conv_s = zx_s.reshape(bs, ls, -1)[None, :, ls - keep:, SSM_DINNER:]
    y_s = _out_proj(gated_s, ssm_wout, h1_s, tm=512, final_g=g_fin)

    return (y_p.reshape(bp, lp, d), y_s.reshape(bs, ls, d), ret_p, ret_s, ssm_p, ssm_s, conv_p, conv_s)
```

```python
import functools
import math

import jax
import jax.numpy as jnp
from jax import lax
from jax.experimental import pallas as pl
from jax.experimental.pallas import tpu as pltpu

F32 = jnp.float32
BF16 = jnp.bfloat16

D_MODEL = 1024
RET_HEADS = 4
RET_DK = 256
RET_DV = 512
RET_VW = RET_HEADS * RET_DV
RET_QK = RET_HEADS * RET_DK
SSM_DINNER = 2048
SSM_HEADDIM = 64
SSM_HEADS = 32
SSM_GROUPS = 8
SSM_DSTATE = 128
SSM_CONV = 4
SSM_CONVDIM = 4096
SSM_GROUP_W = SSM_DINNER // SSM_GROUPS
HEADS_PER_GROUP = SSM_HEADS // SSM_GROUPS
ROPE_BASE = 10000.0
EPS = 1e-6
PAST_LEN = 16384

LANES = 128
SUBLANES = 8
VMEM_LIMIT = 56 * 1024 * 1024

RET_CHUNK = 256
RET_PROMPT_SEQS = 2
SSD_CHUNK = 128
SSD_PROMPT_SEQS = 2
RET_SAMPLE_BB = 16
SSD_SAMPLE_BB = 8


def _cparams(sem, flags=None):
    return pltpu.CompilerParams(dimension_semantics=sem, vmem_limit_bytes=VMEM_LIMIT, flags=flags)


def _sigmoid(x):
    return 1.0 / (1.0 + jnp.exp2(x * (-math.log2(math.e))))


def _silu(x):
    return x * _sigmoid(x)


def _softplus(x):
    return jnp.maximum(x, 0.0) + jnp.log1p(jnp.exp(-jnp.abs(x)))


def _dot(a, b):
    return jnp.dot(a, b, preferred_element_type=F32)


def _dot_nt(a, b):
    return lax.dot_general(a, b, (((1,), (1,)), ((), ())), preferred_element_type=F32)


def _dot_tn(a, b):
    return lax.dot_general(a, b, (((0,), (0,)), ((), ())), preferred_element_type=F32)


def _iota(shape, dim):
    return lax.broadcasted_iota(jnp.int32, shape, dim)


def _norm_proj_kernel(x_ref, g_ref, w_ref, *rest, has_dt, w_is_t):
    if has_dt:
        wdt_ref, o_ref, dt_ref, xn_ref = rest
    else:
        o_ref, xn_ref = rest

    @pl.when(pl.program_id(1) == 0)
    def _():
        x = x_ref[...]
        ms = jnp.mean(x * x, axis=-1, keepdims=True)
        xn = (x * lax.rsqrt(ms + EPS) * g_ref[...]).astype(BF16)
        xn_ref[...] = xn
        if has_dt:
            dt_ref[...] = _dot(xn, wdt_ref[...])

    w = w_ref[...].astype(BF16)
    o_ref[...] = (_dot_nt(xn_ref[...], w) if w_is_t else _dot(xn_ref[...], w)).astype(o_ref.dtype)


def _norm_proj(x, g, w, *, n, tm, tn, out_dtype, w_dt=None, w_is_t=False):
    m, d = x.shape
    has_dt = w_dt is not None
    in_specs = [
        pl.BlockSpec((tm, d), lambda i, j: (i, 0)),
        pl.BlockSpec((1, d), lambda i, j: (0, 0)),
        pl.BlockSpec((tn, d), lambda i, j: (j, 0)) if w_is_t else pl.BlockSpec((d, tn), lambda i, j: (0, j)),
    ]
    args = [x, g, w]
    out_shape = [jax.ShapeDtypeStruct((m, n), out_dtype)]
    out_specs = [pl.BlockSpec((tm, tn), lambda i, j: (i, j))]
    if has_dt:
        in_specs.append(pl.BlockSpec((d, LANES), lambda i, j: (0, 0)))
        args.append(w_dt)
        out_shape.append(jax.ShapeDtypeStruct((m, LANES), F32))
        out_specs.append(pl.BlockSpec((tm, LANES), lambda i, j: (i, 0)))
    res = pl.pallas_call(
        functools.partial(_norm_proj_kernel, has_dt=has_dt, w_is_t=w_is_t),
        grid=(m // tm, n // tn),
        in_specs=in_specs,
        out_specs=out_specs,
        out_shape=out_shape,
        scratch_shapes=[pltpu.VMEM((tm, d), BF16)],
        compiler_params=_cparams(("arbitrary", "arbitrary")),
        name="in_proj",
    )(*args)
    return res if has_dt else res[0]


def _rope(x, cos, sin):
    half = RET_DK // 2
    x1 = x[:, :half]
    x2 = x[:, half:]
    return jnp.concatenate([x1 * cos - x2 * sin, x1 * sin + x2 * cos], axis=1)


def _ret_proj_kernel(x_ref, g_ref, w_ref, cos_ref, sin_ref, o_ref, xn_ref, *, rope_tiles):
    j = pl.program_id(1)
    tn = o_ref.shape[1]

    @pl.when(j == 0)
    def _():
        x = x_ref[...]
        ms = jnp.mean(x * x, axis=-1, keepdims=True)
        xn_ref[...] = (x * lax.rsqrt(ms + EPS) * g_ref[...]).astype(BF16)

    @pl.when(j < rope_tiles)
    def _():
        cos = cos_ref[...]
        sin = sin_ref[...]
        for h in range(tn // RET_DK):
            sl = slice(h * RET_DK, (h + 1) * RET_DK)
            y = _dot(xn_ref[...], w_ref[:, sl].astype(BF16))
            o_ref[:, sl] = _rope(y, cos, sin).astype(o_ref.dtype)

    @pl.when(j >= rope_tiles)
    def _():
        o_ref[...] = _dot(xn_ref[...], w_ref[...].astype(BF16)).astype(o_ref.dtype)


def _ret_proj(x, g, w, cos, sin, *, tm, tn):
    m, d = x.shape
    n = w.shape[1]
    period_tiles = max(cos.shape[0] // tm, 1)
    return pl.pallas_call(
        functools.partial(_ret_proj_kernel, rope_tiles=2 * RET_QK // tn),
        grid=(m // tm, n // tn),
        in_specs=[
            pl.BlockSpec((tm, d), lambda i, j: (i, 0)),
            pl.BlockSpec((1, d), lambda i, j: (0, 0)),
            pl.BlockSpec((d, tn), lambda i, j: (0, j)),
            pl.BlockSpec((tm, RET_DK // 2), lambda i, j: (i % period_tiles, 0)),
            pl.BlockSpec((tm, RET_DK // 2), lambda i, j: (i % period_tiles, 0)),
        ],
        out_specs=pl.BlockSpec((tm, tn), lambda i, j: (i, j)),
        out_shape=jax.ShapeDtypeStruct((m, n), BF16),
        scratch_shapes=[pltpu.VMEM((tm, d), BF16)],
        compiler_params=_cparams(("arbitrary", "arbitrary")),
        name="ret_in_proj",
    )(x, g, w, cos, sin)


def _shift_rows(a, first_rows):
    rows, cols = a.shape
    n = first_rows.shape[0]
    slabs = rows // SUBLANES
    a3 = a.reshape(slabs, SUBLANES, cols)
    first = jnp.concatenate([jnp.zeros((SUBLANES - n, cols), a.dtype), first_rows], axis=0)
    above = jnp.concatenate([first.reshape(1, SUBLANES, cols), a3[:slabs - 1]], axis=0)
    bottom = _iota((slabs, SUBLANES, cols), 1) >= SUBLANES - n
    return pltpu.roll(jnp.where(bottom, above, a3), n, axis=1).reshape(rows, cols)


def _norm_proj_conv_kernel(x_ref, g_ref, w_ref, wdt_ref, cw_ref, cb_ref, o_ref, dt_ref, tail_ref,
                           xn_ref, carry_ref, *, tiles_per_seq, z_tiles, strip):
    i = pl.program_id(0)
    j = pl.program_id(1)
    tm, tn = o_ref.shape

    @pl.when(j == 0)
    def _():
        x = x_ref[...]
        ms = jnp.mean(x * x, axis=-1, keepdims=True)
        xn = (x * lax.rsqrt(ms + EPS) * g_ref[...]).astype(BF16)
        xn_ref[...] = xn
        dt_ref[...] = _dot(xn, wdt_ref[...])

    @pl.when((j == 0) & (i % tiles_per_seq == 0))
    def _():
        carry_ref[...] = jnp.zeros_like(carry_ref)

    @pl.when(j < z_tiles)
    def _():
        o_ref[...] = _dot_nt(xn_ref[...], w_ref[...].astype(BF16)).astype(o_ref.dtype)

    @pl.when(j >= z_tiles)
    def _():
        jc = j - z_tiles
        for s in range(tn // strip):
            sl = slice(s * strip, (s + 1) * strip)
            x = _dot_nt(xn_ref[...], w_ref[sl, :].astype(BF16))
            tail_ref[:, sl] = x[tm - SUBLANES:tm, :]
            x2 = _shift_rows(x, carry_ref[jc, 1:3, sl])
            p = x * cw_ref[2:3, sl] + x2 * cw_ref[0:1, sl]
            out = x * cw_ref[3:4, sl] + x2 * cw_ref[1:2, sl] + _shift_rows(p, carry_ref[jc, 0:1, sl]) + cb_ref[:, sl]
            o_ref[:, sl] = _silu(out).astype(o_ref.dtype)
            carry_ref[jc, 0:1, sl] = p[tm - 1:tm]
            carry_ref[jc, 1:3, sl] = x[tm - 2:tm]


def _norm_proj_conv(x, g, wt, w_dt, cw, cb, *, tm, tn, seq):
    m, d = x.shape
    n = SSM_DINNER + SSM_CONVDIM
    z_tiles = SSM_DINNER // tn
    c_tiles = SSM_CONVDIM // tn
    tiles_per_seq = seq // tm
    cj = lambda j: jnp.maximum(j - z_tiles, 0)
    return pl.pallas_call(
        functools.partial(_norm_proj_conv_kernel, tiles_per_seq=tiles_per_seq, z_tiles=z_tiles, strip=256),
        grid=(m // tm, n // tn),
        in_specs=[
            pl.BlockSpec((tm, d), lambda i, j: (i, 0)),
            pl.BlockSpec((1, d), lambda i, j: (0, 0)),
            pl.BlockSpec((tn, d), lambda i, j: (j, 0)),
            pl.BlockSpec((d, LANES), lambda i, j: (0, 0)),
            pl.BlockSpec((SSM_CONV, tn), lambda i, j: (0, cj(j))),
            pl.BlockSpec((1, tn), lambda i, j: (0, cj(j))),
        ],
        out_specs=[
            pl.BlockSpec((tm, tn), lambda i, j: (i, j)),
            pl.BlockSpec((tm, LANES), lambda i, j: (i, 0)),
            pl.BlockSpec((None, SUBLANES, tn), lambda i, j: (i, 0, cj(j))),
        ],
        out_shape=[
            jax.ShapeDtypeStruct((m, n), BF16),
            jax.ShapeDtypeStruct((m, LANES), F32),
            jax.ShapeDtypeStruct((m // tm, SUBLANES, SSM_CONVDIM), F32),
        ],
        scratch_shapes=[
            pltpu.VMEM((tm, d), BF16),
            pltpu.VMEM((c_tiles, SUBLANES, tn), F32),
        ],
        compiler_params=_cparams(("arbitrary", "arbitrary")),
        name="ssd_in_proj_conv",
    )(x, g, wt, w_dt, cw, cb)


def _out_proj_kernel(a_ref, w_ref, h_ref, *rest, has_norm):
    if has_norm:
        g_ref, o_ref = rest
    else:
        (o_ref,) = rest
    y = h_ref[...] + _dot(a_ref[...], w_ref[...])
    if has_norm:
        ms = jnp.mean(y * y, axis=-1, keepdims=True)
        y = y * lax.rsqrt(ms + EPS) * g_ref[...]
    o_ref[...] = y


def _out_proj(a, w, h, *, tm, final_g=None):
    m, k = a.shape
    d = w.shape[1]
    has_norm = final_g is not None
    in_specs = [
        pl.BlockSpec((tm, k), lambda i: (i, 0)),
        pl.BlockSpec((k, d), lambda i: (0, 0)),
        pl.BlockSpec((tm, d), lambda i: (i, 0)),
    ]
    args = [a, w, h]
    if has_norm:
        in_specs.append(pl.BlockSpec((1, d), lambda i: (0, 0)))
        args.append(final_g)
    return pl.pallas_call(
        functools.partial(_out_proj_kernel, has_norm=has_norm),
        grid=(m // tm,),
        in_specs=in_specs,
        out_specs=pl.BlockSpec((tm, d), lambda i: (i, 0)),
        out_shape=jax.ShapeDtypeStruct((m, d), F32),
        compiler_params=_cparams(("arbitrary",)),
        name="out_proj",
    )(*args)


def _head_norm_gate(o, hn, gate):
    ms = jnp.mean(o * o, axis=-1, keepdims=True)
    on = o * lax.rsqrt(ms + EPS) * hn
    return (on * _silu(gate.astype(F32))).astype(BF16)


def _ret_prompt_kernel(q_ref, k_ref, v_ref, gate_ref, lg_ref, hn_ref,
                       og_ref, st_ref, s_ref, dec_ref, qsc_ref, ksc_ref, gc_ref):
    b = pl.program_id(0)
    c = pl.program_id(1)
    n_seq, rows = q_ref.shape[0], q_ref.shape[1]

    @pl.when((b == 0) & (c == 0))
    def _():
        i = _iota((rows, rows), 0)
        j = _iota((rows, rows), 1)
        r = _iota((rows, RET_DK), 0)
        scale = RET_DK ** -0.5
        for h in range(RET_HEADS):
            lg = lg_ref[h, :, 0:1]
            dec_ref[h] = jnp.where(i >= j, jnp.exp((i - j).astype(F32) * lg), 0.0) * scale
            qsc_ref[h] = jnp.exp((r + 1).astype(F32) * lg)
            ksc_ref[h] = jnp.exp((rows - 1 - r).astype(F32) * lg) * scale
            gc_ref[h] = jnp.exp(jnp.full((1, RET_DV), float(rows), F32) * lg)

    @pl.when(c == 0)
    def _():
        s_ref[...] = jnp.zeros_like(s_ref)

    for n in range(n_seq):
        for h in range(RET_HEADS):
            ksl = slice(h * RET_DK, (h + 1) * RET_DK)
            vsl = slice(h * RET_DV, (h + 1) * RET_DV)
            q = q_ref[n, :, ksl]
            k = k_ref[n, :, ksl]
            v = v_ref[n, :, vsl]
            p = (_dot_nt(q, k) * dec_ref[h]).astype(BF16)
            qs = (q.astype(F32) * qsc_ref[h]).astype(BF16)
            kr = k.astype(F32)
            s = s_ref[n, h]
            o = _dot(p, v) + _dot(qs, s.astype(BF16))
            kt = (kr * ksc_ref[h]).T.astype(BF16)
            s_ref[n, h] = s * gc_ref[h] + _dot(kt, v)
            og_ref[n, :, vsl] = _head_norm_gate(o, hn_ref[:, vsl], gate_ref[n, :, vsl])

    @pl.when(c == pl.num_programs(1) - 1)
    def _():
        st_ref[...] = s_ref[...]


def _ret_prompt(proj, lg, hn, batch, seq):
    rows = RET_CHUNK
    nc = seq // rows
    ns = RET_PROMPT_SEQS
    return pl.pallas_call(
        _ret_prompt_kernel,
        grid=(batch // ns, nc),
        in_specs=[
            pl.BlockSpec((ns, rows, RET_QK), lambda b, c: (b, c, 0)),
            pl.BlockSpec((ns, rows, RET_QK), lambda b, c: (b, c, 1)),
            pl.BlockSpec((ns, rows, RET_VW), lambda b, c: (b, c, 2 * RET_QK // RET_VW)),
            pl.BlockSpec((ns, rows, RET_VW), lambda b, c: (b, c, 2 * RET_QK // RET_VW + 1)),
            pl.BlockSpec((RET_HEADS, 1, LANES), lambda b, c: (0, 0, 0)),
            pl.BlockSpec((1, RET_VW), lambda b, c: (0, 0)),
        ],
        out_specs=[
            pl.BlockSpec((ns, rows, RET_VW), lambda b, c: (b, c, 0)),
            pl.BlockSpec((None, ns, RET_HEADS, RET_DK, RET_DV), lambda b, c: (0, b, 0, 0, 0)),
        ],
        out_shape=[
            jax.ShapeDtypeStruct((batch, seq, RET_VW), BF16),
            jax.ShapeDtypeStruct((1, batch, RET_HEADS, RET_DK, RET_DV), F32),
        ],
        scratch_shapes=[
            pltpu.VMEM((ns, RET_HEADS, RET_DK, RET_DV), F32),
            pltpu.VMEM((RET_HEADS, rows, rows), F32),
            pltpu.VMEM((RET_HEADS, rows, RET_DK), F32),
            pltpu.VMEM((RET_HEADS, rows, RET_DK), F32),
            pltpu.VMEM((RET_HEADS, 1, RET_DV), F32),
        ],
        compiler_params=_cparams(("arbitrary", "arbitrary")),
        name="ret_prompt",
    )(proj, proj, proj, proj, lg, hn)


def _ret_sample_kernel(q_ref, k_ref, v_ref, gate_ref, lg_ref, hn_ref, st_in_ref,
                       og_ref, st_out_ref, *, seq):
    bb = st_in_ref.shape[0]
    rows = bb * seq
    lg = lg_ref[:, 0:1]
    scale = RET_DK ** -0.5
    i = _iota((rows, rows), 0)
    j = _iota((rows, rows), 1)
    same = (i // seq) == (j // seq)
    dec = jnp.where(same & (i >= j), jnp.exp((i - j).astype(F32) * lg), 0.0) * scale
    pos = _iota((rows, RET_DK), 0) % seq
    qsc = jnp.exp((pos + 1).astype(F32) * lg)
    ksc = jnp.exp((seq - 1 - pos).astype(F32) * lg) * scale
    gc = jnp.exp(jnp.full((1, RET_DV), float(seq), F32) * lg)

    qr = q_ref[...].astype(F32)
    kr = k_ref[...].astype(F32)
    v = v_ref[...].astype(F32)
    p = _dot_nt(q_ref[...], k_ref[...]) * dec
    intra = _dot(p.astype(BF16), v.astype(BF16))
    qs = qr * qsc
    kt = kr * ksc
    vrow = _iota((rows, RET_DV), 0) // seq
    cross = []
    for n in range(bb):
        s0 = st_in_ref[n]
        cross.append(_dot(qs[n * seq:(n + 1) * seq], s0))
        vn = jnp.where(vrow == n, v, 0.0)
        st_out_ref[n] = s0 * gc + _dot_tn(kt.astype(BF16), vn.astype(BF16))
    o = intra + jnp.concatenate(cross, axis=0)
    og_ref[...] = _head_norm_gate(o, hn_ref[...], gate_ref[...])


def _ret_sample(proj, lg, hn, state, batch, seq):
    bb = RET_SAMPLE_BB
    rows = bb * seq
    qb = RET_QK // RET_DK
    vb = 2 * RET_QK // RET_DV
    gb = vb + RET_VW // RET_DV
    st_spec = pl.BlockSpec((None, bb, None, RET_DK, RET_DV), lambda n, h: (0, n, h, 0, 0))
    return pl.pallas_call(
        functools.partial(_ret_sample_kernel, seq=seq),
        grid=(batch // bb, RET_HEADS),
        in_specs=[
            pl.BlockSpec((rows, RET_DK), lambda n, h: (n, h)),
            pl.BlockSpec((rows, RET_DK), lambda n, h: (n, qb + h)),
            pl.BlockSpec((rows, RET_DV), lambda n, h: (n, vb + h)),
            pl.BlockSpec((rows, RET_DV), lambda n, h: (n, gb + h)),
            pl.BlockSpec((None, 1, LANES), lambda n, h: (h, 0, 0)),
            pl.BlockSpec((1, RET_DV), lambda n, h: (0, h)),
            st_spec,
        ],
        out_specs=[
            pl.BlockSpec((rows, RET_DV), lambda n, h: (n, h)),
            st_spec,
        ],
        out_shape=[
            jax.ShapeDtypeStruct((batch * seq, RET_VW), BF16),
            jax.ShapeDtypeStruct(state.shape, F32),
        ],
        compiler_params=_cparams(("arbitrary", "arbitrary")),
        name="ret_sample",
    )(proj, proj, proj, proj, lg, hn, state)


def _head_expand(vals, expand):
    parts = []
    for v in vals:
        hi = v.astype(BF16)
        lo = (v - hi.astype(F32)).astype(BF16)
        parts.append(jnp.concatenate([hi, lo], axis=1))
    return _dot(jnp.concatenate(parts, axis=0), expand)


def _ssd_prompt_kernel(z_ref, xc_ref, bc_ref, dtr_ref, dtr_next_ref, dtb_ref, alog_ref, dsk_ref, gn_ref, exp_ref,
                       g_ref, st_ref, s_ref, ex_ref, cum2_ref, cum2t_ref):
    c = pl.program_id(1)
    n_seq, rows = z_ref.shape[0], z_ref.shape[1]
    bw = SSM_GROUP_W
    ri = _iota((rows, rows), 0)
    rj = _iota((rows, rows), 1)
    causal = ri >= rj

    def decay_tables(n, dtr):
        dt = _softplus(dtr + dtb_ref[...])
        la = dt * (-jnp.exp(alog_ref[...]))
        cum = jnp.dot(causal.astype(F32), la, precision=lax.Precision.HIGHEST, preferred_element_type=F32)
        ecum = jnp.exp(cum)
        wdt = jnp.exp(cum[rows - 1:rows, :] - cum) * dt
        ex_ref[n] = _head_expand([ecum, wdt, dt], exp_ref[...])
        cum2 = cum * math.log2(math.e)
        cum2_ref[n] = cum2
        cum2t_ref[n] = cum2.T

    @pl.when(c == 0)
    def _():
        s_ref[...] = jnp.zeros_like(s_ref)
        for n in range(n_seq):
            decay_tables(n, dtr_ref[n])

    head_lane = _iota((rows, bw), 1) // SSM_HEADDIM

    for n in range(n_seq):
        cum2 = cum2_ref[n]
        cum2_t = cum2t_ref[n]
        for g in range(SSM_GROUPS):
            gsl = slice(g * bw, (g + 1) * bw)
            bm = bc_ref[n, :, g * SSM_DSTATE:(g + 1) * SSM_DSTATE]
            cm = bc_ref[n, :, (SSM_GROUPS + g) * SSM_DSTATE:(SSM_GROUPS + g + 1) * SSM_DSTATE]
            xs = xc_ref[n, :, gsl].astype(F32)
            s = s_ref[n, g]
            gmat = _dot_nt(cm, bm)
            ps = []
            for hh in range(HEADS_PER_GROUP):
                h = g * HEADS_PER_GROUP + hh
                colb = jnp.broadcast_to(cum2[:, h:h + 1], (rows, LANES))
                dec = jnp.exp2(jnp.where(causal, colb - cum2_t[h:h + 1, :], -jnp.inf))
                ps.append((gmat * dec).astype(BF16))
            v = (xs * ex_ref[n, 2 * rows:3 * rows, gsl]).astype(BF16)
            y4 = _dot(jnp.concatenate(ps, axis=0), v)
            y = y4[0:rows]
            for hh in range(1, HEADS_PER_GROUP):
                y = jnp.where(head_lane == hh, y4[hh * rows:(hh + 1) * rows], y)
            y = y + _dot(cm, s.astype(BF16)) * ex_ref[n, 0:rows, gsl]
            vw = (xs * ex_ref[n, rows:2 * rows, gsl]).astype(BF16)
            s_ref[n, g] = s * ex_ref[n, rows - 1:rows, gsl] + _dot_tn(bm, vw)

            y = y + xs * dsk_ref[:, gsl]
            gg = y * _silu(z_ref[n, :, gsl].astype(F32))
            ms = jnp.mean(gg * gg, axis=-1, keepdims=True)
            g_ref[n, :, gsl] = (gg * lax.rsqrt(ms + EPS) * gn_ref[:, gsl]).astype(BF16)

        decay_tables(n, dtr_next_ref[n])

    @pl.when(c == pl.num_programs(1) - 1)
    def _():
        for n in range(n_seq):
            for g in range(SSM_GROUPS):
                st_ref[n, g * bw:(g + 1) * bw, :] = s_ref[n, g].T


def _ssd_prompt(zxc, dtr, dtb, alog, dsk, gn, expand, batch, seq):
    rows = SSD_CHUNK
    nc = seq // rows
    ns = SSD_PROMPT_SEQS
    full = lambda shape: pl.BlockSpec(shape, lambda b, c: (0,) * len(shape))
    col = lambda k: pl.BlockSpec((ns, rows, SSM_DINNER), lambda b, c: (b, c, k))
    return pl.pallas_call(
        _ssd_prompt_kernel,
        grid=(batch // ns, nc),
        in_specs=[
            col(0), col(1), col(2),
            pl.BlockSpec((ns, rows, LANES), lambda b, c: (b, c, 0)),
            pl.BlockSpec((ns, rows, LANES), lambda b, c: (b, jnp.minimum(c + 1, nc - 1), 0)),
            full((1, LANES)), full((1, LANES)),
            full((1, SSM_DINNER)), full((1, SSM_DINNER)),
            full((2 * LANES, SSM_DINNER)),
        ],
        out_specs=[
            pl.BlockSpec((ns, rows, SSM_DINNER), lambda b, c: (b, c, 0)),
            pl.BlockSpec((ns, SSM_DINNER, SSM_DSTATE), lambda b, c: (b, 0, 0)),
        ],
        out_shape=[
            jax.ShapeDtypeStruct((batch, seq, SSM_DINNER), BF16),
            jax.ShapeDtypeStruct((batch, SSM_DINNER, SSM_DSTATE), F32),
        ],
        scratch_shapes=[
            pltpu.VMEM((ns, SSM_GROUPS, SSM_DSTATE, SSM_GROUP_W), F32),
            pltpu.VMEM((ns, 3 * rows, SSM_DINNER), F32),
            pltpu.VMEM((ns, rows, LANES), F32),
            pltpu.VMEM((ns, LANES, rows), F32),
        ],
        compiler_params=_cparams(("arbitrary", "arbitrary")),
        name="ssd_prompt",
    )(zxc, zxc, zxc, dtr, dtr, dtb, alog, dsk, gn, expand)


def _ssd_prompt_ret_sample_kernel(*refs, seq_s):
    ssd_in, ret_in = refs[0:10], refs[10:17]
    ssd_out, ret_out = refs[17:19], refs[19:21]
    ssd_scratch = refs[21:25]
    _ssd_prompt_kernel(*ssd_in, *ssd_out, *ssd_scratch)
    _ret_sample_kernel(*ret_in, *ret_out, seq=seq_s)


def _ssd_prompt_ret_sample(zxc, dtr, dtb, alog, dsk, gn, expand, batch, seq,
                           proj_s, lg, hn, state_ret, batch_s, seq_s):
    rows = SSD_CHUNK
    nc = seq // rows
    ns = SSD_PROMPT_SEQS
    steps = (batch // ns) * nc
    assert steps % RET_HEADS == 0 and batch_s % (steps // RET_HEADS) == 0
    bb = batch_s // (steps // RET_HEADS)
    rows_s = bb * seq_s
    full = lambda shape: pl.BlockSpec(shape, lambda b, c: (0,) * len(shape))
    col = lambda k: pl.BlockSpec((ns, rows, SSM_DINNER), lambda b, c: (b, c, k))
    sn = lambda b, c: (b * nc + c) // RET_HEADS
    sh = lambda b, c: (b * nc + c) % RET_HEADS
    qb = RET_QK // RET_DK
    vb = 2 * RET_QK // RET_DV
    gb = vb + RET_VW // RET_DV
    st_spec = pl.BlockSpec((None, bb, None, RET_DK, RET_DV), lambda b, c: (0, sn(b, c), sh(b, c), 0, 0))
    return pl.pallas_call(
        functools.partial(_ssd_prompt_ret_sample_kernel, seq_s=seq_s),
        grid=(batch // ns, nc),
        in_specs=[
            col(0), col(1), col(2),
            pl.BlockSpec((ns, rows, LANES), lambda b, c: (b, c, 0)),
            pl.BlockSpec((ns, rows, LANES), lambda b, c: (b, jnp.minimum(c + 1, nc - 1), 0)),
            full((1, LANES)), full((1, LANES)),
            full((1, SSM_DINNER)), full((1, SSM_DINNER)),
            full((2 * LANES, SSM_DINNER)),
            pl.BlockSpec((rows_s, RET_DK), lambda b, c: (sn(b, c), sh(b, c))),
            pl.BlockSpec((rows_s, RET_DK), lambda b, c: (sn(b, c), qb + sh(b, c))),
            pl.BlockSpec((rows_s, RET_DV), lambda b, c: (sn(b, c), vb + sh(b, c))),
            pl.BlockSpec((rows_s, RET_DV), lambda b, c: (sn(b, c), gb + sh(b, c))),
            pl.BlockSpec((None, 1, LANES), lambda b, c: (sh(b, c), 0, 0)),
            pl.BlockSpec((1, RET_DV), lambda b, c: (0, sh(b, c))),
            st_spec,
        ],
        out_specs=[
            pl.BlockSpec((ns, rows, SSM_DINNER), lambda b, c: (b, c, 0)),
            pl.BlockSpec((ns, SSM_DINNER, SSM_DSTATE), lambda b, c: (b, 0, 0)),
            pl.BlockSpec((rows_s, RET_DV), lambda b, c: (sn(b, c), sh(b, c))),
            st_spec,
        ],
        out_shape=[
            jax.ShapeDtypeStruct((batch, seq, SSM_DINNER), BF16),
            jax.ShapeDtypeStruct((batch, SSM_DINNER, SSM_DSTATE), F32),
            jax.ShapeDtypeStruct((batch_s * seq_s, RET_VW), BF16),
            jax.ShapeDtypeStruct(state_ret.shape, F32),
        ],
        scratch_shapes=[
            pltpu.VMEM((ns, SSM_GROUPS, SSM_DSTATE, SSM_GROUP_W), F32),
            pltpu.VMEM((ns, 3 * rows, SSM_DINNER), F32),
            pltpu.VMEM((ns, rows, LANES), F32),
            pltpu.VMEM((ns, LANES, rows), F32),
        ],
        compiler_params=_cparams(("arbitrary", "arbitrary")),
        name="ssd_prompt_ret_sample",
    )(zxc, zxc, zxc, dtr, dtr, dtb, alog, dsk, gn, expand, proj_s, proj_s, proj_s, proj_s, lg, hn, state_ret)


def _ssd_sample_kernel(z_ref, xa_ref, xb_ref, dtr_ref, conv_in_ref, st_in_ref,
                       cw_ref, cb_ref, dtb_ref, alog_ref, dsk_ref, gn_ref, exp_ref,
                       g_ref, st_out_ref, conv_out_ref, xc_ref, ex_ref, y_ref, *, seq):
    bb = st_in_ref.shape[0]
    rows = bb * seq
    keep = SSM_CONV - 1
    bw = SSM_GROUP_W

    assert seq == SUBLANES
    sub = _iota((bb, SUBLANES, SSM_DINNER), 1)
    hist_pad = jnp.zeros((bb, SUBLANES - keep, SSM_DINNER), F32)
    for half, x_ref in enumerate((xa_ref, xb_ref)):
        csl = slice(half * SSM_DINNER, (half + 1) * SSM_DINNER)
        w0, w1, w2, w3 = (cw_ref[k:k + 1, csl] for k in range(SSM_CONV))
        x = x_ref[...].reshape(bb, SUBLANES, SSM_DINNER)
        hist = jnp.concatenate([hist_pad, conv_in_ref[:, :, csl]], axis=1)
        for k in range(keep):
            conv_out_ref[k, :, csl] = x[:, SUBLANES - keep + k, :]
        x2 = pltpu.roll(jnp.where(sub >= SUBLANES - 2, hist, x), 2, axis=1)
        p = x * w2 + x2 * w0
        p_hist = hist * w2 + pltpu.roll(hist, 2, axis=1) * w0
        p1 = pltpu.roll(jnp.where(sub >= SUBLANES - 1, p_hist, p), 1, axis=1)
        out = x * w3 + x2 * w1 + p1 + cb_ref[:, csl]
        xc_ref[:, csl] = _silu(out).reshape(rows, SSM_DINNER)

    dt = _softplus(dtr_ref[...] + dtb_ref[...])
    la = dt * (-jnp.exp(alog_ref[...]))
    ri = _iota((rows, rows), 0)
    rj = _iota((rows, rows), 1)
    same = (ri // seq) == (rj // seq)
    cum = jnp.dot((same & (ri >= rj)).astype(F32), la, precision=lax.Precision.HIGHEST,
                  preferred_element_type=F32)
    clast = jnp.dot(same.astype(F32), la, precision=lax.Precision.HIGHEST, preferred_element_type=F32)
    eclast = jnp.exp(clast)
    ex_ref[...] = _head_expand([jnp.exp(cum), jnp.exp(clast - cum) * dt, dt], exp_ref[...])

    si = _iota((seq, seq), 0)
    sj = _iota((seq, seq), 1)
    causal = si >= sj
    eye = si == sj
    head_lane = _iota((seq, bw), 1) // SSM_HEADDIM

    for n in range(bb):
        r0 = n * seq
        rsl = slice(r0, r0 + seq)
        cum_n = cum[rsl]
        for g in range(SSM_GROUPS):
            gsl = slice(g * bw, (g + 1) * bw)
            bm = xc_ref[rsl, SSM_DINNER + g * SSM_DSTATE:SSM_DINNER + (g + 1) * SSM_DSTATE]
            cm = xc_ref[rsl, SSM_DINNER + (SSM_GROUPS + g) * SSM_DSTATE:
                        SSM_DINNER + (SSM_GROUPS + g + 1) * SSM_DSTATE]
            xs = xc_ref[rsl, gsl]
            gmat = _dot_nt(cm.astype(BF16), bm.astype(BF16))
            ps = []
            for hh in range(HEADS_PER_GROUP):
                h = g * HEADS_PER_GROUP + hh
                ccol = cum_n[:, h:h + 1]
                crow = jnp.sum(jnp.where(eye, ccol, 0.0), axis=0, keepdims=True)
                ps.append(gmat * jnp.exp(jnp.where(causal, ccol - crow, -jnp.inf)))
            v = xs * ex_ref[2 * rows + r0:2 * rows + r0 + seq, gsl]
            y4 = _dot(jnp.concatenate(ps, axis=0).astype(BF16), v.astype(BF16))
            y = y4[0:seq]
            for hh in range(1, HEADS_PER_GROUP):
                y = jnp.where(head_lane == hh, y4[hh * seq:(hh + 1) * seq], y)
            st = st_in_ref[n, gsl, :]
            y = y + _dot_nt(cm, st) * ex_ref[rsl, gsl]
            y_ref[rsl, gsl] = y
            vw = xs * ex_ref[rows + r0:rows + r0 + seq, gsl]
            upd = _dot_tn(vw.astype(BF16), bm.astype(BF16))
            for hh in range(HEADS_PER_GROUP):
                h = g * HEADS_PER_GROUP + hh
                hsl = slice(hh * SSM_HEADDIM, (hh + 1) * SSM_HEADDIM)
                osl = slice(g * bw + hh * SSM_HEADDIM, g * bw + (hh + 1) * SSM_HEADDIM)
                st_out_ref[n, osl, :] = st[hsl] * eclast[r0:r0 + 1, h:h + 1] + upd[hsl]

    y = y_ref[...] + xc_ref[:, 0:SSM_DINNER] * dsk_ref[...]
    gg = y * _silu(z_ref[...])
    for g in range(SSM_GROUPS):
        gsl = slice(g * bw, (g + 1) * bw)
        blk = gg[:, gsl]
        ms = jnp.mean(blk * blk, axis=-1, keepdims=True)
        g_ref[:, gsl] = (blk * lax.rsqrt(ms + EPS) * gn_ref[:, gsl]).astype(BF16)


def _ssd_sample(zx, dtr, conv_state, ssm_state_t, cw, cb, dtb, alog, dsk, gn, expand, batch, seq):
    bb = SSD_SAMPLE_BB
    rows = bb * seq
    keep = SSM_CONV - 1
    full = lambda shape: pl.BlockSpec(shape, lambda n: (0,) * len(shape))
    col = lambda k: pl.BlockSpec((rows, SSM_DINNER), lambda n: (n, k))
    st_spec = pl.BlockSpec((bb, SSM_DINNER, SSM_DSTATE), lambda n: (n, 0, 0))
    return pl.pallas_call(
        functools.partial(_ssd_sample_kernel, seq=seq),
        grid=(batch // bb,),
        in_specs=[
            col(0), col(1), col(2),
            pl.BlockSpec((rows, LANES), lambda n: (n, 0)),
            pl.BlockSpec((None, bb, keep, SSM_CONVDIM), lambda n: (0, n, 0, 0)),
            st_spec,
            full((SSM_CONV, SSM_CONVDIM)), full((1, SSM_CONVDIM)),
            full((1, LANES)), full((1, LANES)),
            full((1, SSM_DINNER)), full((1, SSM_DINNER)),
            full((2 * LANES, SSM_DINNER)),
        ],
        out_specs=[
            pl.BlockSpec((rows, SSM_DINNER), lambda n: (n, 0)),
            st_spec,
            pl.BlockSpec((keep, bb, SSM_CONVDIM), lambda n: (0, n, 0)),
        ],
        out_shape=[
            jax.ShapeDtypeStruct((batch * seq, SSM_DINNER), BF16),
            jax.ShapeDtypeStruct(ssm_state_t.shape, F32),
            jax.ShapeDtypeStruct((keep, batch, SSM_CONVDIM), F32),
        ],
        scratch_shapes=[
            pltpu.VMEM((rows, SSM_CONVDIM), F32),
            pltpu.VMEM((3 * rows, SSM_DINNER), F32),
            pltpu.VMEM((rows, SSM_DINNER), F32),
        ],
        compiler_params=_cparams(("arbitrary",)),
        name="ssd_sample",
    )(zx, zx, zx, dtr, conv_state, ssm_state_t, cw, cb, dtb, alog, dsk, gn, expand)


def _rope_tables(pos):
    half = RET_DK // 2
    freqs = ROPE_BASE ** (-jnp.arange(half, dtype=F32) / half)
    ang = pos.astype(F32)[:, None] * freqs[None, :]
    return jnp.cos(ang), jnp.sin(ang)


def _row(v):
    return v.reshape(1, -1).astype(F32)


def _pad_lanes(v):
    return jnp.pad(v.astype(F32), (0, LANES - v.shape[0])).reshape(1, LANES)


def kernel(x_prompt, x_sample, state_ret, state_ssm, state_conv, ret_norm, ret_w_in, ret_head_norm, ret_w_out, ssm_norm, ssm_w_in, ssm_conv_w, ssm_conv_b, ssm_dt_bias, ssm_a_log, ssm_d, ssm_gate_norm, ssm_w_out, final_norm):
    bp, lp, d = x_prompt.shape
    bs, ls, _ = x_sample.shape
    assert ret_norm.shape[0] == 1 and ssm_norm.shape[0] == 1, "one retention and one SSD layer"

    ret_win = ret_w_in[0]
    ret_wout = ret_w_out[0].astype(BF16)
    n_main = SSM_DINNER + SSM_CONVDIM
    ssm_win_t = jnp.swapaxes(ssm_w_in[0], 0, 1)
    ssm_wdt = jnp.pad(ssm_w_in[0, :, n_main:], ((0, 0), (0, LANES - SSM_HEADS))).astype(BF16)
    ssm_wout = ssm_w_out[0].astype(BF16)
    expand = (jnp.arange(2 * LANES)[:, None] % LANES == jnp.arange(SSM_DINNER)[None, :] // SSM_HEADDIM).astype(BF16)
    lg = jnp.log(1.0 - 2.0 ** (-5.0 - jnp.arange(RET_HEADS, dtype=F32)))
    lg = jnp.broadcast_to(lg[:, None, None], (RET_HEADS, 1, LANES))
    hn = _row(ret_head_norm[0])
    dsk = _row(jnp.repeat(ssm_d[0], SSM_HEADDIM))
    gn = _row(ssm_gate_norm[0])
    cw = ssm_conv_w[0].astype(F32)
    cb = _row(ssm_conv_b[0])
    dtb = _pad_lanes(ssm_dt_bias[0])
    alog = _pad_lanes(ssm_a_log[0])
    g_ret = _row(ret_norm[0])
    g_ssm = _row(ssm_norm[0])
    g_fin = _row(final_norm)

    xp = x_prompt.reshape(bp * lp, d)
    cos_p, sin_p = _rope_tables(jnp.arange(lp, dtype=jnp.int32))
    n_ret = 2 * RET_QK + 2 * RET_VW
    proj = _ret_proj(xp, g_ret, ret_win, cos_p, sin_p, tm=lp, tn=1024)
    og, ret_p = _ret_prompt(proj.reshape(bp, lp, n_ret), lg, hn, bp, lp)
    h1 = _out_proj(og.reshape(bp * lp, RET_VW), ret_wout, xp, tm=1024)
    tm_c = 1024
    zxc, dtr, tail = _norm_proj_conv(h1, g_ssm, ssm_win_t, ssm_wdt, cw, cb, tm=tm_c, tn=2048, seq=lp)
    tail = tail[lp // tm_c - 1::lp // tm_c]
    xs = x_sample.reshape(bs * ls, d)
    cos_s, sin_s = _rope_tables(PAST_LEN + jnp.arange(ls, dtype=jnp.int32))
    proj_s = _ret_proj(xs, g_ret, ret_win, jnp.tile(cos_s, (bs, 1)), jnp.tile(sin_s, (bs, 1)), tm=bs * ls, tn=1024)

    gated, ssm_pt, og_s, ret_s = _ssd_prompt_ret_sample(
        zxc.reshape(bp, lp, -1), dtr.reshape(bp, lp, LANES), dtb, alog, dsk, gn, expand, bp, lp,
        proj_s, lg, hn, state_ret, bs, ls)
    y_p = _out_proj(gated.reshape(bp * lp, SSM_DINNER), ssm_wout, h1, tm=1024, final_g=g_fin)
    ssm_p = jnp.swapaxes(ssm_pt.reshape(1, bp, SSM_HEADS, SSM_HEADDIM, SSM_DSTATE), -1, -2)
    conv_p = tail[None, :, SUBLANES - (SSM_CONV - 1):, :]

    h1_s = _out_proj(og_s, ret_wout, xs, tm=512)
    zx_s, dtr_s = _norm_proj(h1_s, g_ssm, ssm_win_t, n=n_main, tm=bs * ls, tn=2048, out_dtype=F32, w_dt=ssm_wdt,
                             w_is_t=True)
    st_t = jnp.swapaxes(state_ssm[0], -1, -2).reshape(bs, SSM_DINNER, SSM_DSTATE)
    gated_s, ssm_st, conv_st = _ssd_sample(zx_s, dtr_s, state_conv, st_t, cw, cb, dtb, alog, dsk, gn, expand, bs, ls)
    ssm_s = jnp.swapaxes(ssm_st.reshape(1, bs, SSM_HEADS, SSM_HEADDIM, SSM_DSTATE), -1, -2)
    conv_s = jnp.swapaxes(conv_st, 0, 1)[None]
    y_s = _out_proj(gated_s, ssm_wout, h1_s, tm=512, final_g=g_fin)

    return (y_p.reshape(bp, lp, d), y_s.reshape(bs, ls, d), ret_p, ret_s, ssm_p, ssm_s, conv_p, conv_s)
```

```python
import functools
import math

import jax
import jax.numpy as jnp
from jax import lax
from jax.experimental import pallas as pl
from jax.experimental.pallas import tpu as pltpu

F32 = jnp.float32
BF16 = jnp.bfloat16

D_MODEL = 1024
RET_HEADS = 4
RET_DK = 256
RET_DV = 512
RET_VW = RET_HEADS * RET_DV
RET_QK = RET_HEADS * RET_DK
SSM_DINNER = 2048
SSM_HEADDIM = 64
SSM_HEADS = 32
SSM_GROUPS = 8
SSM_DSTATE = 128
SSM_CONV = 4
SSM_CONVDIM = 4096
SSM_GROUP_W = SSM_DINNER // SSM_GROUPS
HEADS_PER_GROUP = SSM_HEADS // SSM_GROUPS
ROPE_BASE = 10000.0
EPS = 1e-6
PAST_LEN = 16384

LANES = 128
SUBLANES = 8
MXU_WIDTH = 256
VMEM_LIMIT = 56 * 1024 * 1024

RET_CHUNK = 256
RET_PROMPT_SEQS = 2
SSD_CHUNK = 128
SSD_PROMPT_SEQS = 2
SSD_SAMPLE_BB = 8
RET_PROJ_TN = 1024
SSD_PROJ_TM = 1024
SSD_PROJ_TN = 2048
OUT_PROJ_TM = 1024
SAMPLE_TM = 512


def _cparams(sem):
    return pltpu.CompilerParams(dimension_semantics=sem, vmem_limit_bytes=VMEM_LIMIT)


def _sigmoid(x):
    return 1.0 / (1.0 + jnp.exp2(x * (-math.log2(math.e))))


def _silu(x):
    return x * _sigmoid(x)


def _softplus(x):
    return jnp.maximum(x, 0.0) + jnp.log1p(jnp.exp(-jnp.abs(x)))


def _dot(a, b):
    return jnp.dot(a, b, preferred_element_type=F32)


def _dot_nt(a, b):
    return lax.dot_general(a, b, (((1,), (1,)), ((), ())), preferred_element_type=F32)


def _dot_tn(a, b):
    return lax.dot_general(a, b, (((0,), (0,)), ((), ())), preferred_element_type=F32)


def _iota(shape, dim):
    return lax.broadcasted_iota(jnp.int32, shape, dim)


def _norm_proj_kernel(x_ref, g_ref, w_ref, *rest, has_dt, w_is_t):
    if has_dt:
        wdt_ref, o_ref, dt_ref, xn_ref = rest
    else:
        o_ref, xn_ref = rest

    @pl.when(pl.program_id(1) == 0)
    def _():
        x = x_ref[...]
        ms = jnp.mean(x * x, axis=-1, keepdims=True)
        xn = (x * lax.rsqrt(ms + EPS) * g_ref[...]).astype(BF16)
        xn_ref[...] = xn
        if has_dt:
            dt_ref[...] = _dot(xn, wdt_ref[...])

    w = w_ref[...].astype(BF16)
    o_ref[...] = (_dot_nt(xn_ref[...], w) if w_is_t else _dot(xn_ref[...], w)).astype(o_ref.dtype)


def _norm_proj(x, g, w, *, n, tm, tn, out_dtype, w_dt=None, w_is_t=False):
    m, d = x.shape
    has_dt = w_dt is not None
    in_specs = [
        pl.BlockSpec((tm, d), lambda i, j: (i, 0)),
        pl.BlockSpec((1, d), lambda i, j: (0, 0)),
        pl.BlockSpec((tn, d), lambda i, j: (j, 0)) if w_is_t else pl.BlockSpec((d, tn), lambda i, j: (0, j)),
    ]
    args = [x, g, w]
    out_shape = [jax.ShapeDtypeStruct((m, n), out_dtype)]
    out_specs = [pl.BlockSpec((tm, tn), lambda i, j: (i, j))]
    if has_dt:
        in_specs.append(pl.BlockSpec((d, LANES), lambda i, j: (0, 0)))
        args.append(w_dt)
        out_shape.append(jax.ShapeDtypeStruct((m, LANES), F32))
        out_specs.append(pl.BlockSpec((tm, LANES), lambda i, j: (i, 0)))
    res = pl.pallas_call(
        functools.partial(_norm_proj_kernel, has_dt=has_dt, w_is_t=w_is_t),
        grid=(m // tm, n // tn),
        in_specs=in_specs,
        out_specs=out_specs,
        out_shape=out_shape,
        scratch_shapes=[pltpu.VMEM((tm, d), BF16)],
        compiler_params=_cparams(("arbitrary", "arbitrary")),
        name="in_proj",
    )(*args)
    return res if has_dt else res[0]


def _rope(x, cos, sin):
    half = RET_DK // 2
    x1 = x[:, :half]
    x2 = x[:, half:]
    return jnp.concatenate([x1 * cos - x2 * sin, x1 * sin + x2 * cos], axis=1)


def _ret_proj_kernel(x_ref, g_ref, w_ref, cos_ref, sin_ref, o_ref, xn_ref, *, rope_tiles):
    j = pl.program_id(1)
    tn = o_ref.shape[1]

    @pl.when(j == 0)
    def _():
        x = x_ref[...]
        ms = jnp.mean(x * x, axis=-1, keepdims=True)
        xn_ref[...] = (x * lax.rsqrt(ms + EPS) * g_ref[...]).astype(BF16)

    @pl.when(j < rope_tiles)
    def _():
        cos = cos_ref[...]
        sin = sin_ref[...]
        for h in range(tn // RET_DK):
            sl = slice(h * RET_DK, (h + 1) * RET_DK)
            y = _dot(xn_ref[...], w_ref[:, sl].astype(BF16))
            o_ref[:, sl] = _rope(y, cos, sin).astype(o_ref.dtype)

    @pl.when(j >= rope_tiles)
    def _():
        o_ref[...] = _dot(xn_ref[...], w_ref[...].astype(BF16)).astype(o_ref.dtype)


def _ret_proj(x, g, w, cos, sin, *, tm, tn):
    m, d = x.shape
    n = w.shape[1]
    period_tiles = max(cos.shape[0] // tm, 1)
    return pl.pallas_call(
        functools.partial(_ret_proj_kernel, rope_tiles=2 * RET_QK // tn),
        grid=(m // tm, n // tn),
        in_specs=[
            pl.BlockSpec((tm, d), lambda i, j: (i, 0)),
            pl.BlockSpec((1, d), lambda i, j: (0, 0)),
            pl.BlockSpec((d, tn), lambda i, j: (0, j)),
            pl.BlockSpec((tm, RET_DK // 2), lambda i, j: (i % period_tiles, 0)),
            pl.BlockSpec((tm, RET_DK // 2), lambda i, j: (i % period_tiles, 0)),
        ],
        out_specs=pl.BlockSpec((tm, tn), lambda i, j: (i, j)),
        out_shape=jax.ShapeDtypeStruct((m, n), BF16),
        scratch_shapes=[pltpu.VMEM((tm, d), BF16)],
        compiler_params=_cparams(("arbitrary", "arbitrary")),
        name="ret_in_proj",
    )(x, g, w, cos, sin)


def _shift_rows(a, first_rows):
    rows, cols = a.shape
    n = first_rows.shape[0]
    slabs = rows // SUBLANES
    a3 = a.reshape(slabs, SUBLANES, cols)
    first = jnp.concatenate([jnp.zeros((SUBLANES - n, cols), a.dtype), first_rows], axis=0)
    above = jnp.concatenate([first.reshape(1, SUBLANES, cols), a3[:slabs - 1]], axis=0)
    bottom = _iota((slabs, SUBLANES, cols), 1) >= SUBLANES - n
    return pltpu.roll(jnp.where(bottom, above, a3), n, axis=1).reshape(rows, cols)


def _norm_proj_conv_kernel(x_ref, g_ref, w_ref, wdt_ref, cw_ref, cb_ref, o_ref, dt_ref, tail_ref,
                           xn_ref, carry_ref, *, tiles_per_seq, z_tiles, strip):
    i = pl.program_id(0)
    j = pl.program_id(1)
    tm, tn = o_ref.shape

    @pl.when(j == 0)
    def _():
        x = x_ref[...]
        ms = jnp.mean(x * x, axis=-1, keepdims=True)
        xn = (x * lax.rsqrt(ms + EPS) * g_ref[...]).astype(BF16)
        xn_ref[...] = xn
        dt_ref[...] = _dot(xn, wdt_ref[...])

    @pl.when((j == 0) & (i % tiles_per_seq == 0))
    def _():
        carry_ref[...] = jnp.zeros_like(carry_ref)

    @pl.when(j < z_tiles)
    def _():
        o_ref[...] = _dot_nt(xn_ref[...], w_ref[...].astype(BF16)).astype(o_ref.dtype)

    @pl.when(j >= z_tiles)
    def _():
        jc = j - z_tiles
        for s in range(tn // strip):
            sl = slice(s * strip, (s + 1) * strip)
            x = _dot_nt(xn_ref[...], w_ref[sl, :].astype(BF16))
            tail_ref[:, sl] = x[tm - SUBLANES:tm, :]
            x2 = _shift_rows(x, carry_ref[jc, 1:3, sl])
            p = x * cw_ref[2:3, sl] + x2 * cw_ref[0:1, sl]
            out = x * cw_ref[3:4, sl] + x2 * cw_ref[1:2, sl] + _shift_rows(p, carry_ref[jc, 0:1, sl]) + cb_ref[:, sl]
            o_ref[:, sl] = _silu(out).astype(o_ref.dtype)
            carry_ref[jc, 0:1, sl] = p[tm - 1:tm]
            carry_ref[jc, 1:3, sl] = x[tm - 2:tm]


def _norm_proj_conv(x, g, wt, w_dt, cw, cb, *, tm, tn, seq):
    m, d = x.shape
    n = SSM_DINNER + SSM_CONVDIM
    z_tiles = SSM_DINNER // tn
    c_tiles = SSM_CONVDIM // tn
    tiles_per_seq = seq // tm
    cj = lambda j: jnp.maximum(j - z_tiles, 0)
    return pl.pallas_call(
        functools.partial(_norm_proj_conv_kernel, tiles_per_seq=tiles_per_seq, z_tiles=z_tiles,
                          strip=MXU_WIDTH),
        grid=(m // tm, n // tn),
        in_specs=[
            pl.BlockSpec((tm, d), lambda i, j: (i, 0)),
            pl.BlockSpec((1, d), lambda i, j: (0, 0)),
            pl.BlockSpec((tn, d), lambda i, j: (j, 0)),
            pl.BlockSpec((d, LANES), lambda i, j: (0, 0)),
            pl.BlockSpec((SSM_CONV, tn), lambda i, j: (0, cj(j))),
            pl.BlockSpec((1, tn), lambda i, j: (0, cj(j))),
        ],
        out_specs=[
            pl.BlockSpec((tm, tn), lambda i, j: (i, j)),
            pl.BlockSpec((tm, LANES), lambda i, j: (i, 0)),
            pl.BlockSpec((None, SUBLANES, tn), lambda i, j: (i, 0, cj(j))),
        ],
        out_shape=[
            jax.ShapeDtypeStruct((m, n), BF16),
            jax.ShapeDtypeStruct((m, LANES), F32),
            jax.ShapeDtypeStruct((m // tm, SUBLANES, SSM_CONVDIM), F32),
        ],
        scratch_shapes=[
            pltpu.VMEM((tm, d), BF16),
            pltpu.VMEM((c_tiles, SUBLANES, tn), F32),
        ],
        compiler_params=_cparams(("arbitrary", "arbitrary")),
        name="ssd_in_proj_conv",
    )(x, g, wt, w_dt, cw, cb)


def _out_proj_kernel(a_ref, w_ref, h_ref, *rest, has_norm):
    if has_norm:
        g_ref, o_ref = rest
    else:
        (o_ref,) = rest
    y = h_ref[...] + _dot(a_ref[...], w_ref[...])
    if has_norm:
        ms = jnp.mean(y * y, axis=-1, keepdims=True)
        y = y * lax.rsqrt(ms + EPS) * g_ref[...]
    o_ref[...] = y


def _out_proj(a, w, h, *, tm, final_g=None):
    m, k = a.shape
    d = w.shape[1]
    has_norm = final_g is not None
    in_specs = [
        pl.BlockSpec((tm, k), lambda i: (i, 0)),
        pl.BlockSpec((k, d), lambda i: (0, 0)),
        pl.BlockSpec((tm, d), lambda i: (i, 0)),
    ]
    args = [a, w, h]
    if has_norm:
        in_specs.append(pl.BlockSpec((1, d), lambda i: (0, 0)))
        args.append(final_g)
    return pl.pallas_call(
        functools.partial(_out_proj_kernel, has_norm=has_norm),
        grid=(m // tm,),
        in_specs=in_specs,
        out_specs=pl.BlockSpec((tm, d), lambda i: (i, 0)),
        out_shape=jax.ShapeDtypeStruct((m, d), F32),
        compiler_params=_cparams(("arbitrary",)),
        name="out_proj",
    )(*args)


def _head_norm_gate(o, hn, gate):
    ms = jnp.mean(o * o, axis=-1, keepdims=True)
    on = o * lax.rsqrt(ms + EPS) * hn
    return (on * _silu(gate.astype(F32))).astype(BF16)


def _ret_prompt_kernel(q_ref, k_ref, v_ref, gate_ref, lg_ref, hn_ref,
                       og_ref, st_ref, s_ref, dec_ref, qsc_ref, ksc_ref, gc_ref):
    b = pl.program_id(0)
    c = pl.program_id(1)
    n_seq, rows = q_ref.shape[0], q_ref.shape[1]

    @pl.when((b == 0) & (c == 0))
    def _():
        i = _iota((rows, rows), 0)
        j = _iota((rows, rows), 1)
        r = _iota((rows, RET_DK), 0)
        scale = RET_DK ** -0.5
        for h in range(RET_HEADS):
            lg = lg_ref[h, :, 0:1]
            dec_ref[h] = jnp.where(i >= j, jnp.exp((i - j).astype(F32) * lg), 0.0) * scale
            qsc_ref[h] = jnp.exp((r + 1).astype(F32) * lg)
            ksc_ref[h] = jnp.exp((rows - 1 - r).astype(F32) * lg) * scale
            gc_ref[h] = jnp.exp(jnp.full((1, RET_DV), float(rows), F32) * lg)

    @pl.when(c == 0)
    def _():
        s_ref[...] = jnp.zeros_like(s_ref)

    for n in range(n_seq):
        for h in range(RET_HEADS):
            ksl = slice(h * RET_DK, (h + 1) * RET_DK)
            vsl = slice(h * RET_DV, (h + 1) * RET_DV)
            q = q_ref[n, :, ksl]
            k = k_ref[n, :, ksl]
            v = v_ref[n, :, vsl]
            p = (_dot_nt(q, k) * dec_ref[h]).astype(BF16)
            qs = (q.astype(F32) * qsc_ref[h]).astype(BF16)
            kr = k.astype(F32)
            s = s_ref[n, h]
            o = _dot(p, v) + _dot(qs, s.astype(BF16))
            kt = (kr * ksc_ref[h]).T.astype(BF16)
            s_ref[n, h] = s * gc_ref[h] + _dot(kt, v)
            og_ref[n, :, vsl] = _head_norm_gate(o, hn_ref[:, vsl], gate_ref[n, :, vsl])

    @pl.when(c == pl.num_programs(1) - 1)
    def _():
        st_ref[...] = s_ref[...]


def _ret_prompt(proj, lg, hn, batch, seq):
    rows = RET_CHUNK
    nc = seq // rows
    ns = RET_PROMPT_SEQS
    return pl.pallas_call(
        _ret_prompt_kernel,
        grid=(batch // ns, nc),
        in_specs=[
            pl.BlockSpec((ns, rows, RET_QK), lambda b, c: (b, c, 0)),
            pl.BlockSpec((ns, rows, RET_QK), lambda b, c: (b, c, 1)),
            pl.BlockSpec((ns, rows, RET_VW), lambda b, c: (b, c, 2 * RET_QK // RET_VW)),
            pl.BlockSpec((ns, rows, RET_VW), lambda b, c: (b, c, 2 * RET_QK // RET_VW + 1)),
            pl.BlockSpec((RET_HEADS, 1, LANES), lambda b, c: (0, 0, 0)),
            pl.BlockSpec((1, RET_VW), lambda b, c: (0, 0)),
        ],
        out_specs=[
            pl.BlockSpec((ns, rows, RET_VW), lambda b, c: (b, c, 0)),
            pl.BlockSpec((None, ns, RET_HEADS, RET_DK, RET_DV), lambda b, c: (0, b, 0, 0, 0)),
        ],
        out_shape=[
            jax.ShapeDtypeStruct((batch, seq, RET_VW), BF16),
            jax.ShapeDtypeStruct((1, batch, RET_HEADS, RET_DK, RET_DV), F32),
        ],
        scratch_shapes=[
            pltpu.VMEM((ns, RET_HEADS, RET_DK, RET_DV), F32),
            pltpu.VMEM((RET_HEADS, rows, rows), F32),
            pltpu.VMEM((RET_HEADS, rows, RET_DK), F32),
            pltpu.VMEM((RET_HEADS, rows, RET_DK), F32),
            pltpu.VMEM((RET_HEADS, 1, RET_DV), F32),
        ],
        compiler_params=_cparams(("arbitrary", "arbitrary")),
        name="ret_prompt",
    )(proj, proj, proj, proj, lg, hn)


def _ret_sample_kernel(q_ref, k_ref, v_ref, gate_ref, lg_ref, hn_ref, st_in_ref,
                       og_ref, st_out_ref, *, seq):
    bb = st_in_ref.shape[0]
    rows = bb * seq
    lg = lg_ref[:, 0:1]
    scale = RET_DK ** -0.5
    i = _iota((rows, rows), 0)
    j = _iota((rows, rows), 1)
    same = (i // seq) == (j // seq)
    dec = jnp.where(same & (i >= j), jnp.exp((i - j).astype(F32) * lg), 0.0) * scale
    pos = _iota((rows, RET_DK), 0) % seq
    qsc = jnp.exp((pos + 1).astype(F32) * lg)
    ksc = jnp.exp((seq - 1 - pos).astype(F32) * lg) * scale
    gc = jnp.exp(jnp.full((1, RET_DV), float(seq), F32) * lg)

    qr = q_ref[...].astype(F32)
    kr = k_ref[...].astype(F32)
    v = v_ref[...].astype(F32)
    p = _dot_nt(q_ref[...], k_ref[...]) * dec
    intra = _dot(p.astype(BF16), v.astype(BF16))
    qs = qr * qsc
    kt = kr * ksc
    vrow = _iota((rows, RET_DV), 0) // seq
    cross = []
    for n in range(bb):
        s0 = st_in_ref[n]
        cross.append(_dot(qs[n * seq:(n + 1) * seq], s0))
        vn = jnp.where(vrow == n, v, 0.0)
        st_out_ref[n] = s0 * gc + _dot_tn(kt.astype(BF16), vn.astype(BF16))
    o = intra + jnp.concatenate(cross, axis=0)
    og_ref[...] = _head_norm_gate(o, hn_ref[...], gate_ref[...])


def _head_expand(vals, expand):
    parts = []
    for v in vals:
        hi = v.astype(BF16)
        lo = (v - hi.astype(F32)).astype(BF16)
        parts.append(jnp.concatenate([hi, lo], axis=1))
    return _dot(jnp.concatenate(parts, axis=0), expand)


def _ssd_prompt_kernel(z_ref, xc_ref, bc_ref, dtr_ref, dtr_next_ref, dtb_ref, alog_ref, dsk_ref, gn_ref, exp_ref,
                       g_ref, st_ref, s_ref, ex_ref, cum2_ref, cum2t_ref):
    c = pl.program_id(1)
    n_seq, rows = z_ref.shape[0], z_ref.shape[1]
    bw = SSM_GROUP_W
    ri = _iota((rows, rows), 0)
    rj = _iota((rows, rows), 1)
    causal = ri >= rj

    def decay_tables(n, dtr):
        dt = _softplus(dtr + dtb_ref[...])
        la = dt * (-jnp.exp(alog_ref[...]))
        cum = jnp.dot(causal.astype(F32), la, precision=lax.Precision.HIGHEST, preferred_element_type=F32)
        ecum = jnp.exp(cum)
        wdt = jnp.exp(cum[rows - 1:rows, :] - cum) * dt
        ex_ref[n] = _head_expand([ecum, wdt, dt], exp_ref[...])
        cum2 = cum * math.log2(math.e)
        cum2_ref[n] = cum2
        cum2t_ref[n] = cum2.T

    @pl.when(c == 0)
    def _():
        s_ref[...] = jnp.zeros_like(s_ref)
        for n in range(n_seq):
            decay_tables(n, dtr_ref[n])

    head_lane = _iota((rows, bw), 1) // SSM_HEADDIM

    for n in range(n_seq):
        cum2 = cum2_ref[n]
        cum2_t = cum2t_ref[n]
        for g in range(SSM_GROUPS):
            gsl = slice(g * bw, (g + 1) * bw)
            bm = bc_ref[n, :, g * SSM_DSTATE:(g + 1) * SSM_DSTATE]
            cm = bc_ref[n, :, (SSM_GROUPS + g) * SSM_DSTATE:(SSM_GROUPS + g + 1) * SSM_DSTATE]
            xs = xc_ref[n, :, gsl].astype(F32)
            s = s_ref[n, g]
            gmat = _dot_nt(cm, bm)
            ps = []
            for hh in range(HEADS_PER_GROUP):
                h = g * HEADS_PER_GROUP + hh
                colb = jnp.broadcast_to(cum2[:, h:h + 1], (rows, LANES))
                dec = jnp.exp2(jnp.where(causal, colb - cum2_t[h:h + 1, :], -jnp.inf))
                ps.append((gmat * dec).astype(BF16))
            v = (xs * ex_ref[n, 2 * rows:3 * rows, gsl]).astype(BF16)
            y4 = _dot(jnp.concatenate(ps, axis=0), v)
            y = y4[0:rows]
            for hh in range(1, HEADS_PER_GROUP):
                y = jnp.where(head_lane == hh, y4[hh * rows:(hh + 1) * rows], y)
            y = y + _dot(cm, s.astype(BF16)) * ex_ref[n, 0:rows, gsl]
            vw = (xs * ex_ref[n, rows:2 * rows, gsl]).astype(BF16)
            s_ref[n, g] = s * ex_ref[n, rows - 1:rows, gsl] + _dot_tn(bm, vw)

            y = y + xs * dsk_ref[:, gsl]
            gg = y * _silu(z_ref[n, :, gsl].astype(F32))
            ms = jnp.mean(gg * gg, axis=-1, keepdims=True)
            g_ref[n, :, gsl] = (gg * lax.rsqrt(ms + EPS) * gn_ref[:, gsl]).astype(BF16)

        decay_tables(n, dtr_next_ref[n])

    @pl.when(c == pl.num_programs(1) - 1)
    def _():
        for n in range(n_seq):
            for g in range(SSM_GROUPS):
                st_ref[n, g * bw:(g + 1) * bw, :] = s_ref[n, g].T


def _ssd_prompt_ret_sample_kernel(*refs, seq_s):
    ssd_in, ret_in = refs[0:10], refs[10:17]
    ssd_out, ret_out = refs[17:19], refs[19:21]
    ssd_scratch = refs[21:25]
    _ssd_prompt_kernel(*ssd_in, *ssd_out, *ssd_scratch)
    _ret_sample_kernel(*ret_in, *ret_out, seq=seq_s)


def _ssd_prompt_ret_sample(zxc, dtr, dtb, alog, dsk, gn, expand, batch, seq,
                           proj_s, lg, hn, state_ret, batch_s, seq_s):
    rows = SSD_CHUNK
    nc = seq // rows
    ns = SSD_PROMPT_SEQS
    steps = (batch // ns) * nc
    assert steps % RET_HEADS == 0 and batch_s % (steps // RET_HEADS) == 0
    bb = batch_s // (steps // RET_HEADS)
    rows_s = bb * seq_s
    full = lambda shape: pl.BlockSpec(shape, lambda b, c: (0,) * len(shape))
    col = lambda k: pl.BlockSpec((ns, rows, SSM_DINNER), lambda b, c: (b, c, k))
    sn = lambda b, c: (b * nc + c) // RET_HEADS
    sh = lambda b, c: (b * nc + c) % RET_HEADS
    qb = RET_QK // RET_DK
    vb = 2 * RET_QK // RET_DV
    gb = vb + RET_VW // RET_DV
    st_spec = pl.BlockSpec((None, bb, None, RET_DK, RET_DV), lambda b, c: (0, sn(b, c), sh(b, c), 0, 0))
    return pl.pallas_call(
        functools.partial(_ssd_prompt_ret_sample_kernel, seq_s=seq_s),
        grid=(batch // ns, nc),
        in_specs=[
            col(0), col(1), col(2),
            pl.BlockSpec((ns, rows, LANES), lambda b, c: (b, c, 0)),
            pl.BlockSpec((ns, rows, LANES), lambda b, c: (b, jnp.minimum(c + 1, nc - 1), 0)),
            full((1, LANES)), full((1, LANES)),
            full((1, SSM_DINNER)), full((1, SSM_DINNER)),
            full((2 * LANES, SSM_DINNER)),
            pl.BlockSpec((rows_s, RET_DK), lambda b, c: (sn(b, c), sh(b, c))),
            pl.BlockSpec((rows_s, RET_DK), lambda b, c: (sn(b, c), qb + sh(b, c))),
            pl.BlockSpec((rows_s, RET_DV), lambda b, c: (sn(b, c), vb + sh(b, c))),
            pl.BlockSpec((rows_s, RET_DV), lambda b, c: (sn(b, c), gb + sh(b, c))),
            pl.BlockSpec((None, 1, LANES), lambda b, c: (sh(b, c), 0, 0)),
            pl.BlockSpec((1, RET_DV), lambda b, c: (0, sh(b, c))),
            st_spec,
        ],
        out_specs=[
            pl.BlockSpec((ns, rows, SSM_DINNER), lambda b, c: (b, c, 0)),
            pl.BlockSpec((ns, SSM_DINNER, SSM_DSTATE), lambda b, c: (b, 0, 0)),
            pl.BlockSpec((rows_s, RET_DV), lambda b, c: (sn(b, c), sh(b, c))),
            st_spec,
        ],
        out_shape=[
            jax.ShapeDtypeStruct((batch, seq, SSM_DINNER), BF16),
            jax.ShapeDtypeStruct((batch, SSM_DINNER, SSM_DSTATE), F32),
            jax.ShapeDtypeStruct((batch_s * seq_s, RET_VW), BF16),
            jax.ShapeDtypeStruct(state_ret.shape, F32),
        ],
        scratch_shapes=[
            pltpu.VMEM((ns, SSM_GROUPS, SSM_DSTATE, SSM_GROUP_W), F32),
            pltpu.VMEM((ns, 3 * rows, SSM_DINNER), F32),
            pltpu.VMEM((ns, rows, LANES), F32),
            pltpu.VMEM((ns, LANES, rows), F32),
        ],
        compiler_params=_cparams(("arbitrary", "arbitrary")),
        name="ssd_prompt_ret_sample",
    )(zxc, zxc, zxc, dtr, dtr, dtb, alog, dsk, gn, expand, proj_s, proj_s, proj_s, proj_s, lg, hn, state_ret)


def _ssd_sample_kernel(z_ref, xa_ref, xb_ref, dtr_ref, conv_in_ref, st_in_ref,
                       cw_ref, cb_ref, dtb_ref, alog_ref, dsk_ref, gn_ref, exp_ref,
                       g_ref, st_out_ref, conv_out_ref, xc_ref, ex_ref, y_ref, *, seq):
    bb = st_in_ref.shape[0]
    rows = bb * seq
    keep = SSM_CONV - 1
    bw = SSM_GROUP_W

    assert seq == SUBLANES
    sub = _iota((bb, SUBLANES, SSM_DINNER), 1)
    hist_pad = jnp.zeros((bb, SUBLANES - keep, SSM_DINNER), F32)
    for half, x_ref in enumerate((xa_ref, xb_ref)):
        csl = slice(half * SSM_DINNER, (half + 1) * SSM_DINNER)
        w0, w1, w2, w3 = (cw_ref[k:k + 1, csl] for k in range(SSM_CONV))
        x = x_ref[...].reshape(bb, SUBLANES, SSM_DINNER)
        hist = jnp.concatenate([hist_pad, conv_in_ref[:, :, csl]], axis=1)
        for k in range(keep):
            conv_out_ref[k, :, csl] = x[:, SUBLANES - keep + k, :]
        x2 = pltpu.roll(jnp.where(sub >= SUBLANES - 2, hist, x), 2, axis=1)
        p = x * w2 + x2 * w0
        p_hist = hist * w2 + pltpu.roll(hist, 2, axis=1) * w0
        p1 = pltpu.roll(jnp.where(sub >= SUBLANES - 1, p_hist, p), 1, axis=1)
        out = x * w3 + x2 * w1 + p1 + cb_ref[:, csl]
        xc_ref[:, csl] = _silu(out).reshape(rows, SSM_DINNER)

    dt = _softplus(dtr_ref[...] + dtb_ref[...])
    la = dt * (-jnp.exp(alog_ref[...]))
    ri = _iota((rows, rows), 0)
    rj = _iota((rows, rows), 1)
    same = (ri // seq) == (rj // seq)
    cum = jnp.dot((same & (ri >= rj)).astype(F32), la, precision=lax.Precision.HIGHEST,
                  preferred_element_type=F32)
    clast = jnp.dot(same.astype(F32), la, precision=lax.Precision.HIGHEST, preferred_element_type=F32)
    eclast = jnp.exp(clast)
    ex_ref[...] = _head_expand([jnp.exp(cum), jnp.exp(clast - cum) * dt, dt], exp_ref[...])

    si = _iota((seq, seq), 0)
    sj = _iota((seq, seq), 1)
    causal = si >= sj
    eye = si == sj
    head_lane = _iota((seq, bw), 1) // SSM_HEADDIM

    for n in range(bb):
        r0 = n * seq
        rsl = slice(r0, r0 + seq)
        cum_n = cum[rsl]
        for g in range(SSM_GROUPS):
            gsl = slice(g * bw, (g + 1) * bw)
            bm = xc_ref[rsl, SSM_DINNER + g * SSM_DSTATE:SSM_DINNER + (g + 1) * SSM_DSTATE]
            cm = xc_ref[rsl, SSM_DINNER + (SSM_GROUPS + g) * SSM_DSTATE:
                        SSM_DINNER + (SSM_GROUPS + g + 1) * SSM_DSTATE]
            xs = xc_ref[rsl, gsl]
            gmat = _dot_nt(cm.astype(BF16), bm.astype(BF16))
            ps = []
            for hh in range(HEADS_PER_GROUP):
                h = g * HEADS_PER_GROUP + hh
                ccol = cum_n[:, h:h + 1]
                crow = jnp.sum(jnp.where(eye, ccol, 0.0), axis=0, keepdims=True)
                ps.append(gmat * jnp.exp(jnp.where(causal, ccol - crow, -jnp.inf)))
            v = xs * ex_ref[2 * rows + r0:2 * rows + r0 + seq, gsl]
            y4 = _dot(jnp.concatenate(ps, axis=0).astype(BF16), v.astype(BF16))
            y = y4[0:seq]
            for hh in range(1, HEADS_PER_GROUP):
                y = jnp.where(head_lane == hh, y4[hh * seq:(hh + 1) * seq], y)
            st = st_in_ref[n, gsl, :]
            y = y + _dot_nt(cm, st) * ex_ref[rsl, gsl]
            y_ref[rsl, gsl] = y
            vw = xs * ex_ref[rows + r0:rows + r0 + seq, gsl]
            upd = _dot_tn(vw.astype(BF16), bm.astype(BF16))
            for hh in range(HEADS_PER_GROUP):
                h = g * HEADS_PER_GROUP + hh
                hsl = slice(hh * SSM_HEADDIM, (hh + 1) * SSM_HEADDIM)
                osl = slice(g * bw + hh * SSM_HEADDIM, g * bw + (hh + 1) * SSM_HEADDIM)
                st_out_ref[n, osl, :] = st[hsl] * eclast[r0:r0 + 1, h:h + 1] + upd[hsl]

    y = y_ref[...] + xc_ref[:, 0:SSM_DINNER] * dsk_ref[...]
    gg = y * _silu(z_ref[...])
    for g in range(SSM_GROUPS):
        gsl = slice(g * bw, (g + 1) * bw)
        blk = gg[:, gsl]
        ms = jnp.mean(blk * blk, axis=-1, keepdims=True)
        g_ref[:, gsl] = (blk * lax.rsqrt(ms + EPS) * gn_ref[:, gsl]).astype(BF16)


def _ssd_sample(zx, dtr, conv_state, ssm_state_t, cw, cb, dtb, alog, dsk, gn, expand, batch, seq):
    bb = SSD_SAMPLE_BB
    rows = bb * seq
    keep = SSM_CONV - 1
    full = lambda shape: pl.BlockSpec(shape, lambda n: (0,) * len(shape))
    col = lambda k: pl.BlockSpec((rows, SSM_DINNER), lambda n: (n, k))
    st_spec = pl.BlockSpec((bb, SSM_DINNER, SSM_DSTATE), lambda n: (n, 0, 0))
    return pl.pallas_call(
        functools.partial(_ssd_sample_kernel, seq=seq),
        grid=(batch // bb,),
        in_specs=[
            col(0), col(1), col(2),
            pl.BlockSpec((rows, LANES), lambda n: (n, 0)),
            pl.BlockSpec((None, bb, keep, SSM_CONVDIM), lambda n: (0, n, 0, 0)),
            st_spec,
            full((SSM_CONV, SSM_CONVDIM)), full((1, SSM_CONVDIM)),
            full((1, LANES)), full((1, LANES)),
            full((1, SSM_DINNER)), full((1, SSM_DINNER)),
            full((2 * LANES, SSM_DINNER)),
        ],
        out_specs=[
            pl.BlockSpec((rows, SSM_DINNER), lambda n: (n, 0)),
            st_spec,
            pl.BlockSpec((keep, bb, SSM_CONVDIM), lambda n: (0, n, 0)),
        ],
        out_shape=[
            jax.ShapeDtypeStruct((batch * seq, SSM_DINNER), BF16),
            jax.ShapeDtypeStruct(ssm_state_t.shape, F32),
            jax.ShapeDtypeStruct((keep, batch, SSM_CONVDIM), F32),
        ],
        scratch_shapes=[
            pltpu.VMEM((rows, SSM_CONVDIM), F32),
            pltpu.VMEM((3 * rows, SSM_DINNER), F32),
            pltpu.VMEM((rows, SSM_DINNER), F32),
        ],
        compiler_params=_cparams(("arbitrary",)),
        name="ssd_sample",
    )(zx, zx, zx, dtr, conv_state, ssm_state_t, cw, cb, dtb, alog, dsk, gn, expand)


def _rope_tables(pos):
    half = RET_DK // 2
    freqs = ROPE_BASE ** (-jnp.arange(half, dtype=F32) / half)
    ang = pos.astype(F32)[:, None] * freqs[None, :]
    return jnp.cos(ang), jnp.sin(ang)


def _row(v):
    return v.reshape(1, -1).astype(F32)


def _pad_lanes(v):
    return jnp.pad(v.astype(F32), (0, LANES - v.shape[0])).reshape(1, LANES)


def kernel(x_prompt, x_sample, state_ret, state_ssm, state_conv, ret_norm, ret_w_in, ret_head_norm, ret_w_out, ssm_norm, ssm_w_in, ssm_conv_w, ssm_conv_b, ssm_dt_bias, ssm_a_log, ssm_d, ssm_gate_norm, ssm_w_out, final_norm):
    bp, lp, d = x_prompt.shape
    bs, ls, _ = x_sample.shape
    assert ret_norm.shape[0] == 1 and ssm_norm.shape[0] == 1, "one retention and one SSD layer"

    ret_win = ret_w_in[0]
    ret_wout = ret_w_out[0].astype(BF16)
    n_main = SSM_DINNER + SSM_CONVDIM
    ssm_win_t = jnp.swapaxes(ssm_w_in[0], 0, 1)
    ssm_wdt = jnp.pad(ssm_w_in[0, :, n_main:], ((0, 0), (0, LANES - SSM_HEADS))).astype(BF16)
    ssm_wout = ssm_w_out[0].astype(BF16)
    expand = (jnp.arange(2 * LANES)[:, None] % LANES == jnp.arange(SSM_DINNER)[None, :] // SSM_HEADDIM).astype(BF16)
    lg = jnp.log(1.0 - 2.0 ** (-5.0 - jnp.arange(RET_HEADS, dtype=F32)))
    lg = jnp.broadcast_to(lg[:, None, None], (RET_HEADS, 1, LANES))
    hn = _row(ret_head_norm[0])
    dsk = _row(jnp.repeat(ssm_d[0], SSM_HEADDIM))
    gn = _row(ssm_gate_norm[0])
    cw = ssm_conv_w[0].astype(F32)
    cb = _row(ssm_conv_b[0])
    dtb = _pad_lanes(ssm_dt_bias[0])
    alog = _pad_lanes(ssm_a_log[0])
    g_ret = _row(ret_norm[0])
    g_ssm = _row(ssm_norm[0])
    g_fin = _row(final_norm)

    xp = x_prompt.reshape(bp * lp, d)
    cos_p, sin_p = _rope_tables(jnp.arange(lp, dtype=jnp.int32))
    n_ret = 2 * RET_QK + 2 * RET_VW
    proj = _ret_proj(xp, g_ret, ret_win, cos_p, sin_p, tm=lp, tn=RET_PROJ_TN)
    og, ret_p = _ret_prompt(proj.reshape(bp, lp, n_ret), lg, hn, bp, lp)
    h1 = _out_proj(og.reshape(bp * lp, RET_VW), ret_wout, xp, tm=OUT_PROJ_TM)
    zxc, dtr, tail = _norm_proj_conv(h1, g_ssm, ssm_win_t, ssm_wdt, cw, cb, tm=SSD_PROJ_TM, tn=SSD_PROJ_TN, seq=lp)
    tail = tail[lp // SSD_PROJ_TM - 1::lp // SSD_PROJ_TM]
    xs = x_sample.reshape(bs * ls, d)
    cos_s, sin_s = _rope_tables(PAST_LEN + jnp.arange(ls, dtype=jnp.int32))
    proj_s = _ret_proj(xs, g_ret, ret_win, jnp.tile(cos_s, (bs, 1)), jnp.tile(sin_s, (bs, 1)), tm=bs * ls,
                       tn=RET_PROJ_TN)

    gated, ssm_pt, og_s, ret_s = _ssd_prompt_ret_sample(
        zxc.reshape(bp, lp, -1), dtr.reshape(bp, lp, LANES), dtb, alog, dsk, gn, expand, bp, lp,
        proj_s, lg, hn, state_ret, bs, ls)
    y_p = _out_proj(gated.reshape(bp * lp, SSM_DINNER), ssm_wout, h1, tm=OUT_PROJ_TM, final_g=g_fin)
    ssm_p = jnp.swapaxes(ssm_pt.reshape(1, bp, SSM_HEADS, SSM_HEADDIM, SSM_DSTATE), -1, -2)
    conv_p = tail[None, :, SUBLANES - (SSM_CONV - 1):, :]

    h1_s = _out_proj(og_s, ret_wout, xs, tm=SAMPLE_TM)
    zx_s, dtr_s = _norm_proj(h1_s, g_ssm, ssm_win_t, n=n_main, tm=bs * ls, tn=SSD_PROJ_TN, out_dtype=F32,
                             w_dt=ssm_wdt, w_is_t=True)
    st_t = jnp.swapaxes(state_ssm[0], -1, -2).reshape(bs, SSM_DINNER, SSM_DSTATE)
    gated_s, ssm_st, conv_st = _ssd_sample(zx_s, dtr_s, state_conv, st_t, cw, cb, dtb, alog, dsk, gn, expand, bs, ls)
    ssm_s = jnp.swapaxes(ssm_st.reshape(1, bs, SSM_HEADS, SSM_HEADDIM, SSM_DSTATE), -1, -2)
    conv_s = jnp.swapaxes(conv_st, 0, 1)[None]
    y_s = _out_proj(gated_s, ssm_wout, h1_s, tm=SAMPLE_TM, final_g=g_fin)

    return (y_p.reshape(bp, lp, d), y_s.reshape(bs, ls, d), ret_p, ret_s, ssm_p, ssm_s, conv_p, conv_s)
```

```python
import functools
import math

import jax
import jax.numpy as jnp
from jax import lax
from jax.experimental import pallas as pl
from jax.experimental.pallas import tpu as pltpu

F32 = jnp.float32
BF16 = jnp.bfloat16

D_MODEL = 1024
RET_HEADS = 4
RET_DK = 256
RET_DV = 512
RET_VW = RET_HEADS * RET_DV
RET_QK = RET_HEADS * RET_DK
SSM_DINNER = 2048
SSM_HEADDIM = 64
SSM_HEADS = 32
SSM_GROUPS = 8
SSM_DSTATE = 128
SSM_CONV = 4
SSM_CONVDIM = 4096
SSM_GROUP_W = SSM_DINNER // SSM_GROUPS
HEADS_PER_GROUP = SSM_HEADS // SSM_GROUPS
ROPE_BASE = 10000.0
EPS = 1e-6
PAST_LEN = 16384

LANES = 128
SUBLANES = 8
MXU_WIDTH = 256
VMEM_LIMIT = 56 * 1024 * 1024

RET_CHUNK = 256
RET_PROMPT_SEQS = 1
SSD_CHUNK = 128
SSD_PROMPT_SEQS = 2
SSD_SAMPLE_BB = 8
RET_PROJ_TN = 1024
SSD_PROJ_TM = 1024
SSD_PROJ_TN = 2048
OUT_PROJ_TM = 1024
SAMPLE_TM = 512


def _cparams(sem):
    return pltpu.CompilerParams(dimension_semantics=sem, vmem_limit_bytes=VMEM_LIMIT)


def _sigmoid(x):
    return 1.0 / (1.0 + jnp.exp2(x * (-math.log2(math.e))))


def _silu(x):
    return x * _sigmoid(x)


def _softplus(x):
    return jnp.maximum(x, 0.0) + jnp.log1p(jnp.exp(-jnp.abs(x)))


def _dot(a, b):
    return jnp.dot(a, b, preferred_element_type=F32)


def _dot_nt(a, b):
    return lax.dot_general(a, b, (((1,), (1,)), ((), ())), preferred_element_type=F32)


def _dot_tn(a, b):
    return lax.dot_general(a, b, (((0,), (0,)), ((), ())), preferred_element_type=F32)


def _iota(shape, dim):
    return lax.broadcasted_iota(jnp.int32, shape, dim)


def _norm_proj_kernel(x_ref, g_ref, w_ref, *rest, has_dt, w_is_t):
    if has_dt:
        wdt_ref, o_ref, dt_ref, xn_ref = rest
    else:
        o_ref, xn_ref = rest

    @pl.when(pl.program_id(1) == 0)
    def _():
        x = x_ref[...]
        ms = jnp.mean(x * x, axis=-1, keepdims=True)
        xn = (x * lax.rsqrt(ms + EPS) * g_ref[...]).astype(BF16)
        xn_ref[...] = xn
        if has_dt:
            dt_ref[...] = _dot(xn, wdt_ref[...])

    w = w_ref[...].astype(BF16)
    o_ref[...] = (_dot_nt(xn_ref[...], w) if w_is_t else _dot(xn_ref[...], w)).astype(o_ref.dtype)


def _norm_proj(x, g, w, *, n, tm, tn, out_dtype, w_dt=None, w_is_t=False):
    m, d = x.shape
    has_dt = w_dt is not None
    in_specs = [
        pl.BlockSpec((tm, d), lambda i, j: (i, 0)),
        pl.BlockSpec((1, d), lambda i, j: (0, 0)),
        pl.BlockSpec((tn, d), lambda i, j: (j, 0)) if w_is_t else pl.BlockSpec((d, tn), lambda i, j: (0, j)),
    ]
    args = [x, g, w]
    out_shape = [jax.ShapeDtypeStruct((m, n), out_dtype)]
    out_specs = [pl.BlockSpec((tm, tn), lambda i, j: (i, j))]
    if has_dt:
        in_specs.append(pl.BlockSpec((d, LANES), lambda i, j: (0, 0)))
        args.append(w_dt)
        out_shape.append(jax.ShapeDtypeStruct((m, LANES), F32))
        out_specs.append(pl.BlockSpec((tm, LANES), lambda i, j: (i, 0)))
    res = pl.pallas_call(
        functools.partial(_norm_proj_kernel, has_dt=has_dt, w_is_t=w_is_t),
        grid=(m // tm, n // tn),
        in_specs=in_specs,
        out_specs=out_specs,
        out_shape=out_shape,
        scratch_shapes=[pltpu.VMEM((tm, d), BF16)],
        compiler_params=_cparams(("arbitrary", "arbitrary")),
        name="in_proj",
    )(*args)
    return res if has_dt else res[0]


def _rope(x, cos, sin):
    half = RET_DK // 2
    x1 = x[:, :half]
    x2 = x[:, half:]
    return jnp.concatenate([x1 * cos - x2 * sin, x1 * sin + x2 * cos], axis=1)


def _ret_proj_kernel(x_ref, g_ref, w_ref, cos_ref, sin_ref, o_ref, xn_ref, *, rope_tiles):
    j = pl.program_id(1)
    tn = o_ref.shape[1]

    @pl.when(j == 0)
    def _():
        x = x_ref[...]
        ms = jnp.mean(x * x, axis=-1, keepdims=True)
        xn_ref[...] = (x * lax.rsqrt(ms + EPS) * g_ref[...]).astype(BF16)

    @pl.when(j < rope_tiles)
    def _():
        cos = cos_ref[...]
        sin = sin_ref[...]
        for h in range(tn // RET_DK):
            sl = slice(h * RET_DK, (h + 1) * RET_DK)
            y = _dot(xn_ref[...], w_ref[:, sl].astype(BF16))
            o_ref[:, sl] = _rope(y, cos, sin).astype(o_ref.dtype)

    @pl.when(j >= rope_tiles)
    def _():
        o_ref[...] = _dot(xn_ref[...], w_ref[...].astype(BF16)).astype(o_ref.dtype)


def _ret_proj(x, g, w, cos, sin, *, tm, tn):
    m, d = x.shape
    n = w.shape[1]
    period_tiles = max(cos.shape[0] // tm, 1)
    return pl.pallas_call(
        functools.partial(_ret_proj_kernel, rope_tiles=2 * RET_QK // tn),
        grid=(m // tm, n // tn),
        in_specs=[
            pl.BlockSpec((tm, d), lambda i, j: (i, 0)),
            pl.BlockSpec((1, d), lambda i, j: (0, 0)),
            pl.BlockSpec((d, tn), lambda i, j: (0, j)),
            pl.BlockSpec((tm, RET_DK // 2), lambda i, j: (i % period_tiles, 0)),
            pl.BlockSpec((tm, RET_DK // 2), lambda i, j: (i % period_tiles, 0)),
        ],
        out_specs=pl.BlockSpec((tm, tn), lambda i, j: (i, j)),
        out_shape=jax.ShapeDtypeStruct((m, n), BF16),
        scratch_shapes=[pltpu.VMEM((tm, d), BF16)],
        compiler_params=_cparams(("arbitrary", "arbitrary")),
        name="ret_in_proj",
    )(x, g, w, cos, sin)


def _shift_rows(a, first_rows):
    rows, cols = a.shape
    n = first_rows.shape[0]
    slabs = rows // SUBLANES
    a3 = a.reshape(slabs, SUBLANES, cols)
    first = jnp.concatenate([jnp.zeros((SUBLANES - n, cols), a.dtype), first_rows], axis=0)
    above = jnp.concatenate([first.reshape(1, SUBLANES, cols), a3[:slabs - 1]], axis=0)
    bottom = _iota((slabs, SUBLANES, cols), 1) >= SUBLANES - n
    return pltpu.roll(jnp.where(bottom, above, a3), n, axis=1).reshape(rows, cols)


def _norm_proj_conv_kernel(x_ref, g_ref, w_ref, wdt_ref, cw_ref, cb_ref, o_ref, dt_ref, tail_ref,
                           xn_ref, carry_ref, *, tiles_per_seq, z_tiles, strip):
    i = pl.program_id(0)
    j = pl.program_id(1)
    tm, tn = o_ref.shape

    @pl.when(j == 0)
    def _():
        x = x_ref[...]
        ms = jnp.mean(x * x, axis=-1, keepdims=True)
        xn = (x * lax.rsqrt(ms + EPS) * g_ref[...]).astype(BF16)
        xn_ref[...] = xn
        dt_ref[...] = _dot(xn, wdt_ref[...])

    @pl.when((j == 0) & (i % tiles_per_seq == 0))
    def _():
        carry_ref[...] = jnp.zeros_like(carry_ref)

    @pl.when(j < z_tiles)
    def _():
        o_ref[...] = _dot_nt(xn_ref[...], w_ref[...].astype(BF16)).astype(o_ref.dtype)

    @pl.when(j >= z_tiles)
    def _():
        jc = j - z_tiles
        for s in range(tn // strip):
            sl = slice(s * strip, (s + 1) * strip)
            x = _dot_nt(xn_ref[...], w_ref[sl, :].astype(BF16))
            tail_ref[:, sl] = x[tm - SUBLANES:tm, :]
            x2 = _shift_rows(x, carry_ref[jc, 1:3, sl])
            p = x * cw_ref[2:3, sl] + x2 * cw_ref[0:1, sl]
            out = x * cw_ref[3:4, sl] + x2 * cw_ref[1:2, sl] + _shift_rows(p, carry_ref[jc, 0:1, sl]) + cb_ref[:, sl]
            o_ref[:, sl] = _silu(out).astype(o_ref.dtype)
            carry_ref[jc, 0:1, sl] = p[tm - 1:tm]
            carry_ref[jc, 1:3, sl] = x[tm - 2:tm]


def _norm_proj_conv(x, g, wt, w_dt, cw, cb, *, tm, tn, seq):
    m, d = x.shape
    n = SSM_DINNER + SSM_CONVDIM
    z_tiles = SSM_DINNER // tn
    c_tiles = SSM_CONVDIM // tn
    tiles_per_seq = seq // tm
    cj = lambda j: jnp.maximum(j - z_tiles, 0)
    return pl.pallas_call(
        functools.partial(_norm_proj_conv_kernel, tiles_per_seq=tiles_per_seq, z_tiles=z_tiles,
                          strip=MXU_WIDTH),
        grid=(m // tm, n // tn),
        in_specs=[
            pl.BlockSpec((tm, d), lambda i, j: (i, 0)),
            pl.BlockSpec((1, d), lambda i, j: (0, 0)),
            pl.BlockSpec((tn, d), lambda i, j: (j, 0)),
            pl.BlockSpec((d, LANES), lambda i, j: (0, 0)),
            pl.BlockSpec((SSM_CONV, tn), lambda i, j: (0, cj(j))),
            pl.BlockSpec((1, tn), lambda i, j: (0, cj(j))),
        ],
        out_specs=[
            pl.BlockSpec((tm, tn), lambda i, j: (i, j)),
            pl.BlockSpec((tm, LANES), lambda i, j: (i, 0)),
            pl.BlockSpec((None, SUBLANES, tn), lambda i, j: (i, 0, cj(j))),
        ],
        out_shape=[
            jax.ShapeDtypeStruct((m, n), BF16),
            jax.ShapeDtypeStruct((m, LANES), F32),
            jax.ShapeDtypeStruct((m // tm, SUBLANES, SSM_CONVDIM), F32),
        ],
        scratch_shapes=[
            pltpu.VMEM((tm, d), BF16),
            pltpu.VMEM((c_tiles, SUBLANES, tn), F32),
        ],
        compiler_params=_cparams(("arbitrary", "arbitrary")),
        name="ssd_in_proj_conv",
    )(x, g, wt, w_dt, cw, cb)


def _out_proj_kernel(a_ref, w_ref, h_ref, *rest, has_norm):
    if has_norm:
        g_ref, o_ref = rest
    else:
        (o_ref,) = rest
    y = h_ref[...] + _dot(a_ref[...], w_ref[...])
    if has_norm:
        ms = jnp.mean(y * y, axis=-1, keepdims=True)
        y = y * lax.rsqrt(ms + EPS) * g_ref[...]
    o_ref[...] = y


def _out_proj(a, w, h, *, tm, final_g=None):
    m, k = a.shape
    d = w.shape[1]
    has_norm = final_g is not None
    in_specs = [
        pl.BlockSpec((tm, k), lambda i: (i, 0)),
        pl.BlockSpec((k, d), lambda i: (0, 0)),
        pl.BlockSpec((tm, d), lambda i: (i, 0)),
    ]
    args = [a, w, h]
    if has_norm:
        in_specs.append(pl.BlockSpec((1, d), lambda i: (0, 0)))
        args.append(final_g)
    return pl.pallas_call(
        functools.partial(_out_proj_kernel, has_norm=has_norm),
        grid=(m // tm,),
        in_specs=in_specs,
        out_specs=pl.BlockSpec((tm, d), lambda i: (i, 0)),
        out_shape=jax.ShapeDtypeStruct((m, d), F32),
        compiler_params=_cparams(("arbitrary",)),
        name="out_proj",
    )(*args)


def _head_norm_gate(o, hn, gate):
    ms = jnp.mean(o * o, axis=-1, keepdims=True)
    on = o * lax.rsqrt(ms + EPS) * hn
    return (on * _silu(gate.astype(F32))).astype(BF16)


def _ret_prompt_kernel(proj_ref, lg_ref, hn_ref,
                       og_ref, st_ref, s_ref, dec_ref, qsc_ref, ksc_ref, gc_ref):
    b = pl.program_id(0)
    c = pl.program_id(1)
    n_seq, rows = proj_ref.shape[0], proj_ref.shape[1]
    k0, v0, g0 = RET_QK, 2 * RET_QK, 2 * RET_QK + RET_VW

    @pl.when((b == 0) & (c == 0))
    def _():
        i = _iota((rows, rows), 0)
        j = _iota((rows, rows), 1)
        r = _iota((rows, RET_DK), 0)
        scale = RET_DK ** -0.5
        for h in range(RET_HEADS):
            lg = lg_ref[h, :, 0:1]
            dec_ref[h] = jnp.where(i >= j, jnp.exp((i - j).astype(F32) * lg), 0.0) * scale
            qsc_ref[h] = jnp.exp((r + 1).astype(F32) * lg)
            ksc_ref[h] = jnp.exp((rows - 1 - r).astype(F32) * lg) * scale
            gc_ref[h] = jnp.exp(jnp.full((1, RET_DV), float(rows), F32) * lg)

    @pl.when(c == 0)
    def _():
        s_ref[...] = jnp.zeros_like(s_ref)

    for n in range(n_seq):
        for h in range(RET_HEADS):
            ksl = slice(h * RET_DK, (h + 1) * RET_DK)
            vsl = slice(h * RET_DV, (h + 1) * RET_DV)
            q = proj_ref[n, :, ksl]
            k = proj_ref[n, :, k0 + h * RET_DK:k0 + (h + 1) * RET_DK]
            v = proj_ref[n, :, v0 + h * RET_DV:v0 + (h + 1) * RET_DV]
            gate = proj_ref[n, :, g0 + h * RET_DV:g0 + (h + 1) * RET_DV]
            p = (_dot_nt(q, k) * dec_ref[h]).astype(BF16)
            qs = (q.astype(F32) * qsc_ref[h]).astype(BF16)
            kr = k.astype(F32)
            s = s_ref[n, h]
            o = _dot(p, v) + _dot(qs, s.astype(BF16))
            kt = (kr * ksc_ref[h]).T.astype(BF16)
            s_ref[n, h] = s * gc_ref[h] + _dot(kt, v)
            og_ref[n, :, vsl] = _head_norm_gate(o, hn_ref[:, vsl], gate)

    @pl.when(c == pl.num_programs(1) - 1)
    def _():
        st_ref[...] = s_ref[...]


def _ret_prompt(proj, lg, hn, batch, seq):
    rows = RET_CHUNK
    nc = seq // rows
    ns = RET_PROMPT_SEQS
    return pl.pallas_call(
        _ret_prompt_kernel,
        grid=(batch // ns, nc),
        in_specs=[
            pl.BlockSpec((ns, rows, 2 * RET_QK + 2 * RET_VW), lambda b, c: (b, c, 0)),
            pl.BlockSpec((RET_HEADS, 1, LANES), lambda b, c: (0, 0, 0)),
            pl.BlockSpec((1, RET_VW), lambda b, c: (0, 0)),
        ],
        out_specs=[
            pl.BlockSpec((ns, rows, RET_VW), lambda b, c: (b, c, 0)),
            pl.BlockSpec((None, ns, RET_HEADS, RET_DK, RET_DV), lambda b, c: (0, b, 0, 0, 0)),
        ],
        out_shape=[
            jax.ShapeDtypeStruct((batch, seq, RET_VW), BF16),
            jax.ShapeDtypeStruct((1, batch, RET_HEADS, RET_DK, RET_DV), F32),
        ],
        scratch_shapes=[
            pltpu.VMEM((ns, RET_HEADS, RET_DK, RET_DV), F32),
            pltpu.VMEM((RET_HEADS, rows, rows), F32),
            pltpu.VMEM((RET_HEADS, rows, RET_DK), F32),
            pltpu.VMEM((RET_HEADS, rows, RET_DK), F32),
            pltpu.VMEM((RET_HEADS, 1, RET_DV), F32),
        ],
        compiler_params=_cparams(("arbitrary", "arbitrary")),
        name="ret_prompt",
    )(proj, lg, hn)


def _ret_sample_kernel(q_ref, k_ref, v_ref, gate_ref, lg_ref, hn_ref, st_in_ref,
                       og_ref, st_out_ref, *, seq):
    bb = st_in_ref.shape[0]
    rows = bb * seq
    lg = lg_ref[:, 0:1]
    scale = RET_DK ** -0.5
    i = _iota((rows, rows), 0)
    j = _iota((rows, rows), 1)
    same = (i // seq) == (j // seq)
    dec = jnp.where(same & (i >= j), jnp.exp((i - j).astype(F32) * lg), 0.0) * scale
    pos = _iota((rows, RET_DK), 0) % seq
    qsc = jnp.exp((pos + 1).astype(F32) * lg)
    ksc = jnp.exp((seq - 1 - pos).astype(F32) * lg) * scale
    gc = jnp.exp(jnp.full((1, RET_DV), float(seq), F32) * lg)

    qr = q_ref[...].astype(F32)
    kr = k_ref[...].astype(F32)
    v = v_ref[...].astype(F32)
    p = _dot_nt(q_ref[...], k_ref[...]) * dec
    intra = _dot(p.astype(BF16), v.astype(BF16))
    qs = qr * qsc
    kt = kr * ksc
    vrow = _iota((rows, RET_DV), 0) // seq
    cross = []
    for n in range(bb):
        s0 = st_in_ref[n]
        cross.append(_dot(qs[n * seq:(n + 1) * seq], s0))
        vn = jnp.where(vrow == n, v, 0.0)
        st_out_ref[n] = s0 * gc + _dot_tn(kt.astype(BF16), vn.astype(BF16))
    o = intra + jnp.concatenate(cross, axis=0)
    og_ref[...] = _head_norm_gate(o, hn_ref[...], gate_ref[...])


def _head_expand(vals, expand):
    parts = []
    for v in vals:
        hi = v.astype(BF16)
        lo = (v - hi.astype(F32)).astype(BF16)
        parts.append(jnp.concatenate([hi, lo], axis=1))
    return _dot(jnp.concatenate(parts, axis=0), expand)


def _ssd_prompt_kernel(zxc_ref, dtr_ref, dtr_next_ref, dtb_ref, alog_ref, dsk_ref, gn_ref, exp_ref,
                       g_ref, st_ref, s_ref, ex_ref, cum2_ref, cum2t_ref):
    c = pl.program_id(1)
    n_seq, rows = zxc_ref.shape[0], zxc_ref.shape[1]
    x0, b0, c0 = SSM_DINNER, 2 * SSM_DINNER, 2 * SSM_DINNER + SSM_GROUPS * SSM_DSTATE
    bw = SSM_GROUP_W
    ri = _iota((rows, rows), 0)
    rj = _iota((rows, rows), 1)
    causal = ri >= rj

    def decay_tables(n, dtr):
        dt = _softplus(dtr + dtb_ref[...])
        la = dt * (-jnp.exp(alog_ref[...]))
        cum = jnp.dot(causal.astype(F32), la, precision=lax.Precision.HIGHEST, preferred_element_type=F32)
        ecum = jnp.exp(cum)
        wdt = jnp.exp(cum[rows - 1:rows, :] - cum) * dt
        ex_ref[n] = _head_expand([ecum, wdt, dt], exp_ref[...])
        cum2 = cum * math.log2(math.e)
        cum2_ref[n] = cum2
        cum2t_ref[n] = cum2.T

    @pl.when(c == 0)
    def _():
        s_ref[...] = jnp.zeros_like(s_ref)
        for n in range(n_seq):
            decay_tables(n, dtr_ref[n])

    head_lane = _iota((rows, bw), 1) // SSM_HEADDIM

    for n in range(n_seq):
        cum2 = cum2_ref[n]
        cum2_t = cum2t_ref[n]
        for g in range(SSM_GROUPS):
            gsl = slice(g * bw, (g + 1) * bw)
            bm = zxc_ref[n, :, b0 + g * SSM_DSTATE:b0 + (g + 1) * SSM_DSTATE]
            cm = zxc_ref[n, :, c0 + g * SSM_DSTATE:c0 + (g + 1) * SSM_DSTATE]
            xs = zxc_ref[n, :, x0 + g * bw:x0 + (g + 1) * bw].astype(F32)
            s = s_ref[n, g]
            gmat = _dot_nt(cm, bm)
            ps = []
            for hh in range(HEADS_PER_GROUP):
                h = g * HEADS_PER_GROUP + hh
                colb = jnp.broadcast_to(cum2[:, h:h + 1], (rows, LANES))
                dec = jnp.exp2(jnp.where(causal, colb - cum2_t[h:h + 1, :], -jnp.inf))
                ps.append((gmat * dec).astype(BF16))
            v = (xs * ex_ref[n, 2 * rows:3 * rows, gsl]).astype(BF16)
            y4 = _dot(jnp.concatenate(ps, axis=0), v)
            y = y4[0:rows]
            for hh in range(1, HEADS_PER_GROUP):
                y = jnp.where(head_lane == hh, y4[hh * rows:(hh + 1) * rows], y)
            y = y + _dot(cm, s.astype(BF16)) * ex_ref[n, 0:rows, gsl]
            vw = (xs * ex_ref[n, rows:2 * rows, gsl]).astype(BF16)
            s_ref[n, g] = s * ex_ref[n, rows - 1:rows, gsl] + _dot_tn(bm, vw)

            y = y + xs * dsk_ref[:, gsl]
            gg = y * _silu(zxc_ref[n, :, gsl].astype(F32))
            ms = jnp.mean(gg * gg, axis=-1, keepdims=True)
            g_ref[n, :, gsl] = (gg * lax.rsqrt(ms + EPS) * gn_ref[:, gsl]).astype(BF16)

        decay_tables(n, dtr_next_ref[n])

    @pl.when(c == pl.num_programs(1) - 1)
    def _():
        for n in range(n_seq):
            for g in range(SSM_GROUPS):
                st_ref[n, g * bw:(g + 1) * bw, :] = s_ref[n, g].T


def _ssd_prompt_ret_sample_kernel(*refs, seq_s):
    ssd_in, ret_in = refs[0:8], refs[8:15]
    ssd_out, ret_out = refs[15:17], refs[17:19]
    ssd_scratch = refs[19:23]
    _ssd_prompt_kernel(*ssd_in, *ssd_out, *ssd_scratch)
    _ret_sample_kernel(*ret_in, *ret_out, seq=seq_s)


def _ssd_prompt_ret_sample(zxc, dtr, dtb, alog, dsk, gn, expand, batch, seq,
                           proj_s, lg, hn, state_ret, batch_s, seq_s):
    rows = SSD_CHUNK
    nc = seq // rows
    ns = SSD_PROMPT_SEQS
    steps = (batch // ns) * nc
    assert steps % RET_HEADS == 0 and batch_s % (steps // RET_HEADS) == 0
    bb = batch_s // (steps // RET_HEADS)
    rows_s = bb * seq_s
    full = lambda shape: pl.BlockSpec(shape, lambda b, c: (0,) * len(shape))
    sn = lambda b, c: (b * nc + c) // RET_HEADS
    sh = lambda b, c: (b * nc + c) % RET_HEADS
    qb = RET_QK // RET_DK
    vb = 2 * RET_QK // RET_DV
    gb = vb + RET_VW // RET_DV
    st_spec = pl.BlockSpec((None, bb, None, RET_DK, RET_DV), lambda b, c: (0, sn(b, c), sh(b, c), 0, 0))
    return pl.pallas_call(
        functools.partial(_ssd_prompt_ret_sample_kernel, seq_s=seq_s),
        grid=(batch // ns, nc),
        in_specs=[
            pl.BlockSpec((ns, rows, SSM_DINNER + SSM_CONVDIM), lambda b, c: (b, c, 0)),
            pl.BlockSpec((ns, rows, LANES), lambda b, c: (b, c, 0)),
            pl.BlockSpec((ns, rows, LANES), lambda b, c: (b, jnp.minimum(c + 1, nc - 1), 0)),
            full((1, LANES)), full((1, LANES)),
            full((1, SSM_DINNER)), full((1, SSM_DINNER)),
            full((2 * LANES, SSM_DINNER)),
            pl.BlockSpec((rows_s, RET_DK), lambda b, c: (sn(b, c), sh(b, c))),
            pl.BlockSpec((rows_s, RET_DK), lambda b, c: (sn(b, c), qb + sh(b, c))),
            pl.BlockSpec((rows_s, RET_DV), lambda b, c: (sn(b, c), vb + sh(b, c))),
            pl.BlockSpec((rows_s, RET_DV), lambda b, c: (sn(b, c), gb + sh(b, c))),
            pl.BlockSpec((None, 1, LANES), lambda b, c: (sh(b, c), 0, 0)),
            pl.BlockSpec((1, RET_DV), lambda b, c: (0, sh(b, c))),
            st_spec,
        ],
        out_specs=[
            pl.BlockSpec((ns, rows, SSM_DINNER), lambda b, c: (b, c, 0)),
            pl.BlockSpec((ns, SSM_DINNER, SSM_DSTATE), lambda b, c: (b, 0, 0)),
            pl.BlockSpec((rows_s, RET_DV), lambda b, c: (sn(b, c), sh(b, c))),
            st_spec,
        ],
        out_shape=[
            jax.ShapeDtypeStruct((batch, seq, SSM_DINNER), BF16),
            jax.ShapeDtypeStruct((batch, SSM_DINNER, SSM_DSTATE), F32),
            jax.ShapeDtypeStruct((batch_s * seq_s, RET_VW), BF16),
            jax.ShapeDtypeStruct(state_ret.shape, F32),
        ],
        scratch_shapes=[
            pltpu.VMEM((ns, SSM_GROUPS, SSM_DSTATE, SSM_GROUP_W), F32),
            pltpu.VMEM((ns, 3 * rows, SSM_DINNER), F32),
            pltpu.VMEM((ns, rows, LANES), F32),
            pltpu.VMEM((ns, LANES, rows), F32),
        ],
        compiler_params=_cparams(("arbitrary", "arbitrary")),
        name="ssd_prompt_ret_sample",
    )(zxc, dtr, dtr, dtb, alog, dsk, gn, expand, proj_s, proj_s, proj_s, proj_s, lg, hn, state_ret)


def _ssd_sample_kernel(z_ref, xa_ref, xb_ref, dtr_ref, conv_in_ref, st_in_ref,
                       cw_ref, cb_ref, dtb_ref, alog_ref, dsk_ref, gn_ref, exp_ref,
                       g_ref, st_out_ref, conv_out_ref, xc_ref, ex_ref, y_ref, *, seq):
    bb = st_in_ref.shape[0]
    rows = bb * seq
    keep = SSM_CONV - 1
    bw = SSM_GROUP_W

    assert seq == SUBLANES
    sub = _iota((bb, SUBLANES, SSM_DINNER), 1)
    hist_pad = jnp.zeros((bb, SUBLANES - keep, SSM_DINNER), F32)
    for half, x_ref in enumerate((xa_ref, xb_ref)):
        csl = slice(half * SSM_DINNER, (half + 1) * SSM_DINNER)
        w0, w1, w2, w3 = (cw_ref[k:k + 1, csl] for k in range(SSM_CONV))
        x = x_ref[...].reshape(bb, SUBLANES, SSM_DINNER)
        hist = jnp.concatenate([hist_pad, conv_in_ref[:, :, csl]], axis=1)
        for k in range(keep):
            conv_out_ref[k, :, csl] = x[:, SUBLANES - keep + k, :]
        x2 = pltpu.roll(jnp.where(sub >= SUBLANES - 2, hist, x), 2, axis=1)
        p = x * w2 + x2 * w0
        p_hist = hist * w2 + pltpu.roll(hist, 2, axis=1) * w0
        p1 = pltpu.roll(jnp.where(sub >= SUBLANES - 1, p_hist, p), 1, axis=1)
        out = x * w3 + x2 * w1 + p1 + cb_ref[:, csl]
        xc_ref[:, csl] = _silu(out).reshape(rows, SSM_DINNER)

    dt = _softplus(dtr_ref[...] + dtb_ref[...])
    la = dt * (-jnp.exp(alog_ref[...]))
    ri = _iota((rows, rows), 0)
    rj = _iota((rows, rows), 1)
    same = (ri // seq) == (rj // seq)
    cum = jnp.dot((same & (ri >= rj)).astype(F32), la, precision=lax.Precision.HIGHEST,
                  preferred_element_type=F32)
    clast = jnp.dot(same.astype(F32), la, precision=lax.Precision.HIGHEST, preferred_element_type=F32)
    eclast = jnp.exp(clast)
    ex_ref[...] = _head_expand([jnp.exp(cum), jnp.exp(clast - cum) * dt, dt], exp_ref[...])

    si = _iota((seq, seq), 0)
    sj = _iota((seq, seq), 1)
    causal = si >= sj
    eye = si == sj
    head_lane = _iota((seq, bw), 1) // SSM_HEADDIM

    for n in range(bb):
        r0 = n * seq
        rsl = slice(r0, r0 + seq)
        cum_n = cum[rsl]
        for g in range(SSM_GROUPS):
            gsl = slice(g * bw, (g + 1) * bw)
            bm = xc_ref[rsl, SSM_DINNER + g * SSM_DSTATE:SSM_DINNER + (g + 1) * SSM_DSTATE]
            cm = xc_ref[rsl, SSM_DINNER + (SSM_GROUPS + g) * SSM_DSTATE:
                        SSM_DINNER + (SSM_GROUPS + g + 1) * SSM_DSTATE]
            xs = xc_ref[rsl, gsl]
            gmat = _dot_nt(cm.astype(BF16), bm.astype(BF16))
            ps = []
            for hh in range(HEADS_PER_GROUP):
                h = g * HEADS_PER_GROUP + hh
                ccol = cum_n[:, h:h + 1]
                crow = jnp.sum(jnp.where(eye, ccol, 0.0), axis=0, keepdims=True)
                ps.append(gmat * jnp.exp(jnp.where(causal, ccol - crow, -jnp.inf)))
            v = xs * ex_ref[2 * rows + r0:2 * rows + r0 + seq, gsl]
            y4 = _dot(jnp.concatenate(ps, axis=0).astype(BF16), v.astype(BF16))
            y = y4[0:seq]
            for hh in range(1, HEADS_PER_GROUP):
                y = jnp.where(head_lane == hh, y4[hh * seq:(hh + 1) * seq], y)
            st = st_in_ref[n, gsl, :]
            y = y + _dot_nt(cm, st) * ex_ref[rsl, gsl]
            y_ref[rsl, gsl] = y
            vw = xs * ex_ref[rows + r0:rows + r0 + seq, gsl]
            upd = _dot_tn(vw.astype(BF16), bm.astype(BF16))
            for hh in range(HEADS_PER_GROUP):
                h = g * HEADS_PER_GROUP + hh
                hsl = slice(hh * SSM_HEADDIM, (hh + 1) * SSM_HEADDIM)
                osl = slice(g * bw + hh * SSM_HEADDIM, g * bw + (hh + 1) * SSM_HEADDIM)
                st_out_ref[n, osl, :] = st[hsl] * eclast[r0:r0 + 1, h:h + 1] + upd[hsl]

    y = y_ref[...] + xc_ref[:, 0:SSM_DINNER] * dsk_ref[...]
    gg = y * _silu(z_ref[...])
    for g in range(SSM_GROUPS):
        gsl = slice(g * bw, (g + 1) * bw)
        blk = gg[:, gsl]
        ms = jnp.mean(blk * blk, axis=-1, keepdims=True)
        g_ref[:, gsl] = (blk * lax.rsqrt(ms + EPS) * gn_ref[:, gsl]).astype(BF16)


def _ssd_sample(zx, dtr, conv_state, ssm_state_t, cw, cb, dtb, alog, dsk, gn, expand, batch, seq):
    bb = SSD_SAMPLE_BB
    rows = bb * seq
    keep = SSM_CONV - 1
    full = lambda shape: pl.BlockSpec(shape, lambda n: (0,) * len(shape))
    col = lambda k: pl.BlockSpec((rows, SSM_DINNER), lambda n: (n, k))
    st_spec = pl.BlockSpec((bb, SSM_DINNER, SSM_DSTATE), lambda n: (n, 0, 0))
    return pl.pallas_call(
        functools.partial(_ssd_sample_kernel, seq=seq),
        grid=(batch // bb,),
        in_specs=[
            col(0), col(1), col(2),
            pl.BlockSpec((rows, LANES), lambda n: (n, 0)),
            pl.BlockSpec((None, bb, keep, SSM_CONVDIM), lambda n: (0, n, 0, 0)),
            st_spec,
            full((SSM_CONV, SSM_CONVDIM)), full((1, SSM_CONVDIM)),
            full((1, LANES)), full((1, LANES)),
            full((1, SSM_DINNER)), full((1, SSM_DINNER)),
            full((2 * LANES, SSM_DINNER)),
        ],
        out_specs=[
            pl.BlockSpec((rows, SSM_DINNER), lambda n: (n, 0)),
            st_spec,
            pl.BlockSpec((keep, bb, SSM_CONVDIM), lambda n: (0, n, 0)),
        ],
        out_shape=[
            jax.ShapeDtypeStruct((batch * seq, SSM_DINNER), BF16),
            jax.ShapeDtypeStruct(ssm_state_t.shape, F32),
            jax.ShapeDtypeStruct((keep, batch, SSM_CONVDIM), F32),
        ],
        scratch_shapes=[
            pltpu.VMEM((rows, SSM_CONVDIM), F32),
            pltpu.VMEM((3 * rows, SSM_DINNER), F32),
            pltpu.VMEM((rows, SSM_DINNER), F32),
        ],
        compiler_params=_cparams(("arbitrary",)),
        name="ssd_sample",
    )(zx, zx, zx, dtr, conv_state, ssm_state_t, cw, cb, dtb, alog, dsk, gn, expand)


def _rope_tables(pos):
    half = RET_DK // 2
    freqs = ROPE_BASE ** (-jnp.arange(half, dtype=F32) / half)
    ang = pos.astype(F32)[:, None] * freqs[None, :]
    return jnp.cos(ang), jnp.sin(ang)


def _row(v):
    return v.reshape(1, -1).astype(F32)


def _pad_lanes(v):
    return jnp.pad(v.astype(F32), (0, LANES - v.shape[0])).reshape(1, LANES)


def kernel(x_prompt, x_sample, state_ret, state_ssm, state_conv, ret_norm, ret_w_in, ret_head_norm, ret_w_out, ssm_norm, ssm_w_in, ssm_conv_w, ssm_conv_b, ssm_dt_bias, ssm_a_log, ssm_d, ssm_gate_norm, ssm_w_out, final_norm):
    bp, lp, d = x_prompt.shape
    bs, ls, _ = x_sample.shape
    assert ret_norm.shape[0] == 1 and ssm_norm.shape[0] == 1, "one retention and one SSD layer"

    ret_win = ret_w_in[0]
    ret_wout = ret_w_out[0].astype(BF16)
    n_main = SSM_DINNER + SSM_CONVDIM
    ssm_win_t = jnp.swapaxes(ssm_w_in[0], 0, 1)
    ssm_wdt = jnp.pad(ssm_w_in[0, :, n_main:], ((0, 0), (0, LANES - SSM_HEADS))).astype(BF16)
    ssm_wout = ssm_w_out[0].astype(BF16)
    expand = (jnp.arange(2 * LANES)[:, None] % LANES == jnp.arange(SSM_DINNER)[None, :] // SSM_HEADDIM).astype(BF16)
    lg = jnp.log(1.0 - 2.0 ** (-5.0 - jnp.arange(RET_HEADS, dtype=F32)))
    lg = jnp.broadcast_to(lg[:, None, None], (RET_HEADS, 1, LANES))
    hn = _row(ret_head_norm[0])
    dsk = _row(jnp.repeat(ssm_d[0], SSM_HEADDIM))
    gn = _row(ssm_gate_norm[0])
    cw = ssm_conv_w[0].astype(F32)
    cb = _row(ssm_conv_b[0])
    dtb = _pad_lanes(ssm_dt_bias[0])
    alog = _pad_lanes(ssm_a_log[0])
    g_ret = _row(ret_norm[0])
    g_ssm = _row(ssm_norm[0])
    g_fin = _row(final_norm)

    xp = x_prompt.reshape(bp * lp, d)
    cos_p, sin_p = _rope_tables(jnp.arange(lp, dtype=jnp.int32))
    n_ret = 2 * RET_QK + 2 * RET_VW
    proj = _ret_proj(xp, g_ret, ret_win, cos_p, sin_p, tm=lp, tn=RET_PROJ_TN)
    og, ret_p = _ret_prompt(proj.reshape(bp, lp, n_ret), lg, hn, bp, lp)
    h1 = _out_proj(og.reshape(bp * lp, RET_VW), ret_wout, xp, tm=OUT_PROJ_TM)
    zxc, dtr, tail = _norm_proj_conv(h1, g_ssm, ssm_win_t, ssm_wdt, cw, cb, tm=SSD_PROJ_TM, tn=SSD_PROJ_TN, seq=lp)
    tail = tail[lp // SSD_PROJ_TM - 1::lp // SSD_PROJ_TM]
    xs = x_sample.reshape(bs * ls, d)
    cos_s, sin_s = _rope_tables(PAST_LEN + jnp.arange(ls, dtype=jnp.int32))
    proj_s = _ret_proj(xs, g_ret, ret_win, jnp.tile(cos_s, (bs, 1)), jnp.tile(sin_s, (bs, 1)), tm=bs * ls,
                       tn=RET_PROJ_TN)

    gated, ssm_pt, og_s, ret_s = _ssd_prompt_ret_sample(
        zxc.reshape(bp, lp, -1), dtr.reshape(bp, lp, LANES), dtb, alog, dsk, gn, expand, bp, lp,
        proj_s, lg, hn, state_ret, bs, ls)
    y_p = _out_proj(gated.reshape(bp * lp, SSM_DINNER), ssm_wout, h1, tm=OUT_PROJ_TM, final_g=g_fin)
    ssm_p = jnp.swapaxes(ssm_pt.reshape(1, bp, SSM_HEADS, SSM_HEADDIM, SSM_DSTATE), -1, -2)
    conv_p = tail[None, :, SUBLANES - (SSM_CONV - 1):, :]

    h1_s = _out_proj(og_s, ret_wout, xs, tm=SAMPLE_TM)
    zx_s, dtr_s = _norm_proj(h1_s, g_ssm, ssm_win_t, n=n_main, tm=bs * ls, tn=SSD_PROJ_TN, out_dtype=F32,
                             w_dt=ssm_wdt, w_is_t=True)
    st_t = jnp.swapaxes(state_ssm[0], -1, -2).reshape(bs, SSM_DINNER, SSM_DSTATE)
    gated_s, ssm_st, conv_st = _ssd_sample(zx_s, dtr_s, state_conv, st_t, cw, cb, dtb, alog, dsk, gn, expand, bs, ls)
    ssm_s = jnp.swapaxes(ssm_st.reshape(1, bs, SSM_HEADS, SSM_HEADDIM, SSM_DSTATE), -1, -2)
    conv_s = jnp.swapaxes(conv_st, 0, 1)[None]
    y_s = _out_proj(gated_s, ssm_wout, h1_s, tm=SAMPLE_TM, final_g=g_fin)

    return (y_p.reshape(bp, lp, d), y_s.reshape(bs, ls, d), ret_p, ret_s, ssm_p, ssm_s, conv_p, conv_s)
```

```python
import functools
import math

import jax
import jax.numpy as jnp
from jax import lax
from jax.experimental import pallas as pl
from jax.experimental.pallas import tpu as pltpu

F32 = jnp.float32
BF16 = jnp.bfloat16

D_MODEL = 1024
RET_HEADS = 4
RET_DK = 256
RET_DV = 512
RET_VW = RET_HEADS * RET_DV
RET_QK = RET_HEADS * RET_DK
SSM_DINNER = 2048
SSM_HEADDIM = 64
SSM_HEADS = 32
SSM_GROUPS = 8
SSM_DSTATE = 128
SSM_CONV = 4
SSM_CONVDIM = 4096
SSM_GROUP_W = SSM_DINNER // SSM_GROUPS
HEADS_PER_GROUP = SSM_HEADS // SSM_GROUPS
ROPE_BASE = 10000.0
EPS = 1e-6
PAST_LEN = 16384

LANES = 128
SUBLANES = 8
MXU_WIDTH = 256
VMEM_LIMIT = 56 * 1024 * 1024

RET_CHUNK = 256
RET_PROMPT_SEQS = 2
SSD_CHUNK = 128
SSD_PROMPT_SEQS = 2
SSD_SAMPLE_BB = 8
RET_PROJ_TN = 1024
SSD_PROJ_TM = 1024
SSD_PROJ_TN = 2048
OUT_PROJ_TM = 1024
SAMPLE_TM = 512


def _cparams(sem):
    return pltpu.CompilerParams(dimension_semantics=sem, vmem_limit_bytes=VMEM_LIMIT)


def _sigmoid(x):
    return 1.0 / (1.0 + jnp.exp2(x * (-math.log2(math.e))))


def _silu(x):
    return x * _sigmoid(x)


def _softplus(x):
    return jnp.maximum(x, 0.0) + jnp.log1p(jnp.exp(-jnp.abs(x)))


def _dot(a, b):
    return jnp.dot(a, b, preferred_element_type=F32)


def _dot_nt(a, b):
    return lax.dot_general(a, b, (((1,), (1,)), ((), ())), preferred_element_type=F32)


def _dot_tn(a, b):
    return lax.dot_general(a, b, (((0,), (0,)), ((), ())), preferred_element_type=F32)


def _iota(shape, dim):
    return lax.broadcasted_iota(jnp.int32, shape, dim)


def _ssd_proj_kernel(x_ref, g_ref, w_ref, wdt_ref, o_ref, dt_ref, xn_ref):
    @pl.when(pl.program_id(1) == 0)
    def _():
        x = x_ref[...]
        ms = jnp.mean(x * x, axis=-1, keepdims=True)
        xn = (x * lax.rsqrt(ms + EPS) * g_ref[...]).astype(BF16)
        xn_ref[...] = xn
        dt_ref[...] = _dot(xn, wdt_ref[...])

    o_ref[...] = _dot_nt(xn_ref[...], w_ref[...].astype(BF16))


def _ssd_proj(x, g, wt, w_dt, *, n, tm, tn):
    m, d = x.shape
    return pl.pallas_call(
        _ssd_proj_kernel,
        grid=(m // tm, n // tn),
        in_specs=[
            pl.BlockSpec((tm, d), lambda i, j: (i, 0)),
            pl.BlockSpec((1, d), lambda i, j: (0, 0)),
            pl.BlockSpec((tn, d), lambda i, j: (j, 0)),
            pl.BlockSpec((d, LANES), lambda i, j: (0, 0)),
        ],
        out_specs=[
            pl.BlockSpec((tm, tn), lambda i, j: (i, j)),
            pl.BlockSpec((tm, LANES), lambda i, j: (i, 0)),
        ],
        out_shape=[
            jax.ShapeDtypeStruct((m, n), F32),
            jax.ShapeDtypeStruct((m, LANES), F32),
        ],
        scratch_shapes=[pltpu.VMEM((tm, d), BF16)],
        compiler_params=_cparams(("arbitrary", "arbitrary")),
        name="ssd_in_proj",
    )(x, g, wt, w_dt)


def _rope(x, cos, sin):
    half = RET_DK // 2
    x1 = x[:, :half]
    x2 = x[:, half:]
    return jnp.concatenate([x1 * cos - x2 * sin, x1 * sin + x2 * cos], axis=1)


def _ret_proj_kernel(x_ref, g_ref, w_ref, cos_ref, sin_ref, lg_ref, o_ref, xn_ref, *, chunk):
    j = pl.program_id(1)
    tm, tn = o_ref.shape

    @pl.when(j == 0)
    def _():
        x = x_ref[...]
        ms = jnp.mean(x * x, axis=-1, keepdims=True)
        xn_ref[...] = (x * lax.rsqrt(ms + EPS) * g_ref[...]).astype(BF16)

    def rotated(sign, mult):
        cos = cos_ref[...]
        sin = sin_ref[...]
        pos = (_iota((chunk, RET_DK // 2), 0) + 1).astype(F32)
        for h in range(tn // RET_DK):
            sl = slice(h * RET_DK, (h + 1) * RET_DK)
            decay = jnp.tile(jnp.exp(pos * (lg_ref[h, :, 0:1] * sign)) * mult, (tm // chunk, 1))
            y = _dot(xn_ref[...], w_ref[:, sl].astype(BF16))
            o_ref[:, sl] = _rope(y, cos * decay, sin * decay).astype(o_ref.dtype)

    @pl.when(j == 0)
    def _():
        rotated(1.0, 1.0)

    @pl.when(j == 1)
    def _():
        rotated(-1.0, RET_DK ** -0.5)

    @pl.when(j >= 2)
    def _():
        o_ref[...] = _dot(xn_ref[...], w_ref[...].astype(BF16)).astype(o_ref.dtype)


def _ret_proj(x, g, w, cos, sin, lg, *, tm, tn, chunk):
    m, d = x.shape
    n = w.shape[1]
    assert cos.shape == (tm, RET_DK // 2) and sin.shape == cos.shape
    assert tn == RET_QK and tm % chunk == 0
    return pl.pallas_call(
        functools.partial(_ret_proj_kernel, chunk=chunk),
        grid=(m // tm, n // tn),
        in_specs=[
            pl.BlockSpec((tm, d), lambda i, j: (i, 0)),
            pl.BlockSpec((1, d), lambda i, j: (0, 0)),
            pl.BlockSpec((d, tn), lambda i, j: (0, j)),
            pl.BlockSpec((tm, RET_DK // 2), lambda i, j: (0, 0)),
            pl.BlockSpec((tm, RET_DK // 2), lambda i, j: (0, 0)),
            pl.BlockSpec((RET_HEADS, 1, LANES), lambda i, j: (0, 0, 0)),
        ],
        out_specs=pl.BlockSpec((tm, tn), lambda i, j: (i, j)),
        out_shape=jax.ShapeDtypeStruct((m, n), BF16),
        scratch_shapes=[pltpu.VMEM((tm, d), BF16)],
        compiler_params=_cparams(("arbitrary", "arbitrary")),
        name="ret_in_proj",
    )(x, g, w, cos, sin, lg)


def _shift_rows(a, first_rows):
    rows, cols = a.shape
    n = first_rows.shape[0]
    slabs = rows // SUBLANES
    a3 = a.reshape(slabs, SUBLANES, cols)
    first = jnp.concatenate([jnp.zeros((SUBLANES - n, cols), a.dtype), first_rows], axis=0)
    above = jnp.concatenate([first.reshape(1, SUBLANES, cols), a3[:slabs - 1]], axis=0)
    bottom = _iota((slabs, SUBLANES, cols), 1) >= SUBLANES - n
    return pltpu.roll(jnp.where(bottom, above, a3), n, axis=1).reshape(rows, cols)


def _norm_proj_conv_kernel(x_ref, g_ref, w_ref, wdt_ref, cw_ref, cb_ref, o_ref, dt_ref, tail_ref,
                           xn_ref, carry_ref, *, tiles_per_seq, z_tiles, strip):
    i = pl.program_id(0)
    j = pl.program_id(1)
    tm, tn = o_ref.shape

    @pl.when(j == 0)
    def _():
        x = x_ref[...]
        ms = jnp.mean(x * x, axis=-1, keepdims=True)
        xn = (x * lax.rsqrt(ms + EPS) * g_ref[...]).astype(BF16)
        xn_ref[...] = xn
        dt_ref[...] = _dot(xn, wdt_ref[...])

    @pl.when((j == 0) & (i % tiles_per_seq == 0))
    def _():
        carry_ref[...] = jnp.zeros_like(carry_ref)

    @pl.when(j < z_tiles)
    def _():
        o_ref[...] = _dot_nt(xn_ref[...], w_ref[...].astype(BF16)).astype(o_ref.dtype)

    @pl.when(j >= z_tiles)
    def _():
        jc = j - z_tiles
        for s in range(tn // strip):
            sl = slice(s * strip, (s + 1) * strip)
            x = _dot_nt(xn_ref[...], w_ref[sl, :].astype(BF16))
            tail_ref[:, sl] = x[tm - SUBLANES:tm, :]
            x2 = _shift_rows(x, carry_ref[jc, 1:3, sl])
            p = x * cw_ref[2:3, sl] + x2 * cw_ref[0:1, sl]
            out = x * cw_ref[3:4, sl] + x2 * cw_ref[1:2, sl] + _shift_rows(p, carry_ref[jc, 0:1, sl]) + cb_ref[:, sl]
            o_ref[:, sl] = _silu(out).astype(o_ref.dtype)
            carry_ref[jc, 0:1, sl] = p[tm - 1:tm]
            carry_ref[jc, 1:3, sl] = x[tm - 2:tm]


def _norm_proj_conv(x, g, wt, w_dt, cw, cb, *, tm, tn, seq):
    m, d = x.shape
    n = SSM_DINNER + SSM_CONVDIM
    z_tiles = SSM_DINNER // tn
    c_tiles = SSM_CONVDIM // tn
    tiles_per_seq = seq // tm
    cj = lambda j: jnp.maximum(j - z_tiles, 0)
    return pl.pallas_call(
        functools.partial(_norm_proj_conv_kernel, tiles_per_seq=tiles_per_seq, z_tiles=z_tiles,
                          strip=MXU_WIDTH),
        grid=(m // tm, n // tn),
        in_specs=[
            pl.BlockSpec((tm, d), lambda i, j: (i, 0)),
            pl.BlockSpec((1, d), lambda i, j: (0, 0)),
            pl.BlockSpec((tn, d), lambda i, j: (j, 0)),
            pl.BlockSpec((d, LANES), lambda i, j: (0, 0)),
            pl.BlockSpec((SSM_CONV, tn), lambda i, j: (0, cj(j))),
            pl.BlockSpec((1, tn), lambda i, j: (0, cj(j))),
        ],
        out_specs=[
            pl.BlockSpec((tm, tn), lambda i, j: (i, j)),
            pl.BlockSpec((tm, LANES), lambda i, j: (i, 0)),
            pl.BlockSpec((None, SUBLANES, tn), lambda i, j: (i, 0, cj(j))),
        ],
        out_shape=[
            jax.ShapeDtypeStruct((m, n), BF16),
            jax.ShapeDtypeStruct((m, LANES), F32),
            jax.ShapeDtypeStruct((m // tm, SUBLANES, SSM_CONVDIM), F32),
        ],
        scratch_shapes=[
            pltpu.VMEM((tm, d), BF16),
            pltpu.VMEM((c_tiles, SUBLANES, tn), F32),
        ],
        compiler_params=_cparams(("arbitrary", "arbitrary")),
        name="ssd_in_proj_conv",
    )(x, g, wt, w_dt, cw, cb)


def _out_proj_kernel(a_ref, w_ref, h_ref, *rest, has_norm):
    if has_norm:
        g_ref, o_ref = rest
    else:
        (o_ref,) = rest
    y = h_ref[...] + _dot(a_ref[...], w_ref[...])
    if has_norm:
        ms = jnp.mean(y * y, axis=-1, keepdims=True)
        y = y * lax.rsqrt(ms + EPS) * g_ref[...]
    o_ref[...] = y


def _out_proj(a, w, h, *, tm, final_g=None):
    m, k = a.shape
    d = w.shape[1]
    has_norm = final_g is not None
    in_specs = [
        pl.BlockSpec((tm, k), lambda i: (i, 0)),
        pl.BlockSpec((k, d), lambda i: (0, 0)),
        pl.BlockSpec((tm, d), lambda i: (i, 0)),
    ]
    args = [a, w, h]
    if has_norm:
        in_specs.append(pl.BlockSpec((1, d), lambda i: (0, 0)))
        args.append(final_g)
    return pl.pallas_call(
        functools.partial(_out_proj_kernel, has_norm=has_norm),
        grid=(m // tm,),
        in_specs=in_specs,
        out_specs=pl.BlockSpec((tm, d), lambda i: (i, 0)),
        out_shape=jax.ShapeDtypeStruct((m, d), F32),
        compiler_params=_cparams(("arbitrary",)),
        name="out_proj",
    )(*args)


def _head_norm_gate(o, hn, gate):
    ms = jnp.mean(o * o, axis=-1, keepdims=True)
    on = o * lax.rsqrt(ms + EPS) * hn
    return (on * _silu(gate.astype(F32))).astype(BF16)


def _ret_prompt_kernel(proj_ref, lg_ref, hn_ref, og_ref, st_ref, s_ref):
    c = pl.program_id(1)
    n_seq, rows = proj_ref.shape[0], proj_ref.shape[1]
    k0, v0, g0 = RET_QK, 2 * RET_QK, 2 * RET_QK + RET_VW
    causal = _iota((rows, rows), 0) >= _iota((rows, rows), 1)

    @pl.when(c == 0)
    def _():
        s_ref[...] = jnp.zeros_like(s_ref)

    for n in range(n_seq):
        for h in range(RET_HEADS):
            vsl = slice(h * RET_DV, (h + 1) * RET_DV)
            q = proj_ref[n, :, h * RET_DK:(h + 1) * RET_DK]
            k = proj_ref[n, :, k0 + h * RET_DK:k0 + (h + 1) * RET_DK]
            v = proj_ref[n, :, v0 + h * RET_DV:v0 + (h + 1) * RET_DV]
            gate = proj_ref[n, :, g0 + h * RET_DV:g0 + (h + 1) * RET_DV]
            p = jnp.where(causal, _dot_nt(q, k), 0.0).astype(BF16)
            s = s_ref[n, h]
            o = _dot(p, v) + _dot(q, s.astype(BF16))
            chunk_decay = jnp.exp(jnp.full((1, RET_DV), float(rows), F32) * lg_ref[h, :, 0:1])
            s_ref[n, h] = (s + _dot_tn(k, v)) * chunk_decay
            og_ref[n, :, vsl] = _head_norm_gate(o, hn_ref[:, vsl], gate)

    @pl.when(c == pl.num_programs(1) - 1)
    def _():
        st_ref[...] = s_ref[...]


def _ret_prompt(proj, lg, hn, batch, seq):
    rows = RET_CHUNK
    nc = seq // rows
    ns = RET_PROMPT_SEQS
    return pl.pallas_call(
        _ret_prompt_kernel,
        grid=(batch // ns, nc),
        in_specs=[
            pl.BlockSpec((ns, rows, 2 * RET_QK + 2 * RET_VW), lambda b, c: (b, c, 0)),
            pl.BlockSpec((RET_HEADS, 1, LANES), lambda b, c: (0, 0, 0)),
            pl.BlockSpec((1, RET_VW), lambda b, c: (0, 0)),
        ],
        out_specs=[
            pl.BlockSpec((ns, rows, RET_VW), lambda b, c: (b, c, 0)),
            pl.BlockSpec((None, ns, RET_HEADS, RET_DK, RET_DV), lambda b, c: (0, b, 0, 0, 0)),
        ],
        out_shape=[
            jax.ShapeDtypeStruct((batch, seq, RET_VW), BF16),
            jax.ShapeDtypeStruct((1, batch, RET_HEADS, RET_DK, RET_DV), F32),
        ],
        scratch_shapes=[pltpu.VMEM((ns, RET_HEADS, RET_DK, RET_DV), F32)],
        compiler_params=_cparams(("arbitrary", "arbitrary")),
        name="ret_prompt",
    )(proj, lg, hn)


def _ret_sample_kernel(q_ref, k_ref, v_ref, gate_ref, lg_ref, hn_ref, st_in_ref,
                       og_ref, st_out_ref, *, seq):
    bb = st_in_ref.shape[0]
    rows = bb * seq
    i = _iota((rows, rows), 0)
    j = _iota((rows, rows), 1)
    same_causal = ((i // seq) == (j // seq)) & (i >= j)
    chunk_decay = jnp.exp(jnp.full((1, RET_DV), float(seq), F32) * lg_ref[:, 0:1])

    k = k_ref[...]
    v = v_ref[...].astype(F32)
    p = jnp.where(same_causal, _dot_nt(q_ref[...], k), 0.0)
    intra = _dot(p.astype(BF16), v_ref[...])
    q32 = q_ref[...].astype(F32)
    vrow = _iota((rows, RET_DV), 0) // seq
    cross = []
    for n in range(bb):
        s0 = st_in_ref[n]
        cross.append(_dot(q32[n * seq:(n + 1) * seq], s0))
        vn = jnp.where(vrow == n, v, 0.0)
        st_out_ref[n] = (s0 + _dot_tn(k, vn.astype(BF16))) * chunk_decay
    o = intra + jnp.concatenate(cross, axis=0)
    og_ref[...] = _head_norm_gate(o, hn_ref[...], gate_ref[...])


def _head_expand(vals, expand):
    parts = []
    for v in vals:
        hi = v.astype(BF16)
        lo = (v - hi.astype(F32)).astype(BF16)
        parts.append(jnp.concatenate([hi, lo], axis=1))
    return _dot(jnp.concatenate(parts, axis=0), expand)


def _ssd_prompt_kernel(zxc_ref, dtr_ref, dtr_next_ref, dtb_ref, alog_ref, dsk_ref, gn_ref, exp_ref,
                       g_ref, st_ref, s_ref, ex_ref, cum2_ref, cum2t_ref):
    c = pl.program_id(1)
    n_seq, rows = zxc_ref.shape[0], zxc_ref.shape[1]
    x0, b0, c0 = SSM_DINNER, 2 * SSM_DINNER, 2 * SSM_DINNER + SSM_GROUPS * SSM_DSTATE
    bw = SSM_GROUP_W
    ri = _iota((rows, rows), 0)
    rj = _iota((rows, rows), 1)
    causal = ri >= rj

    def decay_tables(n, dtr):
        dt = _softplus(dtr + dtb_ref[...])
        la = dt * (-jnp.exp(alog_ref[...]))
        cum = jnp.dot(causal.astype(F32), la, precision=lax.Precision.HIGHEST, preferred_element_type=F32)
        ecum = jnp.exp(cum)
        wdt = jnp.exp(cum[rows - 1:rows, :] - cum) * dt
        ex_ref[n] = _head_expand([ecum, wdt, dt], exp_ref[...])
        cum2 = cum * math.log2(math.e)
        cum2_ref[n] = cum2
        cum2t_ref[n] = cum2.T

    @pl.when(c == 0)
    def _():
        s_ref[...] = jnp.zeros_like(s_ref)
        for n in range(n_seq):
            decay_tables(n, dtr_ref[n])

    head_lane = _iota((rows, bw), 1) // SSM_HEADDIM

    for n in range(n_seq):
        cum2 = cum2_ref[n]
        cum2_t = cum2t_ref[n]
        for g in range(SSM_GROUPS):
            gsl = slice(g * bw, (g + 1) * bw)
            bm = zxc_ref[n, :, b0 + g * SSM_DSTATE:b0 + (g + 1) * SSM_DSTATE]
            cm = zxc_ref[n, :, c0 + g * SSM_DSTATE:c0 + (g + 1) * SSM_DSTATE]
            xs = zxc_ref[n, :, x0 + g * bw:x0 + (g + 1) * bw].astype(F32)
            s = s_ref[n, g]
            gmat = _dot_nt(cm, bm)
            ps = []
            for hh in range(HEADS_PER_GROUP):
                h = g * HEADS_PER_GROUP + hh
                colb = jnp.broadcast_to(cum2[:, h:h + 1], (rows, LANES))
                dec = jnp.exp2(jnp.where(causal, colb - cum2_t[h:h + 1, :], -jnp.inf))
                ps.append((gmat * dec).astype(BF16))
            v = (xs * ex_ref[n, 2 * rows:3 * rows, gsl]).astype(BF16)
            y4 = _dot(jnp.concatenate(ps, axis=0), v)
            y = y4[0:rows]
            for hh in range(1, HEADS_PER_GROUP):
                y = jnp.where(head_lane == hh, y4[hh * rows:(hh + 1) * rows], y)
            y = y + _dot(cm, s.astype(BF16)) * ex_ref[n, 0:rows, gsl]
            vw = (xs * ex_ref[n, rows:2 * rows, gsl]).astype(BF16)
            s_ref[n, g] = s * ex_ref[n, rows - 1:rows, gsl] + _dot_tn(bm, vw)

            y = y + xs * dsk_ref[:, gsl]
            gg = y * _silu(zxc_ref[n, :, gsl].astype(F32))
            ms = jnp.mean(gg * gg, axis=-1, keepdims=True)
            g_ref[n, :, gsl] = (gg * lax.rsqrt(ms + EPS) * gn_ref[:, gsl]).astype(BF16)

        decay_tables(n, dtr_next_ref[n])

    @pl.when(c == pl.num_programs(1) - 1)
    def _():
        for n in range(n_seq):
            for g in range(SSM_GROUPS):
                st_ref[n, g * bw:(g + 1) * bw, :] = s_ref[n, g].T


def _ssd_prompt_ret_sample_kernel(*refs, seq_s):
    ssd_in, ret_in = refs[0:8], refs[8:15]
    ssd_out, ret_out = refs[15:17], refs[17:19]
    ssd_scratch = refs[19:23]
    _ssd_prompt_kernel(*ssd_in, *ssd_out, *ssd_scratch)
    _ret_sample_kernel(*ret_in, *ret_out, seq=seq_s)


def _ssd_prompt_ret_sample(zxc, dtr, dtb, alog, dsk, gn, expand, batch, seq,
                           proj_s, lg, hn, state_ret, batch_s, seq_s):
    rows = SSD_CHUNK
    nc = seq // rows
    ns = SSD_PROMPT_SEQS
    steps = (batch // ns) * nc
    assert steps % RET_HEADS == 0 and batch_s % (steps // RET_HEADS) == 0
    bb = batch_s // (steps // RET_HEADS)
    rows_s = bb * seq_s
    full = lambda shape: pl.BlockSpec(shape, lambda b, c: (0,) * len(shape))
    sn = lambda b, c: (b * nc + c) // RET_HEADS
    sh = lambda b, c: (b * nc + c) % RET_HEADS
    qb = RET_QK // RET_DK
    vb = 2 * RET_QK // RET_DV
    gb = vb + RET_VW // RET_DV
    st_spec = pl.BlockSpec((None, bb, None, RET_DK, RET_DV), lambda b, c: (0, sn(b, c), sh(b, c), 0, 0))
    return pl.pallas_call(
        functools.partial(_ssd_prompt_ret_sample_kernel, seq_s=seq_s),
        grid=(batch // ns, nc),
        in_specs=[
            pl.BlockSpec((ns, rows, SSM_DINNER + SSM_CONVDIM), lambda b, c: (b, c, 0)),
            pl.BlockSpec((ns, rows, LANES), lambda b, c: (b, c, 0)),
            pl.BlockSpec((ns, rows, LANES), lambda b, c: (b, jnp.minimum(c + 1, nc - 1), 0)),
            full((1, LANES)), full((1, LANES)),
            full((1, SSM_DINNER)), full((1, SSM_DINNER)),
            full((2 * LANES, SSM_DINNER)),
            pl.BlockSpec((rows_s, RET_DK), lambda b, c: (sn(b, c), sh(b, c))),
            pl.BlockSpec((rows_s, RET_DK), lambda b, c: (sn(b, c), qb + sh(b, c))),
            pl.BlockSpec((rows_s, RET_DV), lambda b, c: (sn(b, c), vb + sh(b, c))),
            pl.BlockSpec((rows_s, RET_DV), lambda b, c: (sn(b, c), gb + sh(b, c))),
            pl.BlockSpec((None, 1, LANES), lambda b, c: (sh(b, c), 0, 0)),
            pl.BlockSpec((1, RET_DV), lambda b, c: (0, sh(b, c))),
            st_spec,
        ],
        out_specs=[
            pl.BlockSpec((ns, rows, SSM_DINNER), lambda b, c: (b, c, 0)),
            pl.BlockSpec((ns, SSM_DINNER, SSM_DSTATE), lambda b, c: (b, 0, 0)),
            pl.BlockSpec((rows_s, RET_DV), lambda b, c: (sn(b, c), sh(b, c))),
            st_spec,
        ],
        out_shape=[
            jax.ShapeDtypeStruct((batch, seq, SSM_DINNER), BF16),
            jax.ShapeDtypeStruct((batch, SSM_DINNER, SSM_DSTATE), F32),
            jax.ShapeDtypeStruct((batch_s * seq_s, RET_VW), BF16),
            jax.ShapeDtypeStruct(state_ret.shape, F32),
        ],
        scratch_shapes=[
            pltpu.VMEM((ns, SSM_GROUPS, SSM_DSTATE, SSM_GROUP_W), F32),
            pltpu.VMEM((ns, 3 * rows, SSM_DINNER), F32),
            pltpu.VMEM((ns, rows, LANES), F32),
            pltpu.VMEM((ns, LANES, rows), F32),
        ],
        compiler_params=_cparams(("arbitrary", "arbitrary")),
        name="ssd_prompt_ret_sample",
    )(zxc, dtr, dtr, dtb, alog, dsk, gn, expand, proj_s, proj_s, proj_s, proj_s, lg, hn, state_ret)


def _ssd_sample_kernel(z_ref, xa_ref, xb_ref, dtr_ref, conv_in_ref, st_in_ref,
                       cw_ref, cb_ref, dtb_ref, alog_ref, dsk_ref, gn_ref, exp_ref,
                       g_ref, st_out_ref, conv_out_ref, xc_ref, ex_ref, y_ref, *, seq):
    bb = st_in_ref.shape[0]
    rows = bb * seq
    keep = SSM_CONV - 1
    bw = SSM_GROUP_W

    assert seq == SUBLANES
    sub = _iota((bb, SUBLANES, SSM_DINNER), 1)
    hist_pad = jnp.zeros((bb, SUBLANES - keep, SSM_DINNER), F32)
    for half, x_ref in enumerate((xa_ref, xb_ref)):
        csl = slice(half * SSM_DINNER, (half + 1) * SSM_DINNER)
        w0, w1, w2, w3 = (cw_ref[k:k + 1, csl] for k in range(SSM_CONV))
        x = x_ref[...].reshape(bb, SUBLANES, SSM_DINNER)
        hist = jnp.concatenate([hist_pad, conv_in_ref[:, :, csl]], axis=1)
        for k in range(keep):
            conv_out_ref[k, :, csl] = x[:, SUBLANES - keep + k, :]
        x2 = pltpu.roll(jnp.where(sub >= SUBLANES - 2, hist, x), 2, axis=1)
        p = x * w2 + x2 * w0
        p_hist = hist * w2 + pltpu.roll(hist, 2, axis=1) * w0
        p1 = pltpu.roll(jnp.where(sub >= SUBLANES - 1, p_hist, p), 1, axis=1)
        out = x * w3 + x2 * w1 + p1 + cb_ref[:, csl]
        xc_ref[:, csl] = _silu(out).reshape(rows, SSM_DINNER)

    dt = _softplus(dtr_ref[...] + dtb_ref[...])
    la = dt * (-jnp.exp(alog_ref[...]))
    ri = _iota((rows, rows), 0)
    rj = _iota((rows, rows), 1)
    same = (ri // seq) == (rj // seq)
    cum = jnp.dot((same & (ri >= rj)).astype(F32), la, precision=lax.Precision.HIGHEST,
                  preferred_element_type=F32)
    clast = jnp.dot(same.astype(F32), la, precision=lax.Precision.HIGHEST, preferred_element_type=F32)
    eclast = jnp.exp(clast)
    ex_ref[...] = _head_expand([jnp.exp(cum), jnp.exp(clast - cum) * dt, dt], exp_ref[...])

    si = _iota((seq, seq), 0)
    sj = _iota((seq, seq), 1)
    causal = si >= sj
    eye = si == sj
    head_lane = _iota((seq, bw), 1) // SSM_HEADDIM

    for n in range(bb):
        r0 = n * seq
        rsl = slice(r0, r0 + seq)
        cum_n = cum[rsl]
        for g in range(SSM_GROUPS):
            gsl = slice(g * bw, (g + 1) * bw)
            bm = xc_ref[rsl, SSM_DINNER + g * SSM_DSTATE:SSM_DINNER + (g + 1) * SSM_DSTATE]
            cm = xc_ref[rsl, SSM_DINNER + (SSM_GROUPS + g) * SSM_DSTATE:
                        SSM_DINNER + (SSM_GROUPS + g + 1) * SSM_DSTATE]
            xs = xc_ref[rsl, gsl]
            gmat = _dot_nt(cm.astype(BF16), bm.astype(BF16))
            ps = []
            for hh in range(HEADS_PER_GROUP):
                h = g * HEADS_PER_GROUP + hh
                ccol = cum_n[:, h:h + 1]
                crow = jnp.sum(jnp.where(eye, ccol, 0.0), axis=0, keepdims=True)
                ps.append(gmat * jnp.exp(jnp.where(causal, ccol - crow, -jnp.inf)))
            v = xs * ex_ref[2 * rows + r0:2 * rows + r0 + seq, gsl]
            y4 = _dot(jnp.concatenate(ps, axis=0).astype(BF16), v.astype(BF16))
            y = y4[0:seq]
            for hh in range(1, HEADS_PER_GROUP):
                y = jnp.where(head_lane == hh, y4[hh * seq:(hh + 1) * seq], y)
            st = st_in_ref[n, gsl, :]
            y = y + _dot_nt(cm, st) * ex_ref[rsl, gsl]
            y_ref[rsl, gsl] = y
            vw = xs * ex_ref[rows + r0:rows + r0 + seq, gsl]
            upd = _dot_tn(vw.astype(BF16), bm.astype(BF16))
            for hh in range(HEADS_PER_GROUP):
                h = g * HEADS_PER_GROUP + hh
                hsl = slice(hh * SSM_HEADDIM, (hh + 1) * SSM_HEADDIM)
                osl = slice(g * bw + hh * SSM_HEADDIM, g * bw + (hh + 1) * SSM_HEADDIM)
                st_out_ref[n, osl, :] = st[hsl] * eclast[r0:r0 + 1, h:h + 1] + upd[hsl]

    y = y_ref[...] + xc_ref[:, 0:SSM_DINNER] * dsk_ref[...]
    gg = y * _silu(z_ref[...])
    for g in range(SSM_GROUPS):
        gsl = slice(g * bw, (g + 1) * bw)
        blk = gg[:, gsl]
        ms = jnp.mean(blk * blk, axis=-1, keepdims=True)
        g_ref[:, gsl] = (blk * lax.rsqrt(ms + EPS) * gn_ref[:, gsl]).astype(BF16)


def _ssd_sample(zx, dtr, conv_state, ssm_state_t, cw, cb, dtb, alog, dsk, gn, expand, batch, seq):
    bb = SSD_SAMPLE_BB
    rows = bb * seq
    keep = SSM_CONV - 1
    full = lambda shape: pl.BlockSpec(shape, lambda n: (0,) * len(shape))
    col = lambda k: pl.BlockSpec((rows, SSM_DINNER), lambda n: (n, k))
    st_spec = pl.BlockSpec((bb, SSM_DINNER, SSM_DSTATE), lambda n: (n, 0, 0))
    return pl.pallas_call(
        functools.partial(_ssd_sample_kernel, seq=seq),
        grid=(batch // bb,),
        in_specs=[
            col(0), col(1), col(2),
            pl.BlockSpec((rows, LANES), lambda n: (n, 0)),
            pl.BlockSpec((None, bb, keep, SSM_CONVDIM), lambda n: (0, n, 0, 0)),
            st_spec,
            full((SSM_CONV, SSM_CONVDIM)), full((1, SSM_CONVDIM)),
            full((1, LANES)), full((1, LANES)),
            full((1, SSM_DINNER)), full((1, SSM_DINNER)),
            full((2 * LANES, SSM_DINNER)),
        ],
        out_specs=[
            pl.BlockSpec((rows, SSM_DINNER), lambda n: (n, 0)),
            st_spec,
            pl.BlockSpec((keep, bb, SSM_CONVDIM), lambda n: (0, n, 0)),
        ],
        out_shape=[
            jax.ShapeDtypeStruct((batch * seq, SSM_DINNER), BF16),
            jax.ShapeDtypeStruct(ssm_state_t.shape, F32),
            jax.ShapeDtypeStruct((keep, batch, SSM_CONVDIM), F32),
        ],
        scratch_shapes=[
            pltpu.VMEM((rows, SSM_CONVDIM), F32),
            pltpu.VMEM((3 * rows, SSM_DINNER), F32),
            pltpu.VMEM((rows, SSM_DINNER), F32),
        ],
        compiler_params=_cparams(("arbitrary",)),
        name="ssd_sample",
    )(zx, zx, zx, dtr, conv_state, ssm_state_t, cw, cb, dtb, alog, dsk, gn, expand)


def _rope_tables(pos):
    half = RET_DK // 2
    freqs = ROPE_BASE ** (-jnp.arange(half, dtype=F32) / half)
    ang = pos.astype(F32)[:, None] * freqs[None, :]
    return jnp.cos(ang), jnp.sin(ang)


def _row(v):
    return v.reshape(1, -1).astype(F32)


def _pad_lanes(v):
    return jnp.pad(v.astype(F32), (0, LANES - v.shape[0])).reshape(1, LANES)


def kernel(x_prompt, x_sample, state_ret, state_ssm, state_conv, ret_norm, ret_w_in, ret_head_norm, ret_w_out, ssm_norm, ssm_w_in, ssm_conv_w, ssm_conv_b, ssm_dt_bias, ssm_a_log, ssm_d, ssm_gate_norm, ssm_w_out, final_norm):
    bp, lp, d = x_prompt.shape
    bs, ls, _ = x_sample.shape
    assert ret_norm.shape[0] == 1 and ssm_norm.shape[0] == 1, "one retention and one SSD layer"

    ret_win = ret_w_in[0]
    ret_wout = ret_w_out[0].astype(BF16)
    n_main = SSM_DINNER + SSM_CONVDIM
    ssm_win_t = jnp.swapaxes(ssm_w_in[0], 0, 1)
    ssm_wdt = jnp.pad(ssm_w_in[0, :, n_main:], ((0, 0), (0, LANES - SSM_HEADS))).astype(BF16)
    ssm_wout = ssm_w_out[0].astype(BF16)
    expand = (jnp.arange(2 * LANES)[:, None] % LANES == jnp.arange(SSM_DINNER)[None, :] // SSM_HEADDIM).astype(BF16)
    lg = jnp.log(1.0 - 2.0 ** (-5.0 - jnp.arange(RET_HEADS, dtype=F32)))
    lg = jnp.broadcast_to(lg[:, None, None], (RET_HEADS, 1, LANES))
    hn = _row(ret_head_norm[0])
    dsk = _row(jnp.repeat(ssm_d[0], SSM_HEADDIM))
    gn = _row(ssm_gate_norm[0])
    cw = ssm_conv_w[0].astype(F32)
    cb = _row(ssm_conv_b[0])
    dtb = _pad_lanes(ssm_dt_bias[0])
    alog = _pad_lanes(ssm_a_log[0])
    g_ret = _row(ret_norm[0])
    g_ssm = _row(ssm_norm[0])
    g_fin = _row(final_norm)

    xp = x_prompt.reshape(bp * lp, d)
    cos_p, sin_p = _rope_tables(jnp.arange(lp, dtype=jnp.int32))
    n_ret = 2 * RET_QK + 2 * RET_VW
    proj = _ret_proj(xp, g_ret, ret_win, cos_p, sin_p, lg, tm=lp, tn=RET_PROJ_TN, chunk=RET_CHUNK)
    og, ret_p = _ret_prompt(proj.reshape(bp, lp, n_ret), lg, hn, bp, lp)
    h1 = _out_proj(og.reshape(bp * lp, RET_VW), ret_wout, xp, tm=OUT_PROJ_TM)
    zxc, dtr, tail = _norm_proj_conv(h1, g_ssm, ssm_win_t, ssm_wdt, cw, cb, tm=SSD_PROJ_TM, tn=SSD_PROJ_TN, seq=lp)
    tail = tail[lp // SSD_PROJ_TM - 1::lp // SSD_PROJ_TM]
    xs = x_sample.reshape(bs * ls, d)
    cos_s, sin_s = _rope_tables(PAST_LEN + jnp.arange(ls, dtype=jnp.int32))
    proj_s = _ret_proj(xs, g_ret, ret_win, jnp.tile(cos_s, (bs, 1)), jnp.tile(sin_s, (bs, 1)), lg, tm=bs * ls,
                       tn=RET_PROJ_TN, chunk=ls)

    gated, ssm_pt, og_s, ret_s = _ssd_prompt_ret_sample(
        zxc.reshape(bp, lp, -1), dtr.reshape(bp, lp, LANES), dtb, alog, dsk, gn, expand, bp, lp,
        proj_s, lg, hn, state_ret, bs, ls)
    y_p = _out_proj(gated.reshape(bp * lp, SSM_DINNER), ssm_wout, h1, tm=OUT_PROJ_TM, final_g=g_fin)
    ssm_p = jnp.swapaxes(ssm_pt.reshape(1, bp, SSM_HEADS, SSM_HEADDIM, SSM_DSTATE), -1, -2)
    conv_p = tail[None, :, SUBLANES - (SSM_CONV - 1):, :]

    h1_s = _out_proj(og_s, ret_wout, xs, tm=SAMPLE_TM)
    zx_s, dtr_s = _ssd_proj(h1_s, g_ssm, ssm_win_t, ssm_wdt, n=n_main, tm=bs * ls, tn=SSD_PROJ_TN)
    st_t = jnp.swapaxes(state_ssm[0], -1, -2).reshape(bs, SSM_DINNER, SSM_DSTATE)
    gated_s, ssm_st, conv_st = _ssd_sample(zx_s, dtr_s, state_conv, st_t, cw, cb, dtb, alog, dsk, gn, expand, bs, ls)
    ssm_s = jnp.swapaxes(ssm_st.reshape(1, bs, SSM_HEADS, SSM_HEADDIM, SSM_DSTATE), -1, -2)
    conv_s = jnp.swapaxes(conv_st, 0, 1)[None]
    y_s = _out_proj(gated_s, ssm_wout, h1_s, tm=SAMPLE_TM, final_g=g_fin)

    return (y_p.reshape(bp, lp, d), y_s.reshape(bs, ls, d), ret_p, ret_s, ssm_p, ssm_s, conv_p, conv_s)
```

```python
import functools
import math

import jax
import jax.numpy as jnp
from jax import lax
from jax.experimental import pallas as pl
from jax.experimental.pallas import tpu as pltpu

F32 = jnp.float32
BF16 = jnp.bfloat16

D_MODEL = 1024
RET_HEADS = 4
RET_DK = 256
RET_DV = 512
RET_VW = RET_HEADS * RET_DV
RET_QK = RET_HEADS * RET_DK
SSM_DINNER = 2048
SSM_HEADDIM = 64
SSM_HEADS = 32
SSM_GROUPS = 8
SSM_DSTATE = 128
SSM_CONV = 4
SSM_CONVDIM = 4096
SSM_GROUP_W = SSM_DINNER // SSM_GROUPS
HEADS_PER_GROUP = SSM_HEADS // SSM_GROUPS
ROPE_BASE = 10000.0
EPS = 1e-6
PAST_LEN = 16384

LANES = 128
SUBLANES = 8
MXU_WIDTH = 256
VMEM_LIMIT = 56 * 1024 * 1024

RET_CHUNK = 256
RET_PROMPT_SEQS = 2
SSD_CHUNK = 128
SSD_PROMPT_SEQS = 2
SSD_SAMPLE_BB = 8
RET_PROJ_TN = 1024
SSD_PROJ_TM = 1024
SSD_PROJ_TN = 2048
OUT_PROJ_TM = 1024
SAMPLE_TM = 512


def _cparams(sem):
    return pltpu.CompilerParams(dimension_semantics=sem, vmem_limit_bytes=VMEM_LIMIT)


def _sigmoid(x):
    return 1.0 / (1.0 + jnp.exp2(x * (-math.log2(math.e))))


def _silu(x):
    return x * _sigmoid(x)


def _softplus(x):
    return jnp.maximum(x, 0.0) + jnp.log1p(jnp.exp(-jnp.abs(x)))


def _dot(a, b):
    return jnp.dot(a, b, preferred_element_type=F32)


def _dot_nt(a, b):
    return lax.dot_general(a, b, (((1,), (1,)), ((), ())), preferred_element_type=F32)


def _dot_tn(a, b):
    return lax.dot_general(a, b, (((0,), (0,)), ((), ())), preferred_element_type=F32)


def _iota(shape, dim):
    return lax.broadcasted_iota(jnp.int32, shape, dim)


def _ssd_proj_kernel(x_ref, g_ref, w_ref, wdt_ref, o_ref, dt_ref, xn_ref):
    @pl.when(pl.program_id(1) == 0)
    def _():
        x = x_ref[...]
        ms = jnp.mean(x * x, axis=-1, keepdims=True)
        xn = (x * lax.rsqrt(ms + EPS) * g_ref[...]).astype(BF16)
        xn_ref[...] = xn
        dt_ref[...] = _dot(xn, wdt_ref[...])

    o_ref[...] = _dot_nt(xn_ref[...], w_ref[...].astype(BF16))


def _ssd_proj(x, g, wt, w_dt, *, n, tm, tn):
    m, d = x.shape
    return pl.pallas_call(
        _ssd_proj_kernel,
        grid=(m // tm, n // tn),
        in_specs=[
            pl.BlockSpec((tm, d), lambda i, j: (i, 0)),
            pl.BlockSpec((1, d), lambda i, j: (0, 0)),
            pl.BlockSpec((tn, d), lambda i, j: (j, 0)),
            pl.BlockSpec((d, LANES), lambda i, j: (0, 0)),
        ],
        out_specs=[
            pl.BlockSpec((tm, tn), lambda i, j: (i, j)),
            pl.BlockSpec((tm, LANES), lambda i, j: (i, 0)),
        ],
        out_shape=[
            jax.ShapeDtypeStruct((m, n), F32),
            jax.ShapeDtypeStruct((m, LANES), F32),
        ],
        scratch_shapes=[pltpu.VMEM((tm, d), BF16)],
        compiler_params=_cparams(("arbitrary", "arbitrary")),
        name="ssd_in_proj",
    )(x, g, wt, w_dt)


def _rope(x, cos, sin):
    half = RET_DK // 2
    x1 = x[:, :half]
    x2 = x[:, half:]
    return jnp.concatenate([x1 * cos - x2 * sin, x1 * sin + x2 * cos], axis=1)


def _ret_proj_kernel(x_ref, g_ref, w_ref, cos_ref, sin_ref, lg_ref, o_ref, xn_ref, *, chunk):
    j = pl.program_id(1)
    tm, tn = o_ref.shape

    @pl.when(j == 0)
    def _():
        x = x_ref[...]
        ms = jnp.mean(x * x, axis=-1, keepdims=True)
        xn_ref[...] = (x * lax.rsqrt(ms + EPS) * g_ref[...]).astype(BF16)

    def rotated(sign, mult):
        cos = cos_ref[...]
        sin = sin_ref[...]
        pos = (_iota((chunk, RET_DK // 2), 0) + 1).astype(F32)
        for h in range(tn // RET_DK):
            sl = slice(h * RET_DK, (h + 1) * RET_DK)
            decay = jnp.tile(jnp.exp(pos * (lg_ref[h, :, 0:1] * sign)) * mult, (tm // chunk, 1))
            y = _dot(xn_ref[...], w_ref[:, sl].astype(BF16))
            o_ref[:, sl] = _rope(y, cos * decay, sin * decay).astype(o_ref.dtype)

    @pl.when(j == 0)
    def _():
        rotated(1.0, 1.0)

    @pl.when(j == 1)
    def _():
        rotated(-1.0, RET_DK ** -0.5)

    @pl.when(j >= 2)
    def _():
        o_ref[...] = _dot(xn_ref[...], w_ref[...].astype(BF16)).astype(o_ref.dtype)


def _ret_proj(x, g, w, cos, sin, lg, *, tm, tn, chunk):
    m, d = x.shape
    n = w.shape[1]
    assert cos.shape == (tm, RET_DK // 2) and sin.shape == cos.shape
    assert tn == RET_QK and tm % chunk == 0
    return pl.pallas_call(
        functools.partial(_ret_proj_kernel, chunk=chunk),
        grid=(m // tm, n // tn),
        in_specs=[
            pl.BlockSpec((tm, d), lambda i, j: (i, 0)),
            pl.BlockSpec((1, d), lambda i, j: (0, 0)),
            pl.BlockSpec((d, tn), lambda i, j: (0, j)),
            pl.BlockSpec((tm, RET_DK // 2), lambda i, j: (0, 0)),
            pl.BlockSpec((tm, RET_DK // 2), lambda i, j: (0, 0)),
            pl.BlockSpec((RET_HEADS, 1, LANES), lambda i, j: (0, 0, 0)),
        ],
        out_specs=pl.BlockSpec((tm, tn), lambda i, j: (i, j)),
        out_shape=jax.ShapeDtypeStruct((m, n), BF16),
        scratch_shapes=[pltpu.VMEM((tm, d), BF16)],
        compiler_params=_cparams(("arbitrary", "arbitrary")),
        name="ret_in_proj",
    )(x, g, w, cos, sin, lg)


def _shift_rows(a, first_rows):
    rows, cols = a.shape
    n = first_rows.shape[0]
    slabs = rows // SUBLANES
    a3 = a.reshape(slabs, SUBLANES, cols)
    first = jnp.concatenate([jnp.zeros((SUBLANES - n, cols), a.dtype), first_rows], axis=0)
    above = jnp.concatenate([first.reshape(1, SUBLANES, cols), a3[:slabs - 1]], axis=0)
    bottom = _iota((slabs, SUBLANES, cols), 1) >= SUBLANES - n
    return pltpu.roll(jnp.where(bottom, above, a3), n, axis=1).reshape(rows, cols)


def _norm_proj_conv_kernel(x_ref, g_ref, w_ref, wdt_ref, cw_ref, cb_ref, o_ref, dt_ref, tail_ref,
                           xn_ref, carry_ref, *, tiles_per_seq, z_tiles, strip):
    i = pl.program_id(0)
    j = pl.program_id(1)
    tm, tn = o_ref.shape

    @pl.when(j == 0)
    def _():
        x = x_ref[...]
        ms = jnp.mean(x * x, axis=-1, keepdims=True)
        xn = (x * lax.rsqrt(ms + EPS) * g_ref[...]).astype(BF16)
        xn_ref[...] = xn
        dt_ref[...] = _dot(xn, wdt_ref[...])

    @pl.when((j == 0) & (i % tiles_per_seq == 0))
    def _():
        carry_ref[...] = jnp.zeros_like(carry_ref)

    @pl.when(j < z_tiles)
    def _():
        o_ref[...] = _dot_nt(xn_ref[...], w_ref[...].astype(BF16)).astype(o_ref.dtype)

    @pl.when(j >= z_tiles)
    def _():
        jc = j - z_tiles
        for s in range(tn // strip):
            sl = slice(s * strip, (s + 1) * strip)
            x = _dot_nt(xn_ref[...], w_ref[sl, :].astype(BF16))
            tail_ref[:, sl] = x[tm - SUBLANES:tm, :]
            x2 = _shift_rows(x, carry_ref[jc, 1:3, sl])
            p = x * cw_ref[2:3, sl] + x2 * cw_ref[0:1, sl]
            out = x * cw_ref[3:4, sl] + x2 * cw_ref[1:2, sl] + _shift_rows(p, carry_ref[jc, 0:1, sl]) + cb_ref[:, sl]
            o_ref[:, sl] = _silu(out).astype(o_ref.dtype)
            carry_ref[jc, 0:1, sl] = p[tm - 1:tm]
            carry_ref[jc, 1:3, sl] = x[tm - 2:tm]


def _norm_proj_conv(x, g, wt, w_dt, cw, cb, *, tm, tn, seq):
    m, d = x.shape
    n = SSM_DINNER + SSM_CONVDIM
    z_tiles = SSM_DINNER // tn
    c_tiles = SSM_CONVDIM // tn
    tiles_per_seq = seq // tm
    cj = lambda j: jnp.maximum(j - z_tiles, 0)
    return pl.pallas_call(
        functools.partial(_norm_proj_conv_kernel, tiles_per_seq=tiles_per_seq, z_tiles=z_tiles,
                          strip=MXU_WIDTH),
        grid=(m // tm, n // tn),
        in_specs=[
            pl.BlockSpec((tm, d), lambda i, j: (i, 0)),
            pl.BlockSpec((1, d), lambda i, j: (0, 0)),
            pl.BlockSpec((tn, d), lambda i, j: (j, 0)),
            pl.BlockSpec((d, LANES), lambda i, j: (0, 0)),
            pl.BlockSpec((SSM_CONV, tn), lambda i, j: (0, cj(j))),
            pl.BlockSpec((1, tn), lambda i, j: (0, cj(j))),
        ],
        out_specs=[
            pl.BlockSpec((tm, tn), lambda i, j: (i, j)),
            pl.BlockSpec((tm, LANES), lambda i, j: (i, 0)),
            pl.BlockSpec((None, SUBLANES, tn), lambda i, j: (i, 0, cj(j))),
        ],
        out_shape=[
            jax.ShapeDtypeStruct((m, n), BF16),
            jax.ShapeDtypeStruct((m, LANES), F32),
            jax.ShapeDtypeStruct((m // tm, SUBLANES, SSM_CONVDIM), F32),
        ],
        scratch_shapes=[
            pltpu.VMEM((tm, d), BF16),
            pltpu.VMEM((c_tiles, SUBLANES, tn), F32),
        ],
        compiler_params=_cparams(("arbitrary", "arbitrary")),
        name="ssd_in_proj_conv",
    )(x, g, wt, w_dt, cw, cb)


def _out_proj_kernel(a_ref, w_ref, h_ref, *rest, has_norm):
    if has_norm:
        g_ref, o_ref = rest
    else:
        (o_ref,) = rest
    y = h_ref[...] + _dot(a_ref[...], w_ref[...])
    if has_norm:
        ms = jnp.mean(y * y, axis=-1, keepdims=True)
        y = y * lax.rsqrt(ms + EPS) * g_ref[...]
    o_ref[...] = y


def _out_proj(a, w, h, *, tm, final_g=None):
    m, k = a.shape
    d = w.shape[1]
    has_norm = final_g is not None
    in_specs = [
        pl.BlockSpec((tm, k), lambda i: (i, 0)),
        pl.BlockSpec((k, d), lambda i: (0, 0)),
        pl.BlockSpec((tm, d), lambda i: (i, 0)),
    ]
    args = [a, w, h]
    if has_norm:
        in_specs.append(pl.BlockSpec((1, d), lambda i: (0, 0)))
        args.append(final_g)
    return pl.pallas_call(
        functools.partial(_out_proj_kernel, has_norm=has_norm),
        grid=(m // tm,),
        in_specs=in_specs,
        out_specs=pl.BlockSpec((tm, d), lambda i: (i, 0)),
        out_shape=jax.ShapeDtypeStruct((m, d), F32),
        compiler_params=_cparams(("arbitrary",)),
        name="out_proj",
    )(*args)


def _head_norm_gate(o, hn, gate):
    ms = jnp.mean(o * o, axis=-1, keepdims=True)
    on = o * lax.rsqrt(ms + EPS) * hn
    return (on * _silu(gate.astype(F32))).astype(BF16)


def _ret_prompt_kernel(proj_ref, lg_ref, hn_ref, x_ref, wout_ref, h_ref, st_ref, s_ref):
    c = pl.program_id(1)
    n_seq, rows = proj_ref.shape[0], proj_ref.shape[1]
    k0, v0, g0 = RET_QK, 2 * RET_QK, 2 * RET_QK + RET_VW
    causal = _iota((rows, rows), 0) >= _iota((rows, rows), 1)

    @pl.when(c == 0)
    def _():
        s_ref[...] = jnp.zeros_like(s_ref)

    for n in range(n_seq):
        acc = x_ref[n]
        for h in range(RET_HEADS):
            vsl = slice(h * RET_DV, (h + 1) * RET_DV)
            q = proj_ref[n, :, h * RET_DK:(h + 1) * RET_DK]
            k = proj_ref[n, :, k0 + h * RET_DK:k0 + (h + 1) * RET_DK]
            v = proj_ref[n, :, v0 + h * RET_DV:v0 + (h + 1) * RET_DV]
            gate = proj_ref[n, :, g0 + h * RET_DV:g0 + (h + 1) * RET_DV]
            p = jnp.where(causal, _dot_nt(q, k), 0.0).astype(BF16)
            s = s_ref[n, h]
            o = _dot(p, v) + _dot(q, s.astype(BF16))
            chunk_decay = jnp.exp(jnp.full((1, RET_DV), float(rows), F32) * lg_ref[h, :, 0:1])
            s_ref[n, h] = (s + _dot_tn(k, v)) * chunk_decay
            acc = acc + _dot(_head_norm_gate(o, hn_ref[:, vsl], gate), wout_ref[vsl, :])
        h_ref[n] = acc

    @pl.when(c == pl.num_programs(1) - 1)
    def _():
        st_ref[...] = s_ref[...]


def _ret_prompt(proj, lg, hn, x, w_out, batch, seq):
    rows = RET_CHUNK
    nc = seq // rows
    ns = RET_PROMPT_SEQS
    d = x.shape[-1]
    return pl.pallas_call(
        _ret_prompt_kernel,
        grid=(batch // ns, nc),
        in_specs=[
            pl.BlockSpec((ns, rows, 2 * RET_QK + 2 * RET_VW), lambda b, c: (b, c, 0)),
            pl.BlockSpec((RET_HEADS, 1, LANES), lambda b, c: (0, 0, 0)),
            pl.BlockSpec((1, RET_VW), lambda b, c: (0, 0)),
            pl.BlockSpec((ns, rows, d), lambda b, c: (b, c, 0)),
            pl.BlockSpec((RET_VW, d), lambda b, c: (0, 0)),
        ],
        out_specs=[
            pl.BlockSpec((ns, rows, d), lambda b, c: (b, c, 0)),
            pl.BlockSpec((None, ns, RET_HEADS, RET_DK, RET_DV), lambda b, c: (0, b, 0, 0, 0)),
        ],
        out_shape=[
            jax.ShapeDtypeStruct((batch, seq, d), F32),
            jax.ShapeDtypeStruct((1, batch, RET_HEADS, RET_DK, RET_DV), F32),
        ],
        scratch_shapes=[pltpu.VMEM((ns, RET_HEADS, RET_DK, RET_DV), F32)],
        compiler_params=_cparams(("arbitrary", "arbitrary")),
        name="ret_prompt_out_proj",
    )(proj, lg, hn, x, w_out)


def _ret_sample_kernel(q_ref, k_ref, v_ref, gate_ref, lg_ref, hn_ref, st_in_ref,
                       og_ref, st_out_ref, *, seq):
    bb = st_in_ref.shape[0]
    rows = bb * seq
    i = _iota((rows, rows), 0)
    j = _iota((rows, rows), 1)
    same_causal = ((i // seq) == (j // seq)) & (i >= j)
    chunk_decay = jnp.exp(jnp.full((1, RET_DV), float(seq), F32) * lg_ref[:, 0:1])

    k = k_ref[...]
    v = v_ref[...].astype(F32)
    p = jnp.where(same_causal, _dot_nt(q_ref[...], k), 0.0)
    intra = _dot(p.astype(BF16), v_ref[...])
    q32 = q_ref[...].astype(F32)
    vrow = _iota((rows, RET_DV), 0) // seq
    cross = []
    for n in range(bb):
        s0 = st_in_ref[n]
        cross.append(_dot(q32[n * seq:(n + 1) * seq], s0))
        vn = jnp.where(vrow == n, v, 0.0)
        st_out_ref[n] = (s0 + _dot_tn(k, vn.astype(BF16))) * chunk_decay
    o = intra + jnp.concatenate(cross, axis=0)
    og_ref[...] = _head_norm_gate(o, hn_ref[...], gate_ref[...])


def _head_expand(vals, expand):
    parts = []
    for v in vals:
        hi = v.astype(BF16)
        lo = (v - hi.astype(F32)).astype(BF16)
        parts.append(jnp.concatenate([hi, lo], axis=1))
    return _dot(jnp.concatenate(parts, axis=0), expand)


def _ssd_prompt_kernel(zxc_ref, dtr_ref, dtr_next_ref, dtb_ref, alog_ref, dsk_ref, gn_ref, exp_ref,
                       g_ref, st_ref, s_ref, ex_ref, cum2_ref, cum2t_ref):
    c = pl.program_id(1)
    n_seq, rows = zxc_ref.shape[0], zxc_ref.shape[1]
    x0, b0, c0 = SSM_DINNER, 2 * SSM_DINNER, 2 * SSM_DINNER + SSM_GROUPS * SSM_DSTATE
    bw = SSM_GROUP_W
    ri = _iota((rows, rows), 0)
    rj = _iota((rows, rows), 1)
    causal = ri >= rj

    def decay_tables(n, dtr):
        dt = _softplus(dtr + dtb_ref[...])
        la = dt * (-jnp.exp(alog_ref[...]))
        cum = jnp.dot(causal.astype(F32), la, precision=lax.Precision.HIGHEST, preferred_element_type=F32)
        ecum = jnp.exp(cum)
        wdt = jnp.exp(cum[rows - 1:rows, :] - cum) * dt
        ex_ref[n] = _head_expand([ecum, wdt, dt], exp_ref[...])
        cum2 = cum * math.log2(math.e)
        cum2_ref[n] = cum2
        cum2t_ref[n] = cum2.T

    @pl.when(c == 0)
    def _():
        s_ref[...] = jnp.zeros_like(s_ref)
        for n in range(n_seq):
            decay_tables(n, dtr_ref[n])

    head_lane = _iota((rows, bw), 1) // SSM_HEADDIM

    for n in range(n_seq):
        cum2 = cum2_ref[n]
        cum2_t = cum2t_ref[n]
        for g in range(SSM_GROUPS):
            gsl = slice(g * bw, (g + 1) * bw)
            bm = zxc_ref[n, :, b0 + g * SSM_DSTATE:b0 + (g + 1) * SSM_DSTATE]
            cm = zxc_ref[n, :, c0 + g * SSM_DSTATE:c0 + (g + 1) * SSM_DSTATE]
            xs = zxc_ref[n, :, x0 + g * bw:x0 + (g + 1) * bw].astype(F32)
            s = s_ref[n, g]
            gmat = _dot_nt(cm, bm)
            ps = []
            for hh in range(HEADS_PER_GROUP):
                h = g * HEADS_PER_GROUP + hh
                colb = jnp.broadcast_to(cum2[:, h:h + 1], (rows, LANES))
                dec = jnp.exp2(jnp.where(causal, colb - cum2_t[h:h + 1, :], -jnp.inf))
                ps.append((gmat * dec).astype(BF16))
            v = (xs * ex_ref[n, 2 * rows:3 * rows, gsl]).astype(BF16)
            y4 = _dot(jnp.concatenate(ps, axis=0), v)
            y = y4[0:rows]
            for hh in range(1, HEADS_PER_GROUP):
                y = jnp.where(head_lane == hh, y4[hh * rows:(hh + 1) * rows], y)
            y = y + _dot(cm, s.astype(BF16)) * ex_ref[n, 0:rows, gsl]
            vw = (xs * ex_ref[n, rows:2 * rows, gsl]).astype(BF16)
            s_ref[n, g] = s * ex_ref[n, rows - 1:rows, gsl] + _dot_tn(bm, vw)

            y = y + xs * dsk_ref[:, gsl]
            gg = y * _silu(zxc_ref[n, :, gsl].astype(F32))
            ms = jnp.mean(gg * gg, axis=-1, keepdims=True)
            g_ref[n, :, gsl] = (gg * lax.rsqrt(ms + EPS) * gn_ref[:, gsl]).astype(BF16)

        decay_tables(n, dtr_next_ref[n])

    @pl.when(c == pl.num_programs(1) - 1)
    def _():
        for n in range(n_seq):
            for g in range(SSM_GROUPS):
                st_ref[n, g * bw:(g + 1) * bw, :] = s_ref[n, g].T


def _ssd_prompt_ret_sample_kernel(*refs, seq_s):
    ssd_in, ret_in = refs[0:8], refs[8:15]
    ssd_out, ret_out = refs[15:17], refs[17:19]
    ssd_scratch = refs[19:23]
    _ssd_prompt_kernel(*ssd_in, *ssd_out, *ssd_scratch)
    _ret_sample_kernel(*ret_in, *ret_out, seq=seq_s)


def _ssd_prompt_ret_sample(zxc, dtr, dtb, alog, dsk, gn, expand, batch, seq,
                           proj_s, lg, hn, state_ret, batch_s, seq_s):
    rows = SSD_CHUNK
    nc = seq // rows
    ns = SSD_PROMPT_SEQS
    steps = (batch // ns) * nc
    assert steps % RET_HEADS == 0 and batch_s % (steps // RET_HEADS) == 0
    bb = batch_s // (steps // RET_HEADS)
    rows_s = bb * seq_s
    full = lambda shape: pl.BlockSpec(shape, lambda b, c: (0,) * len(shape))
    sn = lambda b, c: (b * nc + c) // RET_HEADS
    sh = lambda b, c: (b * nc + c) % RET_HEADS
    qb = RET_QK // RET_DK
    vb = 2 * RET_QK // RET_DV
    gb = vb + RET_VW // RET_DV
    st_spec = pl.BlockSpec((None, bb, None, RET_DK, RET_DV), lambda b, c: (0, sn(b, c), sh(b, c), 0, 0))
    return pl.pallas_call(
        functools.partial(_ssd_prompt_ret_sample_kernel, seq_s=seq_s),
        grid=(batch // ns, nc),
        in_specs=[
            pl.BlockSpec((ns, rows, SSM_DINNER + SSM_CONVDIM), lambda b, c: (b, c, 0)),
            pl.BlockSpec((ns, rows, LANES), lambda b, c: (b, c, 0)),
            pl.BlockSpec((ns, rows, LANES), lambda b, c: (b, jnp.minimum(c + 1, nc - 1), 0)),
            full((1, LANES)), full((1, LANES)),
            full((1, SSM_DINNER)), full((1, SSM_DINNER)),
            full((2 * LANES, SSM_DINNER)),
            pl.BlockSpec((rows_s, RET_DK), lambda b, c: (sn(b, c), sh(b, c))),
            pl.BlockSpec((rows_s, RET_DK), lambda b, c: (sn(b, c), qb + sh(b, c))),
            pl.BlockSpec((rows_s, RET_DV), lambda b, c: (sn(b, c), vb + sh(b, c))),
            pl.BlockSpec((rows_s, RET_DV), lambda b, c: (sn(b, c), gb + sh(b, c))),
            pl.BlockSpec((None, 1, LANES), lambda b, c: (sh(b, c), 0, 0)),
            pl.BlockSpec((1, RET_DV), lambda b, c: (0, sh(b, c))),
            st_spec,
        ],
        out_specs=[
            pl.BlockSpec((ns, rows, SSM_DINNER), lambda b, c: (b, c, 0)),
            pl.BlockSpec((ns, SSM_DINNER, SSM_DSTATE), lambda b, c: (b, 0, 0)),
            pl.BlockSpec((rows_s, RET_DV), lambda b, c: (sn(b, c), sh(b, c))),
            st_spec,
        ],
        out_shape=[
            jax.ShapeDtypeStruct((batch, seq, SSM_DINNER), BF16),
            jax.ShapeDtypeStruct((batch, SSM_DINNER, SSM_DSTATE), F32),
            jax.ShapeDtypeStruct((batch_s * seq_s, RET_VW), BF16),
            jax.ShapeDtypeStruct(state_ret.shape, F32),
        ],
        scratch_shapes=[
            pltpu.VMEM((ns, SSM_GROUPS, SSM_DSTATE, SSM_GROUP_W), F32),
            pltpu.VMEM((ns, 3 * rows, SSM_DINNER), F32),
            pltpu.VMEM((ns, rows, LANES), F32),
            pltpu.VMEM((ns, LANES, rows), F32),
        ],
        compiler_params=_cparams(("arbitrary", "arbitrary")),
        name="ssd_prompt_ret_sample",
    )(zxc, dtr, dtr, dtb, alog, dsk, gn, expand, proj_s, proj_s, proj_s, proj_s, lg, hn, state_ret)


def _ssd_sample_kernel(z_ref, xa_ref, xb_ref, dtr_ref, conv_in_ref, st_in_ref,
                       cw_ref, cb_ref, dtb_ref, alog_ref, dsk_ref, gn_ref, exp_ref,
                       g_ref, st_out_ref, conv_out_ref, xc_ref, ex_ref, y_ref, *, seq):
    bb = st_in_ref.shape[0]
    rows = bb * seq
    keep = SSM_CONV - 1
    bw = SSM_GROUP_W

    assert seq == SUBLANES
    sub = _iota((bb, SUBLANES, SSM_DINNER), 1)
    hist_pad = jnp.zeros((bb, SUBLANES - keep, SSM_DINNER), F32)
    for half, x_ref in enumerate((xa_ref, xb_ref)):
        csl = slice(half * SSM_DINNER, (half + 1) * SSM_DINNER)
        w0, w1, w2, w3 = (cw_ref[k:k + 1, csl] for k in range(SSM_CONV))
        x = x_ref[...].reshape(bb, SUBLANES, SSM_DINNER)
        hist = jnp.concatenate([hist_pad, conv_in_ref[:, :, csl]], axis=1)
        for k in range(keep):
            conv_out_ref[k, :, csl] = x[:, SUBLANES - keep + k, :]
        x2 = pltpu.roll(jnp.where(sub >= SUBLANES - 2, hist, x), 2, axis=1)
        p = x * w2 + x2 * w0
        p_hist = hist * w2 + pltpu.roll(hist, 2, axis=1) * w0
        p1 = pltpu.roll(jnp.where(sub >= SUBLANES - 1, p_hist, p), 1, axis=1)
        out = x * w3 + x2 * w1 + p1 + cb_ref[:, csl]
        xc_ref[:, csl] = _silu(out).reshape(rows, SSM_DINNER)

    dt = _softplus(dtr_ref[...] + dtb_ref[...])
    la = dt * (-jnp.exp(alog_ref[...]))
    ri = _iota((rows, rows), 0)
    rj = _iota((rows, rows), 1)
    same = (ri // seq) == (rj // seq)
    cum = jnp.dot((same & (ri >= rj)).astype(F32), la, precision=lax.Precision.HIGHEST,
                  preferred_element_type=F32)
    clast = jnp.dot(same.astype(F32), la, precision=lax.Precision.HIGHEST, preferred_element_type=F32)
    eclast = jnp.exp(clast)
    ex_ref[...] = _head_expand([jnp.exp(cum), jnp.exp(clast - cum) * dt, dt], exp_ref[...])

    si = _iota((seq, seq), 0)
    sj = _iota((seq, seq), 1)
    causal = si >= sj
    eye = si == sj
    head_lane = _iota((seq, bw), 1) // SSM_HEADDIM

    for n in range(bb):
        r0 = n * seq
        rsl = slice(r0, r0 + seq)
        cum_n = cum[rsl]
        for g in range(SSM_GROUPS):
            gsl = slice(g * bw, (g + 1) * bw)
            bm = xc_ref[rsl, SSM_DINNER + g * SSM_DSTATE:SSM_DINNER + (g + 1) * SSM_DSTATE]
            cm = xc_ref[rsl, SSM_DINNER + (SSM_GROUPS + g) * SSM_DSTATE:
                        SSM_DINNER + (SSM_GROUPS + g + 1) * SSM_DSTATE]
            xs = xc_ref[rsl, gsl]
            gmat = _dot_nt(cm.astype(BF16), bm.astype(BF16))
            ps = []
            for hh in range(HEADS_PER_GROUP):
                h = g * HEADS_PER_GROUP + hh
                ccol = cum_n[:, h:h + 1]
                crow = jnp.sum(jnp.where(eye, ccol, 0.0), axis=0, keepdims=True)
                ps.append(gmat * jnp.exp(jnp.where(causal, ccol - crow, -jnp.inf)))
            v = xs * ex_ref[2 * rows + r0:2 * rows + r0 + seq, gsl]
            y4 = _dot(jnp.concatenate(ps, axis=0).astype(BF16), v.astype(BF16))
            y = y4[0:seq]
            for hh in range(1, HEADS_PER_GROUP):
                y = jnp.where(head_lane == hh, y4[hh * seq:(hh + 1) * seq], y)
            st = st_in_ref[n, gsl, :]
            y = y + _dot_nt(cm, st) * ex_ref[rsl, gsl]
            y_ref[rsl, gsl] = y
            vw = xs * ex_ref[rows + r0:rows + r0 + seq, gsl]
            upd = _dot_tn(vw.astype(BF16), bm.astype(BF16))
            for hh in range(HEADS_PER_GROUP):
                h = g * HEADS_PER_GROUP + hh
                hsl = slice(hh * SSM_HEADDIM, (hh + 1) * SSM_HEADDIM)
                osl = slice(g * bw + hh * SSM_HEADDIM, g * bw + (hh + 1) * SSM_HEADDIM)
                st_out_ref[n, osl, :] = st[hsl] * eclast[r0:r0 + 1, h:h + 1] + upd[hsl]

    y = y_ref[...] + xc_ref[:, 0:SSM_DINNER] * dsk_ref[...]
    gg = y * _silu(z_ref[...])
    for g in range(SSM_GROUPS):
        gsl = slice(g * bw, (g + 1) * bw)
        blk = gg[:, gsl]
        ms = jnp.mean(blk * blk, axis=-1, keepdims=True)
        g_ref[:, gsl] = (blk * lax.rsqrt(ms + EPS) * gn_ref[:, gsl]).astype(BF16)


def _ssd_sample(zx, dtr, conv_state, ssm_state_t, cw, cb, dtb, alog, dsk, gn, expand, batch, seq):
    bb = SSD_SAMPLE_BB
    rows = bb * seq
    keep = SSM_CONV - 1
    full = lambda shape: pl.BlockSpec(shape, lambda n: (0,) * len(shape))
    col = lambda k: pl.BlockSpec((rows, SSM_DINNER), lambda n: (n, k))
    st_spec = pl.BlockSpec((bb, SSM_DINNER, SSM_DSTATE), lambda n: (n, 0, 0))
    return pl.pallas_call(
        functools.partial(_ssd_sample_kernel, seq=seq),
        grid=(batch // bb,),
        in_specs=[
            col(0), col(1), col(2),
            pl.BlockSpec((rows, LANES), lambda n: (n, 0)),
            pl.BlockSpec((None, bb, keep, SSM_CONVDIM), lambda n: (0, n, 0, 0)),
            st_spec,
            full((SSM_CONV, SSM_CONVDIM)), full((1, SSM_CONVDIM)),
            full((1, LANES)), full((1, LANES)),
            full((1, SSM_DINNER)), full((1, SSM_DINNER)),
            full((2 * LANES, SSM_DINNER)),
        ],
        out_specs=[
            pl.BlockSpec((rows, SSM_DINNER), lambda n: (n, 0)),
            st_spec,
            pl.BlockSpec((keep, bb, SSM_CONVDIM), lambda n: (0, n, 0)),
        ],
        out_shape=[
            jax.ShapeDtypeStruct((batch * seq, SSM_DINNER), BF16),
            jax.ShapeDtypeStruct(ssm_state_t.shape, F32),
            jax.ShapeDtypeStruct((keep, batch, SSM_CONVDIM), F32),
        ],
        scratch_shapes=[
            pltpu.VMEM((rows, SSM_CONVDIM), F32),
            pltpu.VMEM((3 * rows, SSM_DINNER), F32),
            pltpu.VMEM((rows, SSM_DINNER), F32),
        ],
        compiler_params=_cparams(("arbitrary",)),
        name="ssd_sample",
    )(zx, zx, zx, dtr, conv_state, ssm_state_t, cw, cb, dtb, alog, dsk, gn, expand)


def _rope_tables(pos):
    half = RET_DK // 2
    freqs = ROPE_BASE ** (-jnp.arange(half, dtype=F32) / half)
    ang = pos.astype(F32)[:, None] * freqs[None, :]
    return jnp.cos(ang), jnp.sin(ang)


def _row(v):
    return v.reshape(1, -1).astype(F32)


def _pad_lanes(v):
    return jnp.pad(v.astype(F32), (0, LANES - v.shape[0])).reshape(1, LANES)


def kernel(x_prompt, x_sample, state_ret, state_ssm, state_conv, ret_norm, ret_w_in, ret_head_norm, ret_w_out, ssm_norm, ssm_w_in, ssm_conv_w, ssm_conv_b, ssm_dt_bias, ssm_a_log, ssm_d, ssm_gate_norm, ssm_w_out, final_norm):
    bp, lp, d = x_prompt.shape
    bs, ls, _ = x_sample.shape
    assert ret_norm.shape[0] == 1 and ssm_norm.shape[0] == 1, "one retention and one SSD layer"

    ret_win = ret_w_in[0]
    ret_wout = ret_w_out[0].astype(BF16)
    n_main = SSM_DINNER + SSM_CONVDIM
    ssm_win_t = jnp.swapaxes(ssm_w_in[0], 0, 1)
    ssm_wdt = jnp.pad(ssm_w_in[0, :, n_main:], ((0, 0), (0, LANES - SSM_HEADS))).astype(BF16)
    ssm_wout = ssm_w_out[0].astype(BF16)
    expand = (jnp.arange(2 * LANES)[:, None] % LANES == jnp.arange(SSM_DINNER)[None, :] // SSM_HEADDIM).astype(BF16)
    lg = jnp.log(1.0 - 2.0 ** (-5.0 - jnp.arange(RET_HEADS, dtype=F32)))
    lg = jnp.broadcast_to(lg[:, None, None], (RET_HEADS, 1, LANES))
    hn = _row(ret_head_norm[0])
    dsk = _row(jnp.repeat(ssm_d[0], SSM_HEADDIM))
    gn = _row(ssm_gate_norm[0])
    cw = ssm_conv_w[0].astype(F32)
    cb = _row(ssm_conv_b[0])
    dtb = _pad_lanes(ssm_dt_bias[0])
    alog = _pad_lanes(ssm_a_log[0])
    g_ret = _row(ret_norm[0])
    g_ssm = _row(ssm_norm[0])
    g_fin = _row(final_norm)

    xp = x_prompt.reshape(bp * lp, d)
    cos_p, sin_p = _rope_tables(jnp.arange(lp, dtype=jnp.int32))
    n_ret = 2 * RET_QK + 2 * RET_VW
    proj = _ret_proj(xp, g_ret, ret_win, cos_p, sin_p, lg, tm=lp, tn=RET_PROJ_TN, chunk=RET_CHUNK)
    h1, ret_p = _ret_prompt(proj.reshape(bp, lp, n_ret), lg, hn, x_prompt, ret_wout, bp, lp)
    h1 = h1.reshape(bp * lp, d)
    zxc, dtr, tail = _norm_proj_conv(h1, g_ssm, ssm_win_t, ssm_wdt, cw, cb, tm=SSD_PROJ_TM, tn=SSD_PROJ_TN, seq=lp)
    tail = tail[lp // SSD_PROJ_TM - 1::lp // SSD_PROJ_TM]
    xs = x_sample.reshape(bs * ls, d)
    cos_s, sin_s = _rope_tables(PAST_LEN + jnp.arange(ls, dtype=jnp.int32))
    proj_s = _ret_proj(xs, g_ret, ret_win, jnp.tile(cos_s, (bs, 1)), jnp.tile(sin_s, (bs, 1)), lg, tm=bs * ls,
                       tn=RET_PROJ_TN, chunk=ls)

    gated, ssm_pt, og_s, ret_s = _ssd_prompt_ret_sample(
        zxc.reshape(bp, lp, -1), dtr.reshape(bp, lp, LANES), dtb, alog, dsk, gn, expand, bp, lp,
        proj_s, lg, hn, state_ret, bs, ls)
    y_p = _out_proj(gated.reshape(bp * lp, SSM_DINNER), ssm_wout, h1, tm=OUT_PROJ_TM, final_g=g_fin)
    ssm_p = jnp.swapaxes(ssm_pt.reshape(1, bp, SSM_HEADS, SSM_HEADDIM, SSM_DSTATE), -1, -2)
    conv_p = tail[None, :, SUBLANES - (SSM_CONV - 1):, :]

    h1_s = _out_proj(og_s, ret_wout, xs, tm=SAMPLE_TM)
    zx_s, dtr_s = _ssd_proj(h1_s, g_ssm, ssm_win_t, ssm_wdt, n=n_main, tm=bs * ls, tn=SSD_PROJ_TN)
    st_t = jnp.swapaxes(state_ssm[0], -1, -2).reshape(bs, SSM_DINNER, SSM_DSTATE)
    gated_s, ssm_st, conv_st = _ssd_sample(zx_s, dtr_s, state_conv, st_t, cw, cb, dtb, alog, dsk, gn, expand, bs, ls)
    ssm_s = jnp.swapaxes(ssm_st.reshape(1, bs, SSM_HEADS, SSM_HEADDIM, SSM_DSTATE), -1, -2)
    conv_s = jnp.swapaxes(conv_st, 0, 1)[None]
    y_s = _out_proj(gated_s, ssm_wout, h1_s, tm=SAMPLE_TM, final_g=g_fin)

    return (y_p.reshape(bp, lp, d), y_s.reshape(bs, ls, d), ret_p, ret_s, ssm_p, ssm_s, conv_p, conv_s)
```

```python
import functools
import math

import jax
import jax.numpy as jnp
from jax import lax
from jax.experimental import pallas as pl
from jax.experimental.pallas import tpu as pltpu

F32 = jnp.float32
BF16 = jnp.bfloat16

D_MODEL = 1024
RET_HEADS = 4
RET_DK = 256
RET_DV = 512
RET_VW = RET_HEADS * RET_DV
RET_QK = RET_HEADS * RET_DK
SSM_DINNER = 2048
SSM_HEADDIM = 64
SSM_HEADS = 32
SSM_GROUPS = 8
SSM_DSTATE = 128
SSM_CONV = 4
SSM_CONVDIM = 4096
SSM_GROUP_W = SSM_DINNER // SSM_GROUPS
HEADS_PER_GROUP = SSM_HEADS // SSM_GROUPS
ROPE_BASE = 10000.0
EPS = 1e-6
PAST_LEN = 16384

LANES = 128
SUBLANES = 8
MXU_WIDTH = 256
VMEM_LIMIT = 56 * 1024 * 1024

RET_CHUNK = 256
RET_PROMPT_SEQS = 2
SSD_CHUNK = 128
SSD_PROMPT_SEQS = 2
SSD_SAMPLE_BB = 8
RET_PROJ_TN = 1024
SSD_PROJ_TM = 1024
SSD_PROJ_TN = 2048
OUT_PROJ_TM = 1024
SAMPLE_TM = 512


def _cparams(sem):
    return pltpu.CompilerParams(dimension_semantics=sem, vmem_limit_bytes=VMEM_LIMIT)


def _sigmoid(x):
    return 1.0 / (1.0 + jnp.exp2(x * (-math.log2(math.e))))


def _silu(x):
    return x * _sigmoid(x)


def _softplus(x):
    return jnp.maximum(x, 0.0) + jnp.log1p(jnp.exp(-jnp.abs(x)))


def _dot(a, b):
    return jnp.dot(a, b, preferred_element_type=F32)


def _dot_nt(a, b):
    return lax.dot_general(a, b, (((1,), (1,)), ((), ())), preferred_element_type=F32)


def _dot_tn(a, b):
    return lax.dot_general(a, b, (((0,), (0,)), ((), ())), preferred_element_type=F32)


def _iota(shape, dim):
    return lax.broadcasted_iota(jnp.int32, shape, dim)


def _ssd_proj_kernel(x_ref, g_ref, w_ref, wdt_ref, o_ref, dt_ref, xn_ref):
    @pl.when(pl.program_id(1) == 0)
    def _():
        x = x_ref[...]
        ms = jnp.mean(x * x, axis=-1, keepdims=True)
        xn = (x * lax.rsqrt(ms + EPS) * g_ref[...]).astype(BF16)
        xn_ref[...] = xn
        dt_ref[...] = _dot(xn, wdt_ref[...])

    o_ref[...] = _dot_nt(xn_ref[...], w_ref[...].astype(BF16))


def _ssd_proj(x, g, wt, w_dt, *, n, tm, tn):
    m, d = x.shape
    return pl.pallas_call(
        _ssd_proj_kernel,
        grid=(m // tm, n // tn),
        in_specs=[
            pl.BlockSpec((tm, d), lambda i, j: (i, 0)),
            pl.BlockSpec((1, d), lambda i, j: (0, 0)),
            pl.BlockSpec((tn, d), lambda i, j: (j, 0)),
            pl.BlockSpec((d, LANES), lambda i, j: (0, 0)),
        ],
        out_specs=[
            pl.BlockSpec((tm, tn), lambda i, j: (i, j)),
            pl.BlockSpec((tm, LANES), lambda i, j: (i, 0)),
        ],
        out_shape=[
            jax.ShapeDtypeStruct((m, n), F32),
            jax.ShapeDtypeStruct((m, LANES), F32),
        ],
        scratch_shapes=[pltpu.VMEM((tm, d), BF16)],
        compiler_params=_cparams(("arbitrary", "arbitrary")),
        name="ssd_in_proj",
    )(x, g, wt, w_dt)


def _rope(x, cos, sin):
    half = RET_DK // 2
    x1 = x[:, :half]
    x2 = x[:, half:]
    return jnp.concatenate([x1 * cos - x2 * sin, x1 * sin + x2 * cos], axis=1)


def _ret_proj_kernel(x_ref, g_ref, w_ref, cos_ref, sin_ref, lg_ref, o_ref, xn_ref, *, chunk):
    j = pl.program_id(1)
    tm, tn = o_ref.shape

    @pl.when(j == 0)
    def _():
        x = x_ref[...]
        ms = jnp.mean(x * x, axis=-1, keepdims=True)
        xn_ref[...] = (x * lax.rsqrt(ms + EPS) * g_ref[...]).astype(BF16)

    def rotated(sign, mult):
        cos = cos_ref[...]
        sin = sin_ref[...]
        pos = (_iota((chunk, RET_DK // 2), 0) + 1).astype(F32)
        for h in range(tn // RET_DK):
            sl = slice(h * RET_DK, (h + 1) * RET_DK)
            decay = jnp.tile(jnp.exp(pos * (lg_ref[h, :, 0:1] * sign)) * mult, (tm // chunk, 1))
            y = _dot(xn_ref[...], w_ref[:, sl].astype(BF16))
            o_ref[:, sl] = _rope(y, cos * decay, sin * decay).astype(o_ref.dtype)

    @pl.when(j == 0)
    def _():
        rotated(1.0, 1.0)

    @pl.when(j == 1)
    def _():
        rotated(-1.0, RET_DK ** -0.5)

    @pl.when(j >= 2)
    def _():
        o_ref[...] = _dot(xn_ref[...], w_ref[...].astype(BF16)).astype(o_ref.dtype)


def _ret_proj(x, g, w, cos, sin, lg, *, tm, tn, chunk):
    m, d = x.shape
    n = w.shape[1]
    assert cos.shape == (tm, RET_DK // 2) and sin.shape == cos.shape
    assert tn == RET_QK and tm % chunk == 0
    return pl.pallas_call(
        functools.partial(_ret_proj_kernel, chunk=chunk),
        grid=(m // tm, n // tn),
        in_specs=[
            pl.BlockSpec((tm, d), lambda i, j: (i, 0)),
            pl.BlockSpec((1, d), lambda i, j: (0, 0)),
            pl.BlockSpec((d, tn), lambda i, j: (0, j)),
            pl.BlockSpec((tm, RET_DK // 2), lambda i, j: (0, 0)),
            pl.BlockSpec((tm, RET_DK // 2), lambda i, j: (0, 0)),
            pl.BlockSpec((RET_HEADS, 1, LANES), lambda i, j: (0, 0, 0)),
        ],
        out_specs=pl.BlockSpec((tm, tn), lambda i, j: (i, j)),
        out_shape=jax.ShapeDtypeStruct((m, n), BF16),
        scratch_shapes=[pltpu.VMEM((tm, d), BF16)],
        compiler_params=_cparams(("arbitrary", "arbitrary")),
        name="ret_in_proj",
    )(x, g, w, cos, sin, lg)


def _shift_rows(a, first_rows):
    rows, cols = a.shape
    n = first_rows.shape[0]
    slabs = rows // SUBLANES
    a3 = a.reshape(slabs, SUBLANES, cols)
    first = jnp.concatenate([jnp.zeros((SUBLANES - n, cols), a.dtype), first_rows], axis=0)
    above = jnp.concatenate([first.reshape(1, SUBLANES, cols), a3[:slabs - 1]], axis=0)
    bottom = _iota((slabs, SUBLANES, cols), 1) >= SUBLANES - n
    return pltpu.roll(jnp.where(bottom, above, a3), n, axis=1).reshape(rows, cols)


def _norm_proj_conv_kernel(x_ref, g_ref, w_ref, wdt_ref, cw_ref, cb_ref, o_ref, dt_ref, tail_ref,
                           xn_ref, carry_ref, *, tiles_per_seq, z_tiles, strip):
    i = pl.program_id(0)
    j = pl.program_id(1)
    tm, tn = o_ref.shape

    @pl.when(j == 0)
    def _():
        x = x_ref[...]
        ms = jnp.mean(x * x, axis=-1, keepdims=True)
        xn = (x * lax.rsqrt(ms + EPS) * g_ref[...]).astype(BF16)
        xn_ref[...] = xn
        dt_ref[...] = _dot(xn, wdt_ref[...])

    @pl.when((j == 0) & (i % tiles_per_seq == 0))
    def _():
        carry_ref[...] = jnp.zeros_like(carry_ref)

    @pl.when(j < z_tiles)
    def _():
        o_ref[...] = _dot_nt(xn_ref[...], w_ref[...].astype(BF16)).astype(o_ref.dtype)

    @pl.when(j >= z_tiles)
    def _():
        jc = j - z_tiles
        for s in range(tn // strip):
            sl = slice(s * strip, (s + 1) * strip)
            x = _dot_nt(xn_ref[...], w_ref[sl, :].astype(BF16))
            tail_ref[:, sl] = x[tm - SUBLANES:tm, :]
            x2 = _shift_rows(x, carry_ref[jc, 1:3, sl])
            p = x * cw_ref[2:3, sl] + x2 * cw_ref[0:1, sl]
            out = x * cw_ref[3:4, sl] + x2 * cw_ref[1:2, sl] + _shift_rows(p, carry_ref[jc, 0:1, sl]) + cb_ref[:, sl]
            o_ref[:, sl] = _silu(out).astype(o_ref.dtype)
            carry_ref[jc, 0:1, sl] = p[tm - 1:tm]
            carry_ref[jc, 1:3, sl] = x[tm - 2:tm]


def _norm_proj_conv(x, g, wt, w_dt, cw, cb, *, tm, tn, seq):
    m, d = x.shape
    n = SSM_DINNER + SSM_CONVDIM
    z_tiles = SSM_DINNER // tn
    c_tiles = SSM_CONVDIM // tn
    tiles_per_seq = seq // tm
    cj = lambda j: jnp.maximum(j - z_tiles, 0)
    return pl.pallas_call(
        functools.partial(_norm_proj_conv_kernel, tiles_per_seq=tiles_per_seq, z_tiles=z_tiles,
                          strip=MXU_WIDTH),
        grid=(m // tm, n // tn),
        in_specs=[
            pl.BlockSpec((tm, d), lambda i, j: (i, 0)),
            pl.BlockSpec((1, d), lambda i, j: (0, 0)),
            pl.BlockSpec((tn, d), lambda i, j: (j, 0)),
            pl.BlockSpec((d, LANES), lambda i, j: (0, 0)),
            pl.BlockSpec((SSM_CONV, tn), lambda i, j: (0, cj(j))),
            pl.BlockSpec((1, tn), lambda i, j: (0, cj(j))),
        ],
        out_specs=[
            pl.BlockSpec((tm, tn), lambda i, j: (i, j)),
            pl.BlockSpec((tm, LANES), lambda i, j: (i, 0)),
            pl.BlockSpec((None, SUBLANES, tn), lambda i, j: (i, 0, cj(j))),
        ],
        out_shape=[
            jax.ShapeDtypeStruct((m, n), BF16),
            jax.ShapeDtypeStruct((m, LANES), F32),
            jax.ShapeDtypeStruct((m // tm, SUBLANES, SSM_CONVDIM), F32),
        ],
        scratch_shapes=[
            pltpu.VMEM((tm, d), BF16),
            pltpu.VMEM((c_tiles, SUBLANES, tn), F32),
        ],
        compiler_params=_cparams(("arbitrary", "arbitrary")),
        name="ssd_in_proj_conv",
    )(x, g, wt, w_dt, cw, cb)


def _out_proj_kernel(a_ref, w_ref, h_ref, *rest, has_norm):
    if has_norm:
        g_ref, o_ref = rest
    else:
        (o_ref,) = rest
    y = h_ref[...] + _dot(a_ref[...], w_ref[...])
    if has_norm:
        ms = jnp.mean(y * y, axis=-1, keepdims=True)
        y = y * lax.rsqrt(ms + EPS) * g_ref[...]
    o_ref[...] = y


def _out_proj(a, w, h, *, tm, final_g=None):
    m, k = a.shape
    d = w.shape[1]
    has_norm = final_g is not None
    in_specs = [
        pl.BlockSpec((tm, k), lambda i: (i, 0)),
        pl.BlockSpec((k, d), lambda i: (0, 0)),
        pl.BlockSpec((tm, d), lambda i: (i, 0)),
    ]
    args = [a, w, h]
    if has_norm:
        in_specs.append(pl.BlockSpec((1, d), lambda i: (0, 0)))
        args.append(final_g)
    return pl.pallas_call(
        functools.partial(_out_proj_kernel, has_norm=has_norm),
        grid=(m // tm,),
        in_specs=in_specs,
        out_specs=pl.BlockSpec((tm, d), lambda i: (i, 0)),
        out_shape=jax.ShapeDtypeStruct((m, d), F32),
        compiler_params=_cparams(("arbitrary",)),
        name="out_proj",
    )(*args)


def _head_norm_gate(o, hn, gate):
    ms = jnp.mean(o * o, axis=-1, keepdims=True)
    on = o * lax.rsqrt(ms + EPS) * hn
    return (on * _silu(gate.astype(F32))).astype(BF16)


def _ret_prompt_kernel(proj_ref, lg_ref, hn_ref, og_ref, st_ref, s_ref):
    c = pl.program_id(1)
    n_seq, rows = proj_ref.shape[0], proj_ref.shape[1]
    k0, v0, g0 = RET_QK, 2 * RET_QK, 2 * RET_QK + RET_VW
    causal = _iota((rows, rows), 0) >= _iota((rows, rows), 1)

    @pl.when(c == 0)
    def _():
        s_ref[...] = jnp.zeros_like(s_ref)

    for n in range(n_seq):
        for h in range(RET_HEADS):
            vsl = slice(h * RET_DV, (h + 1) * RET_DV)
            q = proj_ref[n, :, h * RET_DK:(h + 1) * RET_DK]
            k = proj_ref[n, :, k0 + h * RET_DK:k0 + (h + 1) * RET_DK]
            v = proj_ref[n, :, v0 + h * RET_DV:v0 + (h + 1) * RET_DV]
            gate = proj_ref[n, :, g0 + h * RET_DV:g0 + (h + 1) * RET_DV]
            p = jnp.where(causal, _dot_nt(q, k), 0.0).astype(BF16)
            s = s_ref[n, h]
            o = _dot(p, v) + _dot(q, s.astype(BF16))
            chunk_decay = jnp.exp(jnp.full((1, RET_DV), float(rows), F32) * lg_ref[h, :, 0:1])
            s_ref[n, h] = (s + _dot_tn(k, v)) * chunk_decay
            og_ref[n, :, vsl] = _head_norm_gate(o, hn_ref[:, vsl], gate)

    @pl.when(c == pl.num_programs(1) - 1)
    def _():
        st_ref[...] = s_ref[...]


def _ret_prompt(proj, lg, hn, batch, seq):
    rows = RET_CHUNK
    nc = seq // rows
    ns = RET_PROMPT_SEQS
    return pl.pallas_call(
        _ret_prompt_kernel,
        grid=(batch // ns, nc),
        in_specs=[
            pl.BlockSpec((ns, rows, 2 * RET_QK + 2 * RET_VW), lambda b, c: (b, c, 0)),
            pl.BlockSpec((RET_HEADS, 1, LANES), lambda b, c: (0, 0, 0)),
            pl.BlockSpec((1, RET_VW), lambda b, c: (0, 0)),
        ],
        out_specs=[
            pl.BlockSpec((ns, rows, RET_VW), lambda b, c: (b, c, 0)),
            pl.BlockSpec((None, ns, RET_HEADS, RET_DK, RET_DV), lambda b, c: (0, b, 0, 0, 0)),
        ],
        out_shape=[
            jax.ShapeDtypeStruct((batch, seq, RET_VW), BF16),
            jax.ShapeDtypeStruct((1, batch, RET_HEADS, RET_DK, RET_DV), F32),
        ],
        scratch_shapes=[pltpu.VMEM((ns, RET_HEADS, RET_DK, RET_DV), F32)],
        compiler_params=_cparams(("arbitrary", "arbitrary")),
        name="ret_prompt",
    )(proj, lg, hn)


def _ret_sample_kernel(q_ref, k_ref, v_ref, gate_ref, lg_ref, hn_ref, st_in_ref,
                       og_ref, st_out_ref, *, seq):
    bb = st_in_ref.shape[0]
    rows = bb * seq
    i = _iota((rows, rows), 0)
    j = _iota((rows, rows), 1)
    same_causal = ((i // seq) == (j // seq)) & (i >= j)
    chunk_decay = jnp.exp(jnp.full((1, RET_DV), float(seq), F32) * lg_ref[:, 0:1])

    k = k_ref[...]
    v = v_ref[...].astype(F32)
    p = jnp.where(same_causal, _dot_nt(q_ref[...], k), 0.0)
    intra = _dot(p.astype(BF16), v_ref[...])
    q32 = q_ref[...].astype(F32)
    vrow = _iota((rows, RET_DV), 0) // seq
    cross = []
    for n in range(bb):
        s0 = st_in_ref[n]
        cross.append(_dot(q32[n * seq:(n + 1) * seq], s0))
        vn = jnp.where(vrow == n, v, 0.0)
        st_out_ref[n] = (s0 + _dot_tn(k, vn.astype(BF16))) * chunk_decay
    o = intra + jnp.concatenate(cross, axis=0)
    og_ref[...] = _head_norm_gate(o, hn_ref[...], gate_ref[...])


def _head_expand(vals, expand):
    parts = []
    for v in vals:
        hi = v.astype(BF16)
        lo = (v - hi.astype(F32)).astype(BF16)
        parts.append(jnp.concatenate([hi, lo], axis=1))
    return _dot(jnp.concatenate(parts, axis=0), expand)


def _ssd_prompt_kernel(zxc_ref, dtr_ref, dtr_next_ref, dtb_ref, alog_ref, dsk_ref, gn_ref, exp_ref,
                       g_ref, st_ref, s_ref, ex_ref, cum2_ref, cum2t_ref):
    c = pl.program_id(1)
    n_seq, rows = zxc_ref.shape[0], zxc_ref.shape[1]
    x0, b0, c0 = SSM_DINNER, 2 * SSM_DINNER, 2 * SSM_DINNER + SSM_GROUPS * SSM_DSTATE
    bw = SSM_GROUP_W
    ri = _iota((rows, rows), 0)
    rj = _iota((rows, rows), 1)
    causal = ri >= rj

    def decay_tables(n, dtr):
        dt = _softplus(dtr + dtb_ref[...])
        la = dt * (-jnp.exp(alog_ref[...]))
        cum = jnp.dot(causal.astype(F32), la, precision=lax.Precision.HIGHEST, preferred_element_type=F32)
        ecum = jnp.exp(cum)
        wdt = jnp.exp(cum[rows - 1:rows, :] - cum) * dt
        ex_ref[n] = _head_expand([ecum, wdt, dt], exp_ref[...])
        cum2 = cum * math.log2(math.e)
        cum2_ref[n] = cum2
        cum2t_ref[n] = cum2.T

    @pl.when(c == 0)
    def _():
        s_ref[...] = jnp.zeros_like(s_ref)
        for n in range(n_seq):
            decay_tables(n, dtr_ref[n])

    head_lane = _iota((rows, bw), 1) // SSM_HEADDIM

    for n in range(n_seq):
        cum2 = cum2_ref[n]
        cum2_t = cum2t_ref[n]
        for g in range(SSM_GROUPS):
            gsl = slice(g * bw, (g + 1) * bw)
            bm = zxc_ref[n, :, b0 + g * SSM_DSTATE:b0 + (g + 1) * SSM_DSTATE]
            cm = zxc_ref[n, :, c0 + g * SSM_DSTATE:c0 + (g + 1) * SSM_DSTATE]
            xs = zxc_ref[n, :, x0 + g * bw:x0 + (g + 1) * bw].astype(F32)
            s = s_ref[n, g]
            gmat = _dot_nt(cm, bm)
            ps = []
            for hh in range(HEADS_PER_GROUP):
                h = g * HEADS_PER_GROUP + hh
                colb = jnp.broadcast_to(cum2[:, h:h + 1], (rows, LANES))
                dec = jnp.exp2(jnp.where(causal, colb - cum2_t[h:h + 1, :], -jnp.inf))
                ps.append((gmat * dec).astype(BF16))
            v = (xs * ex_ref[n, 2 * rows:3 * rows, gsl]).astype(BF16)
            y4 = _dot(jnp.concatenate(ps, axis=0), v)
            y = y4[0:rows]
            for hh in range(1, HEADS_PER_GROUP):
                y = jnp.where(head_lane == hh, y4[hh * rows:(hh + 1) * rows], y)
            y = y + _dot(cm, s.astype(BF16)) * ex_ref[n, 0:rows, gsl]
            vw = (xs * ex_ref[n, rows:2 * rows, gsl]).astype(BF16)
            s_ref[n, g] = s * ex_ref[n, rows - 1:rows, gsl] + _dot_tn(bm, vw)

            y = y + xs * dsk_ref[:, gsl]
            gg = y * _silu(zxc_ref[n, :, gsl].astype(F32))
            ms = jnp.mean(gg * gg, axis=-1, keepdims=True)
            g_ref[n, :, gsl] = (gg * lax.rsqrt(ms + EPS) * gn_ref[:, gsl]).astype(BF16)

        decay_tables(n, dtr_next_ref[n])

    @pl.when(c == pl.num_programs(1) - 1)
    def _():
        for n in range(n_seq):
            for g in range(SSM_GROUPS):
                st_ref[n, g * bw:(g + 1) * bw, :] = s_ref[n, g].T


def _ssd_prompt_ret_sample_kernel(*refs, seq_s):
    ssd_in, ret_in = refs[0:8], refs[8:15]
    ssd_out, ret_out = refs[15:17], refs[17:19]
    ssd_scratch = refs[19:23]
    _ssd_prompt_kernel(*ssd_in, *ssd_out, *ssd_scratch)
    _ret_sample_kernel(*ret_in, *ret_out, seq=seq_s)


def _ssd_prompt_ret_sample(zxc, dtr, dtb, alog, dsk, gn, expand, batch, seq,
                           proj_s, lg, hn, state_ret, batch_s, seq_s):
    rows = SSD_CHUNK
    nc = seq // rows
    ns = SSD_PROMPT_SEQS
    steps = (batch // ns) * nc
    assert steps % RET_HEADS == 0 and batch_s % (steps // RET_HEADS) == 0
    bb = batch_s // (steps // RET_HEADS)
    rows_s = bb * seq_s
    full = lambda shape: pl.BlockSpec(shape, lambda b, c: (0,) * len(shape))
    sn = lambda b, c: (b * nc + c) // RET_HEADS
    sh = lambda b, c: (b * nc + c) % RET_HEADS
    qb = RET_QK // RET_DK
    vb = 2 * RET_QK // RET_DV
    gb = vb + RET_VW // RET_DV
    st_spec = pl.BlockSpec((None, bb, None, RET_DK, RET_DV), lambda b, c: (0, sn(b, c), sh(b, c), 0, 0))
    return pl.pallas_call(
        functools.partial(_ssd_prompt_ret_sample_kernel, seq_s=seq_s),
        grid=(batch // ns, nc),
        in_specs=[
            pl.BlockSpec((ns, rows, SSM_DINNER + SSM_CONVDIM), lambda b, c: (b, c, 0)),
            pl.BlockSpec((ns, rows, LANES), lambda b, c: (b, c, 0)),
            pl.BlockSpec((ns, rows, LANES), lambda b, c: (b, jnp.minimum(c + 1, nc - 1), 0)),
            full((1, LANES)), full((1, LANES)),
            full((1, SSM_DINNER)), full((1, SSM_DINNER)),
            full((2 * LANES, SSM_DINNER)),
            pl.BlockSpec((rows_s, RET_DK), lambda b, c: (sn(b, c), sh(b, c))),
            pl.BlockSpec((rows_s, RET_DK), lambda b, c: (sn(b, c), qb + sh(b, c))),
            pl.BlockSpec((rows_s, RET_DV), lambda b, c: (sn(b, c), vb + sh(b, c))),
            pl.BlockSpec((rows_s, RET_DV), lambda b, c: (sn(b, c), gb + sh(b, c))),
            pl.BlockSpec((None, 1, LANES), lambda b, c: (sh(b, c), 0, 0)),
            pl.BlockSpec((1, RET_DV), lambda b, c: (0, sh(b, c))),
            st_spec,
        ],
        out_specs=[
            pl.BlockSpec((ns, rows, SSM_DINNER), lambda b, c: (b, c, 0)),
            pl.BlockSpec((ns, SSM_DINNER, SSM_DSTATE), lambda b, c: (b, 0, 0)),
            pl.BlockSpec((rows_s, RET_DV), lambda b, c: (sn(b, c), sh(b, c))),
            st_spec,
        ],
        out_shape=[
            jax.ShapeDtypeStruct((batch, seq, SSM_DINNER), BF16),
            jax.ShapeDtypeStruct((batch, SSM_DINNER, SSM_DSTATE), F32),
            jax.ShapeDtypeStruct((batch_s * seq_s, RET_VW), BF16),
            jax.ShapeDtypeStruct(state_ret.shape, F32),
        ],
        scratch_shapes=[
            pltpu.VMEM((ns, SSM_GROUPS, SSM_DSTATE, SSM_GROUP_W), F32),
            pltpu.VMEM((ns, 3 * rows, SSM_DINNER), F32),
            pltpu.VMEM((ns, rows, LANES), F32),
            pltpu.VMEM((ns, LANES, rows), F32),
        ],
        compiler_params=_cparams(("arbitrary", "arbitrary")),
        name="ssd_prompt_ret_sample",
    )(zxc, dtr, dtr, dtb, alog, dsk, gn, expand, proj_s, proj_s, proj_s, proj_s, lg, hn, state_ret)


def _ssd_sample_kernel(z_ref, xa_ref, xb_ref, dtr_ref, conv_in_ref, st_in_ref,
                       cw_ref, cb_ref, dtb_ref, alog_ref, dsk_ref, gn_ref, exp_ref,
                       g_ref, st_out_ref, conv_out_ref, xc_ref, ex_ref, y_ref, *, seq):
    bb = st_in_ref.shape[0]
    rows = bb * seq
    keep = SSM_CONV - 1
    bw = SSM_GROUP_W

    assert seq == SUBLANES
    sub = _iota((bb, SUBLANES, SSM_DINNER), 1)
    hist_pad = jnp.zeros((bb, SUBLANES - keep, SSM_DINNER), F32)
    for half, x_ref in enumerate((xa_ref, xb_ref)):
        csl = slice(half * SSM_DINNER, (half + 1) * SSM_DINNER)
        w0, w1, w2, w3 = (cw_ref[k:k + 1, csl] for k in range(SSM_CONV))
        x = x_ref[...].reshape(bb, SUBLANES, SSM_DINNER)
        hist = jnp.concatenate([hist_pad, conv_in_ref[:, :, csl]], axis=1)
        for k in range(keep):
            conv_out_ref[k, :, csl] = x[:, SUBLANES - keep + k, :]
        x2 = pltpu.roll(jnp.where(sub >= SUBLANES - 2, hist, x), 2, axis=1)
        p = x * w2 + x2 * w0
        p_hist = hist * w2 + pltpu.roll(hist, 2, axis=1) * w0
        p1 = pltpu.roll(jnp.where(sub >= SUBLANES - 1, p_hist, p), 1, axis=1)
        out = x * w3 + x2 * w1 + p1 + cb_ref[:, csl]
        xc_ref[:, csl] = _silu(out).reshape(rows, SSM_DINNER)

    dt = _softplus(dtr_ref[...] + dtb_ref[...])
    la = dt * (-jnp.exp(alog_ref[...]))
    ri = _iota((rows, rows), 0)
    rj = _iota((rows, rows), 1)
    same = (ri // seq) == (rj // seq)
    cum = jnp.dot((same & (ri >= rj)).astype(F32), la, precision=lax.Precision.HIGHEST,
                  preferred_element_type=F32)
    clast = jnp.dot(same.astype(F32), la, precision=lax.Precision.HIGHEST, preferred_element_type=F32)
    eclast = jnp.exp(clast)
    ex_ref[...] = _head_expand([jnp.exp(cum), jnp.exp(clast - cum) * dt, dt], exp_ref[...])

    causal = same & (ri >= rj)
    eye = ri == rj
    head_lane = _iota((rows, bw), 1) // SSM_HEADDIM

    for g in range(SSM_GROUPS):
        gsl = slice(g * bw, (g + 1) * bw)
        bsl = slice(SSM_DINNER + g * SSM_DSTATE, SSM_DINNER + (g + 1) * SSM_DSTATE)
        csl = slice(SSM_DINNER + (SSM_GROUPS + g) * SSM_DSTATE, SSM_DINNER + (SSM_GROUPS + g + 1) * SSM_DSTATE)
        gmat = _dot_nt(xc_ref[:, csl].astype(BF16), xc_ref[:, bsl].astype(BF16))
        ps = []
        for hh in range(HEADS_PER_GROUP):
            h = g * HEADS_PER_GROUP + hh
            ccol = cum[:, h:h + 1]
            crow = jnp.sum(jnp.where(eye, ccol, 0.0), axis=0, keepdims=True)
            ps.append(gmat * jnp.exp(jnp.where(causal, ccol - crow, -jnp.inf)))
        v = xc_ref[:, gsl] * ex_ref[2 * rows:3 * rows, gsl]
        y4 = _dot(jnp.concatenate(ps, axis=0).astype(BF16), v.astype(BF16))
        y = y4[0:rows]
        for hh in range(1, HEADS_PER_GROUP):
            y = jnp.where(head_lane == hh, y4[hh * rows:(hh + 1) * rows], y)
        y_ref[:, gsl] = y
        for n in range(bb):
            r0 = n * seq
            rsl = slice(r0, r0 + seq)
            bm = xc_ref[rsl, bsl]
            cm = xc_ref[rsl, csl]
            xs = xc_ref[rsl, gsl]
            st = st_in_ref[n, gsl, :]
            y_ref[rsl, gsl] = y_ref[rsl, gsl] + _dot_nt(cm, st) * ex_ref[rsl, gsl]
            vw = xs * ex_ref[rows + r0:rows + r0 + seq, gsl]
            upd = _dot_tn(vw.astype(BF16), bm.astype(BF16))
            for hh in range(HEADS_PER_GROUP):
                h = g * HEADS_PER_GROUP + hh
                hsl = slice(hh * SSM_HEADDIM, (hh + 1) * SSM_HEADDIM)
                osl = slice(g * bw + hh * SSM_HEADDIM, g * bw + (hh + 1) * SSM_HEADDIM)
                st_out_ref[n, osl, :] = st[hsl] * eclast[r0:r0 + 1, h:h + 1] + upd[hsl]

    y = y_ref[...] + xc_ref[:, 0:SSM_DINNER] * dsk_ref[...]
    gg = y * _silu(z_ref[...])
    for g in range(SSM_GROUPS):
        gsl = slice(g * bw, (g + 1) * bw)
        blk = gg[:, gsl]
        ms = jnp.mean(blk * blk, axis=-1, keepdims=True)
        g_ref[:, gsl] = (blk * lax.rsqrt(ms + EPS) * gn_ref[:, gsl]).astype(BF16)


def _ssd_sample(zx, dtr, conv_state, ssm_state_t, cw, cb, dtb, alog, dsk, gn, expand, batch, seq):
    bb = SSD_SAMPLE_BB
    rows = bb * seq
    keep = SSM_CONV - 1
    full = lambda shape: pl.BlockSpec(shape, lambda n: (0,) * len(shape))
    col = lambda k: pl.BlockSpec((rows, SSM_DINNER), lambda n: (n, k))
    st_spec = pl.BlockSpec((bb, SSM_DINNER, SSM_DSTATE), lambda n: (n, 0, 0))
    return pl.pallas_call(
        functools.partial(_ssd_sample_kernel, seq=seq),
        grid=(batch // bb,),
        in_specs=[
            col(0), col(1), col(2),
            pl.BlockSpec((rows, LANES), lambda n: (n, 0)),
            pl.BlockSpec((None, bb, keep, SSM_CONVDIM), lambda n: (0, n, 0, 0)),
            st_spec,
            full((SSM_CONV, SSM_CONVDIM)), full((1, SSM_CONVDIM)),
            full((1, LANES)), full((1, LANES)),
            full((1, SSM_DINNER)), full((1, SSM_DINNER)),
            full((2 * LANES, SSM_DINNER)),
        ],
        out_specs=[
            pl.BlockSpec((rows, SSM_DINNER), lambda n: (n, 0)),
            st_spec,
            pl.BlockSpec((keep, bb, SSM_CONVDIM), lambda n: (0, n, 0)),
        ],
        out_shape=[
            jax.ShapeDtypeStruct((batch * seq, SSM_DINNER), BF16),
            jax.ShapeDtypeStruct(ssm_state_t.shape, F32),
            jax.ShapeDtypeStruct((keep, batch, SSM_CONVDIM), F32),
        ],
        scratch_shapes=[
            pltpu.VMEM((rows, SSM_CONVDIM), F32),
            pltpu.VMEM((3 * rows, SSM_DINNER), F32),
            pltpu.VMEM((rows, SSM_DINNER), F32),
        ],
        compiler_params=_cparams(("arbitrary",)),
        name="ssd_sample",
    )(zx, zx, zx, dtr, conv_state, ssm_state_t, cw, cb, dtb, alog, dsk, gn, expand)


def _rope_tables(pos):
    half = RET_DK // 2
    freqs = ROPE_BASE ** (-jnp.arange(half, dtype=F32) / half)
    ang = pos.astype(F32)[:, None] * freqs[None, :]
    return jnp.cos(ang), jnp.sin(ang)


def _row(v):
    return v.reshape(1, -1).astype(F32)


def _pad_lanes(v):
    return jnp.pad(v.astype(F32), (0, LANES - v.shape[0])).reshape(1, LANES)


def kernel(x_prompt, x_sample, state_ret, state_ssm, state_conv, ret_norm, ret_w_in, ret_head_norm, ret_w_out, ssm_norm, ssm_w_in, ssm_conv_w, ssm_conv_b, ssm_dt_bias, ssm_a_log, ssm_d, ssm_gate_norm, ssm_w_out, final_norm):
    bp, lp, d = x_prompt.shape
    bs, ls, _ = x_sample.shape
    assert ret_norm.shape[0] == 1 and ssm_norm.shape[0] == 1, "one retention and one SSD layer"

    ret_win = ret_w_in[0]
    ret_wout = ret_w_out[0].astype(BF16)
    n_main = SSM_DINNER + SSM_CONVDIM
    ssm_win_t = jnp.swapaxes(ssm_w_in[0], 0, 1)
    ssm_wdt = jnp.pad(ssm_w_in[0, :, n_main:], ((0, 0), (0, LANES - SSM_HEADS))).astype(BF16)
    ssm_wout = ssm_w_out[0].astype(BF16)
    expand = (jnp.arange(2 * LANES)[:, None] % LANES == jnp.arange(SSM_DINNER)[None, :] // SSM_HEADDIM).astype(BF16)
    lg = jnp.log(1.0 - 2.0 ** (-5.0 - jnp.arange(RET_HEADS, dtype=F32)))
    lg = jnp.broadcast_to(lg[:, None, None], (RET_HEADS, 1, LANES))
    hn = _row(ret_head_norm[0])
    dsk = _row(jnp.repeat(ssm_d[0], SSM_HEADDIM))
    gn = _row(ssm_gate_norm[0])
    cw = ssm_conv_w[0].astype(F32)
    cb = _row(ssm_conv_b[0])
    dtb = _pad_lanes(ssm_dt_bias[0])
    alog = _pad_lanes(ssm_a_log[0])
    g_ret = _row(ret_norm[0])
    g_ssm = _row(ssm_norm[0])
    g_fin = _row(final_norm)

    xp = x_prompt.reshape(bp * lp, d)
    cos_p, sin_p = _rope_tables(jnp.arange(lp, dtype=jnp.int32))
    n_ret = 2 * RET_QK + 2 * RET_VW
    proj = _ret_proj(xp, g_ret, ret_win, cos_p, sin_p, lg, tm=lp, tn=RET_PROJ_TN, chunk=RET_CHUNK)
    og, ret_p = _ret_prompt(proj.reshape(bp, lp, n_ret), lg, hn, bp, lp)
    h1 = _out_proj(og.reshape(bp * lp, RET_VW), ret_wout, xp, tm=OUT_PROJ_TM)
    zxc, dtr, tail = _norm_proj_conv(h1, g_ssm, ssm_win_t, ssm_wdt, cw, cb, tm=SSD_PROJ_TM, tn=SSD_PROJ_TN, seq=lp)
    tail = tail[lp // SSD_PROJ_TM - 1::lp // SSD_PROJ_TM]
    xs = x_sample.reshape(bs * ls, d)
    cos_s, sin_s = _rope_tables(PAST_LEN + jnp.arange(ls, dtype=jnp.int32))
    proj_s = _ret_proj(xs, g_ret, ret_win, jnp.tile(cos_s, (bs, 1)), jnp.tile(sin_s, (bs, 1)), lg, tm=bs * ls,
                       tn=RET_PROJ_TN, chunk=ls)

    gated, ssm_pt, og_s, ret_s = _ssd_prompt_ret_sample(
        zxc.reshape(bp, lp, -1), dtr.reshape(bp, lp, LANES), dtb, alog, dsk, gn, expand, bp, lp,
        proj_s, lg, hn, state_ret, bs, ls)
    y_p = _out_proj(gated.reshape(bp * lp, SSM_DINNER), ssm_wout, h1, tm=OUT_PROJ_TM, final_g=g_fin)
    ssm_p = jnp.swapaxes(ssm_pt.reshape(1, bp, SSM_HEADS, SSM_HEADDIM, SSM_DSTATE), -1, -2)
    conv_p = tail[None, :, SUBLANES - (SSM_CONV - 1):, :]

    h1_s = _out_proj(og_s, ret_wout, xs, tm=SAMPLE_TM)
    zx_s, dtr_s = _ssd_proj(h1_s, g_ssm, ssm_win_t, ssm_wdt, n=n_main, tm=bs * ls, tn=SSD_PROJ_TN)
    st_t = jnp.swapaxes(state_ssm[0], -1, -2).reshape(bs, SSM_DINNER, SSM_DSTATE)
    gated_s, ssm_st, conv_st = _ssd_sample(zx_s, dtr_s, state_conv, st_t, cw, cb, dtb, alog, dsk, gn, expand, bs, ls)
    ssm_s = jnp.swapaxes(ssm_st.reshape(1, bs, SSM_HEADS, SSM_HEADDIM, SSM_DSTATE), -1, -2)
    conv_s = jnp.swapaxes(conv_st, 0, 1)[None]
    y_s = _out_proj(gated_s, ssm_wout, h1_s, tm=SAMPLE_TM, final_g=g_fin)

    return (y_p.reshape(bp, lp, d), y_s.reshape(bs, ls, d), ret_p, ret_s, ssm_p, ssm_s, conv_p, conv_s)
```
